```python
import math
import jax, jax.numpy as jnp
from jax import lax
import numpy as np

D_MODEL = 1024
BATCH = 32
SEQ = 256
DEPTH = 2
DEC_BATCH = 2
DEC_SEQ = 1024
PAST_LEN = 256

GRID_W = 64
MLA_HEADS = 6
MLA_NOPE = 64
MLA_ROPE = 32
MLA_QK = MLA_NOPE + MLA_ROPE
MLA_V = 64
Q_LORA = 256
KV_LORA = 128
NA_HEADS = 6
NA_HD = 64
NA_KR = 8
NA_KW = 16
DF_HEADS = 4
DF_HD = 64
DF_QK = 32

MLA_W = MLA_HEADS * MLA_V
NA_W = NA_HEADS * NA_HD
DF_W = DF_HEADS * DF_HD
MIX_W = MLA_W + NA_W + DF_W
SPLIT_SIZES = (Q_LORA, KV_LORA, MLA_ROPE, NA_W, NA_W, NA_W, DF_W, DF_W, DF_W)
IN_COLS = Q_LORA + KV_LORA + MLA_ROPE + 3 * NA_W + 3 * DF_W
D_FF = -(-8 * D_MODEL // (3 * 256)) * 256
ROPE_BASE = 10000.0
EPS = 1e-6
Q_BLOCK = 128
NEG = -1e30

kernel_name = 'hybrid_mla_natten_diff_dit_step'


def _rms(x, g):
    xf = x.astype(jnp.float32)
    y = xf * lax.rsqrt(jnp.mean(xf * xf, axis=-1, keepdims=True) + EPS)
    return (y * g.astype(jnp.float32)).astype(x.dtype)


def _rms_pairs(x, g):
    s = x.shape
    return _rms(x.reshape(s[:-1] + (2, DF_QK)), g).reshape(s)


def _heads(t, n):
    b, l, _ = t.shape
    return t.reshape(b, l, n, -1).transpose(0, 2, 1, 3)


def _merge(t):
    b, h, l, d = t.shape
    return t.transpose(0, 2, 1, 3).reshape(b, l, h * d)


def _axial_rope(length, rot_dim):
    t = jnp.arange(length)
    row = (t // GRID_W).astype(jnp.float32)
    col = (t % GRID_W).astype(jnp.float32)
    n = rot_dim // 4
    inv = 1.0 / (ROPE_BASE ** (jnp.arange(n, dtype=jnp.float32) * 2.0 / (rot_dim // 2)))
    ar = row[:, None] * inv
    ac = col[:, None] * inv
    ang = jnp.concatenate([ar, ar, ac, ac], axis=-1)
    return jnp.cos(ang), jnp.sin(ang)


def _rotate_axial(x):
    xs = x.reshape(x.shape[:-1] + (2, 2, -1))
    x1, x2 = xs[..., 0, :], xs[..., 1, :]
    return jnp.stack([-x2, x1], axis=-2).reshape(x.shape)


def _apply_rope(x, cos, sin):
    xf = x.astype(jnp.float32)
    return (xf * cos + _rotate_axial(xf) * sin).astype(x.dtype)


def _rope_tail(x, cos, sin):
    return jnp.concatenate([x[..., :MLA_NOPE], _apply_rope(x[..., MLA_NOPE:], cos, sin)], axis=-1)


def _rope_pairs(x, cos, sin):
    s = x.shape
    xr = x.reshape(s[:-1] + (2, DF_QK))
    return _apply_rope(xr, cos[:, None], sin[:, None]).reshape(s)


def _over_query_blocks(fn, q):
    b, h, l, d = q.shape
    qb = math.gcd(l, Q_BLOCK)
    nb = l // qb
    blocks = q.reshape(b, h, nb, qb, d).transpose(2, 0, 1, 3, 4)
    out = lax.map(fn, blocks)
    return out.transpose(1, 2, 0, 3, 4).reshape(b, h, l, out.shape[-1])


def _softmax_attend(q, k, v, scale):
    def blk(qb):
        s = jnp.einsum('bhqd,bhkd->bhqk', qb, k).astype(jnp.float32) * scale
        p = jax.nn.softmax(s, axis=-1).astype(v.dtype)
        return jnp.einsum('bhqk,bhkd->bhqd', p, v)
    return _over_query_blocks(blk, q)


def _diff_lambda(lp, lam_init):
    f = lambda a: a.astype(jnp.float32)
    return (jnp.exp(jnp.sum(f(lp['df_lq1']) * f(lp['df_lk1'])))
            - jnp.exp(jnp.sum(f(lp['df_lq2']) * f(lp['df_lk2']))) + lam_init)


def _diff_attend(q, k, v, lp, lam_init):
    scale = DF_QK ** -0.5
    lam = _diff_lambda(lp, lam_init)
    k1, k2 = k[..., :DF_QK], k[..., DF_QK:]

    def blk(qb):
        s1 = jnp.einsum('bhqd,bhkd->bhqk', qb[..., :DF_QK], k1).astype(jnp.float32) * scale
        s2 = jnp.einsum('bhqd,bhkd->bhqk', qb[..., DF_QK:], k2).astype(jnp.float32) * scale
        p = jax.nn.softmax(s1, axis=-1) - lam * jax.nn.softmax(s2, axis=-1)
        return jnp.einsum('bhqk,bhkd->bhqd', p.astype(v.dtype), v)
    o = _over_query_blocks(blk, q)
    return _rms(o, lp['g_df_sub']) * (1.0 - lam_init)


def _neighbourhood_attend(q, k, v, k_ctx, v_ctx, rpb):
    b, h, L, d = q.shape
    rows = L // GRID_W
    kr = min(NA_KR, rows)
    kw = NA_KW
    r = jnp.arange(rows)
    cidx = jnp.arange(GRID_W)
    row_idx = jnp.clip(r - kr // 2, 0, rows - kr)[:, None] + jnp.arange(kr)[None, :]
    col_start = jnp.clip(cidx - kw // 2, 0, GRID_W - kw)
    in_win = (cidx[None, :] >= col_start[:, None]) & (cidx[None, :] < col_start[:, None] + kw)
    rel_r = row_idx - r[:, None] + (NA_KR - 1)
    rel_c = jnp.clip(cidx[None, :] - cidx[:, None], -(kw - 1), kw - 1) + (kw - 1)
    bias = rpb[:, rel_r[:, None, :, None], rel_c[None, :, None, :]]
    qg = q.reshape(b, h, rows, GRID_W, d)
    kg = k.reshape(b, h, rows, GRID_W, d)[:, :, row_idx]
    vg = v.reshape(b, h, rows, GRID_W, d)[:, :, row_idx]
    scale = d ** -0.5
    s_win = jnp.einsum('bhrqd,bhrjkd->bhrqjk', qg, kg).astype(jnp.float32) * scale + bias[None].astype(jnp.float32)
    s_win = jnp.where(in_win[:, None, :], s_win, NEG)
    s_ctx = jnp.einsum('bhrqd,bhcd->bhrqc', qg, k_ctx).astype(jnp.float32) * scale
    n_win = kr * GRID_W
    p = jax.nn.softmax(jnp.concatenate([s_win.reshape(b, h, rows, GRID_W, n_win), s_ctx], axis=-1), axis=-1).astype(v.dtype)
    p_win = p[..., :n_win].reshape(b, h, rows, GRID_W, kr, GRID_W)
    o = (jnp.einsum('bhrqjk,bhrjkd->bhrqd', p_win, vg)
         + jnp.einsum('bhrqc,bhcd->bhrqd', p[..., n_win:], v_ctx))
    return o.reshape(b, h, L, d)


def _modulation(cvec, w_mod, b_mod):
    m = jax.nn.silu(cvec) @ w_mod + b_mod
    return jnp.split(m[:, None, :], 6, axis=-1)


def _split_in(z):
    idx = np.cumsum(SPLIT_SIZES)[:-1].tolist()
    return jnp.split(z, idx, axis=-1)


def _mixer_front(x, mod, lp):
    sh, sc = mod[0], mod[1]
    h = _rms(x, lp['g_mix']) * (1.0 + sc) + sh
    cq, ckv, krope, na_q, na_k, na_v, df_q, df_k, df_v = _split_in(h @ lp['w_in'])
    q_mla = _rms(_heads(_rms(cq, lp['g_qa']) @ lp['w_uq'], MLA_HEADS), lp['g_mla_q'])
    ckv_n = _rms(ckv, lp['g_kva'])
    q_na = _rms(_heads(na_q, NA_HEADS), lp['g_na_q'])
    k_na = _rms(_heads(na_k, NA_HEADS), lp['g_na_k'])
    v_na = _heads(na_v, NA_HEADS)
    q_df = _rms_pairs(_heads(df_q, DF_HEADS), lp['g_df_q'])
    k_df = _rms_pairs(_heads(df_k, DF_HEADS), lp['g_df_k'])
    v_df = _heads(df_v, DF_HEADS)
    return q_mla, ckv_n, krope, q_na, k_na, v_na, q_df, k_df, v_df


def _mla_keys(ckv_n, krope, lp):
    kv = _heads(ckv_n @ lp['w_ukv'], MLA_HEADS)
    b, h, l, _ = kv.shape
    kr = jnp.broadcast_to(krope[:, None], (b, h, l, MLA_ROPE))
    k = _rms(jnp.concatenate([kv[..., :MLA_NOPE], kr], axis=-1), lp['g_mla_k'])
    return k, kv[..., MLA_NOPE:]


def _finish(x, o_mla, o_na, o_df, mod, lp):
    mix = jnp.concatenate([_merge(o_mla), _merge(o_na), _merge(o_df)], axis=-1) @ lp['w_out']
    x = x + mod[2] * mix
    h = _rms(x, lp['g_ffn']) * (1.0 + mod[4]) + mod[3]
    ffn = (jax.nn.silu(h @ lp['w_gate']) * (h @ lp['w_up'])) @ lp['w_down']
    return x + mod[5] * ffn


def _context_layer(x, mod, lp, lam_init):
    q_mla, ckv_n, krope, q_na, k_na, v_na, q_df, k_df, v_df = _mixer_front(x, mod, lp)
    k_mla, v_mla = _mla_keys(ckv_n, krope, lp)
    o_mla = _softmax_attend(q_mla, k_mla, v_mla, MLA_QK ** -0.5)
    o_na = _softmax_attend(q_na, k_na, v_na, NA_HD ** -0.5)
    o_df = _diff_attend(q_df, k_df, v_df, lp, lam_init)
    return _finish(x, o_mla, o_na, o_df, mod, lp), (ckv_n, krope, k_na, v_na, k_df, v_df)


def _latent_layer(x, mod, lp, lam_init, ctx):
    ckv_c, krope_c, k_na_c, v_na_c, k_df_c, v_df_c = ctx
    L = x.shape[1]
    cos_m, sin_m = _axial_rope(L, MLA_ROPE)
    cos_d, sin_d = _axial_rope(L, DF_QK)
    q_mla, ckv_n, krope, q_na, k_na, v_na, q_df, k_df, v_df = _mixer_front(x, mod, lp)
    q_mla = _rope_tail(q_mla, cos_m, sin_m)
    k_lat, v_lat = _mla_keys(ckv_n, krope, lp)
    k_lat = _rope_tail(k_lat, cos_m, sin_m)
    k_ctx, v_ctx = _mla_keys(ckv_c, krope_c, lp)
    o_mla = _softmax_attend(q_mla, jnp.concatenate([k_ctx, k_lat], axis=2),
                            jnp.concatenate([v_ctx, v_lat], axis=2), MLA_QK ** -0.5)
    o_na = _neighbourhood_attend(q_na, k_na, v_na, k_na_c, v_na_c, lp['na_rpb'])
    q_df = _rope_pairs(q_df, cos_d, sin_d)
    k_df = _rope_pairs(k_df, cos_d, sin_d)
    o_df = _diff_attend(q_df, jnp.concatenate([k_df_c, k_df], axis=2),
                        jnp.concatenate([v_df_c, v_df], axis=2), lp, lam_init)
    return _finish(x, o_mla, o_na, o_df, mod, lp)


def setup_inputs(seed: int = 0) -> dict:
    key = jax.random.key(seed)
    ks = jax.random.split(key, 36)
    f32 = jnp.float32
    nrm = lambda k, shape, s: jax.random.normal(k, shape, f32) * s
    gain = lambda k, shape: 1.0 + 0.05 * jax.random.normal(k, shape, f32)
    return {
        'x_prompt': nrm(ks[0], (BATCH, SEQ, D_MODEL), 1.0),
        'x_sample': nrm(ks[1], (DEC_BATCH, DEC_SEQ, D_MODEL), 1.0),
        'cache_mla_ckv': nrm(ks[2], (DEC_BATCH, DEPTH, PAST_LEN, KV_LORA), 1.0),
        'cache_mla_krope': nrm(ks[3], (DEC_BATCH, DEPTH, PAST_LEN, MLA_ROPE), 1.0),
        'cache_na_k': nrm(ks[4], (DEC_BATCH, DEPTH, NA_HEADS, PAST_LEN, NA_HD), 1.0),
        'cache_na_v': nrm(ks[5], (DEC_BATCH, DEPTH, NA_HEADS, PAST_LEN, NA_HD), 1.0),
        'cache_df_k': nrm(ks[6], (DEC_BATCH, DEPTH, DF_HEADS, PAST_LEN, 2 * DF_QK), 1.0),
        'cache_df_v': nrm(ks[7], (DEC_BATCH, DEPTH, DF_HEADS, PAST_LEN, DF_HD), 1.0),
        'c': nrm(ks[8], (DEC_BATCH, D_MODEL), 1.0),
        'c_ctx': nrm(ks[9], (D_MODEL,), 1.0),
        'w_mod': nrm(ks[10], (DEPTH, D_MODEL, 6 * D_MODEL), 0.5 * D_MODEL ** -0.5),
        'b_mod': nrm(ks[11], (DEPTH, 6 * D_MODEL), 0.01),
        'g_mix': gain(ks[12], (DEPTH, D_MODEL)),
        'w_in': nrm(ks[13], (DEPTH, D_MODEL, IN_COLS), D_MODEL ** -0.5),
        'g_qa': gain(ks[14], (DEPTH, Q_LORA)),
        'w_uq': nrm(ks[15], (DEPTH, Q_LORA, MLA_HEADS * MLA_QK), Q_LORA ** -0.5),
        'g_kva': gain(ks[16], (DEPTH, KV_LORA)),
        'w_ukv': nrm(ks[17], (DEPTH, KV_LORA, MLA_HEADS * (MLA_NOPE + MLA_V)), KV_LORA ** -0.5),
        'g_mla_q': gain(ks[18], (DEPTH, MLA_QK)),
        'g_mla_k': gain(ks[19], (DEPTH, MLA_QK)),
        'g_na_q': gain(ks[20], (DEPTH, NA_HD)),
        'g_na_k': gain(ks[21], (DEPTH, NA_HD)),
        'na_rpb': nrm(ks[22], (DEPTH, NA_HEADS, 2 * NA_KR - 1, 2 * NA_KW - 1), 0.1),
        'g_df_q': gain(ks[23], (DEPTH, DF_QK)),
        'g_df_k': gain(ks[24], (DEPTH, DF_QK)),
        'df_lq1': nrm(ks[25], (DEPTH, DF_QK), 0.1),
        'df_lk1': nrm(ks[26], (DEPTH, DF_QK), 0.1),
        'df_lq2': nrm(ks[27], (DEPTH, DF_QK), 0.1),
        'df_lk2': nrm(ks[28], (DEPTH, DF_QK), 0.1),
        'g_df_sub': gain(ks[29], (DEPTH, DF_HD)),
        'w_out': nrm(ks[30], (DEPTH, MIX_W, D_MODEL), MIX_W ** -0.5),
        'g_ffn': gain(ks[31], (DEPTH, D_MODEL)),
        'w_gate': nrm(ks[32], (DEPTH, D_MODEL, D_FF), D_MODEL ** -0.5),
        'w_up': nrm(ks[33], (DEPTH, D_MODEL, D_FF), D_MODEL ** -0.5),
        'w_down': nrm(ks[34], (DEPTH, D_FF, D_MODEL), D_FF ** -0.5),
    }


def reference(x_prompt, x_sample, cache_mla_ckv, cache_mla_krope, cache_na_k, cache_na_v,
              cache_df_k, cache_df_v, c, c_ctx, w_mod, b_mod, g_mix, w_in, g_qa, w_uq, g_kva,
              w_ukv, g_mla_q, g_mla_k, g_na_q, g_na_k, na_rpb, g_df_q, g_df_k, df_lq1, df_lk1,
              df_lq2, df_lk2, g_df_sub, w_out, g_ffn, w_gate, w_up, w_down):
    xp = x_prompt
    xs = x_sample
    new = [[], [], [], [], [], []]
    for l in range(DEPTH):
        lp = {'g_mix': g_mix[l], 'w_in': w_in[l], 'g_qa': g_qa[l], 'w_uq': w_uq[l],
              'g_kva': g_kva[l], 'w_ukv': w_ukv[l], 'g_mla_q': g_mla_q[l], 'g_mla_k': g_mla_k[l],
              'g_na_q': g_na_q[l], 'g_na_k': g_na_k[l], 'na_rpb': na_rpb[l],
              'g_df_q': g_df_q[l], 'g_df_k': g_df_k[l], 'df_lq1': df_lq1[l], 'df_lk1': df_lk1[l],
              'df_lq2': df_lq2[l], 'df_lk2': df_lk2[l], 'g_df_sub': g_df_sub[l],
              'w_out': w_out[l], 'g_ffn': g_ffn[l], 'w_gate': w_gate[l], 'w_up': w_up[l],
              'w_down': w_down[l]}
        lam_init = 0.8 - 0.6 * math.exp(-0.3 * l)
        mod_ctx = _modulation(c_ctx[None, :], w_mod[l], b_mod[l])
        mod_lat = _modulation(c, w_mod[l], b_mod[l])
        xp, ctx_new = _context_layer(xp, mod_ctx, lp, lam_init)
        for lst, t in zip(new, ctx_new):
            lst.append(t)
        ctx_cached = (cache_mla_ckv[:, l], cache_mla_krope[:, l], cache_na_k[:, l],
                      cache_na_v[:, l], cache_df_k[:, l], cache_df_v[:, l])
        xs = _latent_layer(xs, mod_lat, lp, lam_init, ctx_cached)
    new_mla_ckv = jnp.stack(new[0], axis=1)
    new_mla_krope = jnp.stack(new[1], axis=1)
    new_na_k = jnp.stack(new[2], axis=1)
    new_na_v = jnp.stack(new[3], axis=1)
    new_df_k = jnp.stack(new[4], axis=1)
    new_df_v = jnp.stack(new[5], axis=1)
    return (xp, xs, new_mla_ckv, new_mla_krope, new_na_k, new_na_v, new_df_k, new_df_v)
```

```python
import functools
import math

import numpy as np
import jax
import jax.numpy as jnp
from jax import lax
from jax.experimental import pallas as pl
from jax.experimental.pallas import tpu as pltpu

F32 = jnp.float32
BF16 = jnp.bfloat16

D_MODEL = 1024
BATCH = 32
SEQ = 256
DEPTH = 2
DEC_BATCH = 2
DEC_SEQ = 1024
PAST_LEN = 256
GRID_W = 64
GRID_ROWS = DEC_SEQ // GRID_W
MLA_HEADS = 6
MLA_NOPE = 64
MLA_ROPE = 32
MLA_QK = MLA_NOPE + MLA_ROPE
MLA_V = 64
MLA_PAD = 128
Q_LORA = 256
KV_LORA = 128
NA_HEADS = 6
NA_HD = 64
NA_KR = 8
NA_KW = 16
DF_HEADS = 4
DF_HD = 64
DF_QK = 32
MLA_W = MLA_HEADS * MLA_V
NA_W = NA_HEADS * NA_HD
DF_W = DF_HEADS * DF_HD
D_FF = -(-8 * D_MODEL // (3 * 256)) * 256
ROPE_BASE = 10000.0
EPS = 1e-6
NEG = -1e30

LANES = 128
MXU_DIM = 256

C_CQ = 0
C_CKV = C_CQ + Q_LORA
C_NAQ = C_CKV + KV_LORA
C_NAK = C_NAQ + NA_W
C_NAV = C_NAK + NA_W
C_DFQ = C_NAV + NA_W
C_DFK = C_DFQ + DF_W
C_DFV = C_DFK + DF_W
C_KR = C_DFV + DF_W
IN_COLS_R = C_KR + LANES

QK_PACK = MLA_HEADS * MLA_PAD + NA_W + DF_W
V_PACK = MLA_W + NA_W + DF_W
P_MLA = 0
P_NA = MLA_HEADS * MLA_PAD
P_DF = P_NA + NA_W
PV_MLA = 0
PV_NA = MLA_W
PV_DF = MLA_W + NA_W

(V_GMIX, V_GFFN, V_GQA, V_GKVA, V_GMQ, V_GMK, V_GNQ, V_GNK, V_GDQ, V_GDK, V_GDS,
 V_LQ1, V_LK1, V_LQ2, V_LK2) = range(15)
N_VEC = 16

N_MOD = 8
TM = 256
VMEM_LIMIT = 56 * 1024 * 1024

NA_PAIR_TILES = 2 * NA_KR - 2


def _dot(a, b):
    return jnp.dot(a, b, preferred_element_type=F32)


def _dot_nt(a, b):
    return lax.dot_general(a, b, (((1,), (1,)), ((), ())), preferred_element_type=F32)


def _lane_iota(shape):
    return lax.broadcasted_iota(jnp.int32, shape, len(shape) - 1)


def _rms_rows(x, g):
    ms = jnp.mean(x * x, axis=-1, keepdims=True)
    return x * lax.rsqrt(ms + EPS) * g


def _seg_rms(x, bd_ref, g, n_real):
    width = x.shape[1]
    sq = (x * x).astype(BF16)
    parts = []
    for c0 in range(0, width, MXU_DIM):
        w = min(MXU_DIM, width - c0)
        parts.append(_dot(sq[:, c0:c0 + w], bd_ref[0:w, 0:w]))
    ss = parts[0] if len(parts) == 1 else jnp.concatenate(parts, axis=1)
    return x * lax.rsqrt(ss * (1.0 / n_real) + EPS) * g


def _rope_tiles(x, cos, sa, sb):
    outs = []
    for t in range(x.shape[1] // LANES):
        xt = x[:, t * LANES:(t + 1) * LANES]
        up = pltpu.roll(xt, LANES - MLA_ROPE // 4, 1)
        dn = pltpu.roll(xt, MLA_ROPE // 4, 1)
        outs.append(xt * cos + up * sa + dn * sb)
    return outs[0] if len(outs) == 1 else jnp.concatenate(outs, axis=1)


def _diff_lambda(vec_ref, lam_init):
    a = jnp.sum(vec_ref[V_LQ1:V_LQ1 + 1, 0:DF_QK] * vec_ref[V_LK1:V_LK1 + 1, 0:DF_QK], axis=-1, keepdims=True)
    b = jnp.sum(vec_ref[V_LQ2:V_LQ2 + 1, 0:DF_QK] * vec_ref[V_LK2:V_LK2 + 1, 0:DF_QK], axis=-1, keepdims=True)
    return jnp.exp(a) - jnp.exp(b) + lam_init


def _mixer_front(x, mod, vec_ref, w_in_ref, w_uq_ref, bd128_ref, bd64_ref, bd32_ref):
    sh = mod[:, 0:D_MODEL]
    sc = mod[:, D_MODEL:2 * D_MODEL]
    h = _rms_rows(x, vec_ref[V_GMIX:V_GMIX + 1, :]) * (1.0 + sc) + sh
    z = _dot(h.astype(BF16), w_in_ref[...])
    cqn = _rms_rows(z[:, C_CQ:C_CQ + Q_LORA], vec_ref[V_GQA:V_GQA + 1, 0:Q_LORA])
    q_raw = _dot(cqn.astype(BF16), w_uq_ref[...])
    w_mla = MLA_HEADS * MLA_PAD
    q_mla = _seg_rms(q_raw, bd128_ref, vec_ref[V_GMQ:V_GMQ + 1, 0:w_mla] * (MLA_QK ** -0.5), MLA_QK)
    ckv_n = _rms_rows(z[:, C_CKV:C_CKV + KV_LORA], vec_ref[V_GKVA:V_GKVA + 1, 0:KV_LORA])
    q_na = _seg_rms(z[:, C_NAQ:C_NAQ + NA_W], bd64_ref, vec_ref[V_GNQ:V_GNQ + 1, 0:NA_W] * (NA_HD ** -0.5), NA_HD)
    k_na = _seg_rms(z[:, C_NAK:C_NAK + NA_W], bd64_ref, vec_ref[V_GNK:V_GNK + 1, 0:NA_W], NA_HD)
    v_na = z[:, C_NAV:C_NAV + NA_W]
    q_df = _seg_rms(z[:, C_DFQ:C_DFQ + DF_W], bd32_ref, vec_ref[V_GDQ:V_GDQ + 1, 0:DF_W] * (DF_QK ** -0.5), DF_QK)
    k_df = _seg_rms(z[:, C_DFK:C_DFK + DF_W], bd32_ref, vec_ref[V_GDK:V_GDK + 1, 0:DF_W], DF_QK)
    v_df = z[:, C_DFV:C_DFV + DF_W]
    kr_tile = z[:, C_KR:C_KR + LANES]
    return q_mla, ckv_n, kr_tile, q_na, k_na, v_na, q_df, k_df, v_df


def _mla_kv(ckv_n, kr_tile, vec_ref, w_ukv_ref, bd128_ref):
    w_mla = MLA_HEADS * MLA_PAD
    kv = _dot(ckv_n.astype(BF16), w_ukv_ref[...])
    lane = _lane_iota((1, LANES))
    kr = jnp.where((lane >= MLA_NOPE) & (lane < MLA_QK), kr_tile, 0.0)
    k_pre = kv[:, 0:w_mla] + jnp.concatenate([kr] * MLA_HEADS, axis=1)
    k = _seg_rms(k_pre, bd128_ref, vec_ref[V_GMK:V_GMK + 1, 0:w_mla], MLA_QK)
    return k, kv[:, w_mla:w_mla + MLA_W]


def _softmax_parts(s):
    m = jnp.max(s, axis=-1, keepdims=True)
    p = jnp.exp(s - m)
    return p, 1.0 / jnp.sum(p, axis=-1, keepdims=True)


def _attend_pair(score_fn_e, score_fn_o, pv_fn):
    pe, ile = _softmax_parts(score_fn_e())
    oe = pv_fn(pe.astype(BF16)) * ile
    po, ilo = _softmax_parts(score_fn_o())
    oo = pv_fn(po.astype(BF16)) * ilo
    lane = _lane_iota((1, LANES))
    return jnp.where(lane < NA_HD, oe, oo)


def _scores(q, k_segs, bias=None):
    parts = [_dot_nt(q, k) for k in k_segs]
    if bias is not None:
        parts[0] = parts[0] + bias
    return parts[0] if len(parts) == 1 else jnp.concatenate(parts, axis=1)


def _pv(p, v_segs):
    out = None
    c0 = 0
    for v in v_segs:
        n = v.shape[0]
        o = _dot(p[:, c0:c0 + n], v)
        out = o if out is None else out + o
        c0 += n
    return out


def _mla_attend(q, k_segs, v_segs):
    outs = []
    for t in range(MLA_HEADS // 2):
        vt = [v[:, t * LANES:(t + 1) * LANES] for v in v_segs]

        def sc(h):
            return lambda: _scores(q[:, h * MLA_PAD:(h + 1) * MLA_PAD],
                                   [k[:, h * MLA_PAD:(h + 1) * MLA_PAD] for k in k_segs])

        outs.append(_attend_pair(sc(2 * t), sc(2 * t + 1), lambda p: _pv(p, vt)))
    return jnp.concatenate(outs, axis=1)


def _na_attend_full(q, k_segs, v_segs):
    outs = []
    lane = _lane_iota((1, LANES))
    for t in range(NA_HEADS // 2):
        qt = q[:, t * LANES:(t + 1) * LANES]
        kt = [k[:, t * LANES:(t + 1) * LANES] for k in k_segs]
        vt = [v[:, t * LANES:(t + 1) * LANES] for v in v_segs]
        qe = jnp.where(lane < NA_HD, qt, jnp.zeros_like(qt))
        qo = jnp.where(lane >= NA_HD, qt, jnp.zeros_like(qt))
        outs.append(_attend_pair(lambda: _scores(qe, kt), lambda: _scores(qo, kt), lambda p: _pv(p, vt)))
    return jnp.concatenate(outs, axis=1)


def _df_attend(q, k_segs, v_segs, lam, g_sub, out_scale):
    outs = []
    lane = _lane_iota((1, LANES))
    for t in range(DF_HEADS // 2):
        qt = q[:, t * LANES:(t + 1) * LANES]
        kt = [k[:, t * LANES:(t + 1) * LANES] for k in k_segs]
        vt = [v[:, t * LANES:(t + 1) * LANES] for v in v_segs]
        halves = []
        for hh in range(2):
            ps = []
            for comp in range(2):
                g = 2 * hh + comp
                qg = jnp.where((lane >= g * DF_QK) & (lane < (g + 1) * DF_QK), qt, jnp.zeros_like(qt))
                ps.append(_softmax_parts(_scores(qg, kt)))
            (p1, il1), (p2, il2) = ps
            p = p1 * il1 - p2 * (lam * il2)
            halves.append(_pv(p.astype(BF16), vt))
        o = jnp.where(lane < DF_HD, halves[0], halves[1])
        o2 = o * o
        ms_e = jnp.sum(jnp.where(lane < DF_HD, o2, 0.0), axis=-1, keepdims=True)
        ms_o = jnp.sum(jnp.where(lane >= DF_HD, o2, 0.0), axis=-1, keepdims=True)
        r = lax.rsqrt(jnp.where(lane < DF_HD, ms_e, ms_o) * (1.0 / DF_HD) + EPS)
        outs.append(o * r * (g_sub[:, t * LANES:(t + 1) * LANES] * out_scale))
    return jnp.concatenate(outs, axis=1)


def _mod_kernel(c_ref, w_ref, b_ref, o_ref):
    c = c_ref[...]
    s = c * jax.nn.sigmoid(c)
    o_ref[...] = _dot(s.astype(BF16), w_ref[...].astype(BF16)) + b_ref[...]


def _ctx_kernel(lam_init, x_ref, mod_ref, vec_ref, w_in_ref, w_uq_ref, w_ukv_ref, bd128_ref, bd64_ref, bd32_ref,
                mix_ref, ckv_ref, kr_ref, nak_ref, nav_ref, dfk_ref, dfv_ref):
    q_mla, ckv_n, kr_tile, q_na, k_na, v_na, q_df, k_df, v_df = _mixer_front(
        x_ref[...], mod_ref[...], vec_ref, w_in_ref, w_uq_ref, bd128_ref, bd64_ref, bd32_ref)
    ckv_ref[...] = ckv_n
    kr_ref[...] = kr_tile[:, 0:MLA_ROPE]
    for h in range(NA_HEADS):
        nak_ref[h] = k_na[:, h * NA_HD:(h + 1) * NA_HD]
        nav_ref[h] = v_na[:, h * NA_HD:(h + 1) * NA_HD]
    for h in range(DF_HEADS):
        dfk_ref[h] = k_df[:, h * DF_HD:(h + 1) * DF_HD]
        dfv_ref[h] = v_df[:, h * DF_HD:(h + 1) * DF_HD]
    k_mla, v_mla = _mla_kv(ckv_n, kr_tile, vec_ref, w_ukv_ref, bd128_ref)
    o_mla = _mla_attend(q_mla.astype(BF16), [k_mla.astype(BF16)], [v_mla.astype(BF16)])
    o_na = _na_attend_full(q_na.astype(BF16), [k_na.astype(BF16)], [v_na.astype(BF16)])
    lam = _diff_lambda(vec_ref, lam_init)
    o_df = _df_attend(q_df.astype(BF16), [k_df.astype(BF16)], [v_df.astype(BF16)], lam,
                      vec_ref[V_GDS:V_GDS + 1, 0:DF_W], 1.0 - lam_init)
    mix_ref[...] = jnp.concatenate([o_mla, o_na, o_df], axis=1).astype(BF16)


def _finish_kernel(x_ref, mix_ref, mod_ref, vec_ref, w_out_ref, w_gate_ref, w_up_ref, w_down_ref, y_ref):
    mod = mod_ref[...]
    gate_m = mod[:, 2 * D_MODEL:3 * D_MODEL]
    sh = mod[:, 3 * D_MODEL:4 * D_MODEL]
    sc = mod[:, 4 * D_MODEL:5 * D_MODEL]
    gate_f = mod[:, 5 * D_MODEL:6 * D_MODEL]
    x1 = x_ref[...] + gate_m * _dot(mix_ref[...], w_out_ref[...])
    h = (_rms_rows(x1, vec_ref[V_GFFN:V_GFFN + 1, :]) * (1.0 + sc) + sh).astype(BF16)
    g = _dot(h, w_gate_ref[...])
    u = _dot(h, w_up_ref[...])
    a = (g * jax.nn.sigmoid(g) * u).astype(BF16)
    y_ref[...] = x1 + gate_f * _dot(a, w_down_ref[...])


def _lat_front_kernel(x_ref, mod_ref, vec_ref, rope_ref, w_in_ref, w_uq_ref, w_ukv_ref, bd128_ref, bd64_ref, bd32_ref,
                      q_ref, k_ref, v_ref):
    q_mla, ckv_n, kr_tile, q_na, k_na, v_na, q_df, k_df, v_df = _mixer_front(
        x_ref[...], mod_ref[...], vec_ref, w_in_ref, w_uq_ref, bd128_ref, bd64_ref, bd32_ref)
    k_mla, v_mla = _mla_kv(ckv_n, kr_tile, vec_ref, w_ukv_ref, bd128_ref)
    cm, sam, sbm = rope_ref[0], rope_ref[1], rope_ref[2]
    cd, sad, sbd = rope_ref[3], rope_ref[4], rope_ref[5]
    q_ref[...] = jnp.concatenate(
        [_rope_tiles(q_mla, cm, sam, sbm), q_na, _rope_tiles(q_df, cd, sad, sbd)], axis=1).astype(BF16)
    k_ref[...] = jnp.concatenate(
        [_rope_tiles(k_mla, cm, sam, sbm), k_na, _rope_tiles(k_df, cd, sad, sbd)], axis=1).astype(BF16)
    v_ref[...] = jnp.concatenate([v_mla, v_na, v_df], axis=1).astype(BF16)


def _ctx_kv_kernel(ckv_ref, kr_ref, nak_ref, nav_ref, dfk_ref, dfv_ref, vec_ref, w_ukv_ref, bd128_ref, k_ref, v_ref):
    k_mla, v_mla = _mla_kv(ckv_ref[...], kr_ref[...], vec_ref, w_ukv_ref, bd128_ref)
    k_ref[...] = jnp.concatenate([k_mla, nak_ref[...], dfk_ref[...]], axis=1).astype(BF16)
    v_ref[...] = jnp.concatenate([v_mla, nav_ref[...], dfv_ref[...]], axis=1).astype(BF16)


def _na_lat_kernel(q_ref, kl_ref, vl_ref, kc_ref, vc_ref, g_ref, o_ref, bias_ref):
    b = pl.program_id(0)
    r = pl.program_id(1)

    @pl.when((b == 0) & (r == 0))
    def _build_bias():
        c = lax.broadcasted_iota(jnp.int32, (GRID_W, LANES), 0)
        kc = _lane_iota((GRID_W, LANES)) % GRID_W
        start = jnp.clip(c - NA_KW // 2, 0, GRID_W - NA_KW)
        in_win = (kc >= start) & (kc < start + NA_KW)

        def body(i, carry):
            row = jnp.broadcast_to(g_ref[pl.ds(i, 1), :], (GRID_W, LANES))
            toep = pltpu.roll(row, 0, 1, stride=1, stride_axis=0)
            bias_ref[i] = jnp.where(in_win, toep, NEG)
            return carry

        lax.fori_loop(0, NA_HEADS * NA_PAIR_TILES, body, 0)

    rs = jnp.clip(r - NA_KR // 2, 0, GRID_ROWS - NA_KR)
    idx0 = rs - r + (NA_KR - 1)
    row0 = pl.multiple_of(rs * GRID_W, GRID_W)
    n_win = NA_KR * GRID_W
    lane = _lane_iota((1, LANES))
    outs = []
    for t in range(NA_HEADS // 2):
        qt = q_ref[:, t * LANES:(t + 1) * LANES]
        kw = kl_ref[pl.ds(row0, n_win), t * LANES:(t + 1) * LANES]
        vw = vl_ref[pl.ds(row0, n_win), t * LANES:(t + 1) * LANES]
        kc = kc_ref[:, t * LANES:(t + 1) * LANES]
        vc = vc_ref[:, t * LANES:(t + 1) * LANES]
        qe = jnp.where(lane < NA_HD, qt, jnp.zeros_like(qt))
        qo = jnp.where(lane >= NA_HD, qt, jnp.zeros_like(qt))

        def bias(h):
            return jnp.concatenate(
                [bias_ref[h * NA_PAIR_TILES + idx0 + 2 * m] for m in range(NA_KR // 2)], axis=1)

        outs.append(_attend_pair(
            lambda: _scores(qe, [kw, kc], bias(2 * t)),
            lambda: _scores(qo, [kw, kc], bias(2 * t + 1)),
            lambda p: _pv(p, [vw, vc])))
    o_ref[...] = jnp.concatenate(outs, axis=1).astype(BF16)


def _lat_attn_kernel(lam_init, q_ref, kl_ref, vl_ref, kc_ref, vc_ref, ona_ref, vec_ref, mix_ref):
    w_mla = MLA_HEADS * MLA_PAD
    o_mla = _mla_attend(q_ref[:, P_MLA:P_MLA + w_mla],
                        [kc_ref[:, P_MLA:P_MLA + w_mla], kl_ref[:, P_MLA:P_MLA + w_mla]],
                        [vc_ref[:, PV_MLA:PV_MLA + MLA_W], vl_ref[:, PV_MLA:PV_MLA + MLA_W]])
    lam = _diff_lambda(vec_ref, lam_init)
    o_df = _df_attend(q_ref[:, P_DF:P_DF + DF_W],
                      [kc_ref[:, P_DF:P_DF + DF_W], kl_ref[:, P_DF:P_DF + DF_W]],
                      [vc_ref[:, PV_DF:PV_DF + DF_W], vl_ref[:, PV_DF:PV_DF + DF_W]],
                      lam, vec_ref[V_GDS:V_GDS + 1, 0:DF_W], 1.0 - lam_init)
    mix_ref[...] = jnp.concatenate([o_mla.astype(BF16), ona_ref[...], o_df.astype(BF16)], axis=1)


def _const_spec(shape):
    nd = len(shape)
    return pl.BlockSpec(shape, lambda *_: (0,) * nd, pipeline_mode=pl.Buffered(1))


def _params(n_axes):
    return pltpu.CompilerParams(dimension_semantics=("arbitrary",) * n_axes, vmem_limit_bytes=VMEM_LIMIT)


def _block_diag(group):
    i = np.arange(MXU_DIM) // group
    return jnp.asarray((i[:, None] == i[None, :]).astype(np.float32), dtype=BF16)


def _rope_tables():
    t = np.arange(DEC_SEQ)
    row = (t // GRID_W).astype(np.float64)
    col = (t % GRID_W).astype(np.float64)
    n = MLA_ROPE // 4
    inv = 1.0 / (ROPE_BASE ** (np.arange(n, dtype=np.float64) * 2.0 / (MLA_ROPE // 2)))
    ar = row[:, None] * inv
    ac = col[:, None] * inv
    ang = np.concatenate([ar, ar, ac, ac], axis=-1)
    cos32, sin32 = np.cos(ang), np.sin(ang)
    first = (np.arange(MLA_ROPE) % (2 * n)) < n
    sa32 = np.where(first, -sin32, 0.0)
    sb32 = np.where(first, 0.0, sin32)

    def mla_tile(v32, fill):
        out = np.full((DEC_SEQ, LANES), fill)
        out[:, MLA_NOPE:MLA_QK] = v32
        return out

    tabs = [mla_tile(cos32, 1.0), mla_tile(sa32, 0.0), mla_tile(sb32, 0.0),
            np.tile(cos32, (1, LANES // DF_QK)), np.tile(sa32, (1, LANES // DF_QK)), np.tile(sb32, (1, LANES // DF_QK))]
    return jnp.asarray(np.stack(tabs).astype(np.float32))


def _pad_lanes(v, width=D_MODEL):
    return jnp.pad(v, (0, width - v.shape[0]))


def _layer_tables(l, p):
    w_in = p['w_in'][l]
    kr = w_in[:, Q_LORA + KV_LORA:Q_LORA + KV_LORA + MLA_ROPE]
    z32 = jnp.zeros_like(kr)
    w_in_r = jnp.concatenate([w_in[:, :Q_LORA + KV_LORA], w_in[:, Q_LORA + KV_LORA + MLA_ROPE:], kr, z32, kr, z32],
                             axis=1).astype(BF16)
    w_uq = p['w_uq'][l].reshape(Q_LORA, MLA_HEADS, MLA_QK)
    w_uq_p = jnp.pad(w_uq, ((0, 0), (0, 0), (0, MLA_PAD - MLA_QK))).reshape(Q_LORA, MLA_HEADS * MLA_PAD).astype(BF16)
    w_ukv = p['w_ukv'][l].reshape(KV_LORA, MLA_HEADS, MLA_NOPE + MLA_V)
    wk = jnp.pad(w_ukv[:, :, :MLA_NOPE], ((0, 0), (0, 0), (0, MLA_PAD - MLA_NOPE))).reshape(KV_LORA, MLA_HEADS * MLA_PAD)
    wv = w_ukv[:, :, MLA_NOPE:].reshape(KV_LORA, MLA_W)
    w_ukv_r = jnp.concatenate([wk, wv], axis=1).astype(BF16)
    pad_head = lambda g: jnp.tile(jnp.pad(g, (0, MLA_PAD - MLA_QK)), MLA_HEADS)
    rows = [p['g_mix'][l], p['g_ffn'][l], p['g_qa'][l], p['g_kva'][l], pad_head(p['g_mla_q'][l]), pad_head(p['g_mla_k'][l]),
            jnp.tile(p['g_na_q'][l], NA_HEADS), jnp.tile(p['g_na_k'][l], NA_HEADS),
            jnp.tile(p['g_df_q'][l], 2 * DF_HEADS), jnp.tile(p['g_df_k'][l], 2 * DF_HEADS),
            jnp.tile(p['g_df_sub'][l], DF_HEADS),
            p['df_lq1'][l], p['df_lk1'][l], p['df_lq2'][l], p['df_lk2'][l]]
    vec = jnp.stack([_pad_lanes(r) for r in rows] + [jnp.zeros((D_MODEL,), F32)] * (N_VEC - len(rows)))
    f = p['na_rpb'][l]
    zpad = jnp.zeros((NA_HEADS, NA_PAIR_TILES, 33), F32)
    g_rows = jnp.concatenate([f[:, :-1, NA_KW - 1:], zpad, f[:, 1:, :], zpad, f[:, :-1, :NA_KW - 1]], axis=-1)
    g_rows = g_rows.reshape(NA_HEADS * NA_PAIR_TILES, LANES)
    return dict(w_in=w_in_r, w_uq=w_uq_p, w_ukv=w_ukv_r, vec=vec, g_rows=g_rows,
                w_out=p['w_out'][l].astype(BF16), w_gate=p['w_gate'][l].astype(BF16),
                w_up=p['w_up'][l].astype(BF16), w_down=p['w_down'][l].astype(BF16))


def _modulation(c_all, w_mod, b_mod):
    tn = 1024
    return pl.pallas_call(
        _mod_kernel,
        grid=(DEPTH, 6 * D_MODEL // tn),
        in_specs=[pl.BlockSpec((N_MOD, D_MODEL), lambda l, j: (0, 0)),
                  pl.BlockSpec((None, D_MODEL, tn), lambda l, j: (l, 0, j)),
                  pl.BlockSpec((None, 1, tn), lambda l, j: (l, 0, j))],
        out_specs=pl.BlockSpec((None, N_MOD, tn), lambda l, j: (l, 0, j)),
        out_shape=jax.ShapeDtypeStruct((DEPTH, N_MOD, 6 * D_MODEL), F32),
        compiler_params=_params(2),
        name="modulation",
    )(c_all, w_mod, b_mod.reshape(DEPTH, 1, 6 * D_MODEL))


def _context_mixer(l, lam_init, x, mods, t, consts):
    n_tok = BATCH * SEQ
    mod_spec = pl.BlockSpec((None, 1, 6 * D_MODEL), lambda b: (l * N_MOD, 0, 0))
    tok = lambda w: pl.BlockSpec((SEQ, w), lambda b: (b, 0))
    head = lambda h, w: pl.BlockSpec((None, h, SEQ, w), lambda b: (b, 0, 0, 0))
    return pl.pallas_call(
        functools.partial(_ctx_kernel, lam_init),
        grid=(BATCH,),
        in_specs=[tok(D_MODEL), mod_spec, _const_spec((N_VEC, D_MODEL)), _const_spec((D_MODEL, IN_COLS_R)),
                  _const_spec((Q_LORA, MLA_HEADS * MLA_PAD)), _const_spec((KV_LORA, MLA_HEADS * MLA_PAD + MLA_W)),
                  _const_spec((MXU_DIM, MXU_DIM)), _const_spec((MXU_DIM, MXU_DIM)), _const_spec((MXU_DIM, MXU_DIM))],
        out_specs=[tok(D_MODEL),
                   pl.BlockSpec((None, SEQ, KV_LORA), lambda b: (b, 0, 0)),
                   pl.BlockSpec((None, SEQ, MLA_ROPE), lambda b: (b, 0, 0)),
                   head(NA_HEADS, NA_HD), head(NA_HEADS, NA_HD), head(DF_HEADS, DF_HD), head(DF_HEADS, DF_HD)],
        out_shape=[jax.ShapeDtypeStruct((n_tok, D_MODEL), BF16),
                   jax.ShapeDtypeStruct((BATCH, SEQ, KV_LORA), F32),
                   jax.ShapeDtypeStruct((BATCH, SEQ, MLA_ROPE), F32),
                   jax.ShapeDtypeStruct((BATCH, NA_HEADS, SEQ, NA_HD), F32),
                   jax.ShapeDtypeStruct((BATCH, NA_HEADS, SEQ, NA_HD), F32),
                   jax.ShapeDtypeStruct((BATCH, DF_HEADS, SEQ, DF_HD), F32),
                   jax.ShapeDtypeStruct((BATCH, DF_HEADS, SEQ, DF_HD), F32)],
        compiler_params=_params(1),
        name="context_mixer",
    )(x, mods, t['vec'], t['w_in'], t['w_uq'], t['w_ukv'], consts['bd128'], consts['bd64'], consts['bd32'])


def _finish(l, x, mix, mods, t, mod_row):
    n_tok = x.shape[0]
    tok = pl.BlockSpec((TM, D_MODEL), lambda i: (i, 0))
    return pl.pallas_call(
        _finish_kernel,
        grid=(n_tok // TM,),
        in_specs=[tok, tok, pl.BlockSpec((None, 1, 6 * D_MODEL), lambda i: (l * N_MOD + mod_row(i), 0, 0)),
                  _const_spec((N_VEC, D_MODEL)), _const_spec((D_MODEL, D_MODEL)), _const_spec((D_MODEL, D_FF)),
                  _const_spec((D_MODEL, D_FF)), _const_spec((D_FF, D_MODEL))],
        out_specs=tok,
        out_shape=jax.ShapeDtypeStruct((n_tok, D_MODEL), F32),
        compiler_params=_params(1),
        name="finish",
    )(x, mix, mods, t['vec'], t['w_out'], t['w_gate'], t['w_up'], t['w_down'])


def _latent_front(l, x, mods, t, consts):
    n_tok = DEC_BATCH * DEC_SEQ
    blocks_per_seq = DEC_SEQ // TM
    tok = lambda w: pl.BlockSpec((TM, w), lambda i: (i, 0))
    return pl.pallas_call(
        _lat_front_kernel,
        grid=(n_tok // TM,),
        in_specs=[tok(D_MODEL),
                  pl.BlockSpec((None, 1, 6 * D_MODEL), lambda i: (l * N_MOD + 1 + i // blocks_per_seq, 0, 0)),
                  _const_spec((N_VEC, D_MODEL)),
                  pl.BlockSpec((6, TM, LANES), lambda i: (0, i % blocks_per_seq, 0)),
                  _const_spec((D_MODEL, IN_COLS_R)), _const_spec((Q_LORA, MLA_HEADS * MLA_PAD)),
                  _const_spec((KV_LORA, MLA_HEADS * MLA_PAD + MLA_W)),
                  _const_spec((MXU_DIM, MXU_DIM)), _const_spec((MXU_DIM, MXU_DIM)), _const_spec((MXU_DIM, MXU_DIM))],
        out_specs=[tok(QK_PACK), tok(QK_PACK), tok(V_PACK)],
        out_shape=[jax.ShapeDtypeStruct((n_tok, QK_PACK), BF16), jax.ShapeDtypeStruct((n_tok, QK_PACK), BF16),
                   jax.ShapeDtypeStruct((n_tok, V_PACK), BF16)],
        compiler_params=_params(1),
        name="latent_front",
    )(x, mods, t['vec'], consts['rope'], t['w_in'], t['w_uq'], t['w_ukv'], consts['bd128'], consts['bd64'], consts['bd32'])


def _context_kv(l, caches, t, consts):
    lay = lambda w: pl.BlockSpec((None, None, PAST_LEN, w), lambda b: (b, l, 0, 0))
    out = lambda w: pl.BlockSpec((None, PAST_LEN, w), lambda b: (b, 0, 0))
    return pl.pallas_call(
        _ctx_kv_kernel,
        grid=(DEC_BATCH,),
        in_specs=[lay(KV_LORA), lay(LANES), lay(NA_W), lay(NA_W), lay(DF_W), lay(DF_W),
                  _const_spec((N_VEC, D_MODEL)), _const_spec((KV_LORA, MLA_HEADS * MLA_PAD + MLA_W)),
                  _const_spec((MXU_DIM, MXU_DIM))],
        out_specs=[out(QK_PACK), out(V_PACK)],
        out_shape=[jax.ShapeDtypeStruct((DEC_BATCH, PAST_LEN, QK_PACK), BF16),
                   jax.ShapeDtypeStruct((DEC_BATCH, PAST_LEN, V_PACK), BF16)],
        compiler_params=_params(1),
        name="context_kv",
    )(*caches, t['vec'], t['w_ukv'], consts['bd128'])


def _latent_na(q, k, v, kc, vc, g_rows):
    n_tok = DEC_BATCH * DEC_SEQ
    na_blk = P_NA // NA_W
    return pl.pallas_call(
        _na_lat_kernel,
        grid=(DEC_BATCH, GRID_ROWS),
        in_specs=[pl.BlockSpec((GRID_W, NA_W), lambda b, r: (b * GRID_ROWS + r, na_blk)),
                  pl.BlockSpec((DEC_SEQ, NA_W), lambda b, r: (b, na_blk)),
                  pl.BlockSpec((DEC_SEQ, NA_W), lambda b, r: (b, PV_NA // NA_W)),
                  pl.BlockSpec((None, PAST_LEN, NA_W), lambda b, r: (b, 0, na_blk)),
                  pl.BlockSpec((None, PAST_LEN, NA_W), lambda b, r: (b, 0, PV_NA // NA_W)),
                  _const_spec((NA_HEADS * NA_PAIR_TILES, LANES))],
        out_specs=pl.BlockSpec((GRID_W, NA_W), lambda b, r: (b * GRID_ROWS + r, 0)),
        out_shape=jax.ShapeDtypeStruct((n_tok, NA_W), BF16),
        scratch_shapes=[pltpu.VMEM((NA_HEADS * NA_PAIR_TILES, GRID_W, LANES), F32)],
        compiler_params=_params(2),
        name="latent_neighbourhood",
    )(q, k, v, kc, vc, g_rows)


def _latent_attention(lam_init, q, k, v, kc, vc, o_na, t):
    n_tok = DEC_BATCH * DEC_SEQ
    nq = DEC_SEQ // TM
    return pl.pallas_call(
        functools.partial(_lat_attn_kernel, lam_init),
        grid=(DEC_BATCH, nq),
        in_specs=[pl.BlockSpec((TM, QK_PACK), lambda b, j: (b * nq + j, 0)),
                  pl.BlockSpec((DEC_SEQ, QK_PACK), lambda b, j: (b, 0)),
                  pl.BlockSpec((DEC_SEQ, V_PACK), lambda b, j: (b, 0)),
                  pl.BlockSpec((None, PAST_LEN, QK_PACK), lambda b, j: (b, 0, 0)),
                  pl.BlockSpec((None, PAST_LEN, V_PACK), lambda b, j: (b, 0, 0)),
                  pl.BlockSpec((TM, NA_W), lambda b, j: (b * nq + j, 0)),
                  _const_spec((N_VEC, D_MODEL))],
        out_specs=pl.BlockSpec((TM, D_MODEL), lambda b, j: (b * nq + j, 0)),
        out_shape=jax.ShapeDtypeStruct((n_tok, D_MODEL), BF16),
        compiler_params=_params(2),
        name="latent_attention",
    )(q, k, v, kc, vc, o_na, t['vec'])


def kernel(x_prompt, x_sample, cache_mla_ckv, cache_mla_krope, cache_na_k, cache_na_v, cache_df_k, cache_df_v, c, c_ctx, w_mod, b_mod, g_mix, w_in, g_qa, w_uq, g_kva, w_ukv, g_mla_q, g_mla_k, g_na_q, g_na_k, na_rpb, g_df_q, g_df_k, df_lq1, df_lk1, df_lq2, df_lk2, g_df_sub, w_out, g_ffn, w_gate, w_up, w_down):
    p = dict(g_mix=g_mix, w_in=w_in, g_qa=g_qa, w_uq=w_uq, g_kva=g_kva, w_ukv=w_ukv, g_mla_q=g_mla_q, g_mla_k=g_mla_k,
             g_na_q=g_na_q, g_na_k=g_na_k, na_rpb=na_rpb, g_df_q=g_df_q, g_df_k=g_df_k, df_lq1=df_lq1, df_lk1=df_lk1,
             df_lq2=df_lq2, df_lk2=df_lk2, g_df_sub=g_df_sub, w_out=w_out, g_ffn=g_ffn, w_gate=w_gate, w_up=w_up,
             w_down=w_down)
    consts = dict(bd128=_block_diag(MLA_PAD), bd64=_block_diag(NA_HD), bd32=_block_diag(DF_QK), rope=_rope_tables())

    c_all = jnp.concatenate([c_ctx[None, :], c, jnp.zeros((N_MOD - 1 - DEC_BATCH, D_MODEL), F32)], axis=0)
    mods = _modulation(c_all, w_mod, b_mod).reshape(DEPTH * N_MOD, 1, 6 * D_MODEL)

    tokmajor = lambda a: a.transpose(0, 1, 3, 2, 4).reshape(DEC_BATCH, DEPTH, PAST_LEN, -1)
    caches = (cache_mla_ckv, jnp.pad(cache_mla_krope, ((0, 0), (0, 0), (0, 0), (MLA_NOPE, LANES - MLA_QK))),
              tokmajor(cache_na_k), tokmajor(cache_na_v), tokmajor(cache_df_k), tokmajor(cache_df_v))

    xp = x_prompt.reshape(BATCH * SEQ, D_MODEL)
    xs = x_sample.reshape(DEC_BATCH * DEC_SEQ, D_MODEL)
    new = [[] for _ in range(6)]
    for l in range(DEPTH):
        t = _layer_tables(l, p)
        lam_init = 0.8 - 0.6 * math.exp(-0.3 * l)
        mix, *ctx_new = _context_mixer(l, lam_init, xp, mods, t, consts)
        for lst, a in zip(new, ctx_new):
            lst.append(a)
        xp = _finish(l, xp, mix, mods, t, lambda i: 0)
        q, k, v = _latent_front(l, xs, mods, t, consts)
        kc, vc = _context_kv(l, caches, t, consts)
        o_na = _latent_na(q, k, v, kc, vc, t['g_rows'])
        mix_s = _latent_attention(lam_init, q, k, v, kc, vc, o_na, t)
        xs = _finish(l, xs, mix_s, mods, t, lambda i: 1 + i // (DEC_SEQ // TM))
    outs = [jnp.stack(a, axis=1) for a in new]
    return (xp.reshape(BATCH, SEQ, D_MODEL), xs.reshape(DEC_BATCH, DEC_SEQ, D_MODEL), *outs)
```

```python
import functools
import math

import numpy as np
import jax
import jax.numpy as jnp
from jax import lax
from jax.experimental import pallas as pl
from jax.experimental.pallas import tpu as pltpu

F32 = jnp.float32
BF16 = jnp.bfloat16

D_MODEL = 1024
BATCH = 32
SEQ = 256
DEPTH = 2
DEC_BATCH = 2
DEC_SEQ = 1024
PAST_LEN = 256
GRID_W = 64
GRID_ROWS = DEC_SEQ // GRID_W
MLA_HEADS = 6
MLA_NOPE = 64
MLA_ROPE = 32
MLA_QK = MLA_NOPE + MLA_ROPE
MLA_V = 64
MLA_PAD = 128
Q_LORA = 256
KV_LORA = 128
NA_HEADS = 6
NA_HD = 64
NA_KR = 8
NA_KW = 16
DF_HEADS = 4
DF_HD = 64
DF_QK = 32
MLA_W = MLA_HEADS * MLA_V
NA_W = NA_HEADS * NA_HD
DF_W = DF_HEADS * DF_HD
D_FF = -(-8 * D_MODEL // (3 * 256)) * 256
ROPE_BASE = 10000.0
EPS = 1e-6
NEG = -1e30
LOG2E = math.log2(math.e)

LANES = 128
MXU_DIM = 256

C_CQ = 0
C_CKV = C_CQ + Q_LORA
C_NAQ = C_CKV + KV_LORA
C_NAK = C_NAQ + NA_W
C_NAV = C_NAK + NA_W
C_DFQ = C_NAV + NA_W
C_DFK = C_DFQ + DF_W
C_DFV = C_DFK + DF_W
C_KR = C_DFV + DF_W
IN_COLS_R = C_KR + LANES

W_MLA_P = MLA_HEADS * MLA_PAD
QK_PACK = W_MLA_P + NA_W + DF_W
V_PACK = MLA_W + NA_W + DF_W
P_MLA = 0
P_NA = W_MLA_P
P_DF = P_NA + NA_W
PV_MLA = 0
PV_NA = MLA_W
PV_DF = MLA_W + NA_W

(V_GMIX, V_GFFN, V_GQA, V_GKVA, V_GMQ, V_GMK, V_GNQ, V_GNK, V_GDQ, V_GDK, V_GDS,
 V_LQ1, V_LK1, V_LQ2, V_LK2) = range(15)
N_VEC = 16

N_MOD = 8
TM = 256
NA_ROWS_PER_STEP = 4
VMEM_LIMIT = 56 * 1024 * 1024

NA_PAIR_TILES = 2 * NA_KR - 2
N_CACHE = 6


def _dot(a, b):
    return jnp.dot(a, b, preferred_element_type=F32)


def _dot_nt(a, b):
    return lax.dot_general(a, b, (((1,), (1,)), ((), ())), preferred_element_type=F32)


def _lane_iota(shape):
    return lax.broadcasted_iota(jnp.int32, shape, len(shape) - 1)


def _rms_rows(x, g):
    ms = jnp.mean(x * x, axis=-1, keepdims=True)
    return x * lax.rsqrt(ms + EPS) * g


def _seg_rms(x, bd_ref, g, n_real):
    width = x.shape[1]
    sq = (x * x).astype(BF16)
    parts = []
    for c0 in range(0, width, MXU_DIM):
        w = min(MXU_DIM, width - c0)
        parts.append(_dot(sq[:, c0:c0 + w], bd_ref[0:w, 0:w]))
    ss = parts[0] if len(parts) == 1 else jnp.concatenate(parts, axis=1)
    return x * lax.rsqrt(ss * (1.0 / n_real) + EPS) * g


def _rope_tiles(x, cos, sa, sb):
    outs = []
    for t in range(x.shape[1] // LANES):
        xt = x[:, t * LANES:(t + 1) * LANES]
        up = pltpu.roll(xt, LANES - MLA_ROPE // 4, 1)
        dn = pltpu.roll(xt, MLA_ROPE // 4, 1)
        outs.append(xt * cos + up * sa + dn * sb)
    return outs[0] if len(outs) == 1 else jnp.concatenate(outs, axis=1)


def _diff_lambda(vec_ref, lam_init):
    a = jnp.sum(vec_ref[V_LQ1:V_LQ1 + 1, 0:DF_QK] * vec_ref[V_LK1:V_LK1 + 1, 0:DF_QK], axis=-1, keepdims=True)
    b = jnp.sum(vec_ref[V_LQ2:V_LQ2 + 1, 0:DF_QK] * vec_ref[V_LK2:V_LK2 + 1, 0:DF_QK], axis=-1, keepdims=True)
    return jnp.exp(a) - jnp.exp(b) + lam_init


def _mixer_front(x, mod, vec_ref, w_in_ref, w_uq_ref, bd128_ref, bd64_ref, bd32_ref):
    sh = mod[:, 0:D_MODEL]
    sc = mod[:, D_MODEL:2 * D_MODEL]
    h = _rms_rows(x, vec_ref[V_GMIX:V_GMIX + 1, :]) * (1.0 + sc) + sh
    z = _dot(h.astype(BF16), w_in_ref[...])
    cqn = _rms_rows(z[:, C_CQ:C_CQ + Q_LORA], vec_ref[V_GQA:V_GQA + 1, 0:Q_LORA])
    q_raw = _dot(cqn.astype(BF16), w_uq_ref[...])
    q_mla = _seg_rms(q_raw, bd128_ref, vec_ref[V_GMQ:V_GMQ + 1, 0:W_MLA_P] * (MLA_QK ** -0.5 * LOG2E), MLA_QK)
    ckv_n = _rms_rows(z[:, C_CKV:C_CKV + KV_LORA], vec_ref[V_GKVA:V_GKVA + 1, 0:KV_LORA])
    q_na = _seg_rms(z[:, C_NAQ:C_NAQ + NA_W], bd64_ref, vec_ref[V_GNQ:V_GNQ + 1, 0:NA_W] * (NA_HD ** -0.5 * LOG2E), NA_HD)
    k_na = _seg_rms(z[:, C_NAK:C_NAK + NA_W], bd64_ref, vec_ref[V_GNK:V_GNK + 1, 0:NA_W], NA_HD)
    v_na = z[:, C_NAV:C_NAV + NA_W]
    q_df = _seg_rms(z[:, C_DFQ:C_DFQ + DF_W], bd32_ref, vec_ref[V_GDQ:V_GDQ + 1, 0:DF_W] * (DF_QK ** -0.5 * LOG2E), DF_QK)
    k_df = _seg_rms(z[:, C_DFK:C_DFK + DF_W], bd32_ref, vec_ref[V_GDK:V_GDK + 1, 0:DF_W], DF_QK)
    v_df = z[:, C_DFV:C_DFV + DF_W]
    kr_tile = z[:, C_KR:C_KR + LANES]
    return q_mla, ckv_n, kr_tile, q_na, k_na, v_na, q_df, k_df, v_df


def _mla_kv(ckv_n, kr_tile, vec_ref, w_ukv_ref, bd128_ref):
    kv = _dot(ckv_n.astype(BF16), w_ukv_ref[...])
    lane = _lane_iota((1, LANES))
    kr = jnp.where((lane >= MLA_NOPE) & (lane < MLA_QK), kr_tile, 0.0)
    k_pre = kv[:, 0:W_MLA_P] + jnp.concatenate([kr] * MLA_HEADS, axis=1)
    k = _seg_rms(k_pre, bd128_ref, vec_ref[V_GMK:V_GMK + 1, 0:W_MLA_P], MLA_QK)
    return k, kv[:, W_MLA_P:W_MLA_P + MLA_W]


def _softmax_parts(s):
    m = jnp.max(s, axis=-1, keepdims=True)
    p = jnp.exp2(s - m)
    return p, 1.0 / jnp.sum(p, axis=-1, keepdims=True)


def _scores(q, k_segs):
    parts = [_dot_nt(q, k) for k in k_segs]
    return parts[0] if len(parts) == 1 else jnp.concatenate(parts, axis=1)


def _pv(p, v_segs):
    out = None
    c0 = 0
    for v in v_segs:
        n = v.shape[0]
        o = _dot(p[:, c0:c0 + n], v)
        out = o if out is None else out + o
        c0 += n
    return out


def _lane_groups(qt, width):
    lane = _lane_iota((1, LANES))
    zero = jnp.zeros_like(qt)
    return jnp.concatenate(
        [jnp.where((lane >= g * width) & (lane < (g + 1) * width), qt, zero) for g in range(LANES // width)], axis=0)


def _pair_select(o2):
    tq = o2.shape[0] // 2
    return jnp.where(_lane_iota((1, LANES)) < NA_HD, o2[0:tq], o2[tq:2 * tq])


def _mla_attend(q, k_segs, v_segs):
    outs = []
    for t in range(MLA_HEADS // 2):
        vt = [v[:, t * LANES:(t + 1) * LANES] for v in v_segs]
        halves = []
        for h in (2 * t, 2 * t + 1):
            p, il = _softmax_parts(_scores(q[:, h * MLA_PAD:(h + 1) * MLA_PAD],
                                           [k[:, h * MLA_PAD:(h + 1) * MLA_PAD] for k in k_segs]))
            halves.append(_pv(p.astype(BF16), vt) * il)
        outs.append(jnp.where(_lane_iota((1, LANES)) < MLA_V, halves[0], halves[1]))
    return jnp.concatenate(outs, axis=1)


def _na_tile_attend(qt, k_segs, v_segs, bias=None):
    s = _scores(_lane_groups(qt, NA_HD), k_segs)
    if bias is not None:
        nb = bias.shape[1]
        s = jnp.concatenate([s[:, 0:nb] + bias, s[:, nb:]], axis=1)
    p, il = _softmax_parts(s)
    return _pair_select(_pv(p.astype(BF16), v_segs) * il)


def _na_attend_full(q, k_segs, v_segs):
    tile = lambda a, t: a[:, t * LANES:(t + 1) * LANES]
    return jnp.concatenate(
        [_na_tile_attend(tile(q, t), [tile(k, t) for k in k_segs], [tile(v, t) for v in v_segs])
         for t in range(NA_HEADS // 2)], axis=1)


def _df_attend(q, k_segs, v_segs, lam, g_sub, out_scale):
    outs = []
    lane = _lane_iota((1, LANES))
    tq = q.shape[0]
    for t in range(DF_HEADS // 2):
        kt = [k[:, t * LANES:(t + 1) * LANES] for k in k_segs]
        vt = [v[:, t * LANES:(t + 1) * LANES] for v in v_segs]
        p, il = _softmax_parts(_scores(_lane_groups(q[:, t * LANES:(t + 1) * LANES], DF_QK), kt))
        pn = []
        for hh in range(2):
            r1, r2 = 2 * hh * tq, (2 * hh + 1) * tq
            pn.append((p[r1:r1 + tq] * il[r1:r1 + tq] - p[r2:r2 + tq] * (lam * il[r2:r2 + tq])).astype(BF16))
        o = _pair_select(_pv(jnp.concatenate(pn, axis=0), vt))
        o2 = o * o
        ms_e = jnp.sum(jnp.where(lane < DF_HD, o2, 0.0), axis=-1, keepdims=True)
        ms_o = jnp.sum(jnp.where(lane >= DF_HD, o2, 0.0), axis=-1, keepdims=True)
        r = lax.rsqrt(jnp.where(lane < DF_HD, ms_e, ms_o) * (1.0 / DF_HD) + EPS)
        outs.append(o * r * (g_sub[:, t * LANES:(t + 1) * LANES] * out_scale))
    return jnp.concatenate(outs, axis=1)


def _mod_kernel(c_ref, w_ref, b_ref, o_ref):
    c = c_ref[...]
    s = c * jax.nn.sigmoid(c)
    o_ref[...] = _dot(s.astype(BF16), w_ref[...].astype(BF16)) + b_ref[...]


def _ctx_kernel(lam_init, n_alias, x_ref, mod_ref, vec_ref, w_in_ref, w_uq_ref, w_ukv_ref, bd128_ref, bd64_ref,
                bd32_ref, *rest):
    mix_ref, ckv_ref, kr_ref, nak_ref, nav_ref, dfk_ref, dfv_ref = rest[n_alias:]
    q_mla, ckv_n, kr_tile, q_na, k_na, v_na, q_df, k_df, v_df = _mixer_front(
        x_ref[...], mod_ref[...], vec_ref, w_in_ref, w_uq_ref, bd128_ref, bd64_ref, bd32_ref)
    ckv_ref[...] = ckv_n
    kr_ref[...] = kr_tile[:, 0:MLA_ROPE]
    for h in range(NA_HEADS):
        nak_ref[h] = k_na[:, h * NA_HD:(h + 1) * NA_HD]
        nav_ref[h] = v_na[:, h * NA_HD:(h + 1) * NA_HD]
    for h in range(DF_HEADS):
        dfk_ref[h] = k_df[:, h * DF_HD:(h + 1) * DF_HD]
        dfv_ref[h] = v_df[:, h * DF_HD:(h + 1) * DF_HD]
    k_mla, v_mla = _mla_kv(ckv_n, kr_tile, vec_ref, w_ukv_ref, bd128_ref)
    o_mla = _mla_attend(q_mla.astype(BF16), [k_mla.astype(BF16)], [v_mla.astype(BF16)])
    o_na = _na_attend_full(q_na.astype(BF16), [k_na.astype(BF16)], [v_na.astype(BF16)])
    lam = _diff_lambda(vec_ref, lam_init)
    o_df = _df_attend(q_df.astype(BF16), [k_df.astype(BF16)], [v_df.astype(BF16)], lam,
                      vec_ref[V_GDS:V_GDS + 1, 0:DF_W], 1.0 - lam_init)
    mix_ref[...] = jnp.concatenate([o_mla, o_na, o_df], axis=1).astype(BF16)


def _finish_kernel(x_ref, mix_ref, mod_ref, vec_ref, w_out_ref, w_gate_ref, w_up_ref, w_down_ref, y_ref):
    mod = mod_ref[...]
    gate_m = mod[:, 2 * D_MODEL:3 * D_MODEL]
    sh = mod[:, 3 * D_MODEL:4 * D_MODEL]
    sc = mod[:, 4 * D_MODEL:5 * D_MODEL]
    gate_f = mod[:, 5 * D_MODEL:6 * D_MODEL]
    x1 = x_ref[...] + gate_m * _dot(mix_ref[...], w_out_ref[...])
    h = (_rms_rows(x1, vec_ref[V_GFFN:V_GFFN + 1, :]) * (1.0 + sc) + sh).astype(BF16)
    g = _dot(h, w_gate_ref[...])
    u = _dot(h, w_up_ref[...])
    a = (g * jax.nn.sigmoid(g) * u).astype(BF16)
    y_ref[...] = x1 + gate_f * _dot(a, w_down_ref[...])


def _lat_front_kernel(x_ref, mod_ref, vec_ref, rope_ref, w_in_ref, w_uq_ref, w_ukv_ref, bd128_ref, bd64_ref, bd32_ref,
                      q_ref, k_ref, v_ref):
    q_mla, ckv_n, kr_tile, q_na, k_na, v_na, q_df, k_df, v_df = _mixer_front(
        x_ref[...], mod_ref[...], vec_ref, w_in_ref, w_uq_ref, bd128_ref, bd64_ref, bd32_ref)
    k_mla, v_mla = _mla_kv(ckv_n, kr_tile, vec_ref, w_ukv_ref, bd128_ref)
    cm, sam, sbm = rope_ref[0], rope_ref[1], rope_ref[2]
    cd, sad, sbd = rope_ref[3], rope_ref[4], rope_ref[5]
    q_ref[...] = jnp.concatenate(
        [_rope_tiles(q_mla, cm, sam, sbm), q_na, _rope_tiles(q_df, cd, sad, sbd)], axis=1).astype(BF16)
    k_ref[...] = jnp.concatenate(
        [_rope_tiles(k_mla, cm, sam, sbm), k_na, _rope_tiles(k_df, cd, sad, sbd)], axis=1).astype(BF16)
    v_ref[...] = jnp.concatenate([v_mla, v_na, v_df], axis=1).astype(BF16)


def _ctx_kv_kernel(ckv_ref, kr_ref, nak_ref, nav_ref, dfk_ref, dfv_ref, vec_ref, w_ukv_ref, bd128_ref, k_ref, v_ref):
    k_mla, v_mla = _mla_kv(ckv_ref[...], kr_ref[...], vec_ref, w_ukv_ref, bd128_ref)
    k_ref[...] = jnp.concatenate([k_mla, nak_ref[...], dfk_ref[...]], axis=1).astype(BF16)
    v_ref[...] = jnp.concatenate([v_mla, nav_ref[...], dfv_ref[...]], axis=1).astype(BF16)


def _na_lat_kernel(q_ref, kl_ref, vl_ref, kc_ref, vc_ref, g_ref, o_ref, bias_ref):
    b = pl.program_id(0)
    j = pl.program_id(1)

    @pl.when((b == 0) & (j == 0))
    def _build_bias():
        c = lax.broadcasted_iota(jnp.int32, (GRID_W, LANES), 0)
        kc = _lane_iota((GRID_W, LANES)) % GRID_W
        start = jnp.clip(c - NA_KW // 2, 0, GRID_W - NA_KW)
        in_win = (kc >= start) & (kc < start + NA_KW)

        def body(i, carry):
            row = jnp.broadcast_to(g_ref[pl.ds(i, 1), :], (GRID_W, LANES))
            toep = pltpu.roll(row, 0, 1, stride=1, stride_axis=0)
            bias_ref[i] = jnp.where(in_win, toep * LOG2E, NEG)
            return carry

        lax.fori_loop(0, NA_HEADS * NA_PAIR_TILES, body, 0)

    n_win = NA_KR * GRID_W
    rows = []
    for a in range(NA_ROWS_PER_STEP):
        r = j * NA_ROWS_PER_STEP + a
        rs = jnp.clip(r - NA_KR // 2, 0, GRID_ROWS - NA_KR)
        idx0 = rs - r + (NA_KR - 1)
        row0 = pl.multiple_of(rs * GRID_W, GRID_W)
        outs = []
        for t in range(NA_HEADS // 2):
            lanes = slice(t * LANES, (t + 1) * LANES)
            bias = jnp.concatenate(
                [jnp.concatenate([bias_ref[h * NA_PAIR_TILES + idx0 + 2 * m] for m in range(NA_KR // 2)], axis=1)
                 for h in (2 * t, 2 * t + 1)], axis=0)
            outs.append(_na_tile_attend(
                q_ref[a * GRID_W:(a + 1) * GRID_W, lanes],
                [kl_ref[pl.ds(row0, n_win), lanes], kc_ref[:, lanes]],
                [vl_ref[pl.ds(row0, n_win), lanes], vc_ref[:, lanes]], bias))
        rows.append(jnp.concatenate(outs, axis=1))
    o_ref[...] = jnp.concatenate(rows, axis=0).astype(BF16)


def _lat_attn_kernel(lam_init, q_ref, kl_ref, vl_ref, kc_ref, vc_ref, ona_ref, vec_ref, mix_ref):
    o_mla = _mla_attend(q_ref[:, P_MLA:P_MLA + W_MLA_P],
                        [kc_ref[:, P_MLA:P_MLA + W_MLA_P], kl_ref[:, P_MLA:P_MLA + W_MLA_P]],
                        [vc_ref[:, PV_MLA:PV_MLA + MLA_W], vl_ref[:, PV_MLA:PV_MLA + MLA_W]])
    lam = _diff_lambda(vec_ref, lam_init)
    o_df = _df_attend(q_ref[:, P_DF:P_DF + DF_W],
                      [kc_ref[:, P_DF:P_DF + DF_W], kl_ref[:, P_DF:P_DF + DF_W]],
                      [vc_ref[:, PV_DF:PV_DF + DF_W], vl_ref[:, PV_DF:PV_DF + DF_W]],
                      lam, vec_ref[V_GDS:V_GDS + 1, 0:DF_W], 1.0 - lam_init)
    mix_ref[...] = jnp.concatenate([o_mla.astype(BF16), ona_ref[...], o_df.astype(BF16)], axis=1)


def _const_spec(shape):
    nd = len(shape)
    return pl.BlockSpec(shape, lambda *_: (0,) * nd, pipeline_mode=pl.Buffered(1))


def _layer_spec(l, shape):
    nd = len(shape)
    return pl.BlockSpec((None,) + tuple(shape), lambda *_: (l,) + (0,) * nd, pipeline_mode=pl.Buffered(1))


def _params(n_axes):
    return pltpu.CompilerParams(dimension_semantics=("arbitrary",) * n_axes, vmem_limit_bytes=VMEM_LIMIT)


def _block_diag(group):
    i = np.arange(MXU_DIM) // group
    return jnp.asarray((i[:, None] == i[None, :]).astype(np.float32), dtype=BF16)


def _rope_tables():
    t = np.arange(DEC_SEQ)
    row = (t // GRID_W).astype(np.float64)
    col = (t % GRID_W).astype(np.float64)
    n = MLA_ROPE // 4
    inv = 1.0 / (ROPE_BASE ** (np.arange(n, dtype=np.float64) * 2.0 / (MLA_ROPE // 2)))
    ar = row[:, None] * inv
    ac = col[:, None] * inv
    ang = np.concatenate([ar, ar, ac, ac], axis=-1)
    cos32, sin32 = np.cos(ang), np.sin(ang)
    first = (np.arange(MLA_ROPE) % (2 * n)) < n
    sa32 = np.where(first, -sin32, 0.0)
    sb32 = np.where(first, 0.0, sin32)

    def mla_tile(v32, fill):
        out = np.full((DEC_SEQ, LANES), fill)
        out[:, MLA_NOPE:MLA_QK] = v32
        return out

    tabs = [mla_tile(cos32, 1.0), mla_tile(sa32, 0.0), mla_tile(sb32, 0.0),
            np.tile(cos32, (1, LANES // DF_QK)), np.tile(sa32, (1, LANES // DF_QK)), np.tile(sb32, (1, LANES // DF_QK))]
    return jnp.asarray(np.stack(tabs).astype(np.float32))


def _tables(p):
    w_in = p['w_in']
    kr = w_in[:, :, Q_LORA + KV_LORA:Q_LORA + KV_LORA + MLA_ROPE]
    z32 = jnp.zeros_like(kr)
    w_in_r = jnp.concatenate(
        [w_in[:, :, :Q_LORA + KV_LORA], w_in[:, :, Q_LORA + KV_LORA + MLA_ROPE:], kr, z32, kr, z32], axis=2).astype(BF16)
    w_uq = p['w_uq'].reshape(DEPTH, Q_LORA, MLA_HEADS, MLA_QK)
    w_uq_p = jnp.pad(w_uq, ((0, 0), (0, 0), (0, 0), (0, MLA_PAD - MLA_QK))).reshape(DEPTH, Q_LORA, W_MLA_P).astype(BF16)
    w_ukv = p['w_ukv'].reshape(DEPTH, KV_LORA, MLA_HEADS, MLA_NOPE + MLA_V)
    wk = jnp.pad(w_ukv[..., :MLA_NOPE], ((0, 0), (0, 0), (0, 0), (0, MLA_PAD - MLA_NOPE))).reshape(DEPTH, KV_LORA, W_MLA_P)
    wv = w_ukv[..., MLA_NOPE:].reshape(DEPTH, KV_LORA, MLA_W)
    w_ukv_r = jnp.concatenate([wk, wv], axis=2).astype(BF16)

    def row(v, reps=1):
        v = jnp.tile(v, (1, reps)) if reps > 1 else v
        return [v, jnp.zeros((DEPTH, D_MODEL - v.shape[1]), F32)] if v.shape[1] < D_MODEL else [v]

    pad_head = lambda g: jnp.pad(g, ((0, 0), (0, MLA_PAD - MLA_QK)))
    pieces = (row(p['g_mix']) + row(p['g_ffn']) + row(p['g_qa']) + row(p['g_kva'])
              + row(pad_head(p['g_mla_q']), MLA_HEADS) + row(pad_head(p['g_mla_k']), MLA_HEADS)
              + row(p['g_na_q'], NA_HEADS) + row(p['g_na_k'], NA_HEADS)
              + row(p['g_df_q'], 2 * DF_HEADS) + row(p['g_df_k'], 2 * DF_HEADS) + row(p['g_df_sub'], DF_HEADS)
              + row(p['df_lq1']) + row(p['df_lk1']) + row(p['df_lq2']) + row(p['df_lk2'])
              + [jnp.zeros((DEPTH, D_MODEL), F32)])
    vec = jnp.concatenate(pieces, axis=1).reshape(DEPTH, N_VEC, D_MODEL)
    f = p['na_rpb']
    zpad = jnp.zeros((DEPTH, NA_HEADS, NA_PAIR_TILES, 33), F32)
    g_rows = jnp.concatenate([f[:, :, :-1, NA_KW - 1:], zpad, f[:, :, 1:, :], zpad, f[:, :, :-1, :NA_KW - 1]], axis=-1)
    g_rows = g_rows.reshape(DEPTH, NA_HEADS * NA_PAIR_TILES, LANES)
    return dict(w_in=w_in_r, w_uq=w_uq_p, w_ukv=w_ukv_r, vec=vec, g_rows=g_rows,
                w_out=p['w_out'].astype(BF16), w_gate=p['w_gate'].astype(BF16),
                w_up=p['w_up'].astype(BF16), w_down=p['w_down'].astype(BF16))


def _modulation(c_all, w_mod, b_mod):
    tn = 1024
    return pl.pallas_call(
        _mod_kernel,
        grid=(DEPTH, 6 * D_MODEL // tn),
        in_specs=[pl.BlockSpec((N_MOD, D_MODEL), lambda l, j: (0, 0)),
                  pl.BlockSpec((None, D_MODEL, tn), lambda l, j: (l, 0, j)),
                  pl.BlockSpec((None, 1, tn), lambda l, j: (l, 0, j))],
        out_specs=pl.BlockSpec((None, N_MOD, tn), lambda l, j: (l, 0, j)),
        out_shape=jax.ShapeDtypeStruct((DEPTH, N_MOD, 6 * D_MODEL), F32),
        compiler_params=_params(2),
        name="modulation",
    )(c_all, w_mod, b_mod.reshape(DEPTH, 1, 6 * D_MODEL))


def _front_weight_specs(l):
    return [_layer_spec(l, (N_VEC, D_MODEL)), _layer_spec(l, (D_MODEL, IN_COLS_R)), _layer_spec(l, (Q_LORA, W_MLA_P)),
            _layer_spec(l, (KV_LORA, W_MLA_P + MLA_W)),
            _const_spec((MXU_DIM, MXU_DIM)), _const_spec((MXU_DIM, MXU_DIM)), _const_spec((MXU_DIM, MXU_DIM))]


def _context_mixer(l, lam_init, x, mods, t, consts, prev_caches):
    n_tok = BATCH * SEQ
    n_alias = len(prev_caches)
    tok = lambda w: pl.BlockSpec((SEQ, w), lambda b: (b, 0))
    lay = lambda *s: pl.BlockSpec((None, None) + s, lambda b: (b, l) + (0,) * len(s))
    cache_shapes = [(SEQ, KV_LORA), (SEQ, MLA_ROPE), (NA_HEADS, SEQ, NA_HD), (NA_HEADS, SEQ, NA_HD),
                    (DF_HEADS, SEQ, DF_HD), (DF_HEADS, SEQ, DF_HD)]
    n_in = 2 + 7
    return pl.pallas_call(
        functools.partial(_ctx_kernel, lam_init, n_alias),
        grid=(BATCH,),
        in_specs=[tok(D_MODEL), pl.BlockSpec((None, 1, 6 * D_MODEL), lambda b: (l * N_MOD, 0, 0))]
        + _front_weight_specs(l) + [pl.BlockSpec(memory_space=pl.ANY)] * n_alias,
        out_specs=[tok(D_MODEL)] + [lay(*s) for s in cache_shapes],
        out_shape=[jax.ShapeDtypeStruct((n_tok, D_MODEL), BF16)]
        + [jax.ShapeDtypeStruct((BATCH, DEPTH) + s, F32) for s in cache_shapes],
        input_output_aliases={n_in + i: 1 + i for i in range(n_alias)},
        compiler_params=_params(1),
        name="context_mixer",
    )(x, mods, t['vec'], t['w_in'], t['w_uq'], t['w_ukv'], consts['bd128'], consts['bd64'], consts['bd32'], *prev_caches)


def _finish(l, x, mix, mods, t, mod_row):
    n_tok = x.shape[0]
    tok = pl.BlockSpec((TM, D_MODEL), lambda i: (i, 0))
    return pl.pallas_call(
        _finish_kernel,
        grid=(n_tok // TM,),
        in_specs=[tok, tok, pl.BlockSpec((None, 1, 6 * D_MODEL), lambda i: (l * N_MOD + mod_row(i), 0, 0)),
                  _layer_spec(l, (N_VEC, D_MODEL)), _layer_spec(l, (D_MODEL, D_MODEL)), _layer_spec(l, (D_MODEL, D_FF)),
                  _layer_spec(l, (D_MODEL, D_FF)), _layer_spec(l, (D_FF, D_MODEL))],
        out_specs=tok,
        out_shape=jax.ShapeDtypeStruct((n_tok, D_MODEL), F32),
        compiler_params=_params(1),
        name="finish",
    )(x, mix, mods, t['vec'], t['w_out'], t['w_gate'], t['w_up'], t['w_down'])


def _latent_front(l, x, mods, t, consts):
    n_tok = DEC_BATCH * DEC_SEQ
    blocks_per_seq = DEC_SEQ // TM
    tok = lambda w: pl.BlockSpec((TM, w), lambda i: (i, 0))
    weights = _front_weight_specs(l)
    return pl.pallas_call(
        _lat_front_kernel,
        grid=(n_tok // TM,),
        in_specs=[tok(D_MODEL),
                  pl.BlockSpec((None, 1, 6 * D_MODEL), lambda i: (l * N_MOD + 1 + i // blocks_per_seq, 0, 0)),
                  weights[0], pl.BlockSpec((6, TM, LANES), lambda i: (0, i % blocks_per_seq, 0))] + weights[1:],
        out_specs=[tok(QK_PACK), tok(QK_PACK), tok(V_PACK)],
        out_shape=[jax.ShapeDtypeStruct((n_tok, QK_PACK), BF16), jax.ShapeDtypeStruct((n_tok, QK_PACK), BF16),
                   jax.ShapeDtypeStruct((n_tok, V_PACK), BF16)],
        compiler_params=_params(1),
        name="latent_front",
    )(x, mods, t['vec'], consts['rope'], t['w_in'], t['w_uq'], t['w_ukv'], consts['bd128'], consts['bd64'], consts['bd32'])


def _context_kv(l, caches, t, consts):
    lay = lambda w: pl.BlockSpec((None, None, PAST_LEN, w), lambda b: (b, l, 0, 0))
    out = lambda w: pl.BlockSpec((None, PAST_LEN, w), lambda b: (b, 0, 0))
    return pl.pallas_call(
        _ctx_kv_kernel,
        grid=(DEC_BATCH,),
        in_specs=[lay(KV_LORA), lay(LANES), lay(NA_W), lay(NA_W), lay(DF_W), lay(DF_W),
                  _layer_spec(l, (N_VEC, D_MODEL)), _layer_spec(l, (KV_LORA, W_MLA_P + MLA_W)),
                  _const_spec((MXU_DIM, MXU_DIM))],
        out_specs=[out(QK_PACK), out(V_PACK)],
        out_shape=[jax.ShapeDtypeStruct((DEC_BATCH, PAST_LEN, QK_PACK), BF16),
                   jax.ShapeDtypeStruct((DEC_BATCH, PAST_LEN, V_PACK), BF16)],
        compiler_params=_params(1),
        name="context_kv",
    )(*caches, t['vec'], t['w_ukv'], consts['bd128'])


def _latent_na(l, q, k, v, kc, vc, g_rows):
    n_tok = DEC_BATCH * DEC_SEQ
    na_blk = P_NA // NA_W
    steps = GRID_ROWS // NA_ROWS_PER_STEP
    tq = NA_ROWS_PER_STEP * GRID_W
    return pl.pallas_call(
        _na_lat_kernel,
        grid=(DEC_BATCH, steps),
        in_specs=[pl.BlockSpec((tq, NA_W), lambda b, j: (b * steps + j, na_blk)),
                  pl.BlockSpec((DEC_SEQ, NA_W), lambda b, j: (b, na_blk)),
                  pl.BlockSpec((DEC_SEQ, NA_W), lambda b, j: (b, PV_NA // NA_W)),
                  pl.BlockSpec((None, PAST_LEN, NA_W), lambda b, j: (b, 0, na_blk)),
                  pl.BlockSpec((None, PAST_LEN, NA_W), lambda b, j: (b, 0, PV_NA // NA_W)),
                  _layer_spec(l, (NA_HEADS * NA_PAIR_TILES, LANES))],
        out_specs=pl.BlockSpec((tq, NA_W), lambda b, j: (b * steps + j, 0)),
        out_shape=jax.ShapeDtypeStruct((n_tok, NA_W), BF16),
        scratch_shapes=[pltpu.VMEM((NA_HEADS * NA_PAIR_TILES, GRID_W, LANES), F32)],
        compiler_params=_params(2),
        name="latent_neighbourhood",
    )(q, k, v, kc, vc, g_rows)


def _latent_attention(l, lam_init, q, k, v, kc, vc, o_na, t):
    n_tok = DEC_BATCH * DEC_SEQ
    nq = DEC_SEQ // TM
    return pl.pallas_call(
        functools.partial(_lat_attn_kernel, lam_init),
        grid=(DEC_BATCH, nq),
        in_specs=[pl.BlockSpec((TM, QK_PACK), lambda b, j: (b * nq + j, 0)),
                  pl.BlockSpec((DEC_SEQ, QK_PACK), lambda b, j: (b, 0)),
                  pl.BlockSpec((DEC_SEQ, V_PACK), lambda b, j: (b, 0)),
                  pl.BlockSpec((None, PAST_LEN, QK_PACK), lambda b, j: (b, 0, 0)),
                  pl.BlockSpec((None, PAST_LEN, V_PACK), lambda b, j: (b, 0, 0)),
                  pl.BlockSpec((TM, NA_W), lambda b, j: (b * nq + j, 0)),
                  _layer_spec(l, (N_VEC, D_MODEL))],
        out_specs=pl.BlockSpec((TM, D_MODEL), lambda b, j: (b * nq + j, 0)),
        out_shape=jax.ShapeDtypeStruct((n_tok, D_MODEL), BF16),
        compiler_params=_params(2),
        name="latent_attention",
    )(q, k, v, kc, vc, o_na, t['vec'])


def kernel(x_prompt, x_sample, cache_mla_ckv, cache_mla_krope, cache_na_k, cache_na_v, cache_df_k, cache_df_v, c, c_ctx, w_mod, b_mod, g_mix, w_in, g_qa, w_uq, g_kva, w_ukv, g_mla_q, g_mla_k, g_na_q, g_na_k, na_rpb, g_df_q, g_df_k, df_lq1, df_lk1, df_lq2, df_lk2, g_df_sub, w_out, g_ffn, w_gate, w_up, w_down):
    p = dict(g_mix=g_mix, w_in=w_in, g_qa=g_qa, w_uq=w_uq, g_kva=g_kva, w_ukv=w_ukv, g_mla_q=g_mla_q, g_mla_k=g_mla_k,
             g_na_q=g_na_q, g_na_k=g_na_k, na_rpb=na_rpb, g_df_q=g_df_q, g_df_k=g_df_k, df_lq1=df_lq1, df_lk1=df_lk1,
             df_lq2=df_lq2, df_lk2=df_lk2, g_df_sub=g_df_sub, w_out=w_out, g_ffn=g_ffn, w_gate=w_gate, w_up=w_up,
             w_down=w_down)
    consts = dict(bd128=_block_diag(MLA_PAD), bd64=_block_diag(NA_HD), bd32=_block_diag(DF_QK), rope=_rope_tables())
    t = _tables(p)

    c_all = jnp.concatenate([c_ctx[None, :], c, jnp.zeros((N_MOD - 1 - DEC_BATCH, D_MODEL), F32)], axis=0)
    mods = _modulation(c_all, w_mod, b_mod).reshape(DEPTH * N_MOD, 1, 6 * D_MODEL)

    tokmajor = lambda a: a.transpose(0, 1, 3, 2, 4).reshape(DEC_BATCH, DEPTH, PAST_LEN, -1)
    caches = (cache_mla_ckv, jnp.pad(cache_mla_krope, ((0, 0), (0, 0), (0, 0), (MLA_NOPE, LANES - MLA_QK))),
              tokmajor(cache_na_k), tokmajor(cache_na_v), tokmajor(cache_df_k), tokmajor(cache_df_v))

    xp = x_prompt.reshape(BATCH * SEQ, D_MODEL)
    xs = x_sample.reshape(DEC_BATCH * DEC_SEQ, D_MODEL)
    new_caches = ()
    for l in range(DEPTH):
        lam_init = 0.8 - 0.6 * math.exp(-0.3 * l)
        mix, *new_caches = _context_mixer(l, lam_init, xp, mods, t, consts, new_caches)
        xp = _finish(l, xp, mix, mods, t, lambda i: 0)
        q, k, v = _latent_front(l, xs, mods, t, consts)
        kc, vc = _context_kv(l, caches, t, consts)
        o_na = _latent_na(l, q, k, v, kc, vc, t['g_rows'])
        mix_s = _latent_attention(l, lam_init, q, k, v, kc, vc, o_na, t)
        xs = _finish(l, xs, mix_s, mods, t, lambda i: 1 + i // (DEC_SEQ // TM))
    return (xp.reshape(BATCH, SEQ, D_MODEL), xs.reshape(DEC_BATCH, DEC_SEQ, D_MODEL), *new_caches)
```

```python
import functools
import math

import numpy as np
import jax
import jax.numpy as jnp
from jax import lax
from jax.experimental import pallas as pl
from jax.experimental.pallas import tpu as pltpu

F32 = jnp.float32
BF16 = jnp.bfloat16

D_MODEL = 1024
BATCH = 32
SEQ = 256
DEPTH = 2
DEC_BATCH = 2
DEC_SEQ = 1024
PAST_LEN = 256
GRID_W = 64
GRID_ROWS = DEC_SEQ // GRID_W
MLA_HEADS = 6
MLA_NOPE = 64
MLA_ROPE = 32
MLA_QK = MLA_NOPE + MLA_ROPE
MLA_V = 64
MLA_PAD = 128
Q_LORA = 256
KV_LORA = 128
NA_HEADS = 6
NA_HD = 64
NA_KR = 8
NA_KW = 16
DF_HEADS = 4
DF_HD = 64
DF_QK = 32
MLA_W = MLA_HEADS * MLA_V
NA_W = NA_HEADS * NA_HD
DF_W = DF_HEADS * DF_HD
D_FF = -(-8 * D_MODEL // (3 * 256)) * 256
ROPE_BASE = 10000.0
EPS = 1e-6
NEG = -1e30
LOG2E = math.log2(math.e)

LANES = 128
MXU_DIM = 256

C_CQ = 0
C_CKV = C_CQ + Q_LORA
C_NAQ = C_CKV + KV_LORA
C_NAK = C_NAQ + NA_W
C_NAV = C_NAK + NA_W
C_DFQ = C_NAV + NA_W
C_DFK = C_DFQ + DF_W
C_DFV = C_DFK + DF_W
C_KR = C_DFV + DF_W
IN_COLS_R = C_KR + LANES

W_MLA_P = MLA_HEADS * MLA_PAD
QK_PACK = W_MLA_P + NA_W + DF_W
V_PACK = MLA_W + NA_W + DF_W
P_MLA = 0
P_NA = W_MLA_P
P_DF = P_NA + NA_W
PV_MLA = 0
PV_NA = MLA_W
PV_DF = MLA_W + NA_W

(V_GMIX, V_GFFN, V_GQA, V_GKVA, V_GMQ, V_GMK, V_GNQ, V_GNK, V_GDQ, V_GDK, V_GDS,
 V_LQ1, V_LK1, V_LQ2, V_LK2) = range(15)
N_VEC = 16

N_MOD = 8
TM = 256
TM_FINISH = 512
CTX_SEQS_PER_STEP = 2
NA_ROWS_PER_STEP = 4
VMEM_LIMIT = 56 * 1024 * 1024

NA_PAIR_TILES = 2 * NA_KR - 2
N_CACHE = 6


def _dot(a, b):
    return jnp.dot(a, b, preferred_element_type=F32)


def _dot_nt(a, b):
    return lax.dot_general(a, b, (((1,), (1,)), ((), ())), preferred_element_type=F32)


def _lane_iota(shape):
    return lax.broadcasted_iota(jnp.int32, shape, len(shape) - 1)


def _rms_rows(x, g):
    ms = jnp.mean(x * x, axis=-1, keepdims=True)
    return x * lax.rsqrt(ms + EPS) * g


def _seg_rms(x, bd_ref, g, n_real):
    width = x.shape[1]
    sq = (x * x).astype(BF16)
    parts = []
    for c0 in range(0, width, MXU_DIM):
        w = min(MXU_DIM, width - c0)
        parts.append(_dot(sq[:, c0:c0 + w], bd_ref[0:w, 0:w]))
    ss = parts[0] if len(parts) == 1 else jnp.concatenate(parts, axis=1)
    return x * lax.rsqrt(ss * (1.0 / n_real) + EPS) * g


def _rope_tiles(x, cos, sa, sb):
    outs = []
    for t in range(x.shape[1] // LANES):
        xt = x[:, t * LANES:(t + 1) * LANES]
        up = pltpu.roll(xt, LANES - MLA_ROPE // 4, 1)
        dn = pltpu.roll(xt, MLA_ROPE // 4, 1)
        outs.append(xt * cos + up * sa + dn * sb)
    return outs[0] if len(outs) == 1 else jnp.concatenate(outs, axis=1)


def _diff_lambda(vec_ref, lam_init):
    a = jnp.sum(vec_ref[V_LQ1:V_LQ1 + 1, 0:DF_QK] * vec_ref[V_LK1:V_LK1 + 1, 0:DF_QK], axis=-1, keepdims=True)
    b = jnp.sum(vec_ref[V_LQ2:V_LQ2 + 1, 0:DF_QK] * vec_ref[V_LK2:V_LK2 + 1, 0:DF_QK], axis=-1, keepdims=True)
    return jnp.exp(a) - jnp.exp(b) + lam_init


def _mixer_front(x, mod, vec_ref, w_in_ref, w_uq_ref, bd128_ref, bd64_ref, bd32_ref):
    sh = mod[:, 0:D_MODEL]
    sc = mod[:, D_MODEL:2 * D_MODEL]
    h = _rms_rows(x, vec_ref[V_GMIX:V_GMIX + 1, :]) * (1.0 + sc) + sh
    z = _dot(h.astype(BF16), w_in_ref[...])
    cqn = _rms_rows(z[:, C_CQ:C_CQ + Q_LORA], vec_ref[V_GQA:V_GQA + 1, 0:Q_LORA])
    q_raw = _dot(cqn.astype(BF16), w_uq_ref[...])
    q_mla = _seg_rms(q_raw, bd128_ref, vec_ref[V_GMQ:V_GMQ + 1, 0:W_MLA_P] * (MLA_QK ** -0.5 * LOG2E), MLA_QK)
    ckv_n = _rms_rows(z[:, C_CKV:C_CKV + KV_LORA], vec_ref[V_GKVA:V_GKVA + 1, 0:KV_LORA])
    q_na = _seg_rms(z[:, C_NAQ:C_NAQ + NA_W], bd64_ref, vec_ref[V_GNQ:V_GNQ + 1, 0:NA_W] * (NA_HD ** -0.5 * LOG2E), NA_HD)
    k_na = _seg_rms(z[:, C_NAK:C_NAK + NA_W], bd64_ref, vec_ref[V_GNK:V_GNK + 1, 0:NA_W], NA_HD)
    v_na = z[:, C_NAV:C_NAV + NA_W]
    q_df = _seg_rms(z[:, C_DFQ:C_DFQ + DF_W], bd32_ref, vec_ref[V_GDQ:V_GDQ + 1, 0:DF_W] * (DF_QK ** -0.5 * LOG2E), DF_QK)
    k_df = _seg_rms(z[:, C_DFK:C_DFK + DF_W], bd32_ref, vec_ref[V_GDK:V_GDK + 1, 0:DF_W], DF_QK)
    v_df = z[:, C_DFV:C_DFV + DF_W]
    kr_tile = z[:, C_KR:C_KR + LANES]
    return q_mla, ckv_n, kr_tile, q_na, k_na, v_na, q_df, k_df, v_df


def _mla_kv(ckv_n, kr_tile, vec_ref, w_ukv_ref, bd128_ref):
    kv = _dot(ckv_n.astype(BF16), w_ukv_ref[...])
    lane = _lane_iota((1, LANES))
    kr = jnp.where((lane >= MLA_NOPE) & (lane < MLA_QK), kr_tile, 0.0)
    k_pre = kv[:, 0:W_MLA_P] + jnp.concatenate([kr] * MLA_HEADS, axis=1)
    k = _seg_rms(k_pre, bd128_ref, vec_ref[V_GMK:V_GMK + 1, 0:W_MLA_P], MLA_QK)
    return k, kv[:, W_MLA_P:W_MLA_P + MLA_W]


def _softmax_parts(s):
    m = jnp.max(s, axis=-1, keepdims=True)
    p = jnp.exp2(s - m)
    return p, 1.0 / jnp.sum(p, axis=-1, keepdims=True)


def _scores(q, k_segs):
    parts = [_dot_nt(q, k) for k in k_segs]
    return parts[0] if len(parts) == 1 else jnp.concatenate(parts, axis=1)


def _pv(p, v_segs):
    out = None
    c0 = 0
    for v in v_segs:
        n = v.shape[0]
        o = _dot(p[:, c0:c0 + n], v)
        out = o if out is None else out + o
        c0 += n
    return out


def _lane_groups(qt, width):
    lane = _lane_iota((1, LANES))
    zero = jnp.zeros_like(qt)
    return jnp.concatenate(
        [jnp.where((lane >= g * width) & (lane < (g + 1) * width), qt, zero) for g in range(LANES // width)], axis=0)


def _pair_select(o2):
    tq = o2.shape[0] // 2
    return jnp.where(_lane_iota((1, LANES)) < NA_HD, o2[0:tq], o2[tq:2 * tq])


def _mla_attend(q, k_segs, v_segs):
    outs = []
    for t in range(MLA_HEADS // 2):
        vt = [v[:, t * LANES:(t + 1) * LANES] for v in v_segs]
        halves = []
        for h in (2 * t, 2 * t + 1):
            p, il = _softmax_parts(_scores(q[:, h * MLA_PAD:(h + 1) * MLA_PAD],
                                           [k[:, h * MLA_PAD:(h + 1) * MLA_PAD] for k in k_segs]))
            halves.append(_pv(p.astype(BF16), vt) * il)
        outs.append(jnp.where(_lane_iota((1, LANES)) < MLA_V, halves[0], halves[1]))
    return jnp.concatenate(outs, axis=1)


def _na_tile_attend(qt, k_segs, v_segs, bias=None):
    s = _scores(_lane_groups(qt, NA_HD), k_segs)
    if bias is not None:
        nb = bias.shape[1]
        s = jnp.concatenate([s[:, 0:nb] + bias, s[:, nb:]], axis=1)
    p, il = _softmax_parts(s)
    return _pair_select(_pv(p.astype(BF16), v_segs) * il)


def _na_attend_full(q, k_segs, v_segs):
    tile = lambda a, t: a[:, t * LANES:(t + 1) * LANES]
    return jnp.concatenate(
        [_na_tile_attend(tile(q, t), [tile(k, t) for k in k_segs], [tile(v, t) for v in v_segs])
         for t in range(NA_HEADS // 2)], axis=1)


def _df_attend(q, k_segs, v_segs, lam, g_sub, out_scale):
    outs = []
    lane = _lane_iota((1, LANES))
    tq = q.shape[0]
    for t in range(DF_HEADS // 2):
        kt = [k[:, t * LANES:(t + 1) * LANES] for k in k_segs]
        vt = [v[:, t * LANES:(t + 1) * LANES] for v in v_segs]
        p, il = _softmax_parts(_scores(_lane_groups(q[:, t * LANES:(t + 1) * LANES], DF_QK), kt))
        pn = []
        for hh in range(2):
            r1, r2 = 2 * hh * tq, (2 * hh + 1) * tq
            pn.append((p[r1:r1 + tq] * il[r1:r1 + tq] - p[r2:r2 + tq] * (lam * il[r2:r2 + tq])).astype(BF16))
        o = _pair_select(_pv(jnp.concatenate(pn, axis=0), vt))
        o2 = o * o
        ms_e = jnp.sum(jnp.where(lane < DF_HD, o2, 0.0), axis=-1, keepdims=True)
        ms_o = jnp.sum(jnp.where(lane >= DF_HD, o2, 0.0), axis=-1, keepdims=True)
        r = lax.rsqrt(jnp.where(lane < DF_HD, ms_e, ms_o) * (1.0 / DF_HD) + EPS)
        outs.append(o * r * (g_sub[:, t * LANES:(t + 1) * LANES] * out_scale))
    return jnp.concatenate(outs, axis=1)


def _mod_kernel(c_ref, w_ref, b_ref, o_ref):
    c = c_ref[...]
    s = c * jax.nn.sigmoid(c)
    o_ref[...] = _dot(s.astype(BF16), w_ref[...].astype(BF16)) + b_ref[...]


def _ctx_kernel(lam_init, n_alias, x_ref, mod_ref, vec_ref, w_in_ref, w_uq_ref, w_ukv_ref, bd128_ref, bd64_ref,
                bd32_ref, *rest):
    mix_ref, ckv_ref, kr_ref, nak_ref, nav_ref, dfk_ref, dfv_ref = rest[n_alias:]
    q_mla, ckv_n, kr_tile, q_na, k_na, v_na, q_df, k_df, v_df = _mixer_front(
        x_ref[...], mod_ref[...], vec_ref, w_in_ref, w_uq_ref, bd128_ref, bd64_ref, bd32_ref)
    k_mla, v_mla = _mla_kv(ckv_n, kr_tile, vec_ref, w_ukv_ref, bd128_ref)
    lam = _diff_lambda(vec_ref, lam_init)
    bf = lambda a, rows: a[rows].astype(BF16)
    for s in range(CTX_SEQS_PER_STEP):
        rows = slice(s * SEQ, (s + 1) * SEQ)
        ckv_ref[s] = ckv_n[rows]
        kr_ref[s] = kr_tile[rows, 0:MLA_ROPE]
        for h in range(NA_HEADS):
            nak_ref[s, h] = k_na[rows, h * NA_HD:(h + 1) * NA_HD]
            nav_ref[s, h] = v_na[rows, h * NA_HD:(h + 1) * NA_HD]
        for h in range(DF_HEADS):
            dfk_ref[s, h] = k_df[rows, h * DF_HD:(h + 1) * DF_HD]
            dfv_ref[s, h] = v_df[rows, h * DF_HD:(h + 1) * DF_HD]
        o_mla = _mla_attend(bf(q_mla, rows), [bf(k_mla, rows)], [bf(v_mla, rows)])
        o_na = _na_attend_full(bf(q_na, rows), [bf(k_na, rows)], [bf(v_na, rows)])
        o_df = _df_attend(bf(q_df, rows), [bf(k_df, rows)], [bf(v_df, rows)], lam,
                          vec_ref[V_GDS:V_GDS + 1, 0:DF_W], 1.0 - lam_init)
        mix_ref[rows, :] = jnp.concatenate([o_mla, o_na, o_df], axis=1).astype(BF16)


def _finish_kernel(x_ref, mix_ref, mod_ref, vec_ref, w_out_ref, w_gate_ref, w_up_ref, w_down_ref, y_ref):
    mod = mod_ref[...]
    gate_m = mod[:, 2 * D_MODEL:3 * D_MODEL]
    sh = mod[:, 3 * D_MODEL:4 * D_MODEL]
    sc = mod[:, 4 * D_MODEL:5 * D_MODEL]
    gate_f = mod[:, 5 * D_MODEL:6 * D_MODEL]
    x1 = x_ref[...] + gate_m * _dot(mix_ref[...], w_out_ref[...])
    h = (_rms_rows(x1, vec_ref[V_GFFN:V_GFFN + 1, :]) * (1.0 + sc) + sh).astype(BF16)
    g = _dot(h, w_gate_ref[...])
    u = _dot(h, w_up_ref[...])
    a = (g * jax.nn.sigmoid(g) * u).astype(BF16)
    y_ref[...] = x1 + gate_f * _dot(a, w_down_ref[...])


def _lat_front_kernel(x_ref, mod_ref, vec_ref, rope_ref, w_in_ref, w_uq_ref, w_ukv_ref, bd128_ref, bd64_ref, bd32_ref,
                      q_ref, k_ref, v_ref):
    q_mla, ckv_n, kr_tile, q_na, k_na, v_na, q_df, k_df, v_df = _mixer_front(
        x_ref[...], mod_ref[...], vec_ref, w_in_ref, w_uq_ref, bd128_ref, bd64_ref, bd32_ref)
    k_mla, v_mla = _mla_kv(ckv_n, kr_tile, vec_ref, w_ukv_ref, bd128_ref)
    cm, sam, sbm = rope_ref[0], rope_ref[1], rope_ref[2]
    cd, sad, sbd = rope_ref[3], rope_ref[4], rope_ref[5]
    q_ref[...] = jnp.concatenate(
        [_rope_tiles(q_mla, cm, sam, sbm), q_na, _rope_tiles(q_df, cd, sad, sbd)], axis=1).astype(BF16)
    k_ref[...] = jnp.concatenate(
        [_rope_tiles(k_mla, cm, sam, sbm), k_na, _rope_tiles(k_df, cd, sad, sbd)], axis=1).astype(BF16)
    v_ref[...] = jnp.concatenate([v_mla, v_na, v_df], axis=1).astype(BF16)


def _ctx_kv_kernel(ckv_ref, kr_ref, nak_ref, nav_ref, dfk_ref, dfv_ref, vec_ref, w_ukv_ref, bd128_ref, k_ref, v_ref):
    kr = kr_ref[...]
    kr_tile = jnp.concatenate([jnp.zeros((PAST_LEN, MLA_NOPE), F32), kr, jnp.zeros((PAST_LEN, LANES - MLA_QK), F32)], axis=1)
    k_mla, v_mla = _mla_kv(ckv_ref[...], kr_tile, vec_ref, w_ukv_ref, bd128_ref)
    heads = lambda ref: [ref[h] for h in range(ref.shape[0])]
    k_ref[...] = jnp.concatenate([k_mla] + heads(nak_ref) + heads(dfk_ref), axis=1).astype(BF16)
    v_ref[...] = jnp.concatenate([v_mla] + heads(nav_ref) + heads(dfv_ref), axis=1).astype(BF16)


def _na_lat_kernel(q_ref, kl_ref, vl_ref, kc_ref, vc_ref, g_ref, o_ref, bias_ref):
    b = pl.program_id(0)
    j = pl.program_id(1)

    @pl.when((b == 0) & (j == 0))
    def _build_bias():
        c = lax.broadcasted_iota(jnp.int32, (GRID_W, LANES), 0)
        kc = _lane_iota((GRID_W, LANES)) % GRID_W
        start = jnp.clip(c - NA_KW // 2, 0, GRID_W - NA_KW)
        in_win = (kc >= start) & (kc < start + NA_KW)

        def body(i, carry):
            row = jnp.broadcast_to(g_ref[pl.ds(i, 1), :], (GRID_W, LANES))
            toep = pltpu.roll(row, 0, 1, stride=1, stride_axis=0)
            bias_ref[i] = jnp.where(in_win, toep * LOG2E, NEG)
            return carry

        lax.fori_loop(0, NA_HEADS * NA_PAIR_TILES, body, 0)

    n_win = NA_KR * GRID_W
    rows = []
    for a in range(NA_ROWS_PER_STEP):
        r = j * NA_ROWS_PER_STEP + a
        rs = jnp.clip(r - NA_KR // 2, 0, GRID_ROWS - NA_KR)
        idx0 = rs - r + (NA_KR - 1)
        row0 = pl.multiple_of(rs * GRID_W, GRID_W)
        outs = []
        for t in range(NA_HEADS // 2):
            lanes = slice(t * LANES, (t + 1) * LANES)
            bias = jnp.concatenate(
                [jnp.concatenate([bias_ref[h * NA_PAIR_TILES + idx0 + 2 * m] for m in range(NA_KR // 2)], axis=1)
                 for h in (2 * t, 2 * t + 1)], axis=0)
            outs.append(_na_tile_attend(
                q_ref[a * GRID_W:(a + 1) * GRID_W, lanes],
                [kl_ref[pl.ds(row0, n_win), lanes], kc_ref[:, lanes]],
                [vl_ref[pl.ds(row0, n_win), lanes], vc_ref[:, lanes]], bias))
        rows.append(jnp.concatenate(outs, axis=1))
    o_ref[...] = jnp.concatenate(rows, axis=0).astype(BF16)


def _lat_attn_kernel(lam_init, q_ref, kl_ref, vl_ref, kc_ref, vc_ref, ona_ref, vec_ref, mix_ref):
    o_mla = _mla_attend(q_ref[:, P_MLA:P_MLA + W_MLA_P],
                        [kc_ref[:, P_MLA:P_MLA + W_MLA_P], kl_ref[:, P_MLA:P_MLA + W_MLA_P]],
                        [vc_ref[:, PV_MLA:PV_MLA + MLA_W], vl_ref[:, PV_MLA:PV_MLA + MLA_W]])
    lam = _diff_lambda(vec_ref, lam_init)
    o_df = _df_attend(q_ref[:, P_DF:P_DF + DF_W],
                      [kc_ref[:, P_DF:P_DF + DF_W], kl_ref[:, P_DF:P_DF + DF_W]],
                      [vc_ref[:, PV_DF:PV_DF + DF_W], vl_ref[:, PV_DF:PV_DF + DF_W]],
                      lam, vec_ref[V_GDS:V_GDS + 1, 0:DF_W], 1.0 - lam_init)
    mix_ref[...] = jnp.concatenate([o_mla.astype(BF16), ona_ref[...], o_df.astype(BF16)], axis=1)


def _const_spec(shape):
    nd = len(shape)
    return pl.BlockSpec(shape, lambda *_: (0,) * nd, pipeline_mode=pl.Buffered(1))


def _layer_spec(l, shape):
    nd = len(shape)
    return pl.BlockSpec((None,) + tuple(shape), lambda *_: (l,) + (0,) * nd, pipeline_mode=pl.Buffered(1))


def _params(n_axes):
    return pltpu.CompilerParams(dimension_semantics=("arbitrary",) * n_axes, vmem_limit_bytes=VMEM_LIMIT)


def _block_diag(group):
    i = np.arange(MXU_DIM) // group
    return jnp.asarray((i[:, None] == i[None, :]).astype(np.float32), dtype=BF16)


def _rope_tables():
    t = np.arange(DEC_SEQ)
    row = (t // GRID_W).astype(np.float64)
    col = (t % GRID_W).astype(np.float64)
    n = MLA_ROPE // 4
    inv = 1.0 / (ROPE_BASE ** (np.arange(n, dtype=np.float64) * 2.0 / (MLA_ROPE // 2)))
    ar = row[:, None] * inv
    ac = col[:, None] * inv
    ang = np.concatenate([ar, ar, ac, ac], axis=-1)
    cos32, sin32 = np.cos(ang), np.sin(ang)
    first = (np.arange(MLA_ROPE) % (2 * n)) < n
    sa32 = np.where(first, -sin32, 0.0)
    sb32 = np.where(first, 0.0, sin32)

    def mla_tile(v32, fill):
        out = np.full((DEC_SEQ, LANES), fill)
        out[:, MLA_NOPE:MLA_QK] = v32
        return out

    tabs = [mla_tile(cos32, 1.0), mla_tile(sa32, 0.0), mla_tile(sb32, 0.0),
            np.tile(cos32, (1, LANES // DF_QK)), np.tile(sa32, (1, LANES // DF_QK)), np.tile(sb32, (1, LANES // DF_QK))]
    return jnp.asarray(np.stack(tabs).astype(np.float32))


def _tables(p):
    w_in = p['w_in']
    kr = w_in[:, :, Q_LORA + KV_LORA:Q_LORA + KV_LORA + MLA_ROPE]
    z32 = jnp.zeros_like(kr)
    w_in_r = jnp.concatenate(
        [w_in[:, :, :Q_LORA + KV_LORA], w_in[:, :, Q_LORA + KV_LORA + MLA_ROPE:], kr, z32, kr, z32], axis=2).astype(BF16)
    w_uq = p['w_uq'].reshape(DEPTH, Q_LORA, MLA_HEADS, MLA_QK)
    w_uq_p = jnp.pad(w_uq, ((0, 0), (0, 0), (0, 0), (0, MLA_PAD - MLA_QK))).reshape(DEPTH, Q_LORA, W_MLA_P).astype(BF16)
    w_ukv = p['w_ukv'].reshape(DEPTH, KV_LORA, MLA_HEADS, MLA_NOPE + MLA_V)
    wk = jnp.pad(w_ukv[..., :MLA_NOPE], ((0, 0), (0, 0), (0, 0), (0, MLA_PAD - MLA_NOPE))).reshape(DEPTH, KV_LORA, W_MLA_P)
    wv = w_ukv[..., MLA_NOPE:].reshape(DEPTH, KV_LORA, MLA_W)
    w_ukv_r = jnp.concatenate([wk, wv], axis=2).astype(BF16)

    def row(v, reps=1):
        v = jnp.tile(v, (1, reps)) if reps > 1 else v
        return [v, jnp.zeros((DEPTH, D_MODEL - v.shape[1]), F32)] if v.shape[1] < D_MODEL else [v]

    pad_head = lambda g: jnp.pad(g, ((0, 0), (0, MLA_PAD - MLA_QK)))
    pieces = (row(p['g_mix']) + row(p['g_ffn']) + row(p['g_qa']) + row(p['g_kva'])
              + row(pad_head(p['g_mla_q']), MLA_HEADS) + row(pad_head(p['g_mla_k']), MLA_HEADS)
              + row(p['g_na_q'], NA_HEADS) + row(p['g_na_k'], NA_HEADS)
              + row(p['g_df_q'], 2 * DF_HEADS) + row(p['g_df_k'], 2 * DF_HEADS) + row(p['g_df_sub'], DF_HEADS)
              + row(p['df_lq1']) + row(p['df_lk1']) + row(p['df_lq2']) + row(p['df_lk2'])
              + [jnp.zeros((DEPTH, D_MODEL), F32)])
    vec = jnp.concatenate(pieces, axis=1).reshape(DEPTH, N_VEC, D_MODEL)
    f = p['na_rpb']
    zpad = jnp.zeros((DEPTH, NA_HEADS, NA_PAIR_TILES, 33), F32)
    g_rows = jnp.concatenate([f[:, :, :-1, NA_KW - 1:], zpad, f[:, :, 1:, :], zpad, f[:, :, :-1, :NA_KW - 1]], axis=-1)
    g_rows = g_rows.reshape(DEPTH, NA_HEADS * NA_PAIR_TILES, LANES)
    return dict(w_in=w_in_r, w_uq=w_uq_p, w_ukv=w_ukv_r, vec=vec, g_rows=g_rows,
                w_out=p['w_out'].astype(BF16), w_gate=p['w_gate'].astype(BF16),
                w_up=p['w_up'].astype(BF16), w_down=p['w_down'].astype(BF16))


def _modulation(c_all, w_mod, b_mod):
    tn = 1024
    return pl.pallas_call(
        _mod_kernel,
        grid=(DEPTH, 6 * D_MODEL // tn),
        in_specs=[pl.BlockSpec((N_MOD, D_MODEL), lambda l, j: (0, 0)),
                  pl.BlockSpec((None, D_MODEL, tn), lambda l, j: (l, 0, j)),
                  pl.BlockSpec((None, 1, tn), lambda l, j: (l, 0, j))],
        out_specs=pl.BlockSpec((None, N_MOD, tn), lambda l, j: (l, 0, j)),
        out_shape=jax.ShapeDtypeStruct((DEPTH, N_MOD, 6 * D_MODEL), F32),
        compiler_params=_params(2),
        name="modulation",
    )(c_all, w_mod, b_mod.reshape(DEPTH, 1, 6 * D_MODEL))


def _front_weight_specs(l):
    return [_layer_spec(l, (N_VEC, D_MODEL)), _layer_spec(l, (D_MODEL, IN_COLS_R)), _layer_spec(l, (Q_LORA, W_MLA_P)),
            _layer_spec(l, (KV_LORA, W_MLA_P + MLA_W)),
            _const_spec((MXU_DIM, MXU_DIM)), _const_spec((MXU_DIM, MXU_DIM)), _const_spec((MXU_DIM, MXU_DIM))]


def _context_mixer(l, lam_init, x, mods, t, consts, prev_caches):
    n_tok = BATCH * SEQ
    n_alias = len(prev_caches)
    tok = lambda w: pl.BlockSpec((CTX_SEQS_PER_STEP * SEQ, w), lambda b: (b, 0))
    lay = lambda *s: pl.BlockSpec((CTX_SEQS_PER_STEP, None) + s, lambda b: (b, l) + (0,) * len(s))
    cache_shapes = [(SEQ, KV_LORA), (SEQ, MLA_ROPE), (NA_HEADS, SEQ, NA_HD), (NA_HEADS, SEQ, NA_HD),
                    (DF_HEADS, SEQ, DF_HD), (DF_HEADS, SEQ, DF_HD)]
    n_in = 2 + 7
    return pl.pallas_call(
        functools.partial(_ctx_kernel, lam_init, n_alias),
        grid=(BATCH // CTX_SEQS_PER_STEP,),
        in_specs=[tok(D_MODEL), pl.BlockSpec((None, 1, 6 * D_MODEL), lambda b: (l * N_MOD, 0, 0))]
        + _front_weight_specs(l) + [pl.BlockSpec(memory_space=pl.ANY)] * n_alias,
        out_specs=[tok(D_MODEL)] + [lay(*s) for s in cache_shapes],
        out_shape=[jax.ShapeDtypeStruct((n_tok, D_MODEL), BF16)]
        + [jax.ShapeDtypeStruct((BATCH, DEPTH) + s, F32) for s in cache_shapes],
        input_output_aliases={n_in + i: 1 + i for i in range(n_alias)},
        compiler_params=_params(1),
        name="context_mixer",
    )(x, mods, t['vec'], t['w_in'], t['w_uq'], t['w_ukv'], consts['bd128'], consts['bd64'], consts['bd32'], *prev_caches)


def _finish(l, x, mix, mods, t, mod_row):
    n_tok = x.shape[0]
    tok = pl.BlockSpec((TM_FINISH, D_MODEL), lambda i: (i, 0))
    return pl.pallas_call(
        _finish_kernel,
        grid=(n_tok // TM_FINISH,),
        in_specs=[tok, tok, pl.BlockSpec((None, 1, 6 * D_MODEL), lambda i: (l * N_MOD + mod_row(i), 0, 0)),
                  _layer_spec(l, (N_VEC, D_MODEL)), _layer_spec(l, (D_MODEL, D_MODEL)), _layer_spec(l, (D_MODEL, D_FF)),
                  _layer_spec(l, (D_MODEL, D_FF)), _layer_spec(l, (D_FF, D_MODEL))],
        out_specs=tok,
        out_shape=jax.ShapeDtypeStruct((n_tok, D_MODEL), F32),
        compiler_params=_params(1),
        name="finish",
    )(x, mix, mods, t['vec'], t['w_out'], t['w_gate'], t['w_up'], t['w_down'])


def _latent_front(l, x, mods, t, consts):
    n_tok = DEC_BATCH * DEC_SEQ
    blocks_per_seq = DEC_SEQ // TM
    tok = lambda w: pl.BlockSpec((TM, w), lambda i: (i, 0))
    weights = _front_weight_specs(l)
    return pl.pallas_call(
        _lat_front_kernel,
        grid=(n_tok // TM,),
        in_specs=[tok(D_MODEL),
                  pl.BlockSpec((None, 1, 6 * D_MODEL), lambda i: (l * N_MOD + 1 + i // blocks_per_seq, 0, 0)),
                  weights[0], pl.BlockSpec((6, TM, LANES), lambda i: (0, i % blocks_per_seq, 0))] + weights[1:],
        out_specs=[tok(QK_PACK), tok(QK_PACK), tok(V_PACK)],
        out_shape=[jax.ShapeDtypeStruct((n_tok, QK_PACK), BF16), jax.ShapeDtypeStruct((n_tok, QK_PACK), BF16),
                   jax.ShapeDtypeStruct((n_tok, V_PACK), BF16)],
        compiler_params=_params(1),
        name="latent_front",
    )(x, mods, t['vec'], consts['rope'], t['w_in'], t['w_uq'], t['w_ukv'], consts['bd128'], consts['bd64'], consts['bd32'])


def _context_kv(l, caches, t, consts):
    lay = lambda w: pl.BlockSpec((None, None, PAST_LEN, w), lambda b: (b, l, 0, 0))
    hm = lambda h, w: pl.BlockSpec((None, None, h, PAST_LEN, w), lambda b: (b, l, 0, 0, 0))
    out = lambda w: pl.BlockSpec((None, PAST_LEN, w), lambda b: (b, 0, 0))
    return pl.pallas_call(
        _ctx_kv_kernel,
        grid=(DEC_BATCH,),
        in_specs=[lay(KV_LORA), lay(MLA_ROPE), hm(NA_HEADS, NA_HD), hm(NA_HEADS, NA_HD), hm(DF_HEADS, 2 * DF_QK),
                  hm(DF_HEADS, DF_HD),
                  _layer_spec(l, (N_VEC, D_MODEL)), _layer_spec(l, (KV_LORA, W_MLA_P + MLA_W)),
                  _const_spec((MXU_DIM, MXU_DIM))],
        out_specs=[out(QK_PACK), out(V_PACK)],
        out_shape=[jax.ShapeDtypeStruct((DEC_BATCH, PAST_LEN, QK_PACK), BF16),
                   jax.ShapeDtypeStruct((DEC_BATCH, PAST_LEN, V_PACK), BF16)],
        compiler_params=_params(1),
        name="context_kv",
    )(*caches, t['vec'], t['w_ukv'], consts['bd128'])


def _latent_na(l, q, k, v, kc, vc, g_rows):
    n_tok = DEC_BATCH * DEC_SEQ
    na_blk = P_NA // NA_W
    steps = GRID_ROWS // NA_ROWS_PER_STEP
    tq = NA_ROWS_PER_STEP * GRID_W
    return pl.pallas_call(
        _na_lat_kernel,
        grid=(DEC_BATCH, steps),
        in_specs=[pl.BlockSpec((tq, NA_W), lambda b, j: (b * steps + j, na_blk)),
                  pl.BlockSpec((DEC_SEQ, NA_W), lambda b, j: (b, na_blk)),
                  pl.BlockSpec((DEC_SEQ, NA_W), lambda b, j: (b, PV_NA // NA_W)),
                  pl.BlockSpec((None, PAST_LEN, NA_W), lambda b, j: (b, 0, na_blk)),
                  pl.BlockSpec((None, PAST_LEN, NA_W), lambda b, j: (b, 0, PV_NA // NA_W)),
                  _layer_spec(l, (NA_HEADS * NA_PAIR_TILES, LANES))],
        out_specs=pl.BlockSpec((tq, NA_W), lambda b, j: (b * steps + j, 0)),
        out_shape=jax.ShapeDtypeStruct((n_tok, NA_W), BF16),
        scratch_shapes=[pltpu.VMEM((NA_HEADS * NA_PAIR_TILES, GRID_W, LANES), F32)],
        compiler_params=_params(2),
        name="latent_neighbourhood",
    )(q, k, v, kc, vc, g_rows)


def _latent_attention(l, lam_init, q, k, v, kc, vc, o_na, t):
    n_tok = DEC_BATCH * DEC_SEQ
    nq = DEC_SEQ // TM
    return pl.pallas_call(
        functools.partial(_lat_attn_kernel, lam_init),
        grid=(DEC_BATCH, nq),
        in_specs=[pl.BlockSpec((TM, QK_PACK), lambda b, j: (b * nq + j, 0)),
                  pl.BlockSpec((DEC_SEQ, QK_PACK), lambda b, j: (b, 0)),
                  pl.BlockSpec((DEC_SEQ, V_PACK), lambda b, j: (b, 0)),
                  pl.BlockSpec((None, PAST_LEN, QK_PACK), lambda b, j: (b, 0, 0)),
                  pl.BlockSpec((None, PAST_LEN, V_PACK), lambda b, j: (b, 0, 0)),
                  pl.BlockSpec((TM, NA_W), lambda b, j: (b * nq + j, 0)),
                  _layer_spec(l, (N_VEC, D_MODEL))],
        out_specs=pl.BlockSpec((TM, D_MODEL), lambda b, j: (b * nq + j, 0)),
        out_shape=jax.ShapeDtypeStruct((n_tok, D_MODEL), BF16),
        compiler_params=_params(2),
        name="latent_attention",
    )(q, k, v, kc, vc, o_na, t['vec'])


def kernel(x_prompt, x_sample, cache_mla_ckv, cache_mla_krope, cache_na_k, cache_na_v, cache_df_k, cache_df_v, c, c_ctx, w_mod, b_mod, g_mix, w_in, g_qa, w_uq, g_kva, w_ukv, g_mla_q, g_mla_k, g_na_q, g_na_k, na_rpb, g_df_q, g_df_k, df_lq1, df_lk1, df_lq2, df_lk2, g_df_sub, w_out, g_ffn, w_gate, w_up, w_down):
    p = dict(g_mix=g_mix, w_in=w_in, g_qa=g_qa, w_uq=w_uq, g_kva=g_kva, w_ukv=w_ukv, g_mla_q=g_mla_q, g_mla_k=g_mla_k,
             g_na_q=g_na_q, g_na_k=g_na_k, na_rpb=na_rpb, g_df_q=g_df_q, g_df_k=g_df_k, df_lq1=df_lq1, df_lk1=df_lk1,
             df_lq2=df_lq2, df_lk2=df_lk2, g_df_sub=g_df_sub, w_out=w_out, g_ffn=g_ffn, w_gate=w_gate, w_up=w_up,
             w_down=w_down)
    consts = dict(bd128=_block_diag(MLA_PAD), bd64=_block_diag(NA_HD), bd32=_block_diag(DF_QK), rope=_rope_tables())
    t = _tables(p)

    c_all = jnp.concatenate([c_ctx[None, :], c, jnp.zeros((N_MOD - 1 - DEC_BATCH, D_MODEL), F32)], axis=0)
    mods = _modulation(c_all, w_mod, b_mod).reshape(DEPTH * N_MOD, 1, 6 * D_MODEL)

    caches = (cache_mla_ckv, cache_mla_krope, cache_na_k, cache_na_v, cache_df_k, cache_df_v)

    xp = x_prompt.reshape(BATCH * SEQ, D_MODEL)
    xs = x_sample.reshape(DEC_BATCH * DEC_SEQ, D_MODEL)
    new_caches = ()
    for l in range(DEPTH):
        lam_init = 0.8 - 0.6 * math.exp(-0.3 * l)
        mix, *new_caches = _context_mixer(l, lam_init, xp, mods, t, consts, new_caches)
        xp = _finish(l, xp, mix, mods, t, lambda i: 0)
        q, k, v = _latent_front(l, xs, mods, t, consts)
        kc, vc = _context_kv(l, caches, t, consts)
        o_na = _latent_na(l, q, k, v, kc, vc, t['g_rows'])
        mix_s = _latent_attention(l, lam_init, q, k, v, kc, vc, o_na, t)
        xs = _finish(l, xs, mix_s, mods, t, lambda i: 1 + i // (DEC_SEQ // TM_FINISH))
    return (xp.reshape(BATCH, SEQ, D_MODEL), xs.reshape(DEC_BATCH, DEC_SEQ, D_MODEL), *new_caches)
```

```python
import functools
import math

import numpy as np
import jax
import jax.numpy as jnp
from jax import lax
from jax.experimental import pallas as pl
from jax.experimental.pallas import tpu as pltpu

F32 = jnp.float32
BF16 = jnp.bfloat16

D_MODEL = 1024
BATCH = 32
SEQ = 256
DEPTH = 2
DEC_BATCH = 2
DEC_SEQ = 1024
PAST_LEN = 256
GRID_W = 64
GRID_ROWS = DEC_SEQ // GRID_W
MLA_HEADS = 6
MLA_NOPE = 64
MLA_ROPE = 32
MLA_QK = MLA_NOPE + MLA_ROPE
MLA_V = 64
MLA_PAD = 128
Q_LORA = 256
KV_LORA = 128
NA_HEADS = 6
NA_HD = 64
NA_KR = 8
NA_KW = 16
DF_HEADS = 4
DF_HD = 64
DF_QK = 32
MLA_W = MLA_HEADS * MLA_V
NA_W = NA_HEADS * NA_HD
DF_W = DF_HEADS * DF_HD
D_FF = -(-8 * D_MODEL // (3 * 256)) * 256
ROPE_BASE = 10000.0
EPS = 1e-6
NEG = -1e30
LOG2E = math.log2(math.e)

LANES = 128
MXU_DIM = 256

A_CQ = 0
A_CKV = A_CQ + Q_LORA
A_KR = A_CKV + KV_LORA
A_COLS = A_KR + LANES
B_SRC = Q_LORA + KV_LORA + MLA_ROPE
B_NAQ = 0
B_NAK = B_NAQ + NA_W
B_NAV = B_NAK + NA_W
B_DFQ = B_NAV + NA_W
B_DFK = B_DFQ + DF_W
B_DFV = B_DFK + DF_W
B_COLS = B_DFV + DF_W

W_MLA_P = MLA_HEADS * MLA_PAD
QK_PACK = W_MLA_P + NA_W + DF_W
V_PACK = MLA_W + NA_W + DF_W
P_MLA = 0
P_NA = W_MLA_P
P_DF = P_NA + NA_W
PV_MLA = 0
PV_NA = MLA_W
PV_DF = MLA_W + NA_W

(V_GMIX, V_GFFN, V_GQA, V_GKVA, V_GMQ, V_GMK, V_GNQ, V_GNK, V_GDQ, V_GDK, V_GDS,
 V_LQ1, V_LK1, V_LQ2, V_LK2) = range(15)
N_VEC = 16

N_MOD = 8
TM = 256
TM_FINISH = 512
CTX_SEQS_PER_STEP = 2
NA_ROWS_PER_STEP = 4
VMEM_LIMIT = 56 * 1024 * 1024

NA_PAIR_TILES = 2 * NA_KR - 2


def _dot(a, b):
    return jnp.dot(a, b, preferred_element_type=F32)


def _dot_nt(a, b):
    return lax.dot_general(a, b, (((1,), (1,)), ((), ())), preferred_element_type=F32)


def _lane_iota(shape):
    return lax.broadcasted_iota(jnp.int32, shape, len(shape) - 1)


def _rms_rows(x, g):
    ms = jnp.mean(x * x, axis=-1, keepdims=True)
    return x * lax.rsqrt(ms + EPS) * g


def _tile_rms(x, g, n_real):
    outs = []
    for c0 in range(0, x.shape[1], LANES):
        xt = x[:, c0:c0 + LANES]
        ms = jnp.sum(xt * xt, axis=-1, keepdims=True) * (1.0 / n_real)
        outs.append(xt * lax.rsqrt(ms + EPS) * g[:, c0:c0 + LANES])
    return jnp.concatenate(outs, axis=1)


def _seg_rms(x, bd_ref, g, group):
    width = x.shape[1]
    sq = (x * x).astype(BF16)
    parts = []
    for c0 in range(0, width, MXU_DIM):
        w = min(MXU_DIM, width - c0)
        parts.append(_dot(sq[:, c0:c0 + w], bd_ref[0:w, 0:w]))
    ss = parts[0] if len(parts) == 1 else jnp.concatenate(parts, axis=1)
    return x * lax.rsqrt(ss * (1.0 / group) + EPS) * g


def _rope_tiles(x, cos, sa, sb):
    outs = []
    for t in range(x.shape[1] // LANES):
        xt = x[:, t * LANES:(t + 1) * LANES]
        up = pltpu.roll(xt, LANES - MLA_ROPE // 4, 1)
        dn = pltpu.roll(xt, MLA_ROPE // 4, 1)
        outs.append(xt * cos + up * sa + dn * sb)
    return outs[0] if len(outs) == 1 else jnp.concatenate(outs, axis=1)


def _diff_lambda(vec_ref, lam_init):
    a = jnp.sum(vec_ref[V_LQ1:V_LQ1 + 1, 0:DF_QK] * vec_ref[V_LK1:V_LK1 + 1, 0:DF_QK], axis=-1, keepdims=True)
    b = jnp.sum(vec_ref[V_LQ2:V_LQ2 + 1, 0:DF_QK] * vec_ref[V_LK2:V_LK2 + 1, 0:DF_QK], axis=-1, keepdims=True)
    return jnp.exp(a) - jnp.exp(b) + lam_init


def _mixer_front(x, mod, vec_ref, wa_ref, wb_ref, w_uq_ref, bd64_ref, bd32_ref):
    sh = mod[:, 0:D_MODEL]
    sc = mod[:, D_MODEL:2 * D_MODEL]
    h = (_rms_rows(x, vec_ref[V_GMIX:V_GMIX + 1, :]) * (1.0 + sc) + sh).astype(BF16)
    za = _dot_nt(h, wa_ref[...])
    zb = _dot_nt(h, wb_ref[...])
    cqn = _rms_rows(za[:, A_CQ:A_CQ + Q_LORA], vec_ref[V_GQA:V_GQA + 1, 0:Q_LORA])
    q_raw = _dot_nt(cqn.astype(BF16), w_uq_ref[...])
    q_mla = _tile_rms(q_raw, vec_ref[V_GMQ:V_GMQ + 1, 0:W_MLA_P] * (MLA_QK ** -0.5 * LOG2E), MLA_QK)
    ckv_n = _rms_rows(za[:, A_CKV:A_CKV + KV_LORA], vec_ref[V_GKVA:V_GKVA + 1, 0:KV_LORA])
    kr_tile = za[:, A_KR:A_KR + LANES]
    q_na = _seg_rms(zb[:, B_NAQ:B_NAQ + NA_W], bd64_ref, vec_ref[V_GNQ:V_GNQ + 1, 0:NA_W] * (NA_HD ** -0.5 * LOG2E), NA_HD)
    k_na = _seg_rms(zb[:, B_NAK:B_NAK + NA_W], bd64_ref, vec_ref[V_GNK:V_GNK + 1, 0:NA_W], NA_HD)
    v_na = zb[:, B_NAV:B_NAV + NA_W]
    q_df = _seg_rms(zb[:, B_DFQ:B_DFQ + DF_W], bd32_ref, vec_ref[V_GDQ:V_GDQ + 1, 0:DF_W] * (DF_QK ** -0.5 * LOG2E), DF_QK)
    k_df = _seg_rms(zb[:, B_DFK:B_DFK + DF_W], bd32_ref, vec_ref[V_GDK:V_GDK + 1, 0:DF_W], DF_QK)
    v_df = zb[:, B_DFV:B_DFV + DF_W]
    return q_mla, ckv_n, kr_tile, q_na, k_na, v_na, q_df, k_df, v_df


def _mla_kv(ckv_n, kr_tile, vec_ref, w_ukv_ref):
    kv = _dot(ckv_n.astype(BF16), w_ukv_ref[...])
    lane = _lane_iota((1, LANES))
    kr = jnp.where((lane >= MLA_NOPE) & (lane < MLA_QK), kr_tile, 0.0)
    k_pre = kv[:, 0:W_MLA_P] + jnp.concatenate([kr] * MLA_HEADS, axis=1)
    k = _tile_rms(k_pre, vec_ref[V_GMK:V_GMK + 1, 0:W_MLA_P], MLA_QK)
    return k, kv[:, W_MLA_P:W_MLA_P + MLA_W]


def _softmax_parts(s):
    m = jnp.max(s, axis=-1, keepdims=True)
    p = jnp.exp2(s - m)
    return p, 1.0 / jnp.sum(p, axis=-1, keepdims=True)


def _scores(q, k_segs):
    parts = [_dot(q, k) if feature_major else _dot_nt(q, k) for k, feature_major in k_segs]
    return parts[0] if len(parts) == 1 else jnp.concatenate(parts, axis=1)


def _pv(p, v_segs):
    out = None
    c0 = 0
    for v, feature_major in v_segs:
        n = v.shape[1] if feature_major else v.shape[0]
        o = _dot_nt(p[:, c0:c0 + n], v) if feature_major else _dot(p[:, c0:c0 + n], v)
        out = o if out is None else out + o
        c0 += n
    return out


def _seg_tile(seg, t):
    a, feature_major = seg
    return (a[t * LANES:(t + 1) * LANES, :] if feature_major else a[:, t * LANES:(t + 1) * LANES]), feature_major


def _lane_groups(qt, width):
    lane = _lane_iota((1, LANES))
    zero = jnp.zeros_like(qt)
    return jnp.concatenate(
        [jnp.where((lane >= g * width) & (lane < (g + 1) * width), qt, zero) for g in range(LANES // width)], axis=0)


def _pair_select(o2):
    tq = o2.shape[0] // 2
    return jnp.where(_lane_iota((1, LANES)) < NA_HD, o2[0:tq], o2[tq:2 * tq])


def _mla_attend(q, k_segs, v_segs):
    outs = []
    for t in range(MLA_HEADS // 2):
        vt = [_seg_tile(v, t) for v in v_segs]
        halves = []
        for h in (2 * t, 2 * t + 1):
            p, il = _softmax_parts(_scores(q[:, h * MLA_PAD:(h + 1) * MLA_PAD], [_seg_tile(k, h) for k in k_segs]))
            halves.append(_pv(p.astype(BF16), vt) * il)
        outs.append(jnp.where(_lane_iota((1, LANES)) < MLA_V, halves[0], halves[1]))
    return jnp.concatenate(outs, axis=1)


def _na_tile_attend(qt, k_segs, v_segs, bias=None):
    s = _scores(_lane_groups(qt, NA_HD), k_segs)
    if bias is not None:
        nb = bias.shape[1]
        s = jnp.concatenate([s[:, 0:nb] + bias, s[:, nb:]], axis=1)
    p, il = _softmax_parts(s)
    return _pair_select(_pv(p.astype(BF16), v_segs) * il)


def _na_attend_full(q, k_segs, v_segs):
    return jnp.concatenate(
        [_na_tile_attend(q[:, t * LANES:(t + 1) * LANES], [_seg_tile(k, t) for k in k_segs],
                         [_seg_tile(v, t) for v in v_segs]) for t in range(NA_HEADS // 2)], axis=1)


def _df_attend(q, k_segs, v_segs, lam, g_sub, out_scale):
    outs = []
    lane = _lane_iota((1, LANES))
    tq = q.shape[0]
    for t in range(DF_HEADS // 2):
        kt = [_seg_tile(k, t) for k in k_segs]
        vt = [_seg_tile(v, t) for v in v_segs]
        p, il = _softmax_parts(_scores(_lane_groups(q[:, t * LANES:(t + 1) * LANES], DF_QK), kt))
        pn = []
        for hh in range(2):
            r1, r2 = 2 * hh * tq, (2 * hh + 1) * tq
            pn.append((p[r1:r1 + tq] * il[r1:r1 + tq] - p[r2:r2 + tq] * (lam * il[r2:r2 + tq])).astype(BF16))
        o = _pair_select(_pv(jnp.concatenate(pn, axis=0), vt))
        o2 = o * o
        ms_e = jnp.sum(jnp.where(lane < DF_HD, o2, 0.0), axis=-1, keepdims=True)
        ms_o = jnp.sum(jnp.where(lane >= DF_HD, o2, 0.0), axis=-1, keepdims=True)
        r = lax.rsqrt(jnp.where(lane < DF_HD, ms_e, ms_o) * (1.0 / DF_HD) + EPS)
        outs.append(o * r * (g_sub[:, t * LANES:(t + 1) * LANES] * out_scale))
    return jnp.concatenate(outs, axis=1)


def _mod_kernel(c_ref, w_ref, b_ref, o_ref):
    c = c_ref[...]
    s = c * jax.nn.sigmoid(c)
    o_ref[...] = _dot(s.astype(BF16), w_ref[...].astype(BF16)) + b_ref[...]


def _ctx_kernel(lam_init, n_alias, x_ref, mod_ref, vec_ref, wa_ref, wb_ref, w_uq_ref, w_ukv_ref, bd64_ref, bd32_ref,
                *rest):
    mix_ref, *cache_refs = rest[n_alias:]
    if n_alias == 0:
        for ref in cache_refs:
            ref[:, 1:] = jnp.zeros((ref.shape[0], ref.shape[1] - 1) + ref.shape[2:], F32)
        cache_refs = [ref.at[:, 0] for ref in cache_refs]
    ckv_ref, kr_ref, nak_ref, nav_ref, dfk_ref, dfv_ref = cache_refs
    q_mla, ckv_n, kr_tile, q_na, k_na, v_na, q_df, k_df, v_df = _mixer_front(
        x_ref[...], mod_ref[...], vec_ref, wa_ref, wb_ref, w_uq_ref, bd64_ref, bd32_ref)
    k_mla, v_mla = _mla_kv(ckv_n, kr_tile, vec_ref, w_ukv_ref)
    lam = _diff_lambda(vec_ref, lam_init)
    bf = lambda a, rows: (a[rows].astype(BF16), False)
    for s in range(CTX_SEQS_PER_STEP):
        rows = slice(s * SEQ, (s + 1) * SEQ)
        ckv_ref[s] = ckv_n[rows]
        kr_ref[s] = kr_tile[rows].T[0:MLA_ROPE]
        for ref, a in ((nak_ref, k_na), (nav_ref, v_na), (dfk_ref, k_df), (dfv_ref, v_df)):
            at = a[rows].T
            for h in range(ref.shape[1]):
                ref[s, h] = at[h * NA_HD:(h + 1) * NA_HD]
        o_mla = _mla_attend(bf(q_mla, rows)[0], [bf(k_mla, rows)], [bf(v_mla, rows)])
        o_na = _na_attend_full(bf(q_na, rows)[0], [bf(k_na, rows)], [bf(v_na, rows)])
        o_df = _df_attend(bf(q_df, rows)[0], [bf(k_df, rows)], [bf(v_df, rows)], lam,
                          vec_ref[V_GDS:V_GDS + 1, 0:DF_W], 1.0 - lam_init)
        mix_ref[rows, :] = jnp.concatenate([o_mla, o_na, o_df], axis=1).astype(BF16)


def _finish_kernel(x_ref, mix_ref, mod_ref, vec_ref, w_out_ref, w_gate_ref, w_up_ref, w_down_ref, y_ref):
    mod = mod_ref[...]
    gate_m = mod[:, 2 * D_MODEL:3 * D_MODEL]
    sh = mod[:, 3 * D_MODEL:4 * D_MODEL]
    sc = mod[:, 4 * D_MODEL:5 * D_MODEL]
    gate_f = mod[:, 5 * D_MODEL:6 * D_MODEL]
    x1 = x_ref[...] + gate_m * _dot(mix_ref[...], w_out_ref[...])
    h = (_rms_rows(x1, vec_ref[V_GFFN:V_GFFN + 1, :]) * (1.0 + sc) + sh).astype(BF16)
    g = _dot(h, w_gate_ref[...])
    u = _dot(h, w_up_ref[...])
    a = (g * jax.nn.sigmoid(g) * u).astype(BF16)
    y_ref[...] = x1 + gate_f * _dot(a, w_down_ref[...])


def _lat_front_kernel(x_ref, mod_ref, vec_ref, rope_ref, wa_ref, wb_ref, w_uq_ref, w_ukv_ref, bd64_ref, bd32_ref,
                      q_ref, k_ref, v_ref):
    q_mla, ckv_n, kr_tile, q_na, k_na, v_na, q_df, k_df, v_df = _mixer_front(
        x_ref[...], mod_ref[...], vec_ref, wa_ref, wb_ref, w_uq_ref, bd64_ref, bd32_ref)
    k_mla, v_mla = _mla_kv(ckv_n, kr_tile, vec_ref, w_ukv_ref)
    cm, sam, sbm = rope_ref[0], rope_ref[1], rope_ref[2]
    cd, sad, sbd = rope_ref[3], rope_ref[4], rope_ref[5]
    q_ref[...] = jnp.concatenate(
        [_rope_tiles(q_mla, cm, sam, sbm), q_na, _rope_tiles(q_df, cd, sad, sbd)], axis=1).astype(BF16)
    k_ref[...] = jnp.concatenate(
        [_rope_tiles(k_mla, cm, sam, sbm), k_na, _rope_tiles(k_df, cd, sad, sbd)], axis=1).astype(BF16)
    v_ref[...] = jnp.concatenate([v_mla, v_na, v_df], axis=1).astype(BF16)


def _ctx_kv_kernel(ckv_ref, krt_ref, vec_ref, w_ukv_ref, k_ref, v_ref):
    krt = jnp.concatenate([jnp.zeros((MLA_NOPE, PAST_LEN), F32), krt_ref[...],
                           jnp.zeros((LANES - MLA_QK, PAST_LEN), F32)], axis=0)
    k_mla, v_mla = _mla_kv(ckv_ref[...], krt.T, vec_ref, w_ukv_ref)
    k_ref[...] = k_mla.astype(BF16)
    v_ref[...] = v_mla.astype(BF16)


def _na_lat_kernel(q_ref, kl_ref, vl_ref, kct_ref, vct_ref, g_ref, o_ref, bias_ref):
    b = pl.program_id(0)
    j = pl.program_id(1)

    @pl.when((b == 0) & (j == 0))
    def _build_bias():
        c = lax.broadcasted_iota(jnp.int32, (GRID_W, LANES), 0)
        kc = _lane_iota((GRID_W, LANES)) % GRID_W
        start = jnp.clip(c - NA_KW // 2, 0, GRID_W - NA_KW)
        in_win = (kc >= start) & (kc < start + NA_KW)

        def body(i, carry):
            row = jnp.broadcast_to(g_ref[pl.ds(i, 1), :], (GRID_W, LANES))
            toep = pltpu.roll(row, 0, 1, stride=1, stride_axis=0)
            bias_ref[i] = jnp.where(in_win, toep * LOG2E, NEG)
            return carry

        lax.fori_loop(0, NA_HEADS * NA_PAIR_TILES, body, 0)

    n_win = NA_KR * GRID_W
    kct = kct_ref[...].astype(BF16)
    vct = vct_ref[...].astype(BF16)
    rows = []
    for a in range(NA_ROWS_PER_STEP):
        r = j * NA_ROWS_PER_STEP + a
        rs = jnp.clip(r - NA_KR // 2, 0, GRID_ROWS - NA_KR)
        idx0 = rs - r + (NA_KR - 1)
        row0 = pl.multiple_of(rs * GRID_W, GRID_W)
        outs = []
        for t in range(NA_HEADS // 2):
            lanes = slice(t * LANES, (t + 1) * LANES)
            bias = jnp.concatenate(
                [jnp.concatenate([bias_ref[h * NA_PAIR_TILES + idx0 + 2 * m] for m in range(NA_KR // 2)], axis=1)
                 for h in (2 * t, 2 * t + 1)], axis=0)
            outs.append(_na_tile_attend(
                q_ref[a * GRID_W:(a + 1) * GRID_W, lanes],
                [(kl_ref[pl.ds(row0, n_win), lanes], False), (kct[lanes, :], True)],
                [(vl_ref[pl.ds(row0, n_win), lanes], False), (vct[lanes, :], True)], bias))
        rows.append(jnp.concatenate(outs, axis=1))
    o_ref[...] = jnp.concatenate(rows, axis=0).astype(BF16)


def _lat_attn_kernel(lam_init, q_ref, kl_ref, vl_ref, kc_ref, vc_ref, kct_ref, vct_ref, ona_ref, vec_ref, mix_ref):
    o_mla = _mla_attend(q_ref[:, P_MLA:P_MLA + W_MLA_P],
                        [(kc_ref[...], False), (kl_ref[:, P_MLA:P_MLA + W_MLA_P], False)],
                        [(vc_ref[...], False), (vl_ref[:, PV_MLA:PV_MLA + MLA_W], False)])
    lam = _diff_lambda(vec_ref, lam_init)
    o_df = _df_attend(q_ref[:, P_DF:P_DF + DF_W],
                      [(kct_ref[...].astype(BF16), True), (kl_ref[:, P_DF:P_DF + DF_W], False)],
                      [(vct_ref[...].astype(BF16), True), (vl_ref[:, PV_DF:PV_DF + DF_W], False)],
                      lam, vec_ref[V_GDS:V_GDS + 1, 0:DF_W], 1.0 - lam_init)
    mix_ref[...] = jnp.concatenate([o_mla.astype(BF16), ona_ref[...], o_df.astype(BF16)], axis=1)


def _const_spec(shape):
    nd = len(shape)
    return pl.BlockSpec(shape, lambda *_: (0,) * nd, pipeline_mode=pl.Buffered(1))


def _layer_spec(l, shape):
    nd = len(shape)
    return pl.BlockSpec((None,) + tuple(shape), lambda *_: (l,) + (0,) * nd, pipeline_mode=pl.Buffered(1))


def _params(n_axes):
    return pltpu.CompilerParams(dimension_semantics=("arbitrary",) * n_axes, vmem_limit_bytes=VMEM_LIMIT)


def _block_diag(group):
    i = np.arange(MXU_DIM) // group
    return jnp.asarray((i[:, None] == i[None, :]).astype(np.float32), dtype=BF16)


def _rope_tables():
    t = np.arange(DEC_SEQ)
    row = (t // GRID_W).astype(np.float64)
    col = (t % GRID_W).astype(np.float64)
    n = MLA_ROPE // 4
    inv = 1.0 / (ROPE_BASE ** (np.arange(n, dtype=np.float64) * 2.0 / (MLA_ROPE // 2)))
    ar = row[:, None] * inv
    ac = col[:, None] * inv
    ang = np.concatenate([ar, ar, ac, ac], axis=-1)
    cos32, sin32 = np.cos(ang), np.sin(ang)
    first = (np.arange(MLA_ROPE) % (2 * n)) < n
    sa32 = np.where(first, -sin32, 0.0)
    sb32 = np.where(first, 0.0, sin32)

    def mla_tile(v32, fill):
        out = np.full((DEC_SEQ, LANES), fill)
        out[:, MLA_NOPE:MLA_QK] = v32
        return out

    tabs = [mla_tile(cos32, 1.0), mla_tile(sa32, 0.0), mla_tile(sb32, 0.0),
            np.tile(cos32, (1, LANES // DF_QK)), np.tile(sa32, (1, LANES // DF_QK)), np.tile(sb32, (1, LANES // DF_QK))]
    return jnp.asarray(np.stack(tabs).astype(np.float32))


def _feature_major(a):
    a = jnp.swapaxes(a, -1, -2)
    return a.reshape(a.shape[:-3] + (a.shape[-3] * a.shape[-2], a.shape[-1]))


def _tables(p):
    w_in_t = jnp.swapaxes(p['w_in'], 1, 2)
    kr = w_in_t[:, Q_LORA + KV_LORA:B_SRC]
    z32 = jnp.zeros_like(kr)
    wa = jnp.concatenate([w_in_t[:, :Q_LORA + KV_LORA], kr, z32, kr, z32], axis=1).astype(BF16)
    wb = w_in_t[:, B_SRC:].astype(BF16)
    w_uq_t = jnp.swapaxes(p['w_uq'], 1, 2).reshape(DEPTH, MLA_HEADS, MLA_QK, Q_LORA)
    w_uq_p = jnp.pad(w_uq_t, ((0, 0), (0, 0), (0, MLA_PAD - MLA_QK), (0, 0))).reshape(DEPTH, W_MLA_P, Q_LORA).astype(BF16)
    w_ukv = p['w_ukv'].reshape(DEPTH, KV_LORA, MLA_HEADS, MLA_NOPE + MLA_V)
    wk = jnp.pad(w_ukv[..., :MLA_NOPE], ((0, 0), (0, 0), (0, 0), (0, MLA_PAD - MLA_NOPE))).reshape(DEPTH, KV_LORA, W_MLA_P)
    wv = w_ukv[..., MLA_NOPE:].reshape(DEPTH, KV_LORA, MLA_W)
    w_ukv_r = jnp.concatenate([wk, wv], axis=2).astype(BF16)

    def row(v, reps=1):
        v = jnp.tile(v, (1, reps)) if reps > 1 else v
        return [v, jnp.zeros((DEPTH, D_MODEL - v.shape[1]), F32)] if v.shape[1] < D_MODEL else [v]

    pad_head = lambda g: jnp.pad(g, ((0, 0), (0, MLA_PAD - MLA_QK)))
    pieces = (row(p['g_mix']) + row(p['g_ffn']) + row(p['g_qa']) + row(p['g_kva'])
              + row(pad_head(p['g_mla_q']), MLA_HEADS) + row(pad_head(p['g_mla_k']), MLA_HEADS)
              + row(p['g_na_q'], NA_HEADS) + row(p['g_na_k'], NA_HEADS)
              + row(p['g_df_q'], 2 * DF_HEADS) + row(p['g_df_k'], 2 * DF_HEADS) + row(p['g_df_sub'], DF_HEADS)
              + row(p['df_lq1']) + row(p['df_lk1']) + row(p['df_lq2']) + row(p['df_lk2'])
              + [jnp.zeros((DEPTH, D_MODEL), F32)])
    vec = jnp.concatenate(pieces, axis=1).reshape(DEPTH, N_VEC, D_MODEL)
    f = p['na_rpb']
    zpad = jnp.zeros((DEPTH, NA_HEADS, NA_PAIR_TILES, 33), F32)
    g_rows = jnp.concatenate([f[:, :, :-1, NA_KW - 1:], zpad, f[:, :, 1:, :], zpad, f[:, :, :-1, :NA_KW - 1]], axis=-1)
    g_rows = g_rows.reshape(DEPTH, NA_HEADS * NA_PAIR_TILES, LANES)
    return dict(wa=wa, wb=wb, w_uq=w_uq_p, w_ukv=w_ukv_r, vec=vec, g_rows=g_rows,
                w_out=p['w_out'].astype(BF16), w_gate=p['w_gate'].astype(BF16),
                w_up=p['w_up'].astype(BF16), w_down=p['w_down'].astype(BF16))


def _modulation(c_all, w_mod, b_mod):
    tn = 1024
    return pl.pallas_call(
        _mod_kernel,
        grid=(DEPTH, 6 * D_MODEL // tn),
        in_specs=[pl.BlockSpec((N_MOD, D_MODEL), lambda l, j: (0, 0)),
                  pl.BlockSpec((None, D_MODEL, tn), lambda l, j: (l, 0, j)),
                  pl.BlockSpec((None, 1, tn), lambda l, j: (l, 0, j))],
        out_specs=pl.BlockSpec((None, N_MOD, tn), lambda l, j: (l, 0, j)),
        out_shape=jax.ShapeDtypeStruct((DEPTH, N_MOD, 6 * D_MODEL), F32),
        compiler_params=_params(2),
        name="modulation",
    )(c_all, w_mod, b_mod.reshape(DEPTH, 1, 6 * D_MODEL))


def _front_weight_specs(l):
    return [_layer_spec(l, (N_VEC, D_MODEL)), _layer_spec(l, (A_COLS, D_MODEL)), _layer_spec(l, (B_COLS, D_MODEL)),
            _layer_spec(l, (W_MLA_P, Q_LORA)), _layer_spec(l, (KV_LORA, W_MLA_P + MLA_W)),
            _const_spec((MXU_DIM, MXU_DIM)), _const_spec((MXU_DIM, MXU_DIM))]


def _front_weights(t, consts):
    return (t['vec'], t['wa'], t['wb'], t['w_uq'], t['w_ukv'], consts['bd64'], consts['bd32'])


def _context_mixer(l, lam_init, x, mods, t, consts, prev_caches):
    n_tok = BATCH * SEQ
    n_alias = len(prev_caches)
    tok = lambda w: pl.BlockSpec((CTX_SEQS_PER_STEP * SEQ, w), lambda b: (b, 0))
    if n_alias == 0:
        assert l == 0
        lay = lambda *s: pl.BlockSpec((CTX_SEQS_PER_STEP, DEPTH) + s, lambda b: (b, 0) + (0,) * len(s))
    else:
        lay = lambda *s: pl.BlockSpec((CTX_SEQS_PER_STEP, None) + s, lambda b: (b, l) + (0,) * len(s))
    cache_shapes = [(SEQ, KV_LORA), (MLA_ROPE, SEQ), (NA_HEADS, NA_HD, SEQ), (NA_HEADS, NA_HD, SEQ),
                    (DF_HEADS, DF_HD, SEQ), (DF_HEADS, DF_HD, SEQ)]
    weights = _front_weights(t, consts)
    n_in = 2 + len(weights)
    return pl.pallas_call(
        functools.partial(_ctx_kernel, lam_init, n_alias),
        grid=(BATCH // CTX_SEQS_PER_STEP,),
        in_specs=[tok(D_MODEL), pl.BlockSpec((None, 1, 6 * D_MODEL), lambda b: (l * N_MOD, 0, 0))]
        + _front_weight_specs(l) + [pl.BlockSpec(memory_space=pl.ANY)] * n_alias,
        out_specs=[tok(D_MODEL)] + [lay(*s) for s in cache_shapes],
        out_shape=[jax.ShapeDtypeStruct((n_tok, D_MODEL), BF16)]
        + [jax.ShapeDtypeStruct((BATCH, DEPTH) + s, F32) for s in cache_shapes],
        input_output_aliases={n_in + i: 1 + i for i in range(n_alias)},
        compiler_params=_params(1),
        name="context_mixer",
    )(x, mods, *weights, *prev_caches)


def _finish(l, x, mix, mods, t, mod_row):
    n_tok = x.shape[0]
    tok = pl.BlockSpec((TM_FINISH, D_MODEL), lambda i: (i, 0))
    return pl.pallas_call(
        _finish_kernel,
        grid=(n_tok // TM_FINISH,),
        in_specs=[tok, tok, pl.BlockSpec((None, 1, 6 * D_MODEL), lambda i: (l * N_MOD + mod_row(i), 0, 0)),
                  _layer_spec(l, (N_VEC, D_MODEL)), _layer_spec(l, (D_MODEL, D_MODEL)), _layer_spec(l, (D_MODEL, D_FF)),
                  _layer_spec(l, (D_MODEL, D_FF)), _layer_spec(l, (D_FF, D_MODEL))],
        out_specs=tok,
        out_shape=jax.ShapeDtypeStruct((n_tok, D_MODEL), F32),
        compiler_params=_params(1),
        name="finish",
    )(x, mix, mods, t['vec'], t['w_out'], t['w_gate'], t['w_up'], t['w_down'])


def _latent_front(l, x, mods, t, consts):
    n_tok = DEC_BATCH * DEC_SEQ
    blocks_per_seq = DEC_SEQ // TM
    tok = lambda w: pl.BlockSpec((TM, w), lambda i: (i, 0))
    wspecs = _front_weight_specs(l)
    weights = _front_weights(t, consts)
    return pl.pallas_call(
        _lat_front_kernel,
        grid=(n_tok // TM,),
        in_specs=[tok(D_MODEL),
                  pl.BlockSpec((None, 1, 6 * D_MODEL), lambda i: (l * N_MOD + 1 + i // blocks_per_seq, 0, 0)),
                  wspecs[0], pl.BlockSpec((6, TM, LANES), lambda i: (0, i % blocks_per_seq, 0))] + wspecs[1:],
        out_specs=[tok(QK_PACK), tok(QK_PACK), tok(V_PACK)],
        out_shape=[jax.ShapeDtypeStruct((n_tok, QK_PACK), BF16), jax.ShapeDtypeStruct((n_tok, QK_PACK), BF16),
                   jax.ShapeDtypeStruct((n_tok, V_PACK), BF16)],
        compiler_params=_params(1),
        name="latent_front",
    )(x, mods, weights[0], consts['rope'], *weights[1:])


def _context_kv(l, ckv, krope_t, t):
    lay = lambda r, w: pl.BlockSpec((None, None, r, w), lambda b: (b, l, 0, 0))
    out = lambda w: pl.BlockSpec((None, PAST_LEN, w), lambda b: (b, 0, 0))
    return pl.pallas_call(
        _ctx_kv_kernel,
        grid=(DEC_BATCH,),
        in_specs=[lay(PAST_LEN, KV_LORA), lay(MLA_ROPE, PAST_LEN),
                  _layer_spec(l, (N_VEC, D_MODEL)), _layer_spec(l, (KV_LORA, W_MLA_P + MLA_W))],
        out_specs=[out(W_MLA_P), out(MLA_W)],
        out_shape=[jax.ShapeDtypeStruct((DEC_BATCH, PAST_LEN, W_MLA_P), BF16),
                   jax.ShapeDtypeStruct((DEC_BATCH, PAST_LEN, MLA_W), BF16)],
        compiler_params=_params(1),
        name="context_kv",
    )(ckv, krope_t, t['vec'], t['w_ukv'])


def _latent_na(l, q, k, v, kct, vct, g_rows):
    n_tok = DEC_BATCH * DEC_SEQ
    na_blk = P_NA // NA_W
    steps = GRID_ROWS // NA_ROWS_PER_STEP
    tq = NA_ROWS_PER_STEP * GRID_W
    ctx = pl.BlockSpec((None, None, NA_W, PAST_LEN), lambda b, j: (b, l, 0, 0))
    return pl.pallas_call(
        _na_lat_kernel,
        grid=(DEC_BATCH, steps),
        in_specs=[pl.BlockSpec((tq, NA_W), lambda b, j: (b * steps + j, na_blk)),
                  pl.BlockSpec((DEC_SEQ, NA_W), lambda b, j: (b, na_blk)),
                  pl.BlockSpec((DEC_SEQ, NA_W), lambda b, j: (b, PV_NA // NA_W)),
                  ctx, ctx, _layer_spec(l, (NA_HEADS * NA_PAIR_TILES, LANES))],
        out_specs=pl.BlockSpec((tq, NA_W), lambda b, j: (b * steps + j, 0)),
        out_shape=jax.ShapeDtypeStruct((n_tok, NA_W), BF16),
        scratch_shapes=[pltpu.VMEM((NA_HEADS * NA_PAIR_TILES, GRID_W, LANES), F32)],
        compiler_params=_params(2),
        name="latent_neighbourhood",
    )(q, k, v, kct, vct, g_rows)


def _latent_attention(l, lam_init, q, k, v, kc, vc, kct, vct, o_na, t):
    n_tok = DEC_BATCH * DEC_SEQ
    nq = DEC_SEQ // TM
    ctx = pl.BlockSpec((None, None, DF_W, PAST_LEN), lambda b, j: (b, l, 0, 0))
    return pl.pallas_call(
        functools.partial(_lat_attn_kernel, lam_init),
        grid=(DEC_BATCH, nq),
        in_specs=[pl.BlockSpec((TM, QK_PACK), lambda b, j: (b * nq + j, 0)),
                  pl.BlockSpec((DEC_SEQ, QK_PACK), lambda b, j: (b, 0)),
                  pl.BlockSpec((DEC_SEQ, V_PACK), lambda b, j: (b, 0)),
                  pl.BlockSpec((None, PAST_LEN, W_MLA_P), lambda b, j: (b, 0, 0)),
                  pl.BlockSpec((None, PAST_LEN, MLA_W), lambda b, j: (b, 0, 0)),
                  ctx, ctx,
                  pl.BlockSpec((TM, NA_W), lambda b, j: (b * nq + j, 0)),
                  _layer_spec(l, (N_VEC, D_MODEL))],
        out_specs=pl.BlockSpec((TM, D_MODEL), lambda b, j: (b * nq + j, 0)),
        out_shape=jax.ShapeDtypeStruct((n_tok, D_MODEL), BF16),
        compiler_params=_params(2),
        name="latent_attention",
    )(q, k, v, kc, vc, kct, vct, o_na, t['vec'])


def kernel(x_prompt, x_sample, cache_mla_ckv, cache_mla_krope, cache_na_k, cache_na_v, cache_df_k, cache_df_v, c, c_ctx, w_mod, b_mod, g_mix, w_in, g_qa, w_uq, g_kva, w_ukv, g_mla_q, g_mla_k, g_na_q, g_na_k, na_rpb, g_df_q, g_df_k, df_lq1, df_lk1, df_lq2, df_lk2, g_df_sub, w_out, g_ffn, w_gate, w_up, w_down):
    p = dict(g_mix=g_mix, w_in=w_in, g_qa=g_qa, w_uq=w_uq, g_kva=g_kva, w_ukv=w_ukv, g_mla_q=g_mla_q, g_mla_k=g_mla_k,
             g_na_q=g_na_q, g_na_k=g_na_k, na_rpb=na_rpb, g_df_q=g_df_q, g_df_k=g_df_k, df_lq1=df_lq1, df_lk1=df_lk1,
             df_lq2=df_lq2, df_lk2=df_lk2, g_df_sub=g_df_sub, w_out=w_out, g_ffn=g_ffn, w_gate=w_gate, w_up=w_up,
             w_down=w_down)
    consts = dict(bd64=_block_diag(NA_HD), bd32=_block_diag(DF_QK), rope=_rope_tables())
    t = _tables(p)

    c_all = jnp.concatenate([c_ctx[None, :], c, jnp.zeros((N_MOD - 1 - DEC_BATCH, D_MODEL), F32)], axis=0)
    mods = _modulation(c_all, w_mod, b_mod).reshape(DEPTH * N_MOD, 1, 6 * D_MODEL)

    krope_t = jnp.swapaxes(cache_mla_krope, -1, -2)
    na_kt, na_vt, df_kt, df_vt = (_feature_major(a) for a in (cache_na_k, cache_na_v, cache_df_k, cache_df_v))

    xp = x_prompt.reshape(BATCH * SEQ, D_MODEL)
    xs = x_sample.reshape(DEC_BATCH * DEC_SEQ, D_MODEL)
    new_caches = ()
    for l in range(DEPTH):
        lam_init = 0.8 - 0.6 * math.exp(-0.3 * l)
        mix, *new_caches = _context_mixer(l, lam_init, xp, mods, t, consts, new_caches)
        xp = _finish(l, xp, mix, mods, t, lambda i: 0)
        q, k, v = _latent_front(l, xs, mods, t, consts)
        kc, vc = _context_kv(l, cache_mla_ckv, krope_t, t)
        o_na = _latent_na(l, q, k, v, na_kt, na_vt, t['g_rows'])
        mix_s = _latent_attention(l, lam_init, q, k, v, kc, vc, df_kt, df_vt, o_na, t)
        xs = _finish(l, xs, mix_s, mods, t, lambda i: 1 + i // (DEC_SEQ // TM_FINISH))
    ckv_new, *narrow = new_caches
    return (xp.reshape(BATCH, SEQ, D_MODEL), xs.reshape(DEC_BATCH, DEC_SEQ, D_MODEL), ckv_new,
            *(jnp.swapaxes(a, -1, -2) for a in narrow))
```

```python
import functools
import math

import numpy as np
import jax
import jax.numpy as jnp
from jax import lax
from jax.experimental import pallas as pl
from jax.experimental.pallas import tpu as pltpu

F32 = jnp.float32
BF16 = jnp.bfloat16

D_MODEL = 1024
BATCH = 32
SEQ = 256
DEPTH = 2
DEC_BATCH = 2
DEC_SEQ = 1024
PAST_LEN = 256
GRID_W = 64
GRID_ROWS = DEC_SEQ // GRID_W
MLA_HEADS = 6
MLA_NOPE = 64
MLA_ROPE = 32
MLA_QK = MLA_NOPE + MLA_ROPE
MLA_V = 64
MLA_PAD = 128
Q_LORA = 256
KV_LORA = 128
NA_HEADS = 6
NA_HD = 64
NA_KR = 8
NA_KW = 16
DF_HEADS = 4
DF_HD = 64
DF_QK = 32
MLA_W = MLA_HEADS * MLA_V
NA_W = NA_HEADS * NA_HD
DF_W = DF_HEADS * DF_HD
D_FF = -(-8 * D_MODEL // (3 * 256)) * 256
ROPE_BASE = 10000.0
EPS = 1e-6
NEG = -1e30
LOG2E = math.log2(math.e)

LANES = 128
MXU_DIM = 256

A_CQ = 0
A_CKV = A_CQ + Q_LORA
A_KR = A_CKV + KV_LORA
A_COLS = A_KR + LANES
B_SRC = Q_LORA + KV_LORA + MLA_ROPE
B_NAQ = 0
B_NAK = B_NAQ + NA_W
B_NAV = B_NAK + NA_W
B_DFQ = B_NAV + NA_W
B_DFK = B_DFQ + DF_W
B_DFV = B_DFK + DF_W
B_COLS = B_DFV + DF_W

W_MLA_P = MLA_HEADS * MLA_PAD
QK_PACK = W_MLA_P + NA_W + DF_W
V_PACK = MLA_W + NA_W + DF_W
P_MLA = 0
P_NA = W_MLA_P
P_DF = P_NA + NA_W
PV_MLA = 0
PV_NA = MLA_W
PV_DF = MLA_W + NA_W

(V_GMIX, V_GFFN, V_GQA, V_GKVA, V_GMQ, V_GMK, V_GNQ, V_GNK, V_GDQ, V_GDK, V_GDS,
 V_LQ1, V_LK1, V_LQ2, V_LK2) = range(15)
N_VEC = 16

N_MOD = 8
TM = 256
TM_LAT_FRONT = 256
TM_FINISH = 512
CTX_SEQS_PER_STEP = 2
NA_ROWS_PER_STEP = 4
VMEM_LIMIT = 56 * 1024 * 1024

NA_PAIR_TILES = 2 * NA_KR - 2


def _dot(a, b):
    return jnp.dot(a, b, preferred_element_type=F32)


def _dot_nt(a, b):
    return lax.dot_general(a, b, (((1,), (1,)), ((), ())), preferred_element_type=F32)


def _lane_iota(shape):
    return lax.broadcasted_iota(jnp.int32, shape, len(shape) - 1)


def _rms_rows(x, g):
    ms = jnp.mean(x * x, axis=-1, keepdims=True)
    return x * lax.rsqrt(ms + EPS) * g


def _tile_rms(x, g, n_real):
    outs = []
    for c0 in range(0, x.shape[1], LANES):
        xt = x[:, c0:c0 + LANES]
        ms = jnp.sum(xt * xt, axis=-1, keepdims=True) * (1.0 / n_real)
        outs.append(xt * lax.rsqrt(ms + EPS) * g[:, c0:c0 + LANES])
    return jnp.concatenate(outs, axis=1)


def _seg_rms(x, bd_ref, g, group):
    width = x.shape[1]
    sq = (x * x).astype(BF16)
    parts = []
    for c0 in range(0, width, MXU_DIM):
        w = min(MXU_DIM, width - c0)
        parts.append(_dot(sq[:, c0:c0 + w], bd_ref[0:w, 0:w]))
    ss = parts[0] if len(parts) == 1 else jnp.concatenate(parts, axis=1)
    return x * lax.rsqrt(ss * (1.0 / group) + EPS) * g


def _rope_tiles(x, cos, sa, sb):
    outs = []
    for t in range(x.shape[1] // LANES):
        xt = x[:, t * LANES:(t + 1) * LANES]
        up = pltpu.roll(xt, LANES - MLA_ROPE // 4, 1)
        dn = pltpu.roll(xt, MLA_ROPE // 4, 1)
        outs.append(xt * cos + up * sa + dn * sb)
    return outs[0] if len(outs) == 1 else jnp.concatenate(outs, axis=1)


def _diff_lambda(vec_ref, lam_init):
    a = jnp.sum(vec_ref[V_LQ1:V_LQ1 + 1, 0:DF_QK] * vec_ref[V_LK1:V_LK1 + 1, 0:DF_QK], axis=-1, keepdims=True)
    b = jnp.sum(vec_ref[V_LQ2:V_LQ2 + 1, 0:DF_QK] * vec_ref[V_LK2:V_LK2 + 1, 0:DF_QK], axis=-1, keepdims=True)
    return jnp.exp(a) - jnp.exp(b) + lam_init


def _mixer_front(x, mod, vec_ref, wa_ref, wb_ref, w_uq_ref, bd64_ref, bd32_ref):
    sh = mod[:, 0:D_MODEL]
    sc = mod[:, D_MODEL:2 * D_MODEL]
    h = (_rms_rows(x, vec_ref[V_GMIX:V_GMIX + 1, :]) * (1.0 + sc) + sh).astype(BF16)
    za = _dot_nt(h, wa_ref[...])
    zb = _dot_nt(h, wb_ref[...])
    cqn = _rms_rows(za[:, A_CQ:A_CQ + Q_LORA], vec_ref[V_GQA:V_GQA + 1, 0:Q_LORA])
    q_raw = _dot_nt(cqn.astype(BF16), w_uq_ref[...])
    q_mla = _tile_rms(q_raw, vec_ref[V_GMQ:V_GMQ + 1, 0:W_MLA_P] * (MLA_QK ** -0.5 * LOG2E), MLA_QK)
    ckv_n = _rms_rows(za[:, A_CKV:A_CKV + KV_LORA], vec_ref[V_GKVA:V_GKVA + 1, 0:KV_LORA])
    kr_tile = za[:, A_KR:A_KR + LANES]
    q_na = _seg_rms(zb[:, B_NAQ:B_NAQ + NA_W], bd64_ref, vec_ref[V_GNQ:V_GNQ + 1, 0:NA_W] * (NA_HD ** -0.5 * LOG2E), NA_HD)
    k_na = _seg_rms(zb[:, B_NAK:B_NAK + NA_W], bd64_ref, vec_ref[V_GNK:V_GNK + 1, 0:NA_W], NA_HD)
    v_na = zb[:, B_NAV:B_NAV + NA_W]
    q_df = _seg_rms(zb[:, B_DFQ:B_DFQ + DF_W], bd32_ref, vec_ref[V_GDQ:V_GDQ + 1, 0:DF_W] * (DF_QK ** -0.5 * LOG2E), DF_QK)
    k_df = _seg_rms(zb[:, B_DFK:B_DFK + DF_W], bd32_ref, vec_ref[V_GDK:V_GDK + 1, 0:DF_W], DF_QK)
    v_df = zb[:, B_DFV:B_DFV + DF_W]
    return q_mla, ckv_n, kr_tile, q_na, k_na, v_na, q_df, k_df, v_df


def _mla_kv(ckv_n, kr_tile, vec_ref, w_ukv_ref):
    kv = _dot(ckv_n.astype(BF16), w_ukv_ref[...])
    lane = _lane_iota((1, LANES))
    kr = jnp.where((lane >= MLA_NOPE) & (lane < MLA_QK), kr_tile, 0.0)
    k_pre = kv[:, 0:W_MLA_P] + jnp.concatenate([kr] * MLA_HEADS, axis=1)
    k = _tile_rms(k_pre, vec_ref[V_GMK:V_GMK + 1, 0:W_MLA_P], MLA_QK)
    return k, kv[:, W_MLA_P:W_MLA_P + MLA_W]


def _softmax_parts(s):
    m = jnp.max(s, axis=-1, keepdims=True)
    p = jnp.exp2(s - m)
    return p, 1.0 / jnp.sum(p, axis=-1, keepdims=True)


def _scores(q, k_segs):
    parts = [_dot(q, k) if feature_major else _dot_nt(q, k) for k, feature_major in k_segs]
    return parts[0] if len(parts) == 1 else jnp.concatenate(parts, axis=1)


def _pv(p, v_segs):
    out = None
    c0 = 0
    for v, feature_major in v_segs:
        n = v.shape[1] if feature_major else v.shape[0]
        o = _dot_nt(p[:, c0:c0 + n], v) if feature_major else _dot(p[:, c0:c0 + n], v)
        out = o if out is None else out + o
        c0 += n
    return out


def _seg_tile(seg, t):
    a, feature_major = seg
    return (a[t * LANES:(t + 1) * LANES, :] if feature_major else a[:, t * LANES:(t + 1) * LANES]), feature_major


def _lane_groups(qt, width):
    lane = _lane_iota((1, LANES))
    zero = jnp.zeros_like(qt)
    return jnp.concatenate(
        [jnp.where((lane >= g * width) & (lane < (g + 1) * width), qt, zero) for g in range(LANES // width)], axis=0)


def _pair_select(o2):
    tq = o2.shape[0] // 2
    return jnp.where(_lane_iota((1, LANES)) < NA_HD, o2[0:tq], o2[tq:2 * tq])


def _mla_attend(q, k_segs, v_segs):
    outs = []
    for t in range(MLA_HEADS // 2):
        vt = [_seg_tile(v, t) for v in v_segs]
        halves = []
        for h in (2 * t, 2 * t + 1):
            p, il = _softmax_parts(_scores(q[:, h * MLA_PAD:(h + 1) * MLA_PAD], [_seg_tile(k, h) for k in k_segs]))
            halves.append(_pv(p.astype(BF16), vt) * il)
        outs.append(jnp.where(_lane_iota((1, LANES)) < MLA_V, halves[0], halves[1]))
    return jnp.concatenate(outs, axis=1)


def _na_tile_attend(qt, k_segs, v_segs, bias=None):
    s = _scores(_lane_groups(qt, NA_HD), k_segs)
    if bias is not None:
        nb = bias.shape[1]
        s = jnp.concatenate([s[:, 0:nb] + bias, s[:, nb:]], axis=1)
    p, il = _softmax_parts(s)
    return _pair_select(_pv(p.astype(BF16), v_segs) * il)


def _na_attend_full(q, k_segs, v_segs):
    return jnp.concatenate(
        [_na_tile_attend(q[:, t * LANES:(t + 1) * LANES], [_seg_tile(k, t) for k in k_segs],
                         [_seg_tile(v, t) for v in v_segs]) for t in range(NA_HEADS // 2)], axis=1)


def _df_attend(q, k_segs, v_segs, lam, g_sub, out_scale):
    outs = []
    lane = _lane_iota((1, LANES))
    tq = q.shape[0]
    for t in range(DF_HEADS // 2):
        kt = [_seg_tile(k, t) for k in k_segs]
        vt = [_seg_tile(v, t) for v in v_segs]
        p, il = _softmax_parts(_scores(_lane_groups(q[:, t * LANES:(t + 1) * LANES], DF_QK), kt))
        pn = []
        for hh in range(2):
            r1, r2 = 2 * hh * tq, (2 * hh + 1) * tq
            pn.append((p[r1:r1 + tq] * il[r1:r1 + tq] - p[r2:r2 + tq] * (lam * il[r2:r2 + tq])).astype(BF16))
        o = _pair_select(_pv(jnp.concatenate(pn, axis=0), vt))
        o2 = o * o
        ms_e = jnp.sum(jnp.where(lane < DF_HD, o2, 0.0), axis=-1, keepdims=True)
        ms_o = jnp.sum(jnp.where(lane >= DF_HD, o2, 0.0), axis=-1, keepdims=True)
        r = lax.rsqrt(jnp.where(lane < DF_HD, ms_e, ms_o) * (1.0 / DF_HD) + EPS)
        outs.append(o * r * (g_sub[:, t * LANES:(t + 1) * LANES] * out_scale))
    return jnp.concatenate(outs, axis=1)


def _mod_kernel(c_ref, w_ref, b_ref, o_ref):
    c = c_ref[...]
    s = c * jax.nn.sigmoid(c)
    o_ref[...] = _dot(s.astype(BF16), w_ref[...].astype(BF16)) + b_ref[...]


def _ctx_kernel(lam_init, n_alias, x_ref, mod_ref, vec_ref, wa_ref, wb_ref, w_uq_ref, w_ukv_ref, bd64_ref, bd32_ref,
                *rest):
    mix_ref, *cache_refs = rest[n_alias:]
    if n_alias == 0:
        for ref in cache_refs:
            ref[:, 1:] = jnp.zeros((ref.shape[0], ref.shape[1] - 1) + ref.shape[2:], F32)
        cache_refs = [ref.at[:, 0] for ref in cache_refs]
    ckv_ref, kr_ref, nak_ref, nav_ref, dfk_ref, dfv_ref = cache_refs
    lam = _diff_lambda(vec_ref, lam_init)
    mod = mod_ref[...]
    bf = lambda a: a.astype(BF16)
    seg = lambda a: [(a.astype(BF16), False)]
    for s in range(CTX_SEQS_PER_STEP):
        rows = slice(s * SEQ, (s + 1) * SEQ)
        q_mla, ckv_n, kr_tile, q_na, k_na, v_na, q_df, k_df, v_df = _mixer_front(
            x_ref[rows, :], mod, vec_ref, wa_ref, wb_ref, w_uq_ref, bd64_ref, bd32_ref)
        k_mla, v_mla = _mla_kv(ckv_n, kr_tile, vec_ref, w_ukv_ref)
        ckv_ref[s] = ckv_n
        kr_ref[s] = kr_tile.T[0:MLA_ROPE]
        for ref, a in ((nak_ref, k_na), (nav_ref, v_na), (dfk_ref, k_df), (dfv_ref, v_df)):
            at = a.T
            for h in range(ref.shape[1]):
                ref[s, h] = at[h * NA_HD:(h + 1) * NA_HD]
        o_mla = _mla_attend(bf(q_mla), seg(k_mla), seg(v_mla))
        o_na = _na_attend_full(bf(q_na), seg(k_na), seg(v_na))
        o_df = _df_attend(bf(q_df), seg(k_df), seg(v_df), lam, vec_ref[V_GDS:V_GDS + 1, 0:DF_W], 1.0 - lam_init)
        mix_ref[rows, :] = jnp.concatenate([o_mla, o_na, o_df], axis=1).astype(BF16)


def _finish_kernel(x_ref, mix_ref, mod_ref, vec_ref, w_out_ref, w_gate_ref, w_up_ref, w_down_ref, y_ref):
    mod = mod_ref[...]
    gate_m = mod[:, 2 * D_MODEL:3 * D_MODEL]
    sh = mod[:, 3 * D_MODEL:4 * D_MODEL]
    sc = mod[:, 4 * D_MODEL:5 * D_MODEL]
    gate_f = mod[:, 5 * D_MODEL:6 * D_MODEL]
    x1 = x_ref[...] + gate_m * _dot(mix_ref[...], w_out_ref[...])
    h = (_rms_rows(x1, vec_ref[V_GFFN:V_GFFN + 1, :]) * (1.0 + sc) + sh).astype(BF16)
    g = _dot(h, w_gate_ref[...])
    u = _dot(h, w_up_ref[...])
    a = (g * jax.nn.sigmoid(g) * u).astype(BF16)
    y_ref[...] = x1 + gate_f * _dot(a, w_down_ref[...])


def _lat_front_kernel(x_ref, mod_ref, vec_ref, rope_ref, wa_ref, wb_ref, w_uq_ref, w_ukv_ref, bd64_ref, bd32_ref,
                      q_ref, k_ref, v_ref):
    q_mla, ckv_n, kr_tile, q_na, k_na, v_na, q_df, k_df, v_df = _mixer_front(
        x_ref[...], mod_ref[...], vec_ref, wa_ref, wb_ref, w_uq_ref, bd64_ref, bd32_ref)
    k_mla, v_mla = _mla_kv(ckv_n, kr_tile, vec_ref, w_ukv_ref)
    cm, sam, sbm = rope_ref[0], rope_ref[1], rope_ref[2]
    cd, sad, sbd = rope_ref[3], rope_ref[4], rope_ref[5]
    q_ref[...] = jnp.concatenate(
        [_rope_tiles(q_mla, cm, sam, sbm), q_na, _rope_tiles(q_df, cd, sad, sbd)], axis=1).astype(BF16)
    k_ref[...] = jnp.concatenate(
        [_rope_tiles(k_mla, cm, sam, sbm), k_na, _rope_tiles(k_df, cd, sad, sbd)], axis=1).astype(BF16)
    v_ref[...] = jnp.concatenate([v_mla, v_na, v_df], axis=1).astype(BF16)


def _ctx_kv_kernel(ckv_ref, krt_ref, vec_ref, w_ukv_ref, k_ref, v_ref):
    krt = jnp.concatenate([jnp.zeros((MLA_NOPE, PAST_LEN), F32), krt_ref[...],
                           jnp.zeros((LANES - MLA_QK, PAST_LEN), F32)], axis=0)
    k_mla, v_mla = _mla_kv(ckv_ref[...], krt.T, vec_ref, w_ukv_ref)
    k_ref[...] = k_mla.astype(BF16)
    v_ref[...] = v_mla.astype(BF16)


def _na_lat_kernel(q_ref, kl_ref, vl_ref, kct_ref, vct_ref, g_ref, o_ref, bias_ref):
    b = pl.program_id(0)
    j = pl.program_id(1)

    @pl.when((b == 0) & (j == 0))
    def _build_bias():
        c = lax.broadcasted_iota(jnp.int32, (GRID_W, LANES), 0)
        kc = _lane_iota((GRID_W, LANES)) % GRID_W
        start = jnp.clip(c - NA_KW // 2, 0, GRID_W - NA_KW)
        in_win = (kc >= start) & (kc < start + NA_KW)

        def body(i, carry):
            row = jnp.broadcast_to(g_ref[pl.ds(i, 1), :], (GRID_W, LANES))
            toep = pltpu.roll(row, 0, 1, stride=1, stride_axis=0)
            bias_ref[i] = jnp.where(in_win, toep * LOG2E, NEG)
            return carry

        lax.fori_loop(0, NA_HEADS * NA_PAIR_TILES, body, 0)

    n_win = NA_KR * GRID_W
    kct = kct_ref[...].astype(BF16)
    vct = vct_ref[...].astype(BF16)

    def win_start(r):
        return jnp.clip(r - NA_KR // 2, 0, GRID_ROWS - NA_KR)

    def attend(grid_rows, rs):
        a0, n = grid_rows[0], len(grid_rows)
        row0 = pl.multiple_of(rs * GRID_W, GRID_W)
        outs = []
        for t in range(NA_HEADS // 2):
            lanes = slice(t * LANES, (t + 1) * LANES)
            bias = jnp.concatenate(
                [jnp.concatenate([bias_ref[h * NA_PAIR_TILES + (rs - (j * NA_ROWS_PER_STEP + a) + NA_KR - 1) + 2 * m]
                                  for m in range(NA_KR // 2)], axis=1)
                 for h in (2 * t, 2 * t + 1) for a in grid_rows], axis=0)
            outs.append(_na_tile_attend(
                q_ref[a0 * GRID_W:(a0 + n) * GRID_W, lanes],
                [(kl_ref[pl.ds(row0, n_win), lanes], False), (kct[lanes, :], True)],
                [(vl_ref[pl.ds(row0, n_win), lanes], False), (vct[lanes, :], True)], bias))
        return jnp.concatenate(outs, axis=1).astype(BF16)

    first_r, last_r = j * NA_ROWS_PER_STEP, (j + 1) * NA_ROWS_PER_STEP - 1
    shared = win_start(first_r) == win_start(last_r)

    @pl.when(shared)
    def _shared_window():
        o_ref[...] = attend(tuple(range(NA_ROWS_PER_STEP)), win_start(first_r))

    @pl.when(jnp.logical_not(shared))
    def _per_row_windows():
        for a in range(NA_ROWS_PER_STEP):
            o_ref[a * GRID_W:(a + 1) * GRID_W, :] = attend((a,), win_start(first_r + a))


def _lat_attn_kernel(lam_init, q_ref, kl_ref, vl_ref, kc_ref, vc_ref, kct_ref, vct_ref, ona_ref, vec_ref, mix_ref):
    o_mla = _mla_attend(q_ref[:, P_MLA:P_MLA + W_MLA_P],
                        [(kc_ref[...], False), (kl_ref[:, P_MLA:P_MLA + W_MLA_P], False)],
                        [(vc_ref[...], False), (vl_ref[:, PV_MLA:PV_MLA + MLA_W], False)])
    lam = _diff_lambda(vec_ref, lam_init)
    o_df = _df_attend(q_ref[:, P_DF:P_DF + DF_W],
                      [(kct_ref[...].astype(BF16), True), (kl_ref[:, P_DF:P_DF + DF_W], False)],
                      [(vct_ref[...].astype(BF16), True), (vl_ref[:, PV_DF:PV_DF + DF_W], False)],
                      lam, vec_ref[V_GDS:V_GDS + 1, 0:DF_W], 1.0 - lam_init)
    mix_ref[...] = jnp.concatenate([o_mla.astype(BF16), ona_ref[...], o_df.astype(BF16)], axis=1)


def _const_spec(shape):
    nd = len(shape)
    return pl.BlockSpec(shape, lambda *_: (0,) * nd, pipeline_mode=pl.Buffered(1))


def _layer_spec(l, shape):
    nd = len(shape)
    return pl.BlockSpec((None,) + tuple(shape), lambda *_: (l,) + (0,) * nd, pipeline_mode=pl.Buffered(1))


def _params(n_axes):
    return pltpu.CompilerParams(dimension_semantics=("arbitrary",) * n_axes, vmem_limit_bytes=VMEM_LIMIT)


def _block_diag(group):
    i = np.arange(MXU_DIM) // group
    return jnp.asarray((i[:, None] == i[None, :]).astype(np.float32), dtype=BF16)


def _rope_tables():
    t = np.arange(DEC_SEQ)
    row = (t // GRID_W).astype(np.float64)
    col = (t % GRID_W).astype(np.float64)
    n = MLA_ROPE // 4
    inv = 1.0 / (ROPE_BASE ** (np.arange(n, dtype=np.float64) * 2.0 / (MLA_ROPE // 2)))
    ar = row[:, None] * inv
    ac = col[:, None] * inv
    ang = np.concatenate([ar, ar, ac, ac], axis=-1)
    cos32, sin32 = np.cos(ang), np.sin(ang)
    first = (np.arange(MLA_ROPE) % (2 * n)) < n
    sa32 = np.where(first, -sin32, 0.0)
    sb32 = np.where(first, 0.0, sin32)

    def mla_tile(v32, fill):
        out = np.full((DEC_SEQ, LANES), fill)
        out[:, MLA_NOPE:MLA_QK] = v32
        return out

    tabs = [mla_tile(cos32, 1.0), mla_tile(sa32, 0.0), mla_tile(sb32, 0.0),
            np.tile(cos32, (1, LANES // DF_QK)), np.tile(sa32, (1, LANES // DF_QK)), np.tile(sb32, (1, LANES // DF_QK))]
    return jnp.asarray(np.stack(tabs).astype(np.float32))


def _feature_major(a):
    a = jnp.swapaxes(a, -1, -2)
    return a.reshape(a.shape[:-3] + (a.shape[-3] * a.shape[-2], a.shape[-1]))


def _tables(p):
    w_in_t = jnp.swapaxes(p['w_in'], 1, 2)
    kr = w_in_t[:, Q_LORA + KV_LORA:B_SRC]
    z32 = jnp.zeros_like(kr)
    wa = jnp.concatenate([w_in_t[:, :Q_LORA + KV_LORA], kr, z32, kr, z32], axis=1).astype(BF16)
    wb = w_in_t[:, B_SRC:].astype(BF16)
    w_uq_t = jnp.swapaxes(p['w_uq'], 1, 2).reshape(DEPTH, MLA_HEADS, MLA_QK, Q_LORA)
    w_uq_p = jnp.pad(w_uq_t, ((0, 0), (0, 0), (0, MLA_PAD - MLA_QK), (0, 0))).reshape(DEPTH, W_MLA_P, Q_LORA).astype(BF16)
    w_ukv = p['w_ukv'].reshape(DEPTH, KV_LORA, MLA_HEADS, MLA_NOPE + MLA_V)
    wk = jnp.pad(w_ukv[..., :MLA_NOPE], ((0, 0), (0, 0), (0, 0), (0, MLA_PAD - MLA_NOPE))).reshape(DEPTH, KV_LORA, W_MLA_P)
    wv = w_ukv[..., MLA_NOPE:].reshape(DEPTH, KV_LORA, MLA_W)
    w_ukv_r = jnp.concatenate([wk, wv], axis=2).astype(BF16)

    def row(v, reps=1):
        v = jnp.tile(v, (1, reps)) if reps > 1 else v
        return [v, jnp.zeros((DEPTH, D_MODEL - v.shape[1]), F32)] if v.shape[1] < D_MODEL else [v]

    pad_head = lambda g: jnp.pad(g, ((0, 0), (0, MLA_PAD - MLA_QK)))
    pieces = (row(p['g_mix']) + row(p['g_ffn']) + row(p['g_qa']) + row(p['g_kva'])
              + row(pad_head(p['g_mla_q']), MLA_HEADS) + row(pad_head(p['g_mla_k']), MLA_HEADS)
              + row(p['g_na_q'], NA_HEADS) + row(p['g_na_k'], NA_HEADS)
              + row(p['g_df_q'], 2 * DF_HEADS) + row(p['g_df_k'], 2 * DF_HEADS) + row(p['g_df_sub'], DF_HEADS)
              + row(p['df_lq1']) + row(p['df_lk1']) + row(p['df_lq2']) + row(p['df_lk2'])
              + [jnp.zeros((DEPTH, D_MODEL), F32)])
    vec = jnp.concatenate(pieces, axis=1).reshape(DEPTH, N_VEC, D_MODEL)
    f = p['na_rpb']
    zpad = jnp.zeros((DEPTH, NA_HEADS, NA_PAIR_TILES, 33), F32)
    g_rows = jnp.concatenate([f[:, :, :-1, NA_KW - 1:], zpad, f[:, :, 1:, :], zpad, f[:, :, :-1, :NA_KW - 1]], axis=-1)
    g_rows = g_rows.reshape(DEPTH, NA_HEADS * NA_PAIR_TILES, LANES)
    return dict(wa=wa, wb=wb, w_uq=w_uq_p, w_ukv=w_ukv_r, vec=vec, g_rows=g_rows,
                w_out=p['w_out'].astype(BF16), w_gate=p['w_gate'].astype(BF16),
                w_up=p['w_up'].astype(BF16), w_down=p['w_down'].astype(BF16))


def _modulation(c_all, w_mod, b_mod):
    tn = 2048
    return pl.pallas_call(
        _mod_kernel,
        grid=(DEPTH, 6 * D_MODEL // tn),
        in_specs=[pl.BlockSpec((N_MOD, D_MODEL), lambda l, j: (0, 0)),
                  pl.BlockSpec((None, D_MODEL, tn), lambda l, j: (l, 0, j)),
                  pl.BlockSpec((None, 1, tn), lambda l, j: (l, 0, j))],
        out_specs=pl.BlockSpec((None, N_MOD, tn), lambda l, j: (l, 0, j)),
        out_shape=jax.ShapeDtypeStruct((DEPTH, N_MOD, 6 * D_MODEL), F32),
        compiler_params=_params(2),
        name="modulation",
    )(c_all, w_mod, b_mod.reshape(DEPTH, 1, 6 * D_MODEL))


def _front_weight_specs(l):
    return [_layer_spec(l, (N_VEC, D_MODEL)), _layer_spec(l, (A_COLS, D_MODEL)), _layer_spec(l, (B_COLS, D_MODEL)),
            _layer_spec(l, (W_MLA_P, Q_LORA)), _layer_spec(l, (KV_LORA, W_MLA_P + MLA_W)),
            _const_spec((MXU_DIM, MXU_DIM)), _const_spec((MXU_DIM, MXU_DIM))]


def _front_weights(t, consts):
    return (t['vec'], t['wa'], t['wb'], t['w_uq'], t['w_ukv'], consts['bd64'], consts['bd32'])


def _context_mixer(l, lam_init, x, mods, t, consts, prev_caches):
    n_tok = BATCH * SEQ
    n_alias = len(prev_caches)
    tok = lambda w: pl.BlockSpec((CTX_SEQS_PER_STEP * SEQ, w), lambda b: (b, 0))
    if n_alias == 0:
        assert l == 0
        lay = lambda *s: pl.BlockSpec((CTX_SEQS_PER_STEP, DEPTH) + s, lambda b: (b, 0) + (0,) * len(s))
    else:
        lay = lambda *s: pl.BlockSpec((CTX_SEQS_PER_STEP, None) + s, lambda b: (b, l) + (0,) * len(s))
    cache_shapes = [(SEQ, KV_LORA), (MLA_ROPE, SEQ), (NA_HEADS, NA_HD, SEQ), (NA_HEADS, NA_HD, SEQ),
                    (DF_HEADS, DF_HD, SEQ), (DF_HEADS, DF_HD, SEQ)]
    weights = _front_weights(t, consts)
    n_in = 2 + len(weights)
    return pl.pallas_call(
        functools.partial(_ctx_kernel, lam_init, n_alias),
        grid=(BATCH // CTX_SEQS_PER_STEP,),
        in_specs=[tok(D_MODEL), pl.BlockSpec((None, 1, 6 * D_MODEL), lambda b: (l * N_MOD, 0, 0))]
        + _front_weight_specs(l) + [pl.BlockSpec(memory_space=pl.ANY)] * n_alias,
        out_specs=[tok(D_MODEL)] + [lay(*s) for s in cache_shapes],
        out_shape=[jax.ShapeDtypeStruct((n_tok, D_MODEL), BF16)]
        + [jax.ShapeDtypeStruct((BATCH, DEPTH) + s, F32) for s in cache_shapes],
        input_output_aliases={n_in + i: 1 + i for i in range(n_alias)},
        compiler_params=_params(1),
        name="context_mixer",
    )(x, mods, *weights, *prev_caches)


def _finish(l, x, mix, mods, t, mod_row):
    n_tok = x.shape[0]
    tok = pl.BlockSpec((TM_FINISH, D_MODEL), lambda i: (i, 0))
    return pl.pallas_call(
        _finish_kernel,
        grid=(n_tok // TM_FINISH,),
        in_specs=[tok, tok, pl.BlockSpec((None, 1, 6 * D_MODEL), lambda i: (l * N_MOD + mod_row(i), 0, 0)),
                  _layer_spec(l, (N_VEC, D_MODEL)), _layer_spec(l, (D_MODEL, D_MODEL)), _layer_spec(l, (D_MODEL, D_FF)),
                  _layer_spec(l, (D_MODEL, D_FF)), _layer_spec(l, (D_FF, D_MODEL))],
        out_specs=tok,
        out_shape=jax.ShapeDtypeStruct((n_tok, D_MODEL), F32),
        compiler_params=_params(1),
        name="finish",
    )(x, mix, mods, t['vec'], t['w_out'], t['w_gate'], t['w_up'], t['w_down'])


def _latent_front(l, x, mods, t, consts):
    n_tok = DEC_BATCH * DEC_SEQ
    tm = TM_LAT_FRONT
    blocks_per_seq = DEC_SEQ // tm
    tok = lambda w: pl.BlockSpec((tm, w), lambda i: (i, 0))
    wspecs = _front_weight_specs(l)
    weights = _front_weights(t, consts)
    return pl.pallas_call(
        _lat_front_kernel,
        grid=(n_tok // tm,),
        in_specs=[tok(D_MODEL),
                  pl.BlockSpec((None, 1, 6 * D_MODEL), lambda i: (l * N_MOD + 1 + i // blocks_per_seq, 0, 0)),
                  wspecs[0], pl.BlockSpec((6, tm, LANES), lambda i: (0, i % blocks_per_seq, 0))] + wspecs[1:],
        out_specs=[tok(QK_PACK), tok(QK_PACK), tok(V_PACK)],
        out_shape=[jax.ShapeDtypeStruct((n_tok, QK_PACK), BF16), jax.ShapeDtypeStruct((n_tok, QK_PACK), BF16),
                   jax.ShapeDtypeStruct((n_tok, V_PACK), BF16)],
        compiler_params=_params(1),
        name="latent_front",
    )(x, mods, weights[0], consts['rope'], *weights[1:])


def _context_kv(l, ckv, krope_t, t):
    lay = lambda r, w: pl.BlockSpec((None, None, r, w), lambda b: (b, l, 0, 0))
    out = lambda w: pl.BlockSpec((None, PAST_LEN, w), lambda b: (b, 0, 0))
    return pl.pallas_call(
        _ctx_kv_kernel,
        grid=(DEC_BATCH,),
        in_specs=[lay(PAST_LEN, KV_LORA), lay(MLA_ROPE, PAST_LEN),
                  _layer_spec(l, (N_VEC, D_MODEL)), _layer_spec(l, (KV_LORA, W_MLA_P + MLA_W))],
        out_specs=[out(W_MLA_P), out(MLA_W)],
        out_shape=[jax.ShapeDtypeStruct((DEC_BATCH, PAST_LEN, W_MLA_P), BF16),
                   jax.ShapeDtypeStruct((DEC_BATCH, PAST_LEN, MLA_W), BF16)],
        compiler_params=_params(1),
        name="context_kv",
    )(ckv, krope_t, t['vec'], t['w_ukv'])


def _latent_na(l, q, k, v, kct, vct, g_rows):
    n_tok = DEC_BATCH * DEC_SEQ
    na_blk = P_NA // NA_W
    steps = GRID_ROWS // NA_ROWS_PER_STEP
    tq = NA_ROWS_PER_STEP * GRID_W
    ctx = pl.BlockSpec((None, None, NA_W, PAST_LEN), lambda b, j: (b, l, 0, 0))
    return pl.pallas_call(
        _na_lat_kernel,
        grid=(DEC_BATCH, steps),
        in_specs=[pl.BlockSpec((tq, NA_W), lambda b, j: (b * steps + j, na_blk)),
                  pl.BlockSpec((DEC_SEQ, NA_W), lambda b, j: (b, na_blk)),
                  pl.BlockSpec((DEC_SEQ, NA_W), lambda b, j: (b, PV_NA // NA_W)),
                  ctx, ctx, _layer_spec(l, (NA_HEADS * NA_PAIR_TILES, LANES))],
        out_specs=pl.BlockSpec((tq, NA_W), lambda b, j: (b * steps + j, 0)),
        out_shape=jax.ShapeDtypeStruct((n_tok, NA_W), BF16),
        scratch_shapes=[pltpu.VMEM((NA_HEADS * NA_PAIR_TILES, GRID_W, LANES), F32)],
        compiler_params=_params(2),
        name="latent_neighbourhood",
    )(q, k, v, kct, vct, g_rows)


def _latent_attention(l, lam_init, q, k, v, kc, vc, kct, vct, o_na, t):
    n_tok = DEC_BATCH * DEC_SEQ
    nq = DEC_SEQ // TM
    ctx = pl.BlockSpec((None, None, DF_W, PAST_LEN), lambda b, j: (b, l, 0, 0))
    return pl.pallas_call(
        functools.partial(_lat_attn_kernel, lam_init),
        grid=(DEC_BATCH, nq),
        in_specs=[pl.BlockSpec((TM, QK_PACK), lambda b, j: (b * nq + j, 0)),
                  pl.BlockSpec((DEC_SEQ, QK_PACK), lambda b, j: (b, 0)),
                  pl.BlockSpec((DEC_SEQ, V_PACK), lambda b, j: (b, 0)),
                  pl.BlockSpec((None, PAST_LEN, W_MLA_P), lambda b, j: (b, 0, 0)),
                  pl.BlockSpec((None, PAST_LEN, MLA_W), lambda b, j: (b, 0, 0)),
                  ctx, ctx,
                  pl.BlockSpec((TM, NA_W), lambda b, j: (b * nq + j, 0)),
                  _layer_spec(l, (N_VEC, D_MODEL))],
        out_specs=pl.BlockSpec((TM, D_MODEL), lambda b, j: (b * nq + j, 0)),
        out_shape=jax.ShapeDtypeStruct((n_tok, D_MODEL), BF16),
        compiler_params=_params(2),
        name="latent_attention",
    )(q, k, v, kc, vc, kct, vct, o_na, t['vec'])


def kernel(x_prompt, x_sample, cache_mla_ckv, cache_mla_krope, cache_na_k, cache_na_v, cache_df_k, cache_df_v, c, c_ctx, w_mod, b_mod, g_mix, w_in, g_qa, w_uq, g_kva, w_ukv, g_mla_q, g_mla_k, g_na_q, g_na_k, na_rpb, g_df_q, g_df_k, df_lq1, df_lk1, df_lq2, df_lk2, g_df_sub, w_out, g_ffn, w_gate, w_up, w_down):
    p = dict(g_mix=g_mix, w_in=w_in, g_qa=g_qa, w_uq=w_uq, g_kva=g_kva, w_ukv=w_ukv, g_mla_q=g_mla_q, g_mla_k=g_mla_k,
             g_na_q=g_na_q, g_na_k=g_na_k, na_rpb=na_rpb, g_df_q=g_df_q, g_df_k=g_df_k, df_lq1=df_lq1, df_lk1=df_lk1,
             df_lq2=df_lq2, df_lk2=df_lk2, g_df_sub=g_df_sub, w_out=w_out, g_ffn=g_ffn, w_gate=w_gate, w_up=w_up,
             w_down=w_down)
    consts = dict(bd64=_block_diag(NA_HD), bd32=_block_diag(DF_QK), rope=_rope_tables())
    t = _tables(p)

    c_all = jnp.concatenate([c_ctx[None, :], c, jnp.zeros((N_MOD - 1 - DEC_BATCH, D_MODEL), F32)], axis=0)
    mods = _modulation(c_all, w_mod, b_mod).reshape(DEPTH * N_MOD, 1, 6 * D_MODEL)

    krope_t = jnp.swapaxes(cache_mla_krope, -1, -2)
    na_kt, na_vt, df_kt, df_vt = (_feature_major(a) for a in (cache_na_k, cache_na_v, cache_df_k, cache_df_v))

    xp = x_prompt.reshape(BATCH * SEQ, D_MODEL)
    xs = x_sample.reshape(DEC_BATCH * DEC_SEQ, D_MODEL)
    new_caches = ()
    for l in range(DEPTH):
        lam_init = 0.8 - 0.6 * math.exp(-0.3 * l)
        mix, *new_caches = _context_mixer(l, lam_init, xp, mods, t, consts, new_caches)
        xp = _finish(l, xp, mix, mods, t, lambda i: 0)
        q, k, v = _latent_front(l, xs, mods, t, consts)
        kc, vc = _context_kv(l, cache_mla_ckv, krope_t, t)
        o_na = _latent_na(l, q, k, v, na_kt, na_vt, t['g_rows'])
        mix_s = _latent_attention(l, lam_init, q, k, v, kc, vc, df_kt, df_vt, o_na, t)
        xs = _finish(l, xs, mix_s, mods, t, lambda i: 1 + i // (DEC_SEQ // TM_FINISH))
    ckv_new, *narrow = new_caches
    return (xp.reshape(BATCH, SEQ, D_MODEL), xs.reshape(DEC_BATCH, DEC_SEQ, D_MODEL), ckv_new,
            *(jnp.swapaxes(a, -1, -2) for a in narrow))
```

```python
import functools
import math

import numpy as np
import jax
import jax.numpy as jnp
from jax import lax
from jax.experimental import pallas as pl
from jax.experimental.pallas import tpu as pltpu

F32 = jnp.float32
BF16 = jnp.bfloat16

D_MODEL = 1024
BATCH = 32
SEQ = 256
DEPTH = 2
DEC_BATCH = 2
DEC_SEQ = 1024
PAST_LEN = 256
GRID_W = 64
GRID_ROWS = DEC_SEQ // GRID_W
MLA_HEADS = 6
MLA_NOPE = 64
MLA_ROPE = 32
MLA_QK = MLA_NOPE + MLA_ROPE
MLA_V = 64
MLA_PAD = 128
Q_LORA = 256
KV_LORA = 128
NA_HEADS = 6
NA_HD = 64
NA_KR = 8
NA_KW = 16
DF_HEADS = 4
DF_HD = 64
DF_QK = 32
MLA_W = MLA_HEADS * MLA_V
NA_W = NA_HEADS * NA_HD
DF_W = DF_HEADS * DF_HD
D_FF = -(-8 * D_MODEL // (3 * 256)) * 256
ROPE_BASE = 10000.0
EPS = 1e-6
NEG = -1e30
LOG2E = math.log2(math.e)

LANES = 128
MXU_DIM = 256

A_CQ = 0
A_CKV = A_CQ + Q_LORA
A_KR = A_CKV + KV_LORA
A_COLS = A_KR + LANES
B_SRC = Q_LORA + KV_LORA + MLA_ROPE
B_NAQ = 0
B_NAK = B_NAQ + NA_W
B_NAV = B_NAK + NA_W
B_DFQ = B_NAV + NA_W
B_DFK = B_DFQ + DF_W
B_DFV = B_DFK + DF_W
B_COLS = B_DFV + DF_W

W_MLA_P = MLA_HEADS * MLA_PAD
QK_PACK = W_MLA_P + NA_W + DF_W
V_PACK = MLA_W + NA_W + DF_W
P_MLA = 0
P_NA = W_MLA_P
P_DF = P_NA + NA_W
PV_MLA = 0
PV_NA = MLA_W
PV_DF = MLA_W + NA_W

(V_GMIX, V_GFFN, V_GQA, V_GKVA, V_GMQ, V_GMK, V_GNQ, V_GNK, V_GDQ, V_GDK, V_GDS,
 V_LQ1, V_LK1, V_LQ2, V_LK2) = range(15)
N_VEC = 16

N_MOD = 8
TM = 512
TM_LAT_FRONT = 256
TM_FINISH = 512
CTX_SEQS_PER_STEP = 2
NA_ROWS_PER_STEP = 4
VMEM_LIMIT = 56 * 1024 * 1024

NA_PAIR_TILES = 2 * NA_KR - 2


def _dot(a, b):
    return jnp.dot(a, b, preferred_element_type=F32)


def _dot_nt(a, b):
    return lax.dot_general(a, b, (((1,), (1,)), ((), ())), preferred_element_type=F32)


def _lane_iota(shape):
    return lax.broadcasted_iota(jnp.int32, shape, len(shape) - 1)


def _rms_rows(x, g):
    ms = jnp.mean(x * x, axis=-1, keepdims=True)
    return x * lax.rsqrt(ms + EPS) * g


def _tile_rms(x, g, n_real):
    outs = []
    for c0 in range(0, x.shape[1], LANES):
        xt = x[:, c0:c0 + LANES]
        ms = jnp.sum(xt * xt, axis=-1, keepdims=True) * (1.0 / n_real)
        outs.append(xt * lax.rsqrt(ms + EPS) * g[:, c0:c0 + LANES])
    return jnp.concatenate(outs, axis=1)


def _seg_rms(x, bd_ref, g, group):
    width = x.shape[1]
    sq = (x * x).astype(BF16)
    parts = []
    for c0 in range(0, width, MXU_DIM):
        w = min(MXU_DIM, width - c0)
        parts.append(_dot(sq[:, c0:c0 + w], bd_ref[0:w, 0:w]))
    ss = parts[0] if len(parts) == 1 else jnp.concatenate(parts, axis=1)
    return x * lax.rsqrt(ss * (1.0 / group) + EPS) * g


def _rope_tiles(x, cos, sa, sb):
    outs = []
    for t in range(x.shape[1] // LANES):
        xt = x[:, t * LANES:(t + 1) * LANES]
        up = pltpu.roll(xt, LANES - MLA_ROPE // 4, 1)
        dn = pltpu.roll(xt, MLA_ROPE // 4, 1)
        outs.append(xt * cos + up * sa + dn * sb)
    return outs[0] if len(outs) == 1 else jnp.concatenate(outs, axis=1)


def _diff_lambda(vec_ref, lam_init):
    a = jnp.sum(vec_ref[V_LQ1:V_LQ1 + 1, 0:DF_QK] * vec_ref[V_LK1:V_LK1 + 1, 0:DF_QK], axis=-1, keepdims=True)
    b = jnp.sum(vec_ref[V_LQ2:V_LQ2 + 1, 0:DF_QK] * vec_ref[V_LK2:V_LK2 + 1, 0:DF_QK], axis=-1, keepdims=True)
    return jnp.exp(a) - jnp.exp(b) + lam_init


def _mixer_front(x, mod, vec_ref, wa_ref, wb_ref, w_uq_ref, bd64_ref, bd32_ref):
    sh = mod[:, 0:D_MODEL]
    sc = mod[:, D_MODEL:2 * D_MODEL]
    h = (_rms_rows(x, vec_ref[V_GMIX:V_GMIX + 1, :]) * (1.0 + sc) + sh).astype(BF16)
    za = _dot_nt(h, wa_ref[...])
    zb = _dot_nt(h, wb_ref[...])
    cqn = _rms_rows(za[:, A_CQ:A_CQ + Q_LORA], vec_ref[V_GQA:V_GQA + 1, 0:Q_LORA])
    q_raw = _dot_nt(cqn.astype(BF16), w_uq_ref[...])
    q_mla = _tile_rms(q_raw, vec_ref[V_GMQ:V_GMQ + 1, 0:W_MLA_P] * (MLA_QK ** -0.5 * LOG2E), MLA_QK)
    ckv_n = _rms_rows(za[:, A_CKV:A_CKV + KV_LORA], vec_ref[V_GKVA:V_GKVA + 1, 0:KV_LORA])
    kr_tile = za[:, A_KR:A_KR + LANES]
    q_na = _seg_rms(zb[:, B_NAQ:B_NAQ + NA_W], bd64_ref, vec_ref[V_GNQ:V_GNQ + 1, 0:NA_W] * (NA_HD ** -0.5 * LOG2E), NA_HD)
    k_na = _seg_rms(zb[:, B_NAK:B_NAK + NA_W], bd64_ref, vec_ref[V_GNK:V_GNK + 1, 0:NA_W], NA_HD)
    v_na = zb[:, B_NAV:B_NAV + NA_W]
    q_df = _seg_rms(zb[:, B_DFQ:B_DFQ + DF_W], bd32_ref, vec_ref[V_GDQ:V_GDQ + 1, 0:DF_W] * (DF_QK ** -0.5 * LOG2E), DF_QK)
    k_df = _seg_rms(zb[:, B_DFK:B_DFK + DF_W], bd32_ref, vec_ref[V_GDK:V_GDK + 1, 0:DF_W], DF_QK)
    v_df = zb[:, B_DFV:B_DFV + DF_W]
    return q_mla, ckv_n, kr_tile, q_na, k_na, v_na, q_df, k_df, v_df


def _mla_kv(ckv_n, kr_tile, vec_ref, w_ukv_ref):
    kv = _dot(ckv_n.astype(BF16), w_ukv_ref[...])
    lane = _lane_iota((1, LANES))
    kr = jnp.where((lane >= MLA_NOPE) & (lane < MLA_QK), kr_tile, 0.0)
    k_pre = kv[:, 0:W_MLA_P] + jnp.concatenate([kr] * MLA_HEADS, axis=1)
    k = _tile_rms(k_pre, vec_ref[V_GMK:V_GMK + 1, 0:W_MLA_P], MLA_QK)
    return k, kv[:, W_MLA_P:W_MLA_P + MLA_W]


def _softmax_parts(s):
    m = jnp.max(s, axis=-1, keepdims=True)
    p = jnp.exp2(s - m)
    return p, 1.0 / jnp.sum(p, axis=-1, keepdims=True)


def _scores(q, k_segs):
    parts = [_dot(q, k) if feature_major else _dot_nt(q, k) for k, feature_major in k_segs]
    return parts[0] if len(parts) == 1 else jnp.concatenate(parts, axis=1)


def _pv(p, v_segs):
    out = None
    c0 = 0
    for v, feature_major in v_segs:
        n = v.shape[1] if feature_major else v.shape[0]
        o = _dot_nt(p[:, c0:c0 + n], v) if feature_major else _dot(p[:, c0:c0 + n], v)
        out = o if out is None else out + o
        c0 += n
    return out


def _seg_tile(seg, t):
    a, feature_major = seg
    return (a[t * LANES:(t + 1) * LANES, :] if feature_major else a[:, t * LANES:(t + 1) * LANES]), feature_major


def _lane_groups(qt, width):
    lane = _lane_iota((1, LANES))
    zero = jnp.zeros_like(qt)
    return jnp.concatenate(
        [jnp.where((lane >= g * width) & (lane < (g + 1) * width), qt, zero) for g in range(LANES // width)], axis=0)


def _pair_select(o2):
    tq = o2.shape[0] // 2
    return jnp.where(_lane_iota((1, LANES)) < NA_HD, o2[0:tq], o2[tq:2 * tq])


def _mla_attend(q, k_segs, v_segs):
    outs = []
    for t in range(MLA_HEADS // 2):
        vt = [_seg_tile(v, t) for v in v_segs]
        halves = []
        for h in (2 * t, 2 * t + 1):
            p, il = _softmax_parts(_scores(q[:, h * MLA_PAD:(h + 1) * MLA_PAD], [_seg_tile(k, h) for k in k_segs]))
            halves.append(_pv(p.astype(BF16), vt) * il)
        outs.append(jnp.where(_lane_iota((1, LANES)) < MLA_V, halves[0], halves[1]))
    return jnp.concatenate(outs, axis=1)


def _na_tile_attend(qt, k_segs, v_segs, bias=None):
    s = _scores(_lane_groups(qt, NA_HD), k_segs)
    if bias is not None:
        nb = bias.shape[1]
        s = jnp.concatenate([s[:, 0:nb] + bias, s[:, nb:]], axis=1)
    p, il = _softmax_parts(s)
    return _pair_select(_pv(p.astype(BF16), v_segs) * il)


def _na_attend_full(q, k_segs, v_segs):
    return jnp.concatenate(
        [_na_tile_attend(q[:, t * LANES:(t + 1) * LANES], [_seg_tile(k, t) for k in k_segs],
                         [_seg_tile(v, t) for v in v_segs]) for t in range(NA_HEADS // 2)], axis=1)


def _df_attend(q, k_segs, v_segs, lam, g_sub, out_scale):
    outs = []
    lane = _lane_iota((1, LANES))
    tq = q.shape[0]
    for t in range(DF_HEADS // 2):
        kt = [_seg_tile(k, t) for k in k_segs]
        vt = [_seg_tile(v, t) for v in v_segs]
        p, il = _softmax_parts(_scores(_lane_groups(q[:, t * LANES:(t + 1) * LANES], DF_QK), kt))
        pn = []
        for hh in range(2):
            r1, r2 = 2 * hh * tq, (2 * hh + 1) * tq
            pn.append((p[r1:r1 + tq] * il[r1:r1 + tq] - p[r2:r2 + tq] * (lam * il[r2:r2 + tq])).astype(BF16))
        o = _pair_select(_pv(jnp.concatenate(pn, axis=0), vt))
        o2 = o * o
        ms_e = jnp.sum(jnp.where(lane < DF_HD, o2, 0.0), axis=-1, keepdims=True)
        ms_o = jnp.sum(jnp.where(lane >= DF_HD, o2, 0.0), axis=-1, keepdims=True)
        r = lax.rsqrt(jnp.where(lane < DF_HD, ms_e, ms_o) * (1.0 / DF_HD) + EPS)
        outs.append(o * r * (g_sub[:, t * LANES:(t + 1) * LANES] * out_scale))
    return jnp.concatenate(outs, axis=1)


def _mod_kernel(c_ref, w_ref, b_ref, o_ref):
    c = c_ref[...]
    s = c * jax.nn.sigmoid(c)
    o_ref[...] = _dot(s.astype(BF16), w_ref[...].astype(BF16)) + b_ref[...]


def _ctx_kernel(lam_init, n_alias, x_ref, mod_ref, vec_ref, wa_ref, wb_ref, w_uq_ref, w_ukv_ref, bd64_ref, bd32_ref,
                *rest):
    mix_ref, *cache_refs = rest[n_alias:]
    if n_alias == 0:
        for ref in cache_refs:
            ref[:, 1:] = jnp.zeros((ref.shape[0], ref.shape[1] - 1) + ref.shape[2:], F32)
        cache_refs = [ref.at[:, 0] for ref in cache_refs]
    ckv_ref, kr_ref, nak_ref, nav_ref, dfk_ref, dfv_ref = cache_refs
    lam = _diff_lambda(vec_ref, lam_init)
    mod = mod_ref[...]
    bf = lambda a: a.astype(BF16)
    seg = lambda a: [(a.astype(BF16), False)]
    for s in range(CTX_SEQS_PER_STEP):
        rows = slice(s * SEQ, (s + 1) * SEQ)
        q_mla, ckv_n, kr_tile, q_na, k_na, v_na, q_df, k_df, v_df = _mixer_front(
            x_ref[rows, :], mod, vec_ref, wa_ref, wb_ref, w_uq_ref, bd64_ref, bd32_ref)
        k_mla, v_mla = _mla_kv(ckv_n, kr_tile, vec_ref, w_ukv_ref)
        ckv_ref[s] = ckv_n
        kr_ref[s] = kr_tile.T[0:MLA_ROPE]
        for ref, a in ((nak_ref, k_na), (nav_ref, v_na), (dfk_ref, k_df), (dfv_ref, v_df)):
            at = a.T
            for h in range(ref.shape[1]):
                ref[s, h] = at[h * NA_HD:(h + 1) * NA_HD]
        o_mla = _mla_attend(bf(q_mla), seg(k_mla), seg(v_mla))
        o_na = _na_attend_full(bf(q_na), seg(k_na), seg(v_na))
        o_df = _df_attend(bf(q_df), seg(k_df), seg(v_df), lam, vec_ref[V_GDS:V_GDS + 1, 0:DF_W], 1.0 - lam_init)
        mix_ref[rows, :] = jnp.concatenate([o_mla, o_na, o_df], axis=1).astype(BF16)


def _finish_kernel(x_ref, mix_ref, mod_ref, vec_ref, w_out_ref, w_gate_ref, w_up_ref, w_down_ref, y_ref):
    mod = mod_ref[...]
    gate_m = mod[:, 2 * D_MODEL:3 * D_MODEL]
    sh = mod[:, 3 * D_MODEL:4 * D_MODEL]
    sc = mod[:, 4 * D_MODEL:5 * D_MODEL]
    gate_f = mod[:, 5 * D_MODEL:6 * D_MODEL]
    x1 = x_ref[...] + gate_m * _dot(mix_ref[...], w_out_ref[...])
    h = (_rms_rows(x1, vec_ref[V_GFFN:V_GFFN + 1, :]) * (1.0 + sc) + sh).astype(BF16)
    g = _dot(h, w_gate_ref[...])
    u = _dot(h, w_up_ref[...])
    a = (g * jax.nn.sigmoid(g) * u).astype(BF16)
    y_ref[...] = x1 + gate_f * _dot(a, w_down_ref[...])


def _lat_front_kernel(x_ref, mod_ref, vec_ref, rope_ref, wa_ref, wb_ref, w_uq_ref, w_ukv_ref, bd64_ref, bd32_ref,
                      q_ref, k_ref, v_ref):
    q_mla, ckv_n, kr_tile, q_na, k_na, v_na, q_df, k_df, v_df = _mixer_front(
        x_ref[...], mod_ref[...], vec_ref, wa_ref, wb_ref, w_uq_ref, bd64_ref, bd32_ref)
    k_mla, v_mla = _mla_kv(ckv_n, kr_tile, vec_ref, w_ukv_ref)
    cm, sam, sbm = rope_ref[0], rope_ref[1], rope_ref[2]
    cd, sad, sbd = rope_ref[3], rope_ref[4], rope_ref[5]
    q_ref[...] = jnp.concatenate(
        [_rope_tiles(q_mla, cm, sam, sbm), q_na, _rope_tiles(q_df, cd, sad, sbd)], axis=1).astype(BF16)
    k_ref[...] = jnp.concatenate(
        [_rope_tiles(k_mla, cm, sam, sbm), k_na, _rope_tiles(k_df, cd, sad, sbd)], axis=1).astype(BF16)
    v_ref[...] = jnp.concatenate([v_mla, v_na, v_df], axis=1).astype(BF16)


def _ctx_kv_kernel(ckv_ref, krt_ref, vec_ref, w_ukv_ref, k_ref, v_ref):
    krt = jnp.concatenate([jnp.zeros((MLA_NOPE, PAST_LEN), F32), krt_ref[...],
                           jnp.zeros((LANES - MLA_QK, PAST_LEN), F32)], axis=0)
    k_mla, v_mla = _mla_kv(ckv_ref[...], krt.T, vec_ref, w_ukv_ref)
    k_ref[...] = k_mla.astype(BF16)
    v_ref[...] = v_mla.astype(BF16)


def _na_lat_kernel(q_ref, kl_ref, vl_ref, kct_ref, vct_ref, g_ref, o_ref, bias_ref):
    b = pl.program_id(0)
    j = pl.program_id(1)

    @pl.when((b == 0) & (j == 0))
    def _build_bias():
        c = lax.broadcasted_iota(jnp.int32, (GRID_W, LANES), 0)
        kc = _lane_iota((GRID_W, LANES)) % GRID_W
        start = jnp.clip(c - NA_KW // 2, 0, GRID_W - NA_KW)
        in_win = (kc >= start) & (kc < start + NA_KW)

        def body(i, carry):
            row = jnp.broadcast_to(g_ref[pl.ds(i, 1), :], (GRID_W, LANES))
            toep = pltpu.roll(row, 0, 1, stride=1, stride_axis=0)
            bias_ref[i] = jnp.where(in_win, toep * LOG2E, NEG)
            return carry

        lax.fori_loop(0, NA_HEADS * NA_PAIR_TILES, body, 0)

    n_win = NA_KR * GRID_W
    kct = kct_ref[...].astype(BF16)
    vct = vct_ref[...].astype(BF16)

    def win_start(r):
        return jnp.clip(r - NA_KR // 2, 0, GRID_ROWS - NA_KR)

    def attend(grid_rows, rs):
        a0, n = grid_rows[0], len(grid_rows)
        row0 = pl.multiple_of(rs * GRID_W, GRID_W)
        outs = []
        for t in range(NA_HEADS // 2):
            lanes = slice(t * LANES, (t + 1) * LANES)
            bias = jnp.concatenate(
                [jnp.concatenate([bias_ref[h * NA_PAIR_TILES + (rs - (j * NA_ROWS_PER_STEP + a) + NA_KR - 1) + 2 * m]
                                  for m in range(NA_KR // 2)], axis=1)
                 for h in (2 * t, 2 * t + 1) for a in grid_rows], axis=0)
            outs.append(_na_tile_attend(
                q_ref[a0 * GRID_W:(a0 + n) * GRID_W, lanes],
                [(kl_ref[pl.ds(row0, n_win), lanes], False), (kct[lanes, :], True)],
                [(vl_ref[pl.ds(row0, n_win), lanes], False), (vct[lanes, :], True)], bias))
        return jnp.concatenate(outs, axis=1).astype(BF16)

    first_r, last_r = j * NA_ROWS_PER_STEP, (j + 1) * NA_ROWS_PER_STEP - 1
    shared = win_start(first_r) == win_start(last_r)

    @pl.when(shared)
    def _shared_window():
        o_ref[...] = attend(tuple(range(NA_ROWS_PER_STEP)), win_start(first_r))

    @pl.when(jnp.logical_not(shared))
    def _per_row_windows():
        for a in range(NA_ROWS_PER_STEP):
            o_ref[a * GRID_W:(a + 1) * GRID_W, :] = attend((a,), win_start(first_r + a))


def _lat_attn_kernel(lam_init, q_ref, kl_ref, vl_ref, kc_ref, vc_ref, kct_ref, vct_ref, ona_ref, vec_ref, mix_ref):
    o_mla = _mla_attend(q_ref[:, P_MLA:P_MLA + W_MLA_P],
                        [(kc_ref[...], False), (kl_ref[:, P_MLA:P_MLA + W_MLA_P], False)],
                        [(vc_ref[...], False), (vl_ref[:, PV_MLA:PV_MLA + MLA_W], False)])
    lam = _diff_lambda(vec_ref, lam_init)
    o_df = _df_attend(q_ref[:, P_DF:P_DF + DF_W],
                      [(kct_ref[...].astype(BF16), True), (kl_ref[:, P_DF:P_DF + DF_W], False)],
                      [(vct_ref[...].astype(BF16), True), (vl_ref[:, PV_DF:PV_DF + DF_W], False)],
                      lam, vec_ref[V_GDS:V_GDS + 1, 0:DF_W], 1.0 - lam_init)
    mix_ref[...] = jnp.concatenate([o_mla.astype(BF16), ona_ref[...], o_df.astype(BF16)], axis=1)


def _const_spec(shape):
    nd = len(shape)
    return pl.BlockSpec(shape, lambda *_: (0,) * nd, pipeline_mode=pl.Buffered(1))


def _layer_spec(l, shape):
    nd = len(shape)
    return pl.BlockSpec((None,) + tuple(shape), lambda *_: (l,) + (0,) * nd, pipeline_mode=pl.Buffered(1))


def _params(n_axes):
    return pltpu.CompilerParams(dimension_semantics=("arbitrary",) * n_axes, vmem_limit_bytes=VMEM_LIMIT)


def _block_diag(group):
    i = np.arange(MXU_DIM) // group
    return jnp.asarray((i[:, None] == i[None, :]).astype(np.float32), dtype=BF16)


def _rope_tables():
    t = np.arange(DEC_SEQ)
    row = (t // GRID_W).astype(np.float64)
    col = (t % GRID_W).astype(np.float64)
    n = MLA_ROPE // 4
    inv = 1.0 / (ROPE_BASE ** (np.arange(n, dtype=np.float64) * 2.0 / (MLA_ROPE // 2)))
    ar = row[:, None] * inv
    ac = col[:, None] * inv
    ang = np.concatenate([ar, ar, ac, ac], axis=-1)
    cos32, sin32 = np.cos(ang), np.sin(ang)
    first = (np.arange(MLA_ROPE) % (2 * n)) < n
    sa32 = np.where(first, -sin32, 0.0)
    sb32 = np.where(first, 0.0, sin32)

    def mla_tile(v32, fill):
        out = np.full((DEC_SEQ, LANES), fill)
        out[:, MLA_NOPE:MLA_QK] = v32
        return out

    tabs = [mla_tile(cos32, 1.0), mla_tile(sa32, 0.0), mla_tile(sb32, 0.0),
            np.tile(cos32, (1, LANES // DF_QK)), np.tile(sa32, (1, LANES // DF_QK)), np.tile(sb32, (1, LANES // DF_QK))]
    return jnp.asarray(np.stack(tabs).astype(np.float32))


def _feature_major(a):
    a = jnp.swapaxes(a, -1, -2)
    return a.reshape(a.shape[:-3] + (a.shape[-3] * a.shape[-2], a.shape[-1]))


def _tables(p):
    w_in_t = jnp.swapaxes(p['w_in'], 1, 2)
    kr = w_in_t[:, Q_LORA + KV_LORA:B_SRC]
    z32 = jnp.zeros_like(kr)
    wa = jnp.concatenate([w_in_t[:, :Q_LORA + KV_LORA], kr, z32, kr, z32], axis=1).astype(BF16)
    wb = w_in_t[:, B_SRC:].astype(BF16)
    w_uq_t = jnp.swapaxes(p['w_uq'], 1, 2).reshape(DEPTH, MLA_HEADS, MLA_QK, Q_LORA)
    w_uq_p = jnp.pad(w_uq_t, ((0, 0), (0, 0), (0, MLA_PAD - MLA_QK), (0, 0))).reshape(DEPTH, W_MLA_P, Q_LORA).astype(BF16)
    w_ukv = p['w_ukv'].reshape(DEPTH, KV_LORA, MLA_HEADS, MLA_NOPE + MLA_V)
    wk = jnp.pad(w_ukv[..., :MLA_NOPE], ((0, 0), (0, 0), (0, 0), (0, MLA_PAD - MLA_NOPE))).reshape(DEPTH, KV_LORA, W_MLA_P)
    wv = w_ukv[..., MLA_NOPE:].reshape(DEPTH, KV_LORA, MLA_W)
    w_ukv_r = jnp.concatenate([wk, wv], axis=2).astype(BF16)

    def row(v, reps=1):
        v = jnp.tile(v, (1, reps)) if reps > 1 else v
        return [v, jnp.zeros((DEPTH, D_MODEL - v.shape[1]), F32)] if v.shape[1] < D_MODEL else [v]

    pad_head = lambda g: jnp.pad(g, ((0, 0), (0, MLA_PAD - MLA_QK)))
    pieces = (row(p['g_mix']) + row(p['g_ffn']) + row(p['g_qa']) + row(p['g_kva'])
              + row(pad_head(p['g_mla_q']), MLA_HEADS) + row(pad_head(p['g_mla_k']), MLA_HEADS)
              + row(p['g_na_q'], NA_HEADS) + row(p['g_na_k'], NA_HEADS)
              + row(p['g_df_q'], 2 * DF_HEADS) + row(p['g_df_k'], 2 * DF_HEADS) + row(p['g_df_sub'], DF_HEADS)
              + row(p['df_lq1']) + row(p['df_lk1']) + row(p['df_lq2']) + row(p['df_lk2'])
              + [jnp.zeros((DEPTH, D_MODEL), F32)])
    vec = jnp.concatenate(pieces, axis=1).reshape(DEPTH, N_VEC, D_MODEL)
    f = p['na_rpb']
    zpad = jnp.zeros((DEPTH, NA_HEADS, NA_PAIR_TILES, 33), F32)
    g_rows = jnp.concatenate([f[:, :, :-1, NA_KW - 1:], zpad, f[:, :, 1:, :], zpad, f[:, :, :-1, :NA_KW - 1]], axis=-1)
    g_rows = g_rows.reshape(DEPTH, NA_HEADS * NA_PAIR_TILES, LANES)
    return dict(wa=wa, wb=wb, w_uq=w_uq_p, w_ukv=w_ukv_r, vec=vec, g_rows=g_rows,
                w_out=p['w_out'].astype(BF16), w_gate=p['w_gate'].astype(BF16),
                w_up=p['w_up'].astype(BF16), w_down=p['w_down'].astype(BF16))


def _modulation(c_all, w_mod, b_mod):
    tn = 2048
    return pl.pallas_call(
        _mod_kernel,
        grid=(DEPTH, 6 * D_MODEL // tn),
        in_specs=[pl.BlockSpec((N_MOD, D_MODEL), lambda l, j: (0, 0)),
                  pl.BlockSpec((None, D_MODEL, tn), lambda l, j: (l, 0, j)),
                  pl.BlockSpec((None, 1, tn), lambda l, j: (l, 0, j))],
        out_specs=pl.BlockSpec((None, N_MOD, tn), lambda l, j: (l, 0, j)),
        out_shape=jax.ShapeDtypeStruct((DEPTH, N_MOD, 6 * D_MODEL), F32),
        compiler_params=_params(2),
        name="modulation",
    )(c_all, w_mod, b_mod.reshape(DEPTH, 1, 6 * D_MODEL))


def _front_weight_specs(l):
    return [_layer_spec(l, (N_VEC, D_MODEL)), _layer_spec(l, (A_COLS, D_MODEL)), _layer_spec(l, (B_COLS, D_MODEL)),
            _layer_spec(l, (W_MLA_P, Q_LORA)), _layer_spec(l, (KV_LORA, W_MLA_P + MLA_W)),
            _const_spec((MXU_DIM, MXU_DIM)), _const_spec((MXU_DIM, MXU_DIM))]


def _front_weights(t, consts):
    return (t['vec'], t['wa'], t['wb'], t['w_uq'], t['w_ukv'], consts['bd64'], consts['bd32'])


def _context_mixer(l, lam_init, x, mods, t, consts, prev_caches):
    n_tok = BATCH * SEQ
    n_alias = len(prev_caches)
    tok = lambda w: pl.BlockSpec((CTX_SEQS_PER_STEP * SEQ, w), lambda b: (b, 0))
    if n_alias == 0:
        assert l == 0
        lay = lambda *s: pl.BlockSpec((CTX_SEQS_PER_STEP, DEPTH) + s, lambda b: (b, 0) + (0,) * len(s))
    else:
        lay = lambda *s: pl.BlockSpec((CTX_SEQS_PER_STEP, None) + s, lambda b: (b, l) + (0,) * len(s))
    cache_shapes = [(SEQ, KV_LORA), (MLA_ROPE, SEQ), (NA_HEADS, NA_HD, SEQ), (NA_HEADS, NA_HD, SEQ),
                    (DF_HEADS, DF_HD, SEQ), (DF_HEADS, DF_HD, SEQ)]
    weights = _front_weights(t, consts)
    n_in = 2 + len(weights)
    return pl.pallas_call(
        functools.partial(_ctx_kernel, lam_init, n_alias),
        grid=(BATCH // CTX_SEQS_PER_STEP,),
        in_specs=[tok(D_MODEL), pl.BlockSpec((None, 1, 6 * D_MODEL), lambda b: (l * N_MOD, 0, 0))]
        + _front_weight_specs(l) + [pl.BlockSpec(memory_space=pl.ANY)] * n_alias,
        out_specs=[tok(D_MODEL)] + [lay(*s) for s in cache_shapes],
        out_shape=[jax.ShapeDtypeStruct((n_tok, D_MODEL), BF16)]
        + [jax.ShapeDtypeStruct((BATCH, DEPTH) + s, F32) for s in cache_shapes],
        input_output_aliases={n_in + i: 1 + i for i in range(n_alias)},
        compiler_params=_params(1),
        name="context_mixer",
    )(x, mods, *weights, *prev_caches)


def _finish(l, x, mix, mods, t, mod_row):
    n_tok = x.shape[0]
    tok = pl.BlockSpec((TM_FINISH, D_MODEL), lambda i: (i, 0))
    return pl.pallas_call(
        _finish_kernel,
        grid=(n_tok // TM_FINISH,),
        in_specs=[tok, tok, pl.BlockSpec((None, 1, 6 * D_MODEL), lambda i: (l * N_MOD + mod_row(i), 0, 0)),
                  _layer_spec(l, (N_VEC, D_MODEL)), _layer_spec(l, (D_MODEL, D_MODEL)), _layer_spec(l, (D_MODEL, D_FF)),
                  _layer_spec(l, (D_MODEL, D_FF)), _layer_spec(l, (D_FF, D_MODEL))],
        out_specs=tok,
        out_shape=jax.ShapeDtypeStruct((n_tok, D_MODEL), F32),
        compiler_params=_params(1),
        name="finish",
    )(x, mix, mods, t['vec'], t['w_out'], t['w_gate'], t['w_up'], t['w_down'])


def _latent_front(l, x, mods, t, consts):
    n_tok = DEC_BATCH * DEC_SEQ
    tm = TM_LAT_FRONT
    blocks_per_seq = DEC_SEQ // tm
    tok = lambda w: pl.BlockSpec((tm, w), lambda i: (i, 0))
    wspecs = _front_weight_specs(l)
    weights = _front_weights(t, consts)
    return pl.pallas_call(
        _lat_front_kernel,
        grid=(n_tok // tm,),
        in_specs=[tok(D_MODEL),
                  pl.BlockSpec((None, 1, 6 * D_MODEL), lambda i: (l * N_MOD + 1 + i // blocks_per_seq, 0, 0)),
                  wspecs[0], pl.BlockSpec((6, tm, LANES), lambda i: (0, i % blocks_per_seq, 0))] + wspecs[1:],
        out_specs=[tok(QK_PACK), tok(QK_PACK), tok(V_PACK)],
        out_shape=[jax.ShapeDtypeStruct((n_tok, QK_PACK), BF16), jax.ShapeDtypeStruct((n_tok, QK_PACK), BF16),
                   jax.ShapeDtypeStruct((n_tok, V_PACK), BF16)],
        compiler_params=_params(1),
        name="latent_front",
    )(x, mods, weights[0], consts['rope'], *weights[1:])


def _context_kv(l, ckv, krope_t, t):
    lay = lambda r, w: pl.BlockSpec((None, None, r, w), lambda b: (b, l, 0, 0))
    out = lambda w: pl.BlockSpec((None, PAST_LEN, w), lambda b: (b, 0, 0))
    return pl.pallas_call(
        _ctx_kv_kernel,
        grid=(DEC_BATCH,),
        in_specs=[lay(PAST_LEN, KV_LORA), lay(MLA_ROPE, PAST_LEN),
                  _layer_spec(l, (N_VEC, D_MODEL)), _layer_spec(l, (KV_LORA, W_MLA_P + MLA_W))],
        out_specs=[out(W_MLA_P), out(MLA_W)],
        out_shape=[jax.ShapeDtypeStruct((DEC_BATCH, PAST_LEN, W_MLA_P), BF16),
                   jax.ShapeDtypeStruct((DEC_BATCH, PAST_LEN, MLA_W), BF16)],
        compiler_params=_params(1),
        name="context_kv",
    )(ckv, krope_t, t['vec'], t['w_ukv'])


def _latent_na(l, q, k, v, kct, vct, g_rows):
    n_tok = DEC_BATCH * DEC_SEQ
    na_blk = P_NA // NA_W
    steps = GRID_ROWS // NA_ROWS_PER_STEP
    tq = NA_ROWS_PER_STEP * GRID_W
    ctx = pl.BlockSpec((None, None, NA_W, PAST_LEN), lambda b, j: (b, l, 0, 0))
    return pl.pallas_call(
        _na_lat_kernel,
        grid=(DEC_BATCH, steps),
        in_specs=[pl.BlockSpec((tq, NA_W), lambda b, j: (b * steps + j, na_blk)),
                  pl.BlockSpec((DEC_SEQ, NA_W), lambda b, j: (b, na_blk)),
                  pl.BlockSpec((DEC_SEQ, NA_W), lambda b, j: (b, PV_NA // NA_W)),
                  ctx, ctx, _layer_spec(l, (NA_HEADS * NA_PAIR_TILES, LANES))],
        out_specs=pl.BlockSpec((tq, NA_W), lambda b, j: (b * steps + j, 0)),
        out_shape=jax.ShapeDtypeStruct((n_tok, NA_W), BF16),
        scratch_shapes=[pltpu.VMEM((NA_HEADS * NA_PAIR_TILES, GRID_W, LANES), F32)],
        compiler_params=_params(2),
        name="latent_neighbourhood",
    )(q, k, v, kct, vct, g_rows)


def _latent_attention(l, lam_init, q, k, v, kc, vc, kct, vct, o_na, t):
    n_tok = DEC_BATCH * DEC_SEQ
    nq = DEC_SEQ // TM
    ctx = pl.BlockSpec((None, None, DF_W, PAST_LEN), lambda b, j: (b, l, 0, 0))
    return pl.pallas_call(
        functools.partial(_lat_attn_kernel, lam_init),
        grid=(DEC_BATCH, nq),
        in_specs=[pl.BlockSpec((TM, QK_PACK), lambda b, j: (b * nq + j, 0)),
                  pl.BlockSpec((DEC_SEQ, QK_PACK), lambda b, j: (b, 0)),
                  pl.BlockSpec((DEC_SEQ, V_PACK), lambda b, j: (b, 0)),
                  pl.BlockSpec((None, PAST_LEN, W_MLA_P), lambda b, j: (b, 0, 0)),
                  pl.BlockSpec((None, PAST_LEN, MLA_W), lambda b, j: (b, 0, 0)),
                  ctx, ctx,
                  pl.BlockSpec((TM, NA_W), lambda b, j: (b * nq + j, 0)),
                  _layer_spec(l, (N_VEC, D_MODEL))],
        out_specs=pl.BlockSpec((TM, D_MODEL), lambda b, j: (b * nq + j, 0)),
        out_shape=jax.ShapeDtypeStruct((n_tok, D_MODEL), BF16),
        compiler_params=_params(2),
        name="latent_attention",
    )(q, k, v, kc, vc, kct, vct, o_na, t['vec'])


def kernel(x_prompt, x_sample, cache_mla_ckv, cache_mla_krope, cache_na_k, cache_na_v, cache_df_k, cache_df_v, c, c_ctx, w_mod, b_mod, g_mix, w_in, g_qa, w_uq, g_kva, w_ukv, g_mla_q, g_mla_k, g_na_q, g_na_k, na_rpb, g_df_q, g_df_k, df_lq1, df_lk1, df_lq2, df_lk2, g_df_sub, w_out, g_ffn, w_gate, w_up, w_down):
    p = dict(g_mix=g_mix, w_in=w_in, g_qa=g_qa, w_uq=w_uq, g_kva=g_kva, w_ukv=w_ukv, g_mla_q=g_mla_q, g_mla_k=g_mla_k,
             g_na_q=g_na_q, g_na_k=g_na_k, na_rpb=na_rpb, g_df_q=g_df_q, g_df_k=g_df_k, df_lq1=df_lq1, df_lk1=df_lk1,
             df_lq2=df_lq2, df_lk2=df_lk2, g_df_sub=g_df_sub, w_out=w_out, g_ffn=g_ffn, w_gate=w_gate, w_up=w_up,
             w_down=w_down)
    consts = dict(bd64=_block_diag(NA_HD), bd32=_block_diag(DF_QK), rope=_rope_tables())
    t = _tables(p)

    c_all = jnp.concatenate([c_ctx[None, :], c, jnp.zeros((N_MOD - 1 - DEC_BATCH, D_MODEL), F32)], axis=0)
    mods = _modulation(c_all, w_mod, b_mod).reshape(DEPTH * N_MOD, 1, 6 * D_MODEL)

    krope_t = jnp.swapaxes(cache_mla_krope, -1, -2)
    na_kt, na_vt, df_kt, df_vt = (_feature_major(a) for a in (cache_na_k, cache_na_v, cache_df_k, cache_df_v))

    xp = x_prompt.reshape(BATCH * SEQ, D_MODEL)
    xs = x_sample.reshape(DEC_BATCH * DEC_SEQ, D_MODEL)
    new_caches = ()
    for l in range(DEPTH):
        lam_init = 0.8 - 0.6 * math.exp(-0.3 * l)
        mix, *new_caches = _context_mixer(l, lam_init, xp, mods, t, consts, new_caches)
        xp = _finish(l, xp, mix, mods, t, lambda i: 0)
        q, k, v = _latent_front(l, xs, mods, t, consts)
        kc, vc = _context_kv(l, cache_mla_ckv, krope_t, t)
        o_na = _latent_na(l, q, k, v, na_kt, na_vt, t['g_rows'])
        mix_s = _latent_attention(l, lam_init, q, k, v, kc, vc, df_kt, df_vt, o_na, t)
        xs = _finish(l, xs, mix_s, mods, t, lambda i: 1 + i // (DEC_SEQ // TM_FINISH))
    ckv_new, *narrow = new_caches
    return (xp.reshape(BATCH, SEQ, D_MODEL), xs.reshape(DEC_BATCH, DEC_SEQ, D_MODEL), ckv_new,
            *(jnp.swapaxes(a, -1, -2) for a in narrow))
```

```python
import functools
import math

import numpy as np
import jax
import jax.numpy as jnp
from jax import lax
from jax.experimental import pallas as pl
from jax.experimental.pallas import tpu as pltpu

F32 = jnp.float32
BF16 = jnp.bfloat16

D_MODEL = 1024
BATCH = 32
SEQ = 256
DEPTH = 2
DEC_BATCH = 2
DEC_SEQ = 1024
PAST_LEN = 256
GRID_W = 64
GRID_ROWS = DEC_SEQ // GRID_W
MLA_HEADS = 6
MLA_NOPE = 64
MLA_ROPE = 32
MLA_QK = MLA_NOPE + MLA_ROPE
MLA_V = 64
MLA_PAD = 128
Q_LORA = 256
KV_LORA = 128
NA_HEADS = 6
NA_HD = 64
NA_KR = 8
NA_KW = 16
DF_HEADS = 4
DF_HD = 64
DF_QK = 32
MLA_W = MLA_HEADS * MLA_V
NA_W = NA_HEADS * NA_HD
DF_W = DF_HEADS * DF_HD
D_FF = -(-8 * D_MODEL // (3 * 256)) * 256
ROPE_BASE = 10000.0
EPS = 1e-6
NEG = -1e30
LOG2E = math.log2(math.e)

LANES = 128
MXU_DIM = 256

A_CQ = 0
A_CKV = A_CQ + Q_LORA
A_KR = A_CKV + KV_LORA
A_COLS = A_KR + LANES
B_SRC = Q_LORA + KV_LORA + MLA_ROPE
B_NAQ = 0
B_NAK = B_NAQ + NA_W
B_NAV = B_NAK + NA_W
B_DFQ = B_NAV + NA_W
B_DFK = B_DFQ + DF_W
B_DFV = B_DFK + DF_W
B_COLS = B_DFV + DF_W

W_MLA_P = MLA_HEADS * MLA_PAD
QK_PACK = W_MLA_P + NA_W + DF_W
V_PACK = MLA_W + NA_W + DF_W
P_MLA = 0
P_NA = W_MLA_P
P_DF = P_NA + NA_W
PV_MLA = 0
PV_NA = MLA_W
PV_DF = MLA_W + NA_W

(V_GMIX, V_GFFN, V_GQA, V_GKVA, V_GMQ, V_GMK, V_GNQ, V_GNK, V_GDQ, V_GDK, V_GDS,
 V_LQ1, V_LK1, V_LQ2, V_LK2) = range(15)
N_VEC = 16

N_MOD = 8
TM = 512
TM_LAT_FRONT = 256
TM_FINISH = 512
CTX_SEQS_PER_STEP = 2
NA_ROWS_PER_STEP = 4
VMEM_LIMIT = 56 * 1024 * 1024

NA_PAIR_TILES = 2 * NA_KR - 2
N_CACHE = 6
N_FFN_W = 4


def _dot(a, b):
    return jnp.dot(a, b, preferred_element_type=F32)


def _dot_nt(a, b):
    return lax.dot_general(a, b, (((1,), (1,)), ((), ())), preferred_element_type=F32)


def _lane_iota(shape):
    return lax.broadcasted_iota(jnp.int32, shape, len(shape) - 1)


def _rms_rows(x, g):
    ms = jnp.mean(x * x, axis=-1, keepdims=True)
    return x * lax.rsqrt(ms + EPS) * g


def _tile_rms(x, g, n_real):
    outs = []
    for c0 in range(0, x.shape[1], LANES):
        xt = x[:, c0:c0 + LANES]
        ms = jnp.sum(xt * xt, axis=-1, keepdims=True) * (1.0 / n_real)
        outs.append(xt * lax.rsqrt(ms + EPS) * g[:, c0:c0 + LANES])
    return jnp.concatenate(outs, axis=1)


def _seg_rms(x, bd_ref, g, group):
    width = x.shape[1]
    sq = (x * x).astype(BF16)
    parts = []
    for c0 in range(0, width, MXU_DIM):
        w = min(MXU_DIM, width - c0)
        parts.append(_dot(sq[:, c0:c0 + w], bd_ref[0:w, 0:w]))
    ss = parts[0] if len(parts) == 1 else jnp.concatenate(parts, axis=1)
    return x * lax.rsqrt(ss * (1.0 / group) + EPS) * g


def _rope_tiles(x, cos, sa, sb):
    outs = []
    for t in range(x.shape[1] // LANES):
        xt = x[:, t * LANES:(t + 1) * LANES]
        up = pltpu.roll(xt, LANES - MLA_ROPE // 4, 1)
        dn = pltpu.roll(xt, MLA_ROPE // 4, 1)
        outs.append(xt * cos + up * sa + dn * sb)
    return outs[0] if len(outs) == 1 else jnp.concatenate(outs, axis=1)


def _diff_lambda(vec_ref, lam_init):
    a = jnp.sum(vec_ref[V_LQ1:V_LQ1 + 1, 0:DF_QK] * vec_ref[V_LK1:V_LK1 + 1, 0:DF_QK], axis=-1, keepdims=True)
    b = jnp.sum(vec_ref[V_LQ2:V_LQ2 + 1, 0:DF_QK] * vec_ref[V_LK2:V_LK2 + 1, 0:DF_QK], axis=-1, keepdims=True)
    return jnp.exp(a) - jnp.exp(b) + lam_init


def _mixer_front(x, mod, vec_ref, wa_ref, wb_ref, w_uq_ref, bd64_ref, bd32_ref):
    sh = mod[:, 0:D_MODEL]
    sc = mod[:, D_MODEL:2 * D_MODEL]
    h = (_rms_rows(x, vec_ref[V_GMIX:V_GMIX + 1, :]) * (1.0 + sc) + sh).astype(BF16)
    za = _dot_nt(h, wa_ref[...])
    zb = _dot_nt(h, wb_ref[...])
    cqn = _rms_rows(za[:, A_CQ:A_CQ + Q_LORA], vec_ref[V_GQA:V_GQA + 1, 0:Q_LORA])
    q_raw = _dot_nt(cqn.astype(BF16), w_uq_ref[...])
    q_mla = _tile_rms(q_raw, vec_ref[V_GMQ:V_GMQ + 1, 0:W_MLA_P] * (MLA_QK ** -0.5 * LOG2E), MLA_QK)
    ckv_n = _rms_rows(za[:, A_CKV:A_CKV + KV_LORA], vec_ref[V_GKVA:V_GKVA + 1, 0:KV_LORA])
    kr_tile = za[:, A_KR:A_KR + LANES]
    q_na = _seg_rms(zb[:, B_NAQ:B_NAQ + NA_W], bd64_ref, vec_ref[V_GNQ:V_GNQ + 1, 0:NA_W] * (NA_HD ** -0.5 * LOG2E), NA_HD)
    k_na = _seg_rms(zb[:, B_NAK:B_NAK + NA_W], bd64_ref, vec_ref[V_GNK:V_GNK + 1, 0:NA_W], NA_HD)
    v_na = zb[:, B_NAV:B_NAV + NA_W]
    q_df = _seg_rms(zb[:, B_DFQ:B_DFQ + DF_W], bd32_ref, vec_ref[V_GDQ:V_GDQ + 1, 0:DF_W] * (DF_QK ** -0.5 * LOG2E), DF_QK)
    k_df = _seg_rms(zb[:, B_DFK:B_DFK + DF_W], bd32_ref, vec_ref[V_GDK:V_GDK + 1, 0:DF_W], DF_QK)
    v_df = zb[:, B_DFV:B_DFV + DF_W]
    return q_mla, ckv_n, kr_tile, q_na, k_na, v_na, q_df, k_df, v_df


def _mla_kv(ckv_n, kr_tile, vec_ref, w_ukv_ref):
    kv = _dot(ckv_n.astype(BF16), w_ukv_ref[...])
    lane = _lane_iota((1, LANES))
    kr = jnp.where((lane >= MLA_NOPE) & (lane < MLA_QK), kr_tile, 0.0)
    k_pre = kv[:, 0:W_MLA_P] + jnp.concatenate([kr] * MLA_HEADS, axis=1)
    k = _tile_rms(k_pre, vec_ref[V_GMK:V_GMK + 1, 0:W_MLA_P], MLA_QK)
    return k, kv[:, W_MLA_P:W_MLA_P + MLA_W]


def _softmax_parts(s):
    m = jnp.max(s, axis=-1, keepdims=True)
    p = jnp.exp2(s - m)
    return p, 1.0 / jnp.sum(p, axis=-1, keepdims=True)


def _scores(q, k_segs):
    parts = [_dot(q, k) if feature_major else _dot_nt(q, k) for k, feature_major in k_segs]
    return parts[0] if len(parts) == 1 else jnp.concatenate(parts, axis=1)


def _pv(p, v_segs):
    out = None
    c0 = 0
    for v, feature_major in v_segs:
        n = v.shape[1] if feature_major else v.shape[0]
        o = _dot_nt(p[:, c0:c0 + n], v) if feature_major else _dot(p[:, c0:c0 + n], v)
        out = o if out is None else out + o
        c0 += n
    return out


def _seg_tile(seg, t):
    a, feature_major = seg
    return (a[t * LANES:(t + 1) * LANES, :] if feature_major else a[:, t * LANES:(t + 1) * LANES]), feature_major


def _lane_groups(qt, width):
    lane = _lane_iota((1, LANES))
    zero = jnp.zeros_like(qt)
    return jnp.concatenate(
        [jnp.where((lane >= g * width) & (lane < (g + 1) * width), qt, zero) for g in range(LANES // width)], axis=0)


def _pair_select(o2):
    tq = o2.shape[0] // 2
    return jnp.where(_lane_iota((1, LANES)) < NA_HD, o2[0:tq], o2[tq:2 * tq])


def _mla_attend(q, k_segs, v_segs):
    outs = []
    for t in range(MLA_HEADS // 2):
        vt = [_seg_tile(v, t) for v in v_segs]
        halves = []
        for h in (2 * t, 2 * t + 1):
            p, il = _softmax_parts(_scores(q[:, h * MLA_PAD:(h + 1) * MLA_PAD], [_seg_tile(k, h) for k in k_segs]))
            halves.append(_pv(p.astype(BF16), vt) * il)
        outs.append(jnp.where(_lane_iota((1, LANES)) < MLA_V, halves[0], halves[1]))
    return jnp.concatenate(outs, axis=1)


def _na_tile_attend(qt, k_segs, v_segs, bias=None):
    s = _scores(_lane_groups(qt, NA_HD), k_segs)
    if bias is not None:
        nb = bias.shape[1]
        s = jnp.concatenate([s[:, 0:nb] + bias, s[:, nb:]], axis=1)
    p, il = _softmax_parts(s)
    return _pair_select(_pv(p.astype(BF16), v_segs) * il)


def _na_attend_full(q, k_segs, v_segs):
    return jnp.concatenate(
        [_na_tile_attend(q[:, t * LANES:(t + 1) * LANES], [_seg_tile(k, t) for k in k_segs],
                         [_seg_tile(v, t) for v in v_segs]) for t in range(NA_HEADS // 2)], axis=1)


def _df_attend(q, k_segs, v_segs, lam, g_sub, out_scale):
    outs = []
    lane = _lane_iota((1, LANES))
    tq = q.shape[0]
    for t in range(DF_HEADS // 2):
        kt = [_seg_tile(k, t) for k in k_segs]
        vt = [_seg_tile(v, t) for v in v_segs]
        p, il = _softmax_parts(_scores(_lane_groups(q[:, t * LANES:(t + 1) * LANES], DF_QK), kt))
        pn = []
        for hh in range(2):
            r1, r2 = 2 * hh * tq, (2 * hh + 1) * tq
            pn.append((p[r1:r1 + tq] * il[r1:r1 + tq] - p[r2:r2 + tq] * (lam * il[r2:r2 + tq])).astype(BF16))
        o = _pair_select(_pv(jnp.concatenate(pn, axis=0), vt))
        o2 = o * o
        ms_e = jnp.sum(jnp.where(lane < DF_HD, o2, 0.0), axis=-1, keepdims=True)
        ms_o = jnp.sum(jnp.where(lane >= DF_HD, o2, 0.0), axis=-1, keepdims=True)
        r = lax.rsqrt(jnp.where(lane < DF_HD, ms_e, ms_o) * (1.0 / DF_HD) + EPS)
        outs.append(o * r * (g_sub[:, t * LANES:(t + 1) * LANES] * out_scale))
    return jnp.concatenate(outs, axis=1)


def _mod_kernel(c_ref, w_ref, b_ref, o_ref):
    c = c_ref[...]
    s = c * jax.nn.sigmoid(c)
    o_ref[...] = _dot(s.astype(BF16), w_ref[...].astype(BF16)) + b_ref[...]


def _ctx_kernel(lam_init, n_alias, x_ref, mod_ref, vec_ref, wa_ref, wb_ref, w_uq_ref, w_ukv_ref, bd64_ref, bd32_ref,
                *rest):
    ffn_f32, rest = rest[:N_FFN_W], rest[N_FFN_W + n_alias:]
    mix_ref, *cache_refs = rest[:1 + N_CACHE]
    for src, dst in zip(ffn_f32, rest[1 + N_CACHE:]):
        dst[...] = src[...].astype(BF16)
    if n_alias == 0:
        for ref in cache_refs:
            ref[:, 1:] = jnp.zeros((ref.shape[0], ref.shape[1] - 1) + ref.shape[2:], F32)
        cache_refs = [ref.at[:, 0] for ref in cache_refs]
    ckv_ref, kr_ref, nak_ref, nav_ref, dfk_ref, dfv_ref = cache_refs
    lam = _diff_lambda(vec_ref, lam_init)
    mod = mod_ref[...]
    bf = lambda a: a.astype(BF16)
    seg = lambda a: [(a.astype(BF16), False)]
    for s in range(CTX_SEQS_PER_STEP):
        rows = slice(s * SEQ, (s + 1) * SEQ)
        q_mla, ckv_n, kr_tile, q_na, k_na, v_na, q_df, k_df, v_df = _mixer_front(
            x_ref[rows, :], mod, vec_ref, wa_ref, wb_ref, w_uq_ref, bd64_ref, bd32_ref)
        k_mla, v_mla = _mla_kv(ckv_n, kr_tile, vec_ref, w_ukv_ref)
        ckv_ref[s] = ckv_n
        kr_ref[s] = kr_tile.T[0:MLA_ROPE]
        for ref, a in ((nak_ref, k_na), (nav_ref, v_na), (dfk_ref, k_df), (dfv_ref, v_df)):
            at = a.T
            for h in range(ref.shape[1]):
                ref[s, h] = at[h * NA_HD:(h + 1) * NA_HD]
        o_mla = _mla_attend(bf(q_mla), seg(k_mla), seg(v_mla))
        o_na = _na_attend_full(bf(q_na), seg(k_na), seg(v_na))
        o_df = _df_attend(bf(q_df), seg(k_df), seg(v_df), lam, vec_ref[V_GDS:V_GDS + 1, 0:DF_W], 1.0 - lam_init)
        mix_ref[rows, :] = jnp.concatenate([o_mla, o_na, o_df], axis=1).astype(BF16)


def _finish_kernel(x_ref, mix_ref, mod_ref, vec_ref, w_out_ref, w_gate_ref, w_up_ref, w_down_ref, y_ref):
    mod = mod_ref[...]
    gate_m = mod[:, 2 * D_MODEL:3 * D_MODEL]
    sh = mod[:, 3 * D_MODEL:4 * D_MODEL]
    sc = mod[:, 4 * D_MODEL:5 * D_MODEL]
    gate_f = mod[:, 5 * D_MODEL:6 * D_MODEL]
    x1 = x_ref[...] + gate_m * _dot(mix_ref[...], w_out_ref[...])
    h = (_rms_rows(x1, vec_ref[V_GFFN:V_GFFN + 1, :]) * (1.0 + sc) + sh).astype(BF16)
    g = _dot(h, w_gate_ref[...])
    u = _dot(h, w_up_ref[...])
    a = (g * jax.nn.sigmoid(g) * u).astype(BF16)
    y_ref[...] = x1 + gate_f * _dot(a, w_down_ref[...])


def _lat_front_kernel(x_ref, mod_ref, vec_ref, rope_ref, wa_ref, wb_ref, w_uq_ref, w_ukv_ref, bd64_ref, bd32_ref,
                      q_ref, k_ref, v_ref):
    q_mla, ckv_n, kr_tile, q_na, k_na, v_na, q_df, k_df, v_df = _mixer_front(
        x_ref[...], mod_ref[...], vec_ref, wa_ref, wb_ref, w_uq_ref, bd64_ref, bd32_ref)
    k_mla, v_mla = _mla_kv(ckv_n, kr_tile, vec_ref, w_ukv_ref)
    cm, sam, sbm = rope_ref[0], rope_ref[1], rope_ref[2]
    cd, sad, sbd = rope_ref[3], rope_ref[4], rope_ref[5]
    q_ref[...] = jnp.concatenate(
        [_rope_tiles(q_mla, cm, sam, sbm), q_na, _rope_tiles(q_df, cd, sad, sbd)], axis=1).astype(BF16)
    k_ref[...] = jnp.concatenate(
        [_rope_tiles(k_mla, cm, sam, sbm), k_na, _rope_tiles(k_df, cd, sad, sbd)], axis=1).astype(BF16)
    v_ref[...] = jnp.concatenate([v_mla, v_na, v_df], axis=1).astype(BF16)


def _ctx_kv_kernel(ckv_ref, krt_ref, vec_ref, w_ukv_ref, k_ref, v_ref):
    krt = jnp.concatenate([jnp.zeros((MLA_NOPE, PAST_LEN), F32), krt_ref[...],
                           jnp.zeros((LANES - MLA_QK, PAST_LEN), F32)], axis=0)
    k_mla, v_mla = _mla_kv(ckv_ref[...], krt.T, vec_ref, w_ukv_ref)
    k_ref[...] = k_mla.astype(BF16)
    v_ref[...] = v_mla.astype(BF16)


def _na_lat_kernel(q_ref, kl_ref, vl_ref, kct_ref, vct_ref, g_ref, o_ref, bias_ref):
    b = pl.program_id(0)
    j = pl.program_id(1)

    @pl.when((b == 0) & (j == 0))
    def _build_bias():
        c = lax.broadcasted_iota(jnp.int32, (GRID_W, LANES), 0)
        kc = _lane_iota((GRID_W, LANES)) % GRID_W
        start = jnp.clip(c - NA_KW // 2, 0, GRID_W - NA_KW)
        in_win = (kc >= start) & (kc < start + NA_KW)

        def body(i, carry):
            row = jnp.broadcast_to(g_ref[pl.ds(i, 1), :], (GRID_W, LANES))
            toep = pltpu.roll(row, 0, 1, stride=1, stride_axis=0)
            bias_ref[i] = jnp.where(in_win, toep * LOG2E, NEG)
            return carry

        lax.fori_loop(0, NA_HEADS * NA_PAIR_TILES, body, 0)

    n_win = NA_KR * GRID_W
    kct = kct_ref[...].astype(BF16)
    vct = vct_ref[...].astype(BF16)

    def win_start(r):
        return jnp.clip(r - NA_KR // 2, 0, GRID_ROWS - NA_KR)

    def attend(grid_rows, rs):
        a0, n = grid_rows[0], len(grid_rows)
        row0 = pl.multiple_of(rs * GRID_W, GRID_W)
        outs = []
        for t in range(NA_HEADS // 2):
            lanes = slice(t * LANES, (t + 1) * LANES)
            bias = jnp.concatenate(
                [jnp.concatenate([bias_ref[h * NA_PAIR_TILES + (rs - (j * NA_ROWS_PER_STEP + a) + NA_KR - 1) + 2 * m]
                                  for m in range(NA_KR // 2)], axis=1)
                 for h in (2 * t, 2 * t + 1) for a in grid_rows], axis=0)
            outs.append(_na_tile_attend(
                q_ref[a0 * GRID_W:(a0 + n) * GRID_W, lanes],
                [(kl_ref[pl.ds(row0, n_win), lanes], False), (kct[lanes, :], True)],
                [(vl_ref[pl.ds(row0, n_win), lanes], False), (vct[lanes, :], True)], bias))
        return jnp.concatenate(outs, axis=1).astype(BF16)

    first_r, last_r = j * NA_ROWS_PER_STEP, (j + 1) * NA_ROWS_PER_STEP - 1
    shared = win_start(first_r) == win_start(last_r)

    @pl.when(shared)
    def _shared_window():
        o_ref[...] = attend(tuple(range(NA_ROWS_PER_STEP)), win_start(first_r))

    @pl.when(jnp.logical_not(shared))
    def _per_row_windows():
        for a in range(NA_ROWS_PER_STEP):
            o_ref[a * GRID_W:(a + 1) * GRID_W, :] = attend((a,), win_start(first_r + a))


def _lat_attn_kernel(lam_init, q_ref, kl_ref, vl_ref, kc_ref, vc_ref, kct_ref, vct_ref, ona_ref, vec_ref, mix_ref):
    o_mla = _mla_attend(q_ref[:, P_MLA:P_MLA + W_MLA_P],
                        [(kc_ref[...], False), (kl_ref[:, P_MLA:P_MLA + W_MLA_P], False)],
                        [(vc_ref[...], False), (vl_ref[:, PV_MLA:PV_MLA + MLA_W], False)])
    lam = _diff_lambda(vec_ref, lam_init)
    o_df = _df_attend(q_ref[:, P_DF:P_DF + DF_W],
                      [(kct_ref[...].astype(BF16), True), (kl_ref[:, P_DF:P_DF + DF_W], False)],
                      [(vct_ref[...].astype(BF16), True), (vl_ref[:, PV_DF:PV_DF + DF_W], False)],
                      lam, vec_ref[V_GDS:V_GDS + 1, 0:DF_W], 1.0 - lam_init)
    mix_ref[...] = jnp.concatenate([o_mla.astype(BF16), ona_ref[...], o_df.astype(BF16)], axis=1)


def _const_spec(shape):
    nd = len(shape)
    return pl.BlockSpec(shape, lambda *_: (0,) * nd, pipeline_mode=pl.Buffered(1))


def _layer_spec(l, shape):
    nd = len(shape)
    return pl.BlockSpec((None,) + tuple(shape), lambda *_: (l,) + (0,) * nd, pipeline_mode=pl.Buffered(1))


def _params(n_axes):
    return pltpu.CompilerParams(dimension_semantics=("arbitrary",) * n_axes, vmem_limit_bytes=VMEM_LIMIT)


def _block_diag(group):
    i = np.arange(MXU_DIM) // group
    return jnp.asarray((i[:, None] == i[None, :]).astype(np.float32), dtype=BF16)


def _rope_tables():
    t = np.arange(DEC_SEQ)
    row = (t // GRID_W).astype(np.float64)
    col = (t % GRID_W).astype(np.float64)
    n = MLA_ROPE // 4
    inv = 1.0 / (ROPE_BASE ** (np.arange(n, dtype=np.float64) * 2.0 / (MLA_ROPE // 2)))
    ar = row[:, None] * inv
    ac = col[:, None] * inv
    ang = np.concatenate([ar, ar, ac, ac], axis=-1)
    cos32, sin32 = np.cos(ang), np.sin(ang)
    first = (np.arange(MLA_ROPE) % (2 * n)) < n
    sa32 = np.where(first, -sin32, 0.0)
    sb32 = np.where(first, 0.0, sin32)

    def mla_tile(v32, fill):
        out = np.full((DEC_SEQ, LANES), fill)
        out[:, MLA_NOPE:MLA_QK] = v32
        return out

    tabs = [mla_tile(cos32, 1.0), mla_tile(sa32, 0.0), mla_tile(sb32, 0.0),
            np.tile(cos32, (1, LANES // DF_QK)), np.tile(sa32, (1, LANES // DF_QK)), np.tile(sb32, (1, LANES // DF_QK))]
    return jnp.asarray(np.stack(tabs).astype(np.float32))


def _feature_major(a):
    a = jnp.swapaxes(a, -1, -2)
    return a.reshape(a.shape[:-3] + (a.shape[-3] * a.shape[-2], a.shape[-1]))


def _tables(p):
    w_in_t = jnp.swapaxes(p['w_in'], 1, 2)
    kr = w_in_t[:, Q_LORA + KV_LORA:B_SRC]
    z32 = jnp.zeros_like(kr)
    wa = jnp.concatenate([w_in_t[:, :Q_LORA + KV_LORA], kr, z32, kr, z32], axis=1).astype(BF16)
    wb = w_in_t[:, B_SRC:].astype(BF16)
    w_uq_t = jnp.swapaxes(p['w_uq'], 1, 2).reshape(DEPTH, MLA_HEADS, MLA_QK, Q_LORA)
    w_uq_p = jnp.pad(w_uq_t, ((0, 0), (0, 0), (0, MLA_PAD - MLA_QK), (0, 0))).reshape(DEPTH, W_MLA_P, Q_LORA).astype(BF16)
    w_ukv = p['w_ukv'].reshape(DEPTH, KV_LORA, MLA_HEADS, MLA_NOPE + MLA_V)
    wk = jnp.pad(w_ukv[..., :MLA_NOPE], ((0, 0), (0, 0), (0, 0), (0, MLA_PAD - MLA_NOPE))).reshape(DEPTH, KV_LORA, W_MLA_P)
    wv = w_ukv[..., MLA_NOPE:].reshape(DEPTH, KV_LORA, MLA_W)
    w_ukv_r = jnp.concatenate([wk, wv], axis=2).astype(BF16)

    def row(v, reps=1):
        v = jnp.tile(v, (1, reps)) if reps > 1 else v
        return [v, jnp.zeros((DEPTH, D_MODEL - v.shape[1]), F32)] if v.shape[1] < D_MODEL else [v]

    pad_head = lambda g: jnp.pad(g, ((0, 0), (0, MLA_PAD - MLA_QK)))
    pieces = (row(p['g_mix']) + row(p['g_ffn']) + row(p['g_qa']) + row(p['g_kva'])
              + row(pad_head(p['g_mla_q']), MLA_HEADS) + row(pad_head(p['g_mla_k']), MLA_HEADS)
              + row(p['g_na_q'], NA_HEADS) + row(p['g_na_k'], NA_HEADS)
              + row(p['g_df_q'], 2 * DF_HEADS) + row(p['g_df_k'], 2 * DF_HEADS) + row(p['g_df_sub'], DF_HEADS)
              + row(p['df_lq1']) + row(p['df_lk1']) + row(p['df_lq2']) + row(p['df_lk2'])
              + [jnp.zeros((DEPTH, D_MODEL), F32)])
    vec = jnp.concatenate(pieces, axis=1).reshape(DEPTH, N_VEC, D_MODEL)
    f = p['na_rpb']
    zpad = jnp.zeros((DEPTH, NA_HEADS, NA_PAIR_TILES, 33), F32)
    g_rows = jnp.concatenate([f[:, :, :-1, NA_KW - 1:], zpad, f[:, :, 1:, :], zpad, f[:, :, :-1, :NA_KW - 1]], axis=-1)
    g_rows = g_rows.reshape(DEPTH, NA_HEADS * NA_PAIR_TILES, LANES)
    return dict(wa=wa, wb=wb, w_uq=w_uq_p, w_ukv=w_ukv_r, vec=vec, g_rows=g_rows)


def _modulation(c_all, w_mod, b_mod):
    tn = 2048
    return pl.pallas_call(
        _mod_kernel,
        grid=(DEPTH, 6 * D_MODEL // tn),
        in_specs=[pl.BlockSpec((N_MOD, D_MODEL), lambda l, j: (0, 0)),
                  pl.BlockSpec((None, D_MODEL, tn), lambda l, j: (l, 0, j)),
                  pl.BlockSpec((None, 1, tn), lambda l, j: (l, 0, j))],
        out_specs=pl.BlockSpec((None, N_MOD, tn), lambda l, j: (l, 0, j)),
        out_shape=jax.ShapeDtypeStruct((DEPTH, N_MOD, 6 * D_MODEL), F32),
        compiler_params=_params(2),
        name="modulation",
    )(c_all, w_mod, b_mod.reshape(DEPTH, 1, 6 * D_MODEL))


def _front_weight_specs(l):
    return [_layer_spec(l, (N_VEC, D_MODEL)), _layer_spec(l, (A_COLS, D_MODEL)), _layer_spec(l, (B_COLS, D_MODEL)),
            _layer_spec(l, (W_MLA_P, Q_LORA)), _layer_spec(l, (KV_LORA, W_MLA_P + MLA_W)),
            _const_spec((MXU_DIM, MXU_DIM)), _const_spec((MXU_DIM, MXU_DIM))]


def _front_weights(t, consts):
    return (t['vec'], t['wa'], t['wb'], t['w_uq'], t['w_ukv'], consts['bd64'], consts['bd32'])


def _context_mixer(l, lam_init, x, mods, t, consts, ffn_f32, prev_caches):
    n_tok = BATCH * SEQ
    n_alias = len(prev_caches)
    tok = lambda w: pl.BlockSpec((CTX_SEQS_PER_STEP * SEQ, w), lambda b: (b, 0))
    if n_alias == 0:
        assert l == 0
        lay = lambda *s: pl.BlockSpec((CTX_SEQS_PER_STEP, DEPTH) + s, lambda b: (b, 0) + (0,) * len(s))
    else:
        lay = lambda *s: pl.BlockSpec((CTX_SEQS_PER_STEP, None) + s, lambda b: (b, l) + (0,) * len(s))
    cache_shapes = [(SEQ, KV_LORA), (MLA_ROPE, SEQ), (NA_HEADS, NA_HD, SEQ), (NA_HEADS, NA_HD, SEQ),
                    (DF_HEADS, DF_HD, SEQ), (DF_HEADS, DF_HD, SEQ)]
    weights = _front_weights(t, consts)
    steps = BATCH // CTX_SEQS_PER_STEP
    ffn_chunks = [(w.shape[1] // steps, w.shape[2]) for w in ffn_f32]
    n_in = 2 + len(weights) + len(ffn_f32)
    mix, *outs = pl.pallas_call(
        functools.partial(_ctx_kernel, lam_init, n_alias),
        grid=(steps,),
        in_specs=[tok(D_MODEL), pl.BlockSpec((None, 1, 6 * D_MODEL), lambda b: (l * N_MOD, 0, 0))]
        + _front_weight_specs(l) + [pl.BlockSpec((None,) + c, lambda b: (l, b, 0)) for c in ffn_chunks]
        + [pl.BlockSpec(memory_space=pl.ANY)] * n_alias,
        out_specs=[tok(D_MODEL)] + [lay(*s) for s in cache_shapes] + [pl.BlockSpec(c, lambda b: (b, 0)) for c in ffn_chunks],
        out_shape=[jax.ShapeDtypeStruct((n_tok, D_MODEL), BF16)]
        + [jax.ShapeDtypeStruct((BATCH, DEPTH) + s, F32) for s in cache_shapes]
        + [jax.ShapeDtypeStruct(w.shape[1:], BF16) for w in ffn_f32],
        input_output_aliases={n_in + i: 1 + i for i in range(n_alias)},
        compiler_params=_params(1),
        name="context_mixer",
    )(x, mods, *weights, *ffn_f32, *prev_caches)
    return mix, outs[:N_CACHE], outs[N_CACHE:]


def _finish(l, x, mix, mods, t, ffn_bf16, mod_row):
    n_tok = x.shape[0]
    tok = pl.BlockSpec((TM_FINISH, D_MODEL), lambda i: (i, 0))
    return pl.pallas_call(
        _finish_kernel,
        grid=(n_tok // TM_FINISH,),
        in_specs=[tok, tok, pl.BlockSpec((None, 1, 6 * D_MODEL), lambda i: (l * N_MOD + mod_row(i), 0, 0)),
                  _layer_spec(l, (N_VEC, D_MODEL))] + [_const_spec(w.shape) for w in ffn_bf16],
        out_specs=tok,
        out_shape=jax.ShapeDtypeStruct((n_tok, D_MODEL), F32),
        compiler_params=_params(1),
        name="finish",
    )(x, mix, mods, t['vec'], *ffn_bf16)


def _latent_front(l, x, mods, t, consts):
    n_tok = DEC_BATCH * DEC_SEQ
    tm = TM_LAT_FRONT
    blocks_per_seq = DEC_SEQ // tm
    tok = lambda w: pl.BlockSpec((tm, w), lambda i: (i, 0))
    wspecs = _front_weight_specs(l)
    weights = _front_weights(t, consts)
    return pl.pallas_call(
        _lat_front_kernel,
        grid=(n_tok // tm,),
        in_specs=[tok(D_MODEL),
                  pl.BlockSpec((None, 1, 6 * D_MODEL), lambda i: (l * N_MOD + 1 + i // blocks_per_seq, 0, 0)),
                  wspecs[0], pl.BlockSpec((6, tm, LANES), lambda i: (0, i % blocks_per_seq, 0))] + wspecs[1:],
        out_specs=[tok(QK_PACK), tok(QK_PACK), tok(V_PACK)],
        out_shape=[jax.ShapeDtypeStruct((n_tok, QK_PACK), BF16), jax.ShapeDtypeStruct((n_tok, QK_PACK), BF16),
                   jax.ShapeDtypeStruct((n_tok, V_PACK), BF16)],
        compiler_params=_params(1),
        name="latent_front",
    )(x, mods, weights[0], consts['rope'], *weights[1:])


def _context_kv(l, ckv, krope_t, t):
    lay = lambda r, w: pl.BlockSpec((None, None, r, w), lambda b: (b, l, 0, 0))
    out = lambda w: pl.BlockSpec((None, PAST_LEN, w), lambda b: (b, 0, 0))
    return pl.pallas_call(
        _ctx_kv_kernel,
        grid=(DEC_BATCH,),
        in_specs=[lay(PAST_LEN, KV_LORA), lay(MLA_ROPE, PAST_LEN),
                  _layer_spec(l, (N_VEC, D_MODEL)), _layer_spec(l, (KV_LORA, W_MLA_P + MLA_W))],
        out_specs=[out(W_MLA_P), out(MLA_W)],
        out_shape=[jax.ShapeDtypeStruct((DEC_BATCH, PAST_LEN, W_MLA_P), BF16),
                   jax.ShapeDtypeStruct((DEC_BATCH, PAST_LEN, MLA_W), BF16)],
        compiler_params=_params(1),
        name="context_kv",
    )(ckv, krope_t, t['vec'], t['w_ukv'])


def _latent_na(l, q, k, v, kct, vct, g_rows):
    n_tok = DEC_BATCH * DEC_SEQ
    na_blk = P_NA // NA_W
    steps = GRID_ROWS // NA_ROWS_PER_STEP
    tq = NA_ROWS_PER_STEP * GRID_W
    ctx = pl.BlockSpec((None, None, NA_W, PAST_LEN), lambda b, j: (b, l, 0, 0))
    return pl.pallas_call(
        _na_lat_kernel,
        grid=(DEC_BATCH, steps),
        in_specs=[pl.BlockSpec((tq, NA_W), lambda b, j: (b * steps + j, na_blk)),
                  pl.BlockSpec((DEC_SEQ, NA_W), lambda b, j: (b, na_blk)),
                  pl.BlockSpec((DEC_SEQ, NA_W), lambda b, j: (b, PV_NA // NA_W)),
                  ctx, ctx, _layer_spec(l, (NA_HEADS * NA_PAIR_TILES, LANES))],
        out_specs=pl.BlockSpec((tq, NA_W), lambda b, j: (b * steps + j, 0)),
        out_shape=jax.ShapeDtypeStruct((n_tok, NA_W), BF16),
        scratch_shapes=[pltpu.VMEM((NA_HEADS * NA_PAIR_TILES, GRID_W, LANES), F32)],
        compiler_params=_params(2),
        name="latent_neighbourhood",
    )(q, k, v, kct, vct, g_rows)


def _latent_attention(l, lam_init, q, k, v, kc, vc, kct, vct, o_na, t):
    n_tok = DEC_BATCH * DEC_SEQ
    nq = DEC_SEQ // TM
    ctx = pl.BlockSpec((None, None, DF_W, PAST_LEN), lambda b, j: (b, l, 0, 0))
    return pl.pallas_call(
        functools.partial(_lat_attn_kernel, lam_init),
        grid=(DEC_BATCH, nq),
        in_specs=[pl.BlockSpec((TM, QK_PACK), lambda b, j: (b * nq + j, 0)),
                  pl.BlockSpec((DEC_SEQ, QK_PACK), lambda b, j: (b, 0)),
                  pl.BlockSpec((DEC_SEQ, V_PACK), lambda b, j: (b, 0)),
                  pl.BlockSpec((None, PAST_LEN, W_MLA_P), lambda b, j: (b, 0, 0)),
                  pl.BlockSpec((None, PAST_LEN, MLA_W), lambda b, j: (b, 0, 0)),
                  ctx, ctx,
                  pl.BlockSpec((TM, NA_W), lambda b, j: (b * nq + j, 0)),
                  _layer_spec(l, (N_VEC, D_MODEL))],
        out_specs=pl.BlockSpec((TM, D_MODEL), lambda b, j: (b * nq + j, 0)),
        out_shape=jax.ShapeDtypeStruct((n_tok, D_MODEL), BF16),
        compiler_params=_params(2),
        name="latent_attention",
    )(q, k, v, kc, vc, kct, vct, o_na, t['vec'])


def kernel(x_prompt, x_sample, cache_mla_ckv, cache_mla_krope, cache_na_k, cache_na_v, cache_df_k, cache_df_v, c, c_ctx, w_mod, b_mod, g_mix, w_in, g_qa, w_uq, g_kva, w_ukv, g_mla_q, g_mla_k, g_na_q, g_na_k, na_rpb, g_df_q, g_df_k, df_lq1, df_lk1, df_lq2, df_lk2, g_df_sub, w_out, g_ffn, w_gate, w_up, w_down):
    p = dict(g_mix=g_mix, w_in=w_in, g_qa=g_qa, w_uq=w_uq, g_kva=g_kva, w_ukv=w_ukv, g_mla_q=g_mla_q, g_mla_k=g_mla_k,
             g_na_q=g_na_q, g_na_k=g_na_k, na_rpb=na_rpb, g_df_q=g_df_q, g_df_k=g_df_k, df_lq1=df_lq1, df_lk1=df_lk1,
             df_lq2=df_lq2, df_lk2=df_lk2, g_df_sub=g_df_sub, w_out=w_out, g_ffn=g_ffn, w_gate=w_gate, w_up=w_up,
             w_down=w_down)
    consts = dict(bd64=_block_diag(NA_HD), bd32=_block_diag(DF_QK), rope=_rope_tables())
    t = _tables(p)

    c_all = jnp.concatenate([c_ctx[None, :], c, jnp.zeros((N_MOD - 1 - DEC_BATCH, D_MODEL), F32)], axis=0)
    mods = _modulation(c_all, w_mod, b_mod).reshape(DEPTH * N_MOD, 1, 6 * D_MODEL)

    krope_t = jnp.swapaxes(cache_mla_krope, -1, -2)
    na_kt, na_vt, df_kt, df_vt = (_feature_major(a) for a in (cache_na_k, cache_na_v, cache_df_k, cache_df_v))

    xp = x_prompt.reshape(BATCH * SEQ, D_MODEL)
    xs = x_sample.reshape(DEC_BATCH * DEC_SEQ, D_MODEL)
    new_caches = ()
    ffn_f32 = (w_out, w_gate, w_up, w_down)
    for l in range(DEPTH):
        lam_init = 0.8 - 0.6 * math.exp(-0.3 * l)
        mix, new_caches, ffn_bf16 = _context_mixer(l, lam_init, xp, mods, t, consts, ffn_f32, new_caches)
        xp = _finish(l, xp, mix, mods, t, ffn_bf16, lambda i: 0)
        q, k, v = _latent_front(l, xs, mods, t, consts)
        kc, vc = _context_kv(l, cache_mla_ckv, krope_t, t)
        o_na = _latent_na(l, q, k, v, na_kt, na_vt, t['g_rows'])
        mix_s = _latent_attention(l, lam_init, q, k, v, kc, vc, df_kt, df_vt, o_na, t)
        xs = _finish(l, xs, mix_s, mods, t, ffn_bf16, lambda i: 1 + i // (DEC_SEQ // TM_FINISH))
    ckv_new, *narrow = new_caches
    return (xp.reshape(BATCH, SEQ, D_MODEL), xs.reshape(DEC_BATCH, DEC_SEQ, D_MODEL), ckv_new,
            *(jnp.swapaxes(a, -1, -2) for a in narrow))
```

```python
import functools
import math

import numpy as np
import jax
import jax.numpy as jnp
from jax import lax
from jax.experimental import pallas as pl
from jax.experimental.pallas import tpu as pltpu

F32 = jnp.float32
BF16 = jnp.bfloat16

D_MODEL = 1024
BATCH = 32
SEQ = 256
DEPTH = 2
DEC_BATCH = 2
DEC_SEQ = 1024
PAST_LEN = 256
GRID_W = 64
GRID_ROWS = DEC_SEQ // GRID_W
MLA_HEADS = 6
MLA_NOPE = 64
MLA_ROPE = 32
MLA_QK = MLA_NOPE + MLA_ROPE
MLA_V = 64
MLA_PAD = 128
Q_LORA = 256
KV_LORA = 128
NA_HEADS = 6
NA_HD = 64
NA_KR = 8
NA_KW = 16
DF_HEADS = 4
DF_HD = 64
DF_QK = 32
MLA_W = MLA_HEADS * MLA_V
NA_W = NA_HEADS * NA_HD
DF_W = DF_HEADS * DF_HD
D_FF = -(-8 * D_MODEL // (3 * 256)) * 256
ROPE_BASE = 10000.0
EPS = 1e-6
NEG = -1e30
LOG2E = math.log2(math.e)
MAX_FREE_SOFTMAX_BOUND = 60.0

LANES = 128
MXU_DIM = 256

A_CQ = 0
A_CKV = A_CQ + Q_LORA
A_KR = A_CKV + KV_LORA
A_COLS = A_KR + LANES
B_SRC = Q_LORA + KV_LORA + MLA_ROPE
B_NAQ = 0
B_NAK = B_NAQ + NA_W
B_NAV = B_NAK + NA_W
B_DFQ = B_NAV + NA_W
B_DFK = B_DFQ + DF_W
B_DFV = B_DFK + DF_W
B_COLS = B_DFV + DF_W

W_MLA_P = MLA_HEADS * MLA_PAD
QK_PACK = W_MLA_P + NA_W + DF_W
V_PACK = MLA_W + NA_W + DF_W
P_MLA = 0
P_NA = W_MLA_P
P_DF = P_NA + NA_W
PV_MLA = 0
PV_NA = MLA_W
PV_DF = MLA_W + NA_W

(V_GMIX, V_GFFN, V_GQA, V_GKVA, V_GMQ, V_GMK, V_GNQ, V_GNK, V_GDQ, V_GDK, V_GDS,
 V_LQ1, V_LK1, V_LQ2, V_LK2) = range(15)
N_VEC = 16

N_MOD = 8
TM = 512
TM_LAT_FRONT = 256
TM_FINISH = 512
CTX_SEQS_PER_STEP = 2
NA_ROWS_PER_STEP = 4
VMEM_LIMIT = 56 * 1024 * 1024

NA_PAIR_TILES = 2 * NA_KR - 2
N_CACHE = 6
N_FFN_W = 4


def _dot(a, b):
    return jnp.dot(a, b, preferred_element_type=F32)


def _dot_nt(a, b):
    return lax.dot_general(a, b, (((1,), (1,)), ((), ())), preferred_element_type=F32)


def _lane_iota(shape):
    return lax.broadcasted_iota(jnp.int32, shape, len(shape) - 1)


def _rms_rows(x, g):
    ms = jnp.mean(x * x, axis=-1, keepdims=True)
    return x * lax.rsqrt(ms + EPS) * g


def _tile_rms(x, g, n_real):
    outs = []
    for c0 in range(0, x.shape[1], LANES):
        xt = x[:, c0:c0 + LANES]
        ms = jnp.sum(xt * xt, axis=-1, keepdims=True) * (1.0 / n_real)
        outs.append(xt * lax.rsqrt(ms + EPS) * g[:, c0:c0 + LANES])
    return jnp.concatenate(outs, axis=1)


def _seg_rms(x, bd_ref, g, group):
    width = x.shape[1]
    sq = (x * x).astype(BF16)
    parts = []
    for c0 in range(0, width, MXU_DIM):
        w = min(MXU_DIM, width - c0)
        parts.append(_dot(sq[:, c0:c0 + w], bd_ref[0:w, 0:w]))
    ss = parts[0] if len(parts) == 1 else jnp.concatenate(parts, axis=1)
    return x * lax.rsqrt(ss * (1.0 / group) + EPS) * g


def _rope_tiles(x, cos, sa, sb):
    outs = []
    for t in range(x.shape[1] // LANES):
        xt = x[:, t * LANES:(t + 1) * LANES]
        up = pltpu.roll(xt, LANES - MLA_ROPE // 4, 1)
        dn = pltpu.roll(xt, MLA_ROPE // 4, 1)
        outs.append(xt * cos + up * sa + dn * sb)
    return outs[0] if len(outs) == 1 else jnp.concatenate(outs, axis=1)


def _diff_lambda(vec_ref, lam_init):
    a = jnp.sum(vec_ref[V_LQ1:V_LQ1 + 1, 0:DF_QK] * vec_ref[V_LK1:V_LK1 + 1, 0:DF_QK], axis=-1, keepdims=True)
    b = jnp.sum(vec_ref[V_LQ2:V_LQ2 + 1, 0:DF_QK] * vec_ref[V_LK2:V_LK2 + 1, 0:DF_QK], axis=-1, keepdims=True)
    return jnp.exp(a) - jnp.exp(b) + lam_init


def _mixer_front(x, mod, vec_ref, wa_ref, wb_ref, w_uq_ref, bd64_ref, bd32_ref):
    sh = mod[:, 0:D_MODEL]
    sc = mod[:, D_MODEL:2 * D_MODEL]
    h = (_rms_rows(x, vec_ref[V_GMIX:V_GMIX + 1, :]) * (1.0 + sc) + sh).astype(BF16)
    za = _dot_nt(h, wa_ref[...])
    zb = _dot_nt(h, wb_ref[...])
    cqn = _rms_rows(za[:, A_CQ:A_CQ + Q_LORA], vec_ref[V_GQA:V_GQA + 1, 0:Q_LORA])
    q_raw = _dot_nt(cqn.astype(BF16), w_uq_ref[...])
    q_mla = _tile_rms(q_raw, vec_ref[V_GMQ:V_GMQ + 1, 0:W_MLA_P] * (MLA_QK ** -0.5 * LOG2E), MLA_QK)
    ckv_n = _rms_rows(za[:, A_CKV:A_CKV + KV_LORA], vec_ref[V_GKVA:V_GKVA + 1, 0:KV_LORA])
    kr_tile = za[:, A_KR:A_KR + LANES]
    q_na = _seg_rms(zb[:, B_NAQ:B_NAQ + NA_W], bd64_ref, vec_ref[V_GNQ:V_GNQ + 1, 0:NA_W] * (NA_HD ** -0.5 * LOG2E), NA_HD)
    k_na = _seg_rms(zb[:, B_NAK:B_NAK + NA_W], bd64_ref, vec_ref[V_GNK:V_GNK + 1, 0:NA_W], NA_HD)
    v_na = zb[:, B_NAV:B_NAV + NA_W]
    q_df = _seg_rms(zb[:, B_DFQ:B_DFQ + DF_W], bd32_ref, vec_ref[V_GDQ:V_GDQ + 1, 0:DF_W] * (DF_QK ** -0.5 * LOG2E), DF_QK)
    k_df = _seg_rms(zb[:, B_DFK:B_DFK + DF_W], bd32_ref, vec_ref[V_GDK:V_GDK + 1, 0:DF_W], DF_QK)
    v_df = zb[:, B_DFV:B_DFV + DF_W]
    return q_mla, ckv_n, kr_tile, q_na, k_na, v_na, q_df, k_df, v_df


def _mla_kv(ckv_n, kr_tile, vec_ref, w_ukv_ref):
    kv = _dot(ckv_n.astype(BF16), w_ukv_ref[...])
    lane = _lane_iota((1, LANES))
    kr = jnp.where((lane >= MLA_NOPE) & (lane < MLA_QK), kr_tile, 0.0)
    k_pre = kv[:, 0:W_MLA_P] + jnp.concatenate([kr] * MLA_HEADS, axis=1)
    k = _tile_rms(k_pre, vec_ref[V_GMK:V_GMK + 1, 0:W_MLA_P], MLA_QK)
    return k, kv[:, W_MLA_P:W_MLA_P + MLA_W]


def _softmax_parts(s, bound=None):
    m = jnp.max(s, axis=-1, keepdims=True) if bound is None else bound
    p = jnp.exp2(s - m)
    return p, 1.0 / jnp.sum(p, axis=-1, keepdims=True)


def _max_sq_norm(kt, group):
    sq = kt * kt
    best = None
    for r0 in range(0, kt.shape[0], group):
        n2 = jnp.sum(sq[r0:r0 + group], axis=0, keepdims=True)
        best = n2 if best is None else jnp.maximum(best, n2)
    return jnp.max(best, axis=-1, keepdims=True)


def _score_bounds(vec_ref, bias_max=None, na_key_sq=None, df_key_sq=None):
    gmax = lambda row, w: jnp.max(jnp.abs(vec_ref[row:row + 1, 0:w]), axis=-1, keepdims=True)
    slack = LOG2E * (1.0 + 2.0 ** -6)

    def key_norm(row, w, d, measured_sq):
        k = gmax(row, w) * d ** 0.5
        return k if measured_sq is None else jnp.maximum(k, jnp.sqrt(measured_sq))

    b_mla = gmax(V_GMQ, W_MLA_P) * key_norm(V_GMK, W_MLA_P, MLA_QK, None) * slack
    b_na = gmax(V_GNQ, NA_W) * key_norm(V_GNK, NA_W, NA_HD, na_key_sq) * slack
    if bias_max is not None:
        b_na = b_na + bias_max * LOG2E
    b_df = gmax(V_GDQ, DF_W) * key_norm(V_GDK, DF_W, DF_QK, df_key_sq) * slack
    worst = jnp.maximum(b_mla, jnp.maximum(b_na, b_df))
    return (b_mla, b_na, b_df), worst[0, 0] <= MAX_FREE_SOFTMAX_BOUND


def _with_score_bounds(vec_ref, attend, **measured):
    bounds, ok = _score_bounds(vec_ref, **measured)
    pl.when(ok)(lambda: attend(bounds))
    pl.when(jnp.logical_not(ok))(lambda: attend((None, None, None)))


def _scores(q, k_segs):
    parts = [_dot(q, k) if feature_major else _dot_nt(q, k) for k, feature_major in k_segs]
    return parts[0] if len(parts) == 1 else jnp.concatenate(parts, axis=1)


def _pv(p, v_segs):
    out = None
    c0 = 0
    for v, feature_major in v_segs:
        n = v.shape[1] if feature_major else v.shape[0]
        o = _dot_nt(p[:, c0:c0 + n], v) if feature_major else _dot(p[:, c0:c0 + n], v)
        out = o if out is None else out + o
        c0 += n
    return out


def _seg_tile(seg, t):
    a, feature_major = seg
    return (a[t * LANES:(t + 1) * LANES, :] if feature_major else a[:, t * LANES:(t + 1) * LANES]), feature_major


def _lane_groups(qt, width):
    lane = _lane_iota((1, LANES))
    zero = jnp.zeros_like(qt)
    return jnp.concatenate(
        [jnp.where((lane >= g * width) & (lane < (g + 1) * width), qt, zero) for g in range(LANES // width)], axis=0)


def _pair_select(o2):
    tq = o2.shape[0] // 2
    return jnp.where(_lane_iota((1, LANES)) < NA_HD, o2[0:tq], o2[tq:2 * tq])


def _mla_attend(q, k_segs, v_segs, bound=None):
    outs = []
    for t in range(MLA_HEADS // 2):
        vt = [_seg_tile(v, t) for v in v_segs]
        halves = []
        for h in (2 * t, 2 * t + 1):
            p, il = _softmax_parts(_scores(q[:, h * MLA_PAD:(h + 1) * MLA_PAD], [_seg_tile(k, h) for k in k_segs]),
                                   bound)
            halves.append(_pv(p.astype(BF16), vt) * il)
        outs.append(jnp.where(_lane_iota((1, LANES)) < MLA_V, halves[0], halves[1]))
    return jnp.concatenate(outs, axis=1)


def _na_tile_attend(qt, k_segs, v_segs, bias=None, bound=None):
    s = _scores(_lane_groups(qt, NA_HD), k_segs)
    if bias is not None:
        nb = bias.shape[1]
        s = jnp.concatenate([s[:, 0:nb] + bias, s[:, nb:]], axis=1)
    p, il = _softmax_parts(s, bound)
    return _pair_select(_pv(p.astype(BF16), v_segs) * il)


def _na_attend_full(q, k_segs, v_segs, bound=None):
    return jnp.concatenate(
        [_na_tile_attend(q[:, t * LANES:(t + 1) * LANES], [_seg_tile(k, t) for k in k_segs],
                         [_seg_tile(v, t) for v in v_segs], bound=bound) for t in range(NA_HEADS // 2)], axis=1)


def _df_attend(q, k_segs, v_segs, lam, g_sub, out_scale, bound=None):
    outs = []
    lane = _lane_iota((1, LANES))
    tq = q.shape[0]
    for t in range(DF_HEADS // 2):
        kt = [_seg_tile(k, t) for k in k_segs]
        vt = [_seg_tile(v, t) for v in v_segs]
        p, il = _softmax_parts(_scores(_lane_groups(q[:, t * LANES:(t + 1) * LANES], DF_QK), kt), bound)
        pn = []
        for hh in range(2):
            r1, r2 = 2 * hh * tq, (2 * hh + 1) * tq
            pn.append((p[r1:r1 + tq] * il[r1:r1 + tq] - p[r2:r2 + tq] * (lam * il[r2:r2 + tq])).astype(BF16))
        o = _pair_select(_pv(jnp.concatenate(pn, axis=0), vt))
        o2 = o * o
        ms_e = jnp.sum(jnp.where(lane < DF_HD, o2, 0.0), axis=-1, keepdims=True)
        ms_o = jnp.sum(jnp.where(lane >= DF_HD, o2, 0.0), axis=-1, keepdims=True)
        r = lax.rsqrt(jnp.where(lane < DF_HD, ms_e, ms_o) * (1.0 / DF_HD) + EPS)
        outs.append(o * r * (g_sub[:, t * LANES:(t + 1) * LANES] * out_scale))
    return jnp.concatenate(outs, axis=1)


def _mod_kernel(c_ref, w_ref, b_ref, o_ref):
    c = c_ref[...]
    s = c * jax.nn.sigmoid(c)
    o_ref[...] = _dot(s.astype(BF16), w_ref[...].astype(BF16)) + b_ref[...]


def _ctx_kernel(lam_init, n_alias, x_ref, mod_ref, vec_ref, wa_ref, wb_ref, w_uq_ref, w_ukv_ref, bd64_ref, bd32_ref,
                *rest):
    ffn_f32, rest = rest[:N_FFN_W], rest[N_FFN_W + n_alias:]
    mix_ref, *cache_refs = rest[:1 + N_CACHE]
    for src, dst in zip(ffn_f32, rest[1 + N_CACHE:]):
        dst[...] = src[...].astype(BF16)
    if n_alias == 0:
        for ref in cache_refs:
            ref[:, 1:] = jnp.zeros((ref.shape[0], ref.shape[1] - 1) + ref.shape[2:], F32)
        cache_refs = [ref.at[:, 0] for ref in cache_refs]
    ckv_ref, kr_ref, nak_ref, nav_ref, dfk_ref, dfv_ref = cache_refs
    lam = _diff_lambda(vec_ref, lam_init)
    mod = mod_ref[...]
    bf = lambda a: a.astype(BF16)
    seg = lambda a: [(a.astype(BF16), False)]
    for s in range(CTX_SEQS_PER_STEP):
        rows = slice(s * SEQ, (s + 1) * SEQ)
        q_mla, ckv_n, kr_tile, q_na, k_na, v_na, q_df, k_df, v_df = _mixer_front(
            x_ref[rows, :], mod, vec_ref, wa_ref, wb_ref, w_uq_ref, bd64_ref, bd32_ref)
        k_mla, v_mla = _mla_kv(ckv_n, kr_tile, vec_ref, w_ukv_ref)
        ckv_ref[s] = ckv_n
        kr_ref[s] = kr_tile.T[0:MLA_ROPE]
        for ref, a in ((nak_ref, k_na), (nav_ref, v_na), (dfk_ref, k_df), (dfv_ref, v_df)):
            at = a.T
            for h in range(ref.shape[1]):
                ref[s, h] = at[h * NA_HD:(h + 1) * NA_HD]
        o_mla = _mla_attend(bf(q_mla), seg(k_mla), seg(v_mla))
        o_na = _na_attend_full(bf(q_na), seg(k_na), seg(v_na))
        o_df = _df_attend(bf(q_df), seg(k_df), seg(v_df), lam, vec_ref[V_GDS:V_GDS + 1, 0:DF_W], 1.0 - lam_init)
        mix_ref[rows, :] = jnp.concatenate([o_mla, o_na, o_df], axis=1).astype(BF16)


def _finish_kernel(x_ref, mix_ref, mod_ref, vec_ref, w_out_ref, w_gate_ref, w_up_ref, w_down_ref, y_ref):
    mod = mod_ref[...]
    gate_m = mod[:, 2 * D_MODEL:3 * D_MODEL]
    sh = mod[:, 3 * D_MODEL:4 * D_MODEL]
    sc = mod[:, 4 * D_MODEL:5 * D_MODEL]
    gate_f = mod[:, 5 * D_MODEL:6 * D_MODEL]
    x1 = x_ref[...] + gate_m * _dot(mix_ref[...], w_out_ref[...])
    h = (_rms_rows(x1, vec_ref[V_GFFN:V_GFFN + 1, :]) * (1.0 + sc) + sh).astype(BF16)
    g = _dot(h, w_gate_ref[...])
    u = _dot(h, w_up_ref[...])
    a = (g * jax.nn.sigmoid(g) * u).astype(BF16)
    y_ref[...] = x1 + gate_f * _dot(a, w_down_ref[...])


def _lat_front_kernel(x_ref, mod_ref, vec_ref, rope_ref, wa_ref, wb_ref, w_uq_ref, w_ukv_ref, bd64_ref, bd32_ref,
                      q_ref, k_ref, v_ref):
    q_mla, ckv_n, kr_tile, q_na, k_na, v_na, q_df, k_df, v_df = _mixer_front(
        x_ref[...], mod_ref[...], vec_ref, wa_ref, wb_ref, w_uq_ref, bd64_ref, bd32_ref)
    k_mla, v_mla = _mla_kv(ckv_n, kr_tile, vec_ref, w_ukv_ref)
    cm, sam, sbm = rope_ref[0], rope_ref[1], rope_ref[2]
    cd, sad, sbd = rope_ref[3], rope_ref[4], rope_ref[5]
    q_ref[...] = jnp.concatenate(
        [_rope_tiles(q_mla, cm, sam, sbm), q_na, _rope_tiles(q_df, cd, sad, sbd)], axis=1).astype(BF16)
    k_ref[...] = jnp.concatenate(
        [_rope_tiles(k_mla, cm, sam, sbm), k_na, _rope_tiles(k_df, cd, sad, sbd)], axis=1).astype(BF16)
    v_ref[...] = jnp.concatenate([v_mla, v_na, v_df], axis=1).astype(BF16)


def _ctx_kv_kernel(ckv_ref, krt_ref, vec_ref, w_ukv_ref, k_ref, v_ref):
    krt = jnp.concatenate([jnp.zeros((MLA_NOPE, PAST_LEN), F32), krt_ref[...],
                           jnp.zeros((LANES - MLA_QK, PAST_LEN), F32)], axis=0)
    k_mla, v_mla = _mla_kv(ckv_ref[...], krt.T, vec_ref, w_ukv_ref)
    k_ref[...] = k_mla.astype(BF16)
    v_ref[...] = v_mla.astype(BF16)


def _na_lat_kernel(q_ref, kl_ref, vl_ref, kct_ref, vct_ref, g_ref, vec_ref, o_ref, bias_ref):
    b = pl.program_id(0)
    j = pl.program_id(1)

    @pl.when((b == 0) & (j == 0))
    def _build_bias():
        c = lax.broadcasted_iota(jnp.int32, (GRID_W, LANES), 0)
        kc = _lane_iota((GRID_W, LANES)) % GRID_W
        start = jnp.clip(c - NA_KW // 2, 0, GRID_W - NA_KW)
        in_win = (kc >= start) & (kc < start + NA_KW)

        def body(i, carry):
            row = jnp.broadcast_to(g_ref[pl.ds(i, 1), :], (GRID_W, LANES))
            toep = pltpu.roll(row, 0, 1, stride=1, stride_axis=0)
            bias_ref[i] = jnp.where(in_win, toep * LOG2E, NEG)
            return carry

        lax.fori_loop(0, NA_HEADS * NA_PAIR_TILES, body, 0)

    n_win = NA_KR * GRID_W
    kct_f32 = kct_ref[...]
    kct = kct_f32.astype(BF16)
    vct = vct_ref[...].astype(BF16)

    def win_start(r):
        return jnp.clip(r - NA_KR // 2, 0, GRID_ROWS - NA_KR)

    def attend(grid_rows, rs, bound):
        a0, n = grid_rows[0], len(grid_rows)
        row0 = pl.multiple_of(rs * GRID_W, GRID_W)
        outs = []
        for t in range(NA_HEADS // 2):
            lanes = slice(t * LANES, (t + 1) * LANES)
            bias = jnp.concatenate(
                [jnp.concatenate([bias_ref[h * NA_PAIR_TILES + (rs - (j * NA_ROWS_PER_STEP + a) + NA_KR - 1) + 2 * m]
                                  for m in range(NA_KR // 2)], axis=1)
                 for h in (2 * t, 2 * t + 1) for a in grid_rows], axis=0)
            outs.append(_na_tile_attend(
                q_ref[a0 * GRID_W:(a0 + n) * GRID_W, lanes],
                [(kl_ref[pl.ds(row0, n_win), lanes], False), (kct[lanes, :], True)],
                [(vl_ref[pl.ds(row0, n_win), lanes], False), (vct[lanes, :], True)], bias, bound))
        return jnp.concatenate(outs, axis=1).astype(BF16)

    first_r, last_r = j * NA_ROWS_PER_STEP, (j + 1) * NA_ROWS_PER_STEP - 1
    shared = win_start(first_r) == win_start(last_r)

    def attend_step(bounds):
        bound = bounds[1]

        @pl.when(shared)
        def _shared_window():
            o_ref[...] = attend(tuple(range(NA_ROWS_PER_STEP)), win_start(first_r), bound)

        @pl.when(jnp.logical_not(shared))
        def _per_row_windows():
            for a in range(NA_ROWS_PER_STEP):
                o_ref[a * GRID_W:(a + 1) * GRID_W, :] = attend((a,), win_start(first_r + a), bound)

    bias_max = jnp.max(jnp.max(jnp.abs(g_ref[...]), axis=-1, keepdims=True), axis=0, keepdims=True)
    _with_score_bounds(vec_ref, attend_step, bias_max=bias_max, na_key_sq=_max_sq_norm(kct_f32, NA_HD))


def _lat_attn_kernel(lam_init, q_ref, kl_ref, vl_ref, kc_ref, vc_ref, kct_ref, vct_ref, ona_ref, vec_ref, mix_ref):
    lam = _diff_lambda(vec_ref, lam_init)
    kct = kct_ref[...]

    def attend(bounds):
        b_mla, _, b_df = bounds
        o_mla = _mla_attend(q_ref[:, P_MLA:P_MLA + W_MLA_P],
                            [(kc_ref[...], False), (kl_ref[:, P_MLA:P_MLA + W_MLA_P], False)],
                            [(vc_ref[...], False), (vl_ref[:, PV_MLA:PV_MLA + MLA_W], False)], b_mla)
        o_df = _df_attend(q_ref[:, P_DF:P_DF + DF_W],
                          [(kct.astype(BF16), True), (kl_ref[:, P_DF:P_DF + DF_W], False)],
                          [(vct_ref[...].astype(BF16), True), (vl_ref[:, PV_DF:PV_DF + DF_W], False)],
                          lam, vec_ref[V_GDS:V_GDS + 1, 0:DF_W], 1.0 - lam_init, b_df)
        mix_ref[...] = jnp.concatenate([o_mla.astype(BF16), ona_ref[...], o_df.astype(BF16)], axis=1)

    _with_score_bounds(vec_ref, attend, df_key_sq=_max_sq_norm(kct, DF_QK))


def _const_spec(shape):
    nd = len(shape)
    return pl.BlockSpec(shape, lambda *_: (0,) * nd, pipeline_mode=pl.Buffered(1))


def _layer_spec(l, shape):
    nd = len(shape)
    return pl.BlockSpec((None,) + tuple(shape), lambda *_: (l,) + (0,) * nd, pipeline_mode=pl.Buffered(1))


def _params(n_axes):
    return pltpu.CompilerParams(dimension_semantics=("arbitrary",) * n_axes, vmem_limit_bytes=VMEM_LIMIT)


def _block_diag(group):
    i = np.arange(MXU_DIM) // group
    return jnp.asarray((i[:, None] == i[None, :]).astype(np.float32), dtype=BF16)


def _rope_tables():
    t = np.arange(DEC_SEQ)
    row = (t // GRID_W).astype(np.float64)
    col = (t % GRID_W).astype(np.float64)
    n = MLA_ROPE // 4
    inv = 1.0 / (ROPE_BASE ** (np.arange(n, dtype=np.float64) * 2.0 / (MLA_ROPE // 2)))
    ar = row[:, None] * inv
    ac = col[:, None] * inv
    ang = np.concatenate([ar, ar, ac, ac], axis=-1)
    cos32, sin32 = np.cos(ang), np.sin(ang)
    first = (np.arange(MLA_ROPE) % (2 * n)) < n
    sa32 = np.where(first, -sin32, 0.0)
    sb32 = np.where(first, 0.0, sin32)

    def mla_tile(v32, fill):
        out = np.full((DEC_SEQ, LANES), fill)
        out[:, MLA_NOPE:MLA_QK] = v32
        return out

    tabs = [mla_tile(cos32, 1.0), mla_tile(sa32, 0.0), mla_tile(sb32, 0.0),
            np.tile(cos32, (1, LANES // DF_QK)), np.tile(sa32, (1, LANES // DF_QK)), np.tile(sb32, (1, LANES // DF_QK))]
    return jnp.asarray(np.stack(tabs).astype(np.float32))


def _feature_major(a):
    a = jnp.swapaxes(a, -1, -2)
    return a.reshape(a.shape[:-3] + (a.shape[-3] * a.shape[-2], a.shape[-1]))


def _tables(p):
    w_in_t = jnp.swapaxes(p['w_in'], 1, 2)
    kr = w_in_t[:, Q_LORA + KV_LORA:B_SRC]
    z32 = jnp.zeros_like(kr)
    wa = jnp.concatenate([w_in_t[:, :Q_LORA + KV_LORA], kr, z32, kr, z32], axis=1).astype(BF16)
    wb = w_in_t[:, B_SRC:].astype(BF16)
    w_uq_t = jnp.swapaxes(p['w_uq'], 1, 2).reshape(DEPTH, MLA_HEADS, MLA_QK, Q_LORA)
    w_uq_p = jnp.pad(w_uq_t, ((0, 0), (0, 0), (0, MLA_PAD - MLA_QK), (0, 0))).reshape(DEPTH, W_MLA_P, Q_LORA).astype(BF16)
    w_ukv = p['w_ukv'].reshape(DEPTH, KV_LORA, MLA_HEADS, MLA_NOPE + MLA_V)
    wk = jnp.pad(w_ukv[..., :MLA_NOPE], ((0, 0), (0, 0), (0, 0), (0, MLA_PAD - MLA_NOPE))).reshape(DEPTH, KV_LORA, W_MLA_P)
    wv = w_ukv[..., MLA_NOPE:].reshape(DEPTH, KV_LORA, MLA_W)
    w_ukv_r = jnp.concatenate([wk, wv], axis=2).astype(BF16)

    def row(v, reps=1):
        v = jnp.tile(v, (1, reps)) if reps > 1 else v
        return [v, jnp.zeros((DEPTH, D_MODEL - v.shape[1]), F32)] if v.shape[1] < D_MODEL else [v]

    pad_head = lambda g: jnp.pad(g, ((0, 0), (0, MLA_PAD - MLA_QK)))
    pieces = (row(p['g_mix']) + row(p['g_ffn']) + row(p['g_qa']) + row(p['g_kva'])
              + row(pad_head(p['g_mla_q']), MLA_HEADS) + row(pad_head(p['g_mla_k']), MLA_HEADS)
              + row(p['g_na_q'], NA_HEADS) + row(p['g_na_k'], NA_HEADS)
              + row(p['g_df_q'], 2 * DF_HEADS) + row(p['g_df_k'], 2 * DF_HEADS) + row(p['g_df_sub'], DF_HEADS)
              + row(p['df_lq1']) + row(p['df_lk1']) + row(p['df_lq2']) + row(p['df_lk2'])
              + [jnp.zeros((DEPTH, D_MODEL), F32)])
    vec = jnp.concatenate(pieces, axis=1).reshape(DEPTH, N_VEC, D_MODEL)
    f = p['na_rpb']
    zpad = jnp.zeros((DEPTH, NA_HEADS, NA_PAIR_TILES, 33), F32)
    g_rows = jnp.concatenate([f[:, :, :-1, NA_KW - 1:], zpad, f[:, :, 1:, :], zpad, f[:, :, :-1, :NA_KW - 1]], axis=-1)
    g_rows = g_rows.reshape(DEPTH, NA_HEADS * NA_PAIR_TILES, LANES)
    return dict(wa=wa, wb=wb, w_uq=w_uq_p, w_ukv=w_ukv_r, vec=vec, g_rows=g_rows)


def _modulation(c_all, w_mod, b_mod):
    tn = 2048
    return pl.pallas_call(
        _mod_kernel,
        grid=(DEPTH, 6 * D_MODEL // tn),
        in_specs=[pl.BlockSpec((N_MOD, D_MODEL), lambda l, j: (0, 0)),
                  pl.BlockSpec((None, D_MODEL, tn), lambda l, j: (l, 0, j)),
                  pl.BlockSpec((None, 1, tn), lambda l, j: (l, 0, j))],
        out_specs=pl.BlockSpec((None, N_MOD, tn), lambda l, j: (l, 0, j)),
        out_shape=jax.ShapeDtypeStruct((DEPTH, N_MOD, 6 * D_MODEL), F32),
        compiler_params=_params(2),
        name="modulation",
    )(c_all, w_mod, b_mod.reshape(DEPTH, 1, 6 * D_MODEL))


def _front_weight_specs(l):
    return [_layer_spec(l, (N_VEC, D_MODEL)), _layer_spec(l, (A_COLS, D_MODEL)), _layer_spec(l, (B_COLS, D_MODEL)),
            _layer_spec(l, (W_MLA_P, Q_LORA)), _layer_spec(l, (KV_LORA, W_MLA_P + MLA_W)),
            _const_spec((MXU_DIM, MXU_DIM)), _const_spec((MXU_DIM, MXU_DIM))]


def _front_weights(t, consts):
    return (t['vec'], t['wa'], t['wb'], t['w_uq'], t['w_ukv'], consts['bd64'], consts['bd32'])


def _context_mixer(l, lam_init, x, mods, t, consts, ffn_f32, prev_caches):
    n_tok = BATCH * SEQ
    n_alias = len(prev_caches)
    tok = lambda w: pl.BlockSpec((CTX_SEQS_PER_STEP * SEQ, w), lambda b: (b, 0))
    if n_alias == 0:
        assert l == 0
        lay = lambda *s: pl.BlockSpec((CTX_SEQS_PER_STEP, DEPTH) + s, lambda b: (b, 0) + (0,) * len(s))
    else:
        lay = lambda *s: pl.BlockSpec((CTX_SEQS_PER_STEP, None) + s, lambda b: (b, l) + (0,) * len(s))
    cache_shapes = [(SEQ, KV_LORA), (MLA_ROPE, SEQ), (NA_HEADS, NA_HD, SEQ), (NA_HEADS, NA_HD, SEQ),
                    (DF_HEADS, DF_HD, SEQ), (DF_HEADS, DF_HD, SEQ)]
    weights = _front_weights(t, consts)
    steps = BATCH // CTX_SEQS_PER_STEP
    ffn_chunks = [(w.shape[1] // steps, w.shape[2]) for w in ffn_f32]
    n_in = 2 + len(weights) + len(ffn_f32)
    mix, *outs = pl.pallas_call(
        functools.partial(_ctx_kernel, lam_init, n_alias),
        grid=(steps,),
        in_specs=[tok(D_MODEL), pl.BlockSpec((None, 1, 6 * D_MODEL), lambda b: (l * N_MOD, 0, 0))]
        + _front_weight_specs(l) + [pl.BlockSpec((None,) + c, lambda b: (l, b, 0)) for c in ffn_chunks]
        + [pl.BlockSpec(memory_space=pl.ANY)] * n_alias,
        out_specs=[tok(D_MODEL)] + [lay(*s) for s in cache_shapes] + [pl.BlockSpec(c, lambda b: (b, 0)) for c in ffn_chunks],
        out_shape=[jax.ShapeDtypeStruct((n_tok, D_MODEL), BF16)]
        + [jax.ShapeDtypeStruct((BATCH, DEPTH) + s, F32) for s in cache_shapes]
        + [jax.ShapeDtypeStruct(w.shape[1:], BF16) for w in ffn_f32],
        input_output_aliases={n_in + i: 1 + i for i in range(n_alias)},
        compiler_params=_params(1),
        name="context_mixer",
    )(x, mods, *weights, *ffn_f32, *prev_caches)
    return mix, outs[:N_CACHE], outs[N_CACHE:]


def _finish(l, x, mix, mods, t, ffn_bf16, mod_row):
    n_tok = x.shape[0]
    tok = pl.BlockSpec((TM_FINISH, D_MODEL), lambda i: (i, 0))
    return pl.pallas_call(
        _finish_kernel,
        grid=(n_tok // TM_FINISH,),
        in_specs=[tok, tok, pl.BlockSpec((None, 1, 6 * D_MODEL), lambda i: (l * N_MOD + mod_row(i), 0, 0)),
                  _layer_spec(l, (N_VEC, D_MODEL))] + [_const_spec(w.shape) for w in ffn_bf16],
        out_specs=tok,
        out_shape=jax.ShapeDtypeStruct((n_tok, D_MODEL), F32),
        compiler_params=_params(1),
        name="finish",
    )(x, mix, mods, t['vec'], *ffn_bf16)


def _latent_front(l, x, mods, t, consts):
    n_tok = DEC_BATCH * DEC_SEQ
    tm = TM_LAT_FRONT
    blocks_per_seq = DEC_SEQ // tm
    tok = lambda w: pl.BlockSpec((tm, w), lambda i: (i, 0))
    wspecs = _front_weight_specs(l)
    weights = _front_weights(t, consts)
    return pl.pallas_call(
        _lat_front_kernel,
        grid=(n_tok // tm,),
        in_specs=[tok(D_MODEL),
                  pl.BlockSpec((None, 1, 6 * D_MODEL), lambda i: (l * N_MOD + 1 + i // blocks_per_seq, 0, 0)),
                  wspecs[0], pl.BlockSpec((6, tm, LANES), lambda i: (0, i % blocks_per_seq, 0))] + wspecs[1:],
        out_specs=[tok(QK_PACK), tok(QK_PACK), tok(V_PACK)],
        out_shape=[jax.ShapeDtypeStruct((n_tok, QK_PACK), BF16), jax.ShapeDtypeStruct((n_tok, QK_PACK), BF16),
                   jax.ShapeDtypeStruct((n_tok, V_PACK), BF16)],
        compiler_params=_params(1),
        name="latent_front",
    )(x, mods, weights[0], consts['rope'], *weights[1:])


def _context_kv(l, ckv, krope_t, t):
    lay = lambda r, w: pl.BlockSpec((None, None, r, w), lambda b: (b, l, 0, 0))
    out = lambda w: pl.BlockSpec((None, PAST_LEN, w), lambda b: (b, 0, 0))
    return pl.pallas_call(
        _ctx_kv_kernel,
        grid=(DEC_BATCH,),
        in_specs=[lay(PAST_LEN, KV_LORA), lay(MLA_ROPE, PAST_LEN),
                  _layer_spec(l, (N_VEC, D_MODEL)), _layer_spec(l, (KV_LORA, W_MLA_P + MLA_W))],
        out_specs=[out(W_MLA_P), out(MLA_W)],
        out_shape=[jax.ShapeDtypeStruct((DEC_BATCH, PAST_LEN, W_MLA_P), BF16),
                   jax.ShapeDtypeStruct((DEC_BATCH, PAST_LEN, MLA_W), BF16)],
        compiler_params=_params(1),
        name="context_kv",
    )(ckv, krope_t, t['vec'], t['w_ukv'])


def _latent_na(l, q, k, v, kct, vct, g_rows, vec):
    n_tok = DEC_BATCH * DEC_SEQ
    na_blk = P_NA // NA_W
    steps = GRID_ROWS // NA_ROWS_PER_STEP
    tq = NA_ROWS_PER_STEP * GRID_W
    ctx = pl.BlockSpec((None, None, NA_W, PAST_LEN), lambda b, j: (b, l, 0, 0))
    return pl.pallas_call(
        _na_lat_kernel,
        grid=(DEC_BATCH, steps),
        in_specs=[pl.BlockSpec((tq, NA_W), lambda b, j: (b * steps + j, na_blk)),
                  pl.BlockSpec((DEC_SEQ, NA_W), lambda b, j: (b, na_blk)),
                  pl.BlockSpec((DEC_SEQ, NA_W), lambda b, j: (b, PV_NA // NA_W)),
                  ctx, ctx, _layer_spec(l, (NA_HEADS * NA_PAIR_TILES, LANES)), _layer_spec(l, (N_VEC, D_MODEL))],
        out_specs=pl.BlockSpec((tq, NA_W), lambda b, j: (b * steps + j, 0)),
        out_shape=jax.ShapeDtypeStruct((n_tok, NA_W), BF16),
        scratch_shapes=[pltpu.VMEM((NA_HEADS * NA_PAIR_TILES, GRID_W, LANES), F32)],
        compiler_params=_params(2),
        name="latent_neighbourhood",
    )(q, k, v, kct, vct, g_rows, vec)


def _latent_attention(l, lam_init, q, k, v, kc, vc, kct, vct, o_na, t):
    n_tok = DEC_BATCH * DEC_SEQ
    nq = DEC_SEQ // TM
    ctx = pl.BlockSpec((None, None, DF_W, PAST_LEN), lambda b, j: (b, l, 0, 0))
    return pl.pallas_call(
        functools.partial(_lat_attn_kernel, lam_init),
        grid=(DEC_BATCH, nq),
        in_specs=[pl.BlockSpec((TM, QK_PACK), lambda b, j: (b * nq + j, 0)),
                  pl.BlockSpec((DEC_SEQ, QK_PACK), lambda b, j: (b, 0)),
                  pl.BlockSpec((DEC_SEQ, V_PACK), lambda b, j: (b, 0)),
                  pl.BlockSpec((None, PAST_LEN, W_MLA_P), lambda b, j: (b, 0, 0)),
                  pl.BlockSpec((None, PAST_LEN, MLA_W), lambda b, j: (b, 0, 0)),
                  ctx, ctx,
                  pl.BlockSpec((TM, NA_W), lambda b, j: (b * nq + j, 0)),
                  _layer_spec(l, (N_VEC, D_MODEL))],
        out_specs=pl.BlockSpec((TM, D_MODEL), lambda b, j: (b * nq + j, 0)),
        out_shape=jax.ShapeDtypeStruct((n_tok, D_MODEL), BF16),
        compiler_params=_params(2),
        name="latent_attention",
    )(q, k, v, kc, vc, kct, vct, o_na, t['vec'])


def kernel(x_prompt, x_sample, cache_mla_ckv, cache_mla_krope, cache_na_k, cache_na_v, cache_df_k, cache_df_v, c, c_ctx, w_mod, b_mod, g_mix, w_in, g_qa, w_uq, g_kva, w_ukv, g_mla_q, g_mla_k, g_na_q, g_na_k, na_rpb, g_df_q, g_df_k, df_lq1, df_lk1, df_lq2, df_lk2, g_df_sub, w_out, g_ffn, w_gate, w_up, w_down):
    p = dict(g_mix=g_mix, w_in=w_in, g_qa=g_qa, w_uq=w_uq, g_kva=g_kva, w_ukv=w_ukv, g_mla_q=g_mla_q, g_mla_k=g_mla_k,
             g_na_q=g_na_q, g_na_k=g_na_k, na_rpb=na_rpb, g_df_q=g_df_q, g_df_k=g_df_k, df_lq1=df_lq1, df_lk1=df_lk1,
             df_lq2=df_lq2, df_lk2=df_lk2, g_df_sub=g_df_sub, w_out=w_out, g_ffn=g_ffn, w_gate=w_gate, w_up=w_up,
             w_down=w_down)
    consts = dict(bd64=_block_diag(NA_HD), bd32=_block_diag(DF_QK), rope=_rope_tables())
    t = _tables(p)

    c_all = jnp.concatenate([c_ctx[None, :], c, jnp.zeros((N_MOD - 1 - DEC_BATCH, D_MODEL), F32)], axis=0)
    mods = _modulation(c_all, w_mod, b_mod).reshape(DEPTH * N_MOD, 1, 6 * D_MODEL)

    krope_t = jnp.swapaxes(cache_mla_krope, -1, -2)
    na_kt, na_vt, df_kt, df_vt = (_feature_major(a) for a in (cache_na_k, cache_na_v, cache_df_k, cache_df_v))

    xp = x_prompt.reshape(BATCH * SEQ, D_MODEL)
    xs = x_sample.reshape(DEC_BATCH * DEC_SEQ, D_MODEL)
    new_caches = ()
    ffn_f32 = (w_out, w_gate, w_up, w_down)
    for l in range(DEPTH):
        lam_init = 0.8 - 0.6 * math.exp(-0.3 * l)
        mix, new_caches, ffn_bf16 = _context_mixer(l, lam_init, xp, mods, t, consts, ffn_f32, new_caches)
        xp = _finish(l, xp, mix, mods, t, ffn_bf16, lambda i: 0)
        q, k, v = _latent_front(l, xs, mods, t, consts)
        kc, vc = _context_kv(l, cache_mla_ckv, krope_t, t)
        o_na = _latent_na(l, q, k, v, na_kt, na_vt, t['g_rows'], t['vec'])
        mix_s = _latent_attention(l, lam_init, q, k, v, kc, vc, df_kt, df_vt, o_na, t)
        xs = _finish(l, xs, mix_s, mods, t, ffn_bf16, lambda i: 1 + i // (DEC_SEQ // TM_FINISH))
    ckv_new, *narrow = new_caches
    return (xp.reshape(BATCH, SEQ, D_MODEL), xs.reshape(DEC_BATCH, DEC_SEQ, D_MODEL), ckv_new,
            *(jnp.swapaxes(a, -1, -2) for a in narrow))
```

```python
import functools
import math

import numpy as np
import jax
import jax.numpy as jnp
from jax import lax
from jax.experimental import pallas as pl
from jax.experimental.pallas import tpu as pltpu

F32 = jnp.float32
BF16 = jnp.bfloat16

D_MODEL = 1024
BATCH = 32
SEQ = 256
DEPTH = 2
DEC_BATCH = 2
DEC_SEQ = 1024
PAST_LEN = 256
GRID_W = 64
GRID_ROWS = DEC_SEQ // GRID_W
MLA_HEADS = 6
MLA_NOPE = 64
MLA_ROPE = 32
MLA_QK = MLA_NOPE + MLA_ROPE
MLA_V = 64
MLA_PAD = 128
Q_LORA = 256
KV_LORA = 128
NA_HEADS = 6
NA_HD = 64
NA_KR = 8
NA_KW = 16
DF_HEADS = 4
DF_HD = 64
DF_QK = 32
MLA_W = MLA_HEADS * MLA_V
NA_W = NA_HEADS * NA_HD
DF_W = DF_HEADS * DF_HD
D_FF = -(-8 * D_MODEL // (3 * 256)) * 256
ROPE_BASE = 10000.0
EPS = 1e-6
NEG = -1e30
LOG2E = math.log2(math.e)
MAX_FREE_SOFTMAX_BOUND = 60.0

LANES = 128
MXU_DIM = 256

A_CQ = 0
A_CKV = A_CQ + Q_LORA
A_KR = A_CKV + KV_LORA
A_COLS = A_KR + LANES
B_SRC = Q_LORA + KV_LORA + MLA_ROPE
B_NAQ = 0
B_NAK = B_NAQ + NA_W
B_NAV = B_NAK + NA_W
B_DFQ = B_NAV + NA_W
B_DFK = B_DFQ + DF_W
B_DFV = B_DFK + DF_W
B_COLS = B_DFV + DF_W

W_MLA_P = MLA_HEADS * MLA_PAD
QK_PACK = W_MLA_P + NA_W + DF_W
V_PACK = MLA_W + NA_W + DF_W
P_MLA = 0
P_NA = W_MLA_P
P_DF = P_NA + NA_W
PV_MLA = 0
PV_NA = MLA_W
PV_DF = MLA_W + NA_W

(V_GMIX, V_GFFN, V_GQA, V_GKVA, V_GMQ, V_GMK, V_GNQ, V_GNK, V_GDQ, V_GDK, V_GDS,
 V_LQ1, V_LK1, V_LQ2, V_LK2) = range(15)
N_VEC = 16

N_MOD = 8
TM = 512
TM_LAT_FRONT = 256
TM_FINISH = 512
CTX_SEQS_PER_STEP = 2
NA_ROWS_PER_STEP = 4
VMEM_LIMIT = 56 * 1024 * 1024

NA_PAIR_TILES = 2 * NA_KR - 2
N_CACHE = 6
N_FFN_W = 4


def _dot(a, b):
    return jnp.dot(a, b, preferred_element_type=F32)


def _dot_nt(a, b):
    return lax.dot_general(a, b, (((1,), (1,)), ((), ())), preferred_element_type=F32)


def _lane_iota(shape):
    return lax.broadcasted_iota(jnp.int32, shape, len(shape) - 1)


def _rms_rows(x, g):
    ms = jnp.mean(x * x, axis=-1, keepdims=True)
    return x * lax.rsqrt(ms + EPS) * g


def _tile_rms(x, g, n_real):
    outs = []
    for c0 in range(0, x.shape[1], LANES):
        xt = x[:, c0:c0 + LANES]
        ms = jnp.sum(xt * xt, axis=-1, keepdims=True) * (1.0 / n_real)
        outs.append(xt * lax.rsqrt(ms + EPS) * g[:, c0:c0 + LANES])
    return jnp.concatenate(outs, axis=1)


def _seg_rms(x, bd_ref, g, group):
    width = x.shape[1]
    sq = (x * x).astype(BF16)
    parts = []
    for c0 in range(0, width, MXU_DIM):
        w = min(MXU_DIM, width - c0)
        parts.append(_dot(sq[:, c0:c0 + w], bd_ref[0:w, 0:w]))
    ss = parts[0] if len(parts) == 1 else jnp.concatenate(parts, axis=1)
    return x * lax.rsqrt(ss * (1.0 / group) + EPS) * g


def _rope_tiles(x, cos, sa, sb):
    outs = []
    for t in range(x.shape[1] // LANES):
        xt = x[:, t * LANES:(t + 1) * LANES]
        up = pltpu.roll(xt, LANES - MLA_ROPE // 4, 1)
        dn = pltpu.roll(xt, MLA_ROPE // 4, 1)
        outs.append(xt * cos + up * sa + dn * sb)
    return outs[0] if len(outs) == 1 else jnp.concatenate(outs, axis=1)


def _diff_lambda(vec_ref, lam_init):
    a = jnp.sum(vec_ref[V_LQ1:V_LQ1 + 1, 0:DF_QK] * vec_ref[V_LK1:V_LK1 + 1, 0:DF_QK], axis=-1, keepdims=True)
    b = jnp.sum(vec_ref[V_LQ2:V_LQ2 + 1, 0:DF_QK] * vec_ref[V_LK2:V_LK2 + 1, 0:DF_QK], axis=-1, keepdims=True)
    return jnp.exp(a) - jnp.exp(b) + lam_init


def _mixer_front(x, mod, vec_ref, wa_ref, wb_ref, w_uq_ref, bd64_ref, bd32_ref):
    sh = mod[:, 0:D_MODEL]
    sc = mod[:, D_MODEL:2 * D_MODEL]
    h = (_rms_rows(x, vec_ref[V_GMIX:V_GMIX + 1, :]) * (1.0 + sc) + sh).astype(BF16)
    za = _dot_nt(h, wa_ref[...])
    zb = _dot_nt(h, wb_ref[...])
    cqn = _rms_rows(za[:, A_CQ:A_CQ + Q_LORA], vec_ref[V_GQA:V_GQA + 1, 0:Q_LORA])
    q_raw = _dot_nt(cqn.astype(BF16), w_uq_ref[...])
    q_mla = _tile_rms(q_raw, vec_ref[V_GMQ:V_GMQ + 1, 0:W_MLA_P] * (MLA_QK ** -0.5 * LOG2E), MLA_QK)
    ckv_n = _rms_rows(za[:, A_CKV:A_CKV + KV_LORA], vec_ref[V_GKVA:V_GKVA + 1, 0:KV_LORA])
    kr_tile = za[:, A_KR:A_KR + LANES]
    q_na = _seg_rms(zb[:, B_NAQ:B_NAQ + NA_W], bd64_ref, vec_ref[V_GNQ:V_GNQ + 1, 0:NA_W] * (NA_HD ** -0.5 * LOG2E), NA_HD)
    k_na = _seg_rms(zb[:, B_NAK:B_NAK + NA_W], bd64_ref, vec_ref[V_GNK:V_GNK + 1, 0:NA_W], NA_HD)
    v_na = zb[:, B_NAV:B_NAV + NA_W]
    q_df = _seg_rms(zb[:, B_DFQ:B_DFQ + DF_W], bd32_ref, vec_ref[V_GDQ:V_GDQ + 1, 0:DF_W] * (DF_QK ** -0.5 * LOG2E), DF_QK)
    k_df = _seg_rms(zb[:, B_DFK:B_DFK + DF_W], bd32_ref, vec_ref[V_GDK:V_GDK + 1, 0:DF_W], DF_QK)
    v_df = zb[:, B_DFV:B_DFV + DF_W]
    return q_mla, ckv_n, kr_tile, q_na, k_na, v_na, q_df, k_df, v_df


def _mla_kv(ckv_n, kr_tile, vec_ref, w_ukv_ref):
    kv = _dot(ckv_n.astype(BF16), w_ukv_ref[...])
    lane = _lane_iota((1, LANES))
    kr = jnp.where((lane >= MLA_NOPE) & (lane < MLA_QK), kr_tile, 0.0)
    k_pre = kv[:, 0:W_MLA_P] + jnp.concatenate([kr] * MLA_HEADS, axis=1)
    k = _tile_rms(k_pre, vec_ref[V_GMK:V_GMK + 1, 0:W_MLA_P], MLA_QK)
    return k, kv[:, W_MLA_P:W_MLA_P + MLA_W]


def _softmax_parts(s, bound=None):
    m = jnp.max(s, axis=-1, keepdims=True) if bound is None else bound
    p = jnp.exp2(s - m)
    return p, 1.0 / jnp.sum(p, axis=-1, keepdims=True)


def _max_sq_norm(kt, group):
    sq = kt * kt
    best = None
    for r0 in range(0, kt.shape[0], group):
        n2 = jnp.sum(sq[r0:r0 + group], axis=0, keepdims=True)
        best = n2 if best is None else jnp.maximum(best, n2)
    return jnp.max(best, axis=-1, keepdims=True)


def _score_bounds(vec_ref, bias_max=None, na_key_sq=None, df_key_sq=None):
    gmax = lambda row, w: jnp.max(jnp.abs(vec_ref[row:row + 1, 0:w]), axis=-1, keepdims=True)
    slack = LOG2E * (1.0 + 2.0 ** -6)

    def key_norm(row, w, d, measured_sq):
        k = gmax(row, w) * d ** 0.5
        return k if measured_sq is None else jnp.maximum(k, jnp.sqrt(measured_sq))

    b_mla = gmax(V_GMQ, W_MLA_P) * key_norm(V_GMK, W_MLA_P, MLA_QK, None) * slack
    b_na = gmax(V_GNQ, NA_W) * key_norm(V_GNK, NA_W, NA_HD, na_key_sq) * slack
    if bias_max is not None:
        b_na = b_na + bias_max * LOG2E
    b_df = gmax(V_GDQ, DF_W) * key_norm(V_GDK, DF_W, DF_QK, df_key_sq) * slack
    worst = jnp.maximum(b_mla, jnp.maximum(b_na, b_df))
    return (b_mla, b_na, b_df), worst[0, 0] <= MAX_FREE_SOFTMAX_BOUND


def _with_score_bounds(vec_ref, attend, **measured):
    bounds, ok = _score_bounds(vec_ref, **measured)
    pl.when(ok)(lambda: attend(bounds))
    pl.when(jnp.logical_not(ok))(lambda: attend((None, None, None)))


def _scores(q, k_segs):
    parts = [_dot(q, k) if feature_major else _dot_nt(q, k) for k, feature_major in k_segs]
    return parts[0] if len(parts) == 1 else jnp.concatenate(parts, axis=1)


def _pv(p, v_segs):
    out = None
    c0 = 0
    for v, feature_major in v_segs:
        n = v.shape[1] if feature_major else v.shape[0]
        o = _dot_nt(p[:, c0:c0 + n], v) if feature_major else _dot(p[:, c0:c0 + n], v)
        out = o if out is None else out + o
        c0 += n
    return out


def _seg_tile(seg, t):
    a, feature_major = seg
    return (a[t * LANES:(t + 1) * LANES, :] if feature_major else a[:, t * LANES:(t + 1) * LANES]), feature_major


def _lane_groups(qt, width):
    lane = _lane_iota((1, LANES))
    zero = jnp.zeros_like(qt)
    return jnp.concatenate(
        [jnp.where((lane >= g * width) & (lane < (g + 1) * width), qt, zero) for g in range(LANES // width)], axis=0)


def _pair_select(o2):
    tq = o2.shape[0] // 2
    return jnp.where(_lane_iota((1, LANES)) < NA_HD, o2[0:tq], o2[tq:2 * tq])


def _mla_attend(q, k_segs, v_segs, bound=None):
    outs = []
    for t in range(MLA_HEADS // 2):
        vt = [_seg_tile(v, t) for v in v_segs]
        halves = []
        for h in (2 * t, 2 * t + 1):
            p, il = _softmax_parts(_scores(q[:, h * MLA_PAD:(h + 1) * MLA_PAD], [_seg_tile(k, h) for k in k_segs]),
                                   bound)
            halves.append(_pv(p.astype(BF16), vt) * il)
        outs.append(jnp.where(_lane_iota((1, LANES)) < MLA_V, halves[0], halves[1]))
    return jnp.concatenate(outs, axis=1)


def _na_tile_attend(qt, k_segs, v_segs, bias=None, bound=None):
    s = _scores(_lane_groups(qt, NA_HD), k_segs)
    if bias is not None:
        nb = bias.shape[1]
        s = jnp.concatenate([s[:, 0:nb] + bias, s[:, nb:]], axis=1)
    p, il = _softmax_parts(s, bound)
    return _pair_select(_pv(p.astype(BF16), v_segs) * il)


def _na_attend_full(q, k_segs, v_segs, bound=None):
    return jnp.concatenate(
        [_na_tile_attend(q[:, t * LANES:(t + 1) * LANES], [_seg_tile(k, t) for k in k_segs],
                         [_seg_tile(v, t) for v in v_segs], bound=bound) for t in range(NA_HEADS // 2)], axis=1)


def _df_attend(q, k_segs, v_segs, lam, g_sub, out_scale, bound=None):
    outs = []
    lane = _lane_iota((1, LANES))
    tq = q.shape[0]
    for t in range(DF_HEADS // 2):
        kt = [_seg_tile(k, t) for k in k_segs]
        vt = [_seg_tile(v, t) for v in v_segs]
        p, il = _softmax_parts(_scores(_lane_groups(q[:, t * LANES:(t + 1) * LANES], DF_QK), kt), bound)
        pn = []
        for hh in range(2):
            r1, r2 = 2 * hh * tq, (2 * hh + 1) * tq
            pn.append((p[r1:r1 + tq] * il[r1:r1 + tq] - p[r2:r2 + tq] * (lam * il[r2:r2 + tq])).astype(BF16))
        o = _pair_select(_pv(jnp.concatenate(pn, axis=0), vt))
        o2 = o * o
        ms_e = jnp.sum(jnp.where(lane < DF_HD, o2, 0.0), axis=-1, keepdims=True)
        ms_o = jnp.sum(jnp.where(lane >= DF_HD, o2, 0.0), axis=-1, keepdims=True)
        r = lax.rsqrt(jnp.where(lane < DF_HD, ms_e, ms_o) * (1.0 / DF_HD) + EPS)
        outs.append(o * r * (g_sub[:, t * LANES:(t + 1) * LANES] * out_scale))
    return jnp.concatenate(outs, axis=1)


def _mod_kernel(c_ref, w_ref, b_ref, o_ref):
    c = c_ref[...]
    s = c * jax.nn.sigmoid(c)
    o_ref[...] = _dot(s.astype(BF16), w_ref[...].astype(BF16)) + b_ref[...]


def _ctx_kernel(lam_init, n_alias, x_ref, mod_ref, vec_ref, wa_ref, wb_ref, w_uq_ref, w_ukv_ref, bd64_ref, bd32_ref,
                *rest):
    ffn_f32, rest = rest[:N_FFN_W], rest[N_FFN_W + n_alias:]
    mix_ref, *cache_refs = rest[:1 + N_CACHE]
    for src, dst in zip(ffn_f32, rest[1 + N_CACHE:]):
        dst[...] = src[...].astype(BF16)
    if n_alias == 0:
        for ref in cache_refs:
            ref[:, 1:] = jnp.zeros((ref.shape[0], ref.shape[1] - 1) + ref.shape[2:], F32)
        cache_refs = [ref.at[:, 0] for ref in cache_refs]
    ckv_ref, kr_ref, nak_ref, nav_ref, dfk_ref, dfv_ref = cache_refs
    lam = _diff_lambda(vec_ref, lam_init)
    mod = mod_ref[...]
    bf = lambda a: a.astype(BF16)
    seg = lambda a: [(a.astype(BF16), False)]
    for s in range(CTX_SEQS_PER_STEP):
        rows = slice(s * SEQ, (s + 1) * SEQ)
        q_mla, ckv_n, kr_tile, q_na, k_na, v_na, q_df, k_df, v_df = _mixer_front(
            x_ref[rows, :], mod, vec_ref, wa_ref, wb_ref, w_uq_ref, bd64_ref, bd32_ref)
        k_mla, v_mla = _mla_kv(ckv_n, kr_tile, vec_ref, w_ukv_ref)
        ckv_ref[s] = ckv_n
        kr_ref[s] = kr_tile.T[0:MLA_ROPE]
        for ref, a in ((nak_ref, k_na), (nav_ref, v_na), (dfk_ref, k_df), (dfv_ref, v_df)):
            at = a.T
            for h in range(ref.shape[1]):
                ref[s, h] = at[h * NA_HD:(h + 1) * NA_HD]
        o_mla = _mla_attend(bf(q_mla), seg(k_mla), seg(v_mla))
        o_na = _na_attend_full(bf(q_na), seg(k_na), seg(v_na))
        o_df = _df_attend(bf(q_df), seg(k_df), seg(v_df), lam, vec_ref[V_GDS:V_GDS + 1, 0:DF_W], 1.0 - lam_init)
        mix_ref[rows, :] = jnp.concatenate([o_mla, o_na, o_df], axis=1).astype(BF16)


def _finish_kernel(n_ctx_blocks, xc_ref, mixc_ref, xl_ref, mixl_ref, mod_ref, vec_ref, w_out_ref, w_gate_ref, w_up_ref,
                   w_down_ref, yc_ref, yl_ref):
    def block(x_ref, mix_ref, y_ref):
        mod = mod_ref[...]
        gate_m = mod[:, 2 * D_MODEL:3 * D_MODEL]
        sh = mod[:, 3 * D_MODEL:4 * D_MODEL]
        sc = mod[:, 4 * D_MODEL:5 * D_MODEL]
        gate_f = mod[:, 5 * D_MODEL:6 * D_MODEL]
        x1 = x_ref[...] + gate_m * _dot(mix_ref[...], w_out_ref[...])
        h = (_rms_rows(x1, vec_ref[V_GFFN:V_GFFN + 1, :]) * (1.0 + sc) + sh).astype(BF16)
        g = _dot(h, w_gate_ref[...])
        u = _dot(h, w_up_ref[...])
        a = (g * jax.nn.sigmoid(g) * u).astype(BF16)
        y_ref[...] = x1 + gate_f * _dot(a, w_down_ref[...])

    is_ctx = pl.program_id(0) < n_ctx_blocks
    pl.when(is_ctx)(lambda: block(xc_ref, mixc_ref, yc_ref))
    pl.when(jnp.logical_not(is_ctx))(lambda: block(xl_ref, mixl_ref, yl_ref))


def _lat_front_kernel(x_ref, mod_ref, vec_ref, rope_ref, wa_ref, wb_ref, w_uq_ref, w_ukv_ref, bd64_ref, bd32_ref,
                      q_ref, k_ref, v_ref):
    q_mla, ckv_n, kr_tile, q_na, k_na, v_na, q_df, k_df, v_df = _mixer_front(
        x_ref[...], mod_ref[...], vec_ref, wa_ref, wb_ref, w_uq_ref, bd64_ref, bd32_ref)
    k_mla, v_mla = _mla_kv(ckv_n, kr_tile, vec_ref, w_ukv_ref)
    cm, sam, sbm = rope_ref[0], rope_ref[1], rope_ref[2]
    cd, sad, sbd = rope_ref[3], rope_ref[4], rope_ref[5]
    q_ref[...] = jnp.concatenate(
        [_rope_tiles(q_mla, cm, sam, sbm), q_na, _rope_tiles(q_df, cd, sad, sbd)], axis=1).astype(BF16)
    k_ref[...] = jnp.concatenate(
        [_rope_tiles(k_mla, cm, sam, sbm), k_na, _rope_tiles(k_df, cd, sad, sbd)], axis=1).astype(BF16)
    v_ref[...] = jnp.concatenate([v_mla, v_na, v_df], axis=1).astype(BF16)


def _ctx_kv_kernel(ckv_ref, krt_ref, vec_ref, w_ukv_ref, k_ref, v_ref):
    krt = jnp.concatenate([jnp.zeros((MLA_NOPE, PAST_LEN), F32), krt_ref[...],
                           jnp.zeros((LANES - MLA_QK, PAST_LEN), F32)], axis=0)
    k_mla, v_mla = _mla_kv(ckv_ref[...], krt.T, vec_ref, w_ukv_ref)
    k_ref[...] = k_mla.astype(BF16)
    v_ref[...] = v_mla.astype(BF16)


def _na_lat_kernel(q_ref, kl_ref, vl_ref, kct_ref, vct_ref, g_ref, vec_ref, o_ref, bias_ref):
    b = pl.program_id(0)
    j = pl.program_id(1)

    @pl.when((b == 0) & (j == 0))
    def _build_bias():
        c = lax.broadcasted_iota(jnp.int32, (GRID_W, LANES), 0)
        kc = _lane_iota((GRID_W, LANES)) % GRID_W
        start = jnp.clip(c - NA_KW // 2, 0, GRID_W - NA_KW)
        in_win = (kc >= start) & (kc < start + NA_KW)

        def body(i, carry):
            row = jnp.broadcast_to(g_ref[pl.ds(i, 1), :], (GRID_W, LANES))
            toep = pltpu.roll(row, 0, 1, stride=1, stride_axis=0)
            bias_ref[i] = jnp.where(in_win, toep * LOG2E, NEG)
            return carry

        lax.fori_loop(0, NA_HEADS * NA_PAIR_TILES, body, 0)

    n_win = NA_KR * GRID_W
    kct_f32 = kct_ref[...]
    kct = kct_f32.astype(BF16)
    vct = vct_ref[...].astype(BF16)

    def win_start(r):
        return jnp.clip(r - NA_KR // 2, 0, GRID_ROWS - NA_KR)

    def attend(grid_rows, rs, bound):
        a0, n = grid_rows[0], len(grid_rows)
        row0 = pl.multiple_of(rs * GRID_W, GRID_W)
        outs = []
        for t in range(NA_HEADS // 2):
            lanes = slice(t * LANES, (t + 1) * LANES)
            bias = jnp.concatenate(
                [jnp.concatenate([bias_ref[h * NA_PAIR_TILES + (rs - (j * NA_ROWS_PER_STEP + a) + NA_KR - 1) + 2 * m]
                                  for m in range(NA_KR // 2)], axis=1)
                 for h in (2 * t, 2 * t + 1) for a in grid_rows], axis=0)
            outs.append(_na_tile_attend(
                q_ref[a0 * GRID_W:(a0 + n) * GRID_W, lanes],
                [(kl_ref[pl.ds(row0, n_win), lanes], False), (kct[lanes, :], True)],
                [(vl_ref[pl.ds(row0, n_win), lanes], False), (vct[lanes, :], True)], bias, bound))
        return jnp.concatenate(outs, axis=1).astype(BF16)

    first_r, last_r = j * NA_ROWS_PER_STEP, (j + 1) * NA_ROWS_PER_STEP - 1
    shared = win_start(first_r) == win_start(last_r)

    def attend_step(bounds):
        bound = bounds[1]

        @pl.when(shared)
        def _shared_window():
            o_ref[...] = attend(tuple(range(NA_ROWS_PER_STEP)), win_start(first_r), bound)

        @pl.when(jnp.logical_not(shared))
        def _per_row_windows():
            for a in range(NA_ROWS_PER_STEP):
                o_ref[a * GRID_W:(a + 1) * GRID_W, :] = attend((a,), win_start(first_r + a), bound)

    bias_max = jnp.max(jnp.max(jnp.abs(g_ref[...]), axis=-1, keepdims=True), axis=0, keepdims=True)
    _with_score_bounds(vec_ref, attend_step, bias_max=bias_max, na_key_sq=_max_sq_norm(kct_f32, NA_HD))


def _lat_attn_kernel(lam_init, q_ref, kl_ref, vl_ref, kc_ref, vc_ref, kct_ref, vct_ref, ona_ref, vec_ref, mix_ref):
    lam = _diff_lambda(vec_ref, lam_init)
    kct = kct_ref[...]

    def attend(bounds):
        b_mla, _, b_df = bounds
        o_mla = _mla_attend(q_ref[:, P_MLA:P_MLA + W_MLA_P],
                            [(kc_ref[...], False), (kl_ref[:, P_MLA:P_MLA + W_MLA_P], False)],
                            [(vc_ref[...], False), (vl_ref[:, PV_MLA:PV_MLA + MLA_W], False)], b_mla)
        o_df = _df_attend(q_ref[:, P_DF:P_DF + DF_W],
                          [(kct.astype(BF16), True), (kl_ref[:, P_DF:P_DF + DF_W], False)],
                          [(vct_ref[...].astype(BF16), True), (vl_ref[:, PV_DF:PV_DF + DF_W], False)],
                          lam, vec_ref[V_GDS:V_GDS + 1, 0:DF_W], 1.0 - lam_init, b_df)
        mix_ref[...] = jnp.concatenate([o_mla.astype(BF16), ona_ref[...], o_df.astype(BF16)], axis=1)

    _with_score_bounds(vec_ref, attend, df_key_sq=_max_sq_norm(kct, DF_QK))


def _const_spec(shape):
    nd = len(shape)
    return pl.BlockSpec(shape, lambda *_: (0,) * nd, pipeline_mode=pl.Buffered(1))


def _layer_spec(l, shape):
    nd = len(shape)
    return pl.BlockSpec((None,) + tuple(shape), lambda *_: (l,) + (0,) * nd, pipeline_mode=pl.Buffered(1))


def _params(n_axes):
    return pltpu.CompilerParams(dimension_semantics=("arbitrary",) * n_axes, vmem_limit_bytes=VMEM_LIMIT)


def _block_diag(group):
    i = np.arange(MXU_DIM) // group
    return jnp.asarray((i[:, None] == i[None, :]).astype(np.float32), dtype=BF16)


def _rope_tables():
    t = np.arange(DEC_SEQ)
    row = (t // GRID_W).astype(np.float64)
    col = (t % GRID_W).astype(np.float64)
    n = MLA_ROPE // 4
    inv = 1.0 / (ROPE_BASE ** (np.arange(n, dtype=np.float64) * 2.0 / (MLA_ROPE // 2)))
    ar = row[:, None] * inv
    ac = col[:, None] * inv
    ang = np.concatenate([ar, ar, ac, ac], axis=-1)
    cos32, sin32 = np.cos(ang), np.sin(ang)
    first = (np.arange(MLA_ROPE) % (2 * n)) < n
    sa32 = np.where(first, -sin32, 0.0)
    sb32 = np.where(first, 0.0, sin32)

    def mla_tile(v32, fill):
        out = np.full((DEC_SEQ, LANES), fill)
        out[:, MLA_NOPE:MLA_QK] = v32
        return out

    tabs = [mla_tile(cos32, 1.0), mla_tile(sa32, 0.0), mla_tile(sb32, 0.0),
            np.tile(cos32, (1, LANES // DF_QK)), np.tile(sa32, (1, LANES // DF_QK)), np.tile(sb32, (1, LANES // DF_QK))]
    return jnp.asarray(np.stack(tabs).astype(np.float32))


def _feature_major(a):
    a = jnp.swapaxes(a, -1, -2)
    return a.reshape(a.shape[:-3] + (a.shape[-3] * a.shape[-2], a.shape[-1]))


def _tables(p):
    w_in_t = jnp.swapaxes(p['w_in'], 1, 2)
    kr = w_in_t[:, Q_LORA + KV_LORA:B_SRC]
    z32 = jnp.zeros_like(kr)
    wa = jnp.concatenate([w_in_t[:, :Q_LORA + KV_LORA], kr, z32, kr, z32], axis=1).astype(BF16)
    wb = w_in_t[:, B_SRC:].astype(BF16)
    w_uq_t = jnp.swapaxes(p['w_uq'], 1, 2).reshape(DEPTH, MLA_HEADS, MLA_QK, Q_LORA)
    w_uq_p = jnp.pad(w_uq_t, ((0, 0), (0, 0), (0, MLA_PAD - MLA_QK), (0, 0))).reshape(DEPTH, W_MLA_P, Q_LORA).astype(BF16)
    w_ukv = p['w_ukv'].reshape(DEPTH, KV_LORA, MLA_HEADS, MLA_NOPE + MLA_V)
    wk = jnp.pad(w_ukv[..., :MLA_NOPE], ((0, 0), (0, 0), (0, 0), (0, MLA_PAD - MLA_NOPE))).reshape(DEPTH, KV_LORA, W_MLA_P)
    wv = w_ukv[..., MLA_NOPE:].reshape(DEPTH, KV_LORA, MLA_W)
    w_ukv_r = jnp.concatenate([wk, wv], axis=2).astype(BF16)

    def row(v, reps=1):
        v = jnp.tile(v, (1, reps)) if reps > 1 else v
        return [v, jnp.zeros((DEPTH, D_MODEL - v.shape[1]), F32)] if v.shape[1] < D_MODEL else [v]

    pad_head = lambda g: jnp.pad(g, ((0, 0), (0, MLA_PAD - MLA_QK)))
    pieces = (row(p['g_mix']) + row(p['g_ffn']) + row(p['g_qa']) + row(p['g_kva'])
              + row(pad_head(p['g_mla_q']), MLA_HEADS) + row(pad_head(p['g_mla_k']), MLA_HEADS)
              + row(p['g_na_q'], NA_HEADS) + row(p['g_na_k'], NA_HEADS)
              + row(p['g_df_q'], 2 * DF_HEADS) + row(p['g_df_k'], 2 * DF_HEADS) + row(p['g_df_sub'], DF_HEADS)
              + row(p['df_lq1']) + row(p['df_lk1']) + row(p['df_lq2']) + row(p['df_lk2'])
              + [jnp.zeros((DEPTH, D_MODEL), F32)])
    vec = jnp.concatenate(pieces, axis=1).reshape(DEPTH, N_VEC, D_MODEL)
    f = p['na_rpb']
    zpad = jnp.zeros((DEPTH, NA_HEADS, NA_PAIR_TILES, 33), F32)
    g_rows = jnp.concatenate([f[:, :, :-1, NA_KW - 1:], zpad, f[:, :, 1:, :], zpad, f[:, :, :-1, :NA_KW - 1]], axis=-1)
    g_rows = g_rows.reshape(DEPTH, NA_HEADS * NA_PAIR_TILES, LANES)
    return dict(wa=wa, wb=wb, w_uq=w_uq_p, w_ukv=w_ukv_r, vec=vec, g_rows=g_rows)


def _modulation(c_all, w_mod, b_mod):
    tn = 2048
    return pl.pallas_call(
        _mod_kernel,
        grid=(DEPTH, 6 * D_MODEL // tn),
        in_specs=[pl.BlockSpec((N_MOD, D_MODEL), lambda l, j: (0, 0)),
                  pl.BlockSpec((None, D_MODEL, tn), lambda l, j: (l, 0, j)),
                  pl.BlockSpec((None, 1, tn), lambda l, j: (l, 0, j))],
        out_specs=pl.BlockSpec((None, N_MOD, tn), lambda l, j: (l, 0, j)),
        out_shape=jax.ShapeDtypeStruct((DEPTH, N_MOD, 6 * D_MODEL), F32),
        compiler_params=_params(2),
        name="modulation",
    )(c_all, w_mod, b_mod.reshape(DEPTH, 1, 6 * D_MODEL))


def _front_weight_specs(l):
    return [_layer_spec(l, (N_VEC, D_MODEL)), _layer_spec(l, (A_COLS, D_MODEL)), _layer_spec(l, (B_COLS, D_MODEL)),
            _layer_spec(l, (W_MLA_P, Q_LORA)), _layer_spec(l, (KV_LORA, W_MLA_P + MLA_W)),
            _const_spec((MXU_DIM, MXU_DIM)), _const_spec((MXU_DIM, MXU_DIM))]


def _front_weights(t, consts):
    return (t['vec'], t['wa'], t['wb'], t['w_uq'], t['w_ukv'], consts['bd64'], consts['bd32'])


def _context_mixer(l, lam_init, x, mods, t, consts, ffn_f32, prev_caches):
    n_tok = BATCH * SEQ
    n_alias = len(prev_caches)
    tok = lambda w: pl.BlockSpec((CTX_SEQS_PER_STEP * SEQ, w), lambda b: (b, 0))
    if n_alias == 0:
        assert l == 0
        lay = lambda *s: pl.BlockSpec((CTX_SEQS_PER_STEP, DEPTH) + s, lambda b: (b, 0) + (0,) * len(s))
    else:
        lay = lambda *s: pl.BlockSpec((CTX_SEQS_PER_STEP, None) + s, lambda b: (b, l) + (0,) * len(s))
    cache_shapes = [(SEQ, KV_LORA), (MLA_ROPE, SEQ), (NA_HEADS, NA_HD, SEQ), (NA_HEADS, NA_HD, SEQ),
                    (DF_HEADS, DF_HD, SEQ), (DF_HEADS, DF_HD, SEQ)]
    weights = _front_weights(t, consts)
    steps = BATCH // CTX_SEQS_PER_STEP
    ffn_chunks = [(w.shape[1] // steps, w.shape[2]) for w in ffn_f32]
    n_in = 2 + len(weights) + len(ffn_f32)
    mix, *outs = pl.pallas_call(
        functools.partial(_ctx_kernel, lam_init, n_alias),
        grid=(steps,),
        in_specs=[tok(D_MODEL), pl.BlockSpec((None, 1, 6 * D_MODEL), lambda b: (l * N_MOD, 0, 0))]
        + _front_weight_specs(l) + [pl.BlockSpec((None,) + c, lambda b: (l, b, 0)) for c in ffn_chunks]
        + [pl.BlockSpec(memory_space=pl.ANY)] * n_alias,
        out_specs=[tok(D_MODEL)] + [lay(*s) for s in cache_shapes] + [pl.BlockSpec(c, lambda b: (b, 0)) for c in ffn_chunks],
        out_shape=[jax.ShapeDtypeStruct((n_tok, D_MODEL), BF16)]
        + [jax.ShapeDtypeStruct((BATCH, DEPTH) + s, F32) for s in cache_shapes]
        + [jax.ShapeDtypeStruct(w.shape[1:], BF16) for w in ffn_f32],
        input_output_aliases={n_in + i: 1 + i for i in range(n_alias)},
        compiler_params=_params(1),
        name="context_mixer",
    )(x, mods, *weights, *ffn_f32, *prev_caches)
    return mix, outs[:N_CACHE], outs[N_CACHE:]


def _finish(l, xc, mixc, xl, mixl, mods, t, ffn_bf16):
    nc, nl = xc.shape[0] // TM_FINISH, xl.shape[0] // TM_FINISH
    blocks_per_seq = DEC_SEQ // TM_FINISH
    ctx_tok = pl.BlockSpec((TM_FINISH, D_MODEL), lambda i: (jnp.minimum(i, nc - 1), 0))
    lat_tok = pl.BlockSpec((TM_FINISH, D_MODEL), lambda i: (jnp.maximum(i - nc, 0), 0))
    mod_row = lambda i: jnp.where(i < nc, 0, 1 + jnp.maximum(i - nc, 0) // blocks_per_seq)
    return pl.pallas_call(
        functools.partial(_finish_kernel, nc),
        grid=(nc + nl,),
        in_specs=[ctx_tok, ctx_tok, lat_tok, lat_tok,
                  pl.BlockSpec((None, 1, 6 * D_MODEL), lambda i: (l * N_MOD + mod_row(i), 0, 0)),
                  _layer_spec(l, (N_VEC, D_MODEL))] + [_const_spec(w.shape) for w in ffn_bf16],
        out_specs=[ctx_tok, lat_tok],
        out_shape=[jax.ShapeDtypeStruct(xc.shape, F32), jax.ShapeDtypeStruct(xl.shape, F32)],
        compiler_params=_params(1),
        name="finish",
    )(xc, mixc, xl, mixl, mods, t['vec'], *ffn_bf16)


def _latent_front(l, x, mods, t, consts):
    n_tok = DEC_BATCH * DEC_SEQ
    tm = TM_LAT_FRONT
    blocks_per_seq = DEC_SEQ // tm
    tok = lambda w: pl.BlockSpec((tm, w), lambda i: (i, 0))
    wspecs = _front_weight_specs(l)
    weights = _front_weights(t, consts)
    return pl.pallas_call(
        _lat_front_kernel,
        grid=(n_tok // tm,),
        in_specs=[tok(D_MODEL),
                  pl.BlockSpec((None, 1, 6 * D_MODEL), lambda i: (l * N_MOD + 1 + i // blocks_per_seq, 0, 0)),
                  wspecs[0], pl.BlockSpec((6, tm, LANES), lambda i: (0, i % blocks_per_seq, 0))] + wspecs[1:],
        out_specs=[tok(QK_PACK), tok(QK_PACK), tok(V_PACK)],
        out_shape=[jax.ShapeDtypeStruct((n_tok, QK_PACK), BF16), jax.ShapeDtypeStruct((n_tok, QK_PACK), BF16),
                   jax.ShapeDtypeStruct((n_tok, V_PACK), BF16)],
        compiler_params=_params(1),
        name="latent_front",
    )(x, mods, weights[0], consts['rope'], *weights[1:])


def _context_kv(l, ckv, krope_t, t):
    lay = lambda r, w: pl.BlockSpec((None, None, r, w), lambda b: (b, l, 0, 0))
    out = lambda w: pl.BlockSpec((None, PAST_LEN, w), lambda b: (b, 0, 0))
    return pl.pallas_call(
        _ctx_kv_kernel,
        grid=(DEC_BATCH,),
        in_specs=[lay(PAST_LEN, KV_LORA), lay(MLA_ROPE, PAST_LEN),
                  _layer_spec(l, (N_VEC, D_MODEL)), _layer_spec(l, (KV_LORA, W_MLA_P + MLA_W))],
        out_specs=[out(W_MLA_P), out(MLA_W)],
        out_shape=[jax.ShapeDtypeStruct((DEC_BATCH, PAST_LEN, W_MLA_P), BF16),
                   jax.ShapeDtypeStruct((DEC_BATCH, PAST_LEN, MLA_W), BF16)],
        compiler_params=_params(1),
        name="context_kv",
    )(ckv, krope_t, t['vec'], t['w_ukv'])


def _latent_na(l, q, k, v, kct, vct, g_rows, vec):
    n_tok = DEC_BATCH * DEC_SEQ
    na_blk = P_NA // NA_W
    steps = GRID_ROWS // NA_ROWS_PER_STEP
    tq = NA_ROWS_PER_STEP * GRID_W
    ctx = pl.BlockSpec((None, None, NA_W, PAST_LEN), lambda b, j: (b, l, 0, 0))
    return pl.pallas_call(
        _na_lat_kernel,
        grid=(DEC_BATCH, steps),
        in_specs=[pl.BlockSpec((tq, NA_W), lambda b, j: (b * steps + j, na_blk)),
                  pl.BlockSpec((DEC_SEQ, NA_W), lambda b, j: (b, na_blk)),
                  pl.BlockSpec((DEC_SEQ, NA_W), lambda b, j: (b, PV_NA // NA_W)),
                  ctx, ctx, _layer_spec(l, (NA_HEADS * NA_PAIR_TILES, LANES)), _layer_spec(l, (N_VEC, D_MODEL))],
        out_specs=pl.BlockSpec((tq, NA_W), lambda b, j: (b * steps + j, 0)),
        out_shape=jax.ShapeDtypeStruct((n_tok, NA_W), BF16),
        scratch_shapes=[pltpu.VMEM((NA_HEADS * NA_PAIR_TILES, GRID_W, LANES), F32)],
        compiler_params=_params(2),
        name="latent_neighbourhood",
    )(q, k, v, kct, vct, g_rows, vec)


def _latent_attention(l, lam_init, q, k, v, kc, vc, kct, vct, o_na, t):
    n_tok = DEC_BATCH * DEC_SEQ
    nq = DEC_SEQ // TM
    ctx = pl.BlockSpec((None, None, DF_W, PAST_LEN), lambda b, j: (b, l, 0, 0))
    return pl.pallas_call(
        functools.partial(_lat_attn_kernel, lam_init),
        grid=(DEC_BATCH, nq),
        in_specs=[pl.BlockSpec((TM, QK_PACK), lambda b, j: (b * nq + j, 0)),
                  pl.BlockSpec((DEC_SEQ, QK_PACK), lambda b, j: (b, 0)),
                  pl.BlockSpec((DEC_SEQ, V_PACK), lambda b, j: (b, 0)),
                  pl.BlockSpec((None, PAST_LEN, W_MLA_P), lambda b, j: (b, 0, 0)),
                  pl.BlockSpec((None, PAST_LEN, MLA_W), lambda b, j: (b, 0, 0)),
                  ctx, ctx,
                  pl.BlockSpec((TM, NA_W), lambda b, j: (b * nq + j, 0)),
                  _layer_spec(l, (N_VEC, D_MODEL))],
        out_specs=pl.BlockSpec((TM, D_MODEL), lambda b, j: (b * nq + j, 0)),
        out_shape=jax.ShapeDtypeStruct((n_tok, D_MODEL), BF16),
        compiler_params=_params(2),
        name="latent_attention",
    )(q, k, v, kc, vc, kct, vct, o_na, t['vec'])


def kernel(x_prompt, x_sample, cache_mla_ckv, cache_mla_krope, cache_na_k, cache_na_v, cache_df_k, cache_df_v, c, c_ctx, w_mod, b_mod, g_mix, w_in, g_qa, w_uq, g_kva, w_ukv, g_mla_q, g_mla_k, g_na_q, g_na_k, na_rpb, g_df_q, g_df_k, df_lq1, df_lk1, df_lq2, df_lk2, g_df_sub, w_out, g_ffn, w_gate, w_up, w_down):
    p = dict(g_mix=g_mix, w_in=w_in, g_qa=g_qa, w_uq=w_uq, g_kva=g_kva, w_ukv=w_ukv, g_mla_q=g_mla_q, g_mla_k=g_mla_k,
             g_na_q=g_na_q, g_na_k=g_na_k, na_rpb=na_rpb, g_df_q=g_df_q, g_df_k=g_df_k, df_lq1=df_lq1, df_lk1=df_lk1,
             df_lq2=df_lq2, df_lk2=df_lk2, g_df_sub=g_df_sub, w_out=w_out, g_ffn=g_ffn, w_gate=w_gate, w_up=w_up,
             w_down=w_down)
    consts = dict(bd64=_block_diag(NA_HD), bd32=_block_diag(DF_QK), rope=_rope_tables())
    t = _tables(p)

    c_all = jnp.concatenate([c_ctx[None, :], c, jnp.zeros((N_MOD - 1 - DEC_BATCH, D_MODEL), F32)], axis=0)
    mods = _modulation(c_all, w_mod, b_mod).reshape(DEPTH * N_MOD, 1, 6 * D_MODEL)

    krope_t = jnp.swapaxes(cache_mla_krope, -1, -2)
    na_kt, na_vt, df_kt, df_vt = (_feature_major(a) for a in (cache_na_k, cache_na_v, cache_df_k, cache_df_v))

    xp = x_prompt.reshape(BATCH * SEQ, D_MODEL)
    xs = x_sample.reshape(DEC_BATCH * DEC_SEQ, D_MODEL)
    new_caches = ()
    ffn_f32 = (w_out, w_gate, w_up, w_down)
    for l in range(DEPTH):
        lam_init = 0.8 - 0.6 * math.exp(-0.3 * l)
        mix, new_caches, ffn_bf16 = _context_mixer(l, lam_init, xp, mods, t, consts, ffn_f32, new_caches)
        q, k, v = _latent_front(l, xs, mods, t, consts)
        kc, vc = _context_kv(l, cache_mla_ckv, krope_t, t)
        o_na = _latent_na(l, q, k, v, na_kt, na_vt, t['g_rows'], t['vec'])
        mix_s = _latent_attention(l, lam_init, q, k, v, kc, vc, df_kt, df_vt, o_na, t)
        xp, xs = _finish(l, xp, mix, xs, mix_s, mods, t, ffn_bf16)
    ckv_new, *narrow = new_caches
    return (xp.reshape(BATCH, SEQ, D_MODEL), xs.reshape(DEC_BATCH, DEC_SEQ, D_MODEL), ckv_new,
            *(jnp.swapaxes(a, -1, -2) for a in narrow))
```

```python
import functools
import math

import numpy as np
import jax
import jax.numpy as jnp
from jax import lax
from jax.experimental import pallas as pl
from jax.experimental.pallas import tpu as pltpu

F32 = jnp.float32
BF16 = jnp.bfloat16

D_MODEL = 1024
BATCH = 32
SEQ = 256
DEPTH = 2
DEC_BATCH = 2
DEC_SEQ = 1024
PAST_LEN = 256
GRID_W = 64
GRID_ROWS = DEC_SEQ // GRID_W
MLA_HEADS = 6
MLA_NOPE = 64
MLA_ROPE = 32
MLA_QK = MLA_NOPE + MLA_ROPE
MLA_V = 64
MLA_PAD = 128
Q_LORA = 256
KV_LORA = 128
NA_HEADS = 6
NA_HD = 64
NA_KR = 8
NA_KW = 16
DF_HEADS = 4
DF_HD = 64
DF_QK = 32
MLA_W = MLA_HEADS * MLA_V
NA_W = NA_HEADS * NA_HD
DF_W = DF_HEADS * DF_HD
D_FF = -(-8 * D_MODEL // (3 * 256)) * 256
ROPE_BASE = 10000.0
EPS = 1e-6
NEG = -1e30
LOG2E = math.log2(math.e)
MAX_FREE_SOFTMAX_BOUND = 60.0

LANES = 128
MXU_DIM = 256

A_CQ = 0
A_CKV = A_CQ + Q_LORA
A_KR = A_CKV + KV_LORA
A_COLS = A_KR + LANES
B_SRC = Q_LORA + KV_LORA + MLA_ROPE
B_NAQ = 0
B_NAK = B_NAQ + NA_W
B_NAV = B_NAK + NA_W
B_DFQ = B_NAV + NA_W
B_DFK = B_DFQ + DF_W
B_DFV = B_DFK + DF_W
B_COLS = B_DFV + DF_W

W_MLA_P = MLA_HEADS * MLA_PAD
QK_PACK = W_MLA_P + NA_W + DF_W
V_PACK = MLA_W + NA_W + DF_W
P_MLA = 0
P_NA = W_MLA_P
P_DF = P_NA + NA_W
PV_MLA = 0
PV_NA = MLA_W
PV_DF = MLA_W + NA_W

(V_GMIX, V_GFFN, V_GQA, V_GKVA, V_GMQ, V_GMK, V_GNQ, V_GNK, V_GDQ, V_GDK, V_GDS,
 V_LQ1, V_LK1, V_LQ2, V_LK2) = range(15)
N_VEC = 16

N_MOD = 8
TM = 512
TM_LAT_FRONT = 256
TM_FINISH = 512
CTX_SEQS_PER_STEP = 2
NA_ROWS_PER_STEP = 4
VMEM_LIMIT = 56 * 1024 * 1024

NA_PAIR_TILES = 2 * NA_KR - 2
N_CACHE = 6
N_FFN_W = 4


def _dot(a, b):
    return jnp.dot(a, b, preferred_element_type=F32)


def _dot_nt(a, b):
    return lax.dot_general(a, b, (((1,), (1,)), ((), ())), preferred_element_type=F32)


def _lane_iota(shape):
    return lax.broadcasted_iota(jnp.int32, shape, len(shape) - 1)


def _rms_rows(x, g):
    ms = jnp.mean(x * x, axis=-1, keepdims=True)
    return x * lax.rsqrt(ms + EPS) * g


def _tile_rms(x, g, n_real):
    outs = []
    for c0 in range(0, x.shape[1], LANES):
        xt = x[:, c0:c0 + LANES]
        ms = jnp.sum(xt * xt, axis=-1, keepdims=True) * (1.0 / n_real)
        outs.append(xt * lax.rsqrt(ms + EPS) * g[:, c0:c0 + LANES])
    return jnp.concatenate(outs, axis=1)


def _seg_rms(x, bd_ref, g, group):
    width = x.shape[1]
    sq = (x * x).astype(BF16)
    parts = []
    for c0 in range(0, width, MXU_DIM):
        w = min(MXU_DIM, width - c0)
        parts.append(_dot(sq[:, c0:c0 + w], bd_ref[0:w, 0:w]))
    ss = parts[0] if len(parts) == 1 else jnp.concatenate(parts, axis=1)
    return x * lax.rsqrt(ss * (1.0 / group) + EPS) * g


def _rope_tiles(x, cos, sa, sb):
    outs = []
    for t in range(x.shape[1] // LANES):
        xt = x[:, t * LANES:(t + 1) * LANES]
        up = pltpu.roll(xt, LANES - MLA_ROPE // 4, 1)
        dn = pltpu.roll(xt, MLA_ROPE // 4, 1)
        outs.append(xt * cos + up * sa + dn * sb)
    return outs[0] if len(outs) == 1 else jnp.concatenate(outs, axis=1)


def _diff_lambda(vec_ref, lam_init):
    a = jnp.sum(vec_ref[V_LQ1:V_LQ1 + 1, 0:DF_QK] * vec_ref[V_LK1:V_LK1 + 1, 0:DF_QK], axis=-1, keepdims=True)
    b = jnp.sum(vec_ref[V_LQ2:V_LQ2 + 1, 0:DF_QK] * vec_ref[V_LK2:V_LK2 + 1, 0:DF_QK], axis=-1, keepdims=True)
    return jnp.exp(a) - jnp.exp(b) + lam_init


def _mixer_front(x, mod, vec_ref, wa_ref, wb_ref, w_uq_ref, bd64_ref, bd32_ref):
    sh = mod[:, 0:D_MODEL]
    sc = mod[:, D_MODEL:2 * D_MODEL]
    h = (_rms_rows(x, vec_ref[V_GMIX:V_GMIX + 1, :]) * (1.0 + sc) + sh).astype(BF16)
    za = _dot_nt(h, wa_ref[...])
    zb = _dot_nt(h, wb_ref[...])
    cqn = _rms_rows(za[:, A_CQ:A_CQ + Q_LORA], vec_ref[V_GQA:V_GQA + 1, 0:Q_LORA])
    q_raw = _dot_nt(cqn.astype(BF16), w_uq_ref[...])
    q_mla = _tile_rms(q_raw, vec_ref[V_GMQ:V_GMQ + 1, 0:W_MLA_P] * (MLA_QK ** -0.5 * LOG2E), MLA_QK)
    ckv_n = _rms_rows(za[:, A_CKV:A_CKV + KV_LORA], vec_ref[V_GKVA:V_GKVA + 1, 0:KV_LORA])
    kr_tile = za[:, A_KR:A_KR + LANES]
    q_na = _seg_rms(zb[:, B_NAQ:B_NAQ + NA_W], bd64_ref, vec_ref[V_GNQ:V_GNQ + 1, 0:NA_W] * (NA_HD ** -0.5 * LOG2E), NA_HD)
    k_na = _seg_rms(zb[:, B_NAK:B_NAK + NA_W], bd64_ref, vec_ref[V_GNK:V_GNK + 1, 0:NA_W], NA_HD)
    v_na = zb[:, B_NAV:B_NAV + NA_W]
    q_df = _seg_rms(zb[:, B_DFQ:B_DFQ + DF_W], bd32_ref, vec_ref[V_GDQ:V_GDQ + 1, 0:DF_W] * (DF_QK ** -0.5 * LOG2E), DF_QK)
    k_df = _seg_rms(zb[:, B_DFK:B_DFK + DF_W], bd32_ref, vec_ref[V_GDK:V_GDK + 1, 0:DF_W], DF_QK)
    v_df = zb[:, B_DFV:B_DFV + DF_W]
    return q_mla, ckv_n, kr_tile, q_na, k_na, v_na, q_df, k_df, v_df


def _mla_kv(ckv_n, kr_tile, vec_ref, w_ukv_ref):
    kv = _dot(ckv_n.astype(BF16), w_ukv_ref[...])
    lane = _lane_iota((1, LANES))
    kr = jnp.where((lane >= MLA_NOPE) & (lane < MLA_QK), kr_tile, 0.0)
    k_pre = kv[:, 0:W_MLA_P] + jnp.concatenate([kr] * MLA_HEADS, axis=1)
    k = _tile_rms(k_pre, vec_ref[V_GMK:V_GMK + 1, 0:W_MLA_P], MLA_QK)
    return k, kv[:, W_MLA_P:W_MLA_P + MLA_W]


def _softmax_parts(s, bound=None):
    m = jnp.max(s, axis=-1, keepdims=True) if bound is None else bound
    p = jnp.exp2(s - m)
    return p, 1.0 / jnp.sum(p, axis=-1, keepdims=True)


def _max_sq_norm(kt, group):
    sq = kt * kt
    best = None
    for r0 in range(0, kt.shape[0], group):
        n2 = jnp.sum(sq[r0:r0 + group], axis=0, keepdims=True)
        best = n2 if best is None else jnp.maximum(best, n2)
    return jnp.max(best, axis=-1, keepdims=True)


def _score_bounds(vec_ref, bias_max=None, na_key_sq=None, df_key_sq=None):
    gmax = lambda row, w: jnp.max(jnp.abs(vec_ref[row:row + 1, 0:w]), axis=-1, keepdims=True)
    slack = LOG2E * (1.0 + 2.0 ** -6)

    def key_norm(row, w, d, measured_sq):
        k = gmax(row, w) * d ** 0.5
        return k if measured_sq is None else jnp.maximum(k, jnp.sqrt(measured_sq))

    b_mla = gmax(V_GMQ, W_MLA_P) * key_norm(V_GMK, W_MLA_P, MLA_QK, None) * slack
    b_na = gmax(V_GNQ, NA_W) * key_norm(V_GNK, NA_W, NA_HD, na_key_sq) * slack
    if bias_max is not None:
        b_na = b_na + bias_max * LOG2E
    b_df = gmax(V_GDQ, DF_W) * key_norm(V_GDK, DF_W, DF_QK, df_key_sq) * slack
    worst = jnp.maximum(b_mla, jnp.maximum(b_na, b_df))
    return (b_mla, b_na, b_df), worst[0, 0] <= MAX_FREE_SOFTMAX_BOUND


def _with_score_bounds(vec_ref, attend, **measured):
    bounds, ok = _score_bounds(vec_ref, **measured)
    pl.when(ok)(lambda: attend(bounds))
    pl.when(jnp.logical_not(ok))(lambda: attend((None, None, None)))


def _scores(q, k_segs):
    parts = [_dot(q, k) if feature_major else _dot_nt(q, k) for k, feature_major in k_segs]
    return parts[0] if len(parts) == 1 else jnp.concatenate(parts, axis=1)


def _pv(p, v_segs):
    out = None
    c0 = 0
    for v, feature_major in v_segs:
        n = v.shape[1] if feature_major else v.shape[0]
        o = _dot_nt(p[:, c0:c0 + n], v) if feature_major else _dot(p[:, c0:c0 + n], v)
        out = o if out is None else out + o
        c0 += n
    return out


def _seg_tile(seg, t):
    a, feature_major = seg
    return (a[t * LANES:(t + 1) * LANES, :] if feature_major else a[:, t * LANES:(t + 1) * LANES]), feature_major


def _lane_groups(qt, width):
    lane = _lane_iota((1, LANES))
    zero = jnp.zeros_like(qt)
    return jnp.concatenate(
        [jnp.where((lane >= g * width) & (lane < (g + 1) * width), qt, zero) for g in range(LANES // width)], axis=0)


def _pair_select(o2):
    tq = o2.shape[0] // 2
    return jnp.where(_lane_iota((1, LANES)) < NA_HD, o2[0:tq], o2[tq:2 * tq])


def _mla_attend(q, k_segs, v_segs, bound=None):
    outs = []
    for t in range(MLA_HEADS // 2):
        vt = [_seg_tile(v, t) for v in v_segs]
        halves = []
        for h in (2 * t, 2 * t + 1):
            p, il = _softmax_parts(_scores(q[:, h * MLA_PAD:(h + 1) * MLA_PAD], [_seg_tile(k, h) for k in k_segs]),
                                   bound)
            halves.append(_pv(p.astype(BF16), vt) * il)
        outs.append(jnp.where(_lane_iota((1, LANES)) < MLA_V, halves[0], halves[1]))
    return jnp.concatenate(outs, axis=1)


def _na_tile_attend(qt, k_segs, v_segs, bias=None, bound=None):
    s = _scores(_lane_groups(qt, NA_HD), k_segs)
    if bias is not None:
        nb = bias.shape[1]
        s = jnp.concatenate([s[:, 0:nb] + bias, s[:, nb:]], axis=1)
    p, il = _softmax_parts(s, bound)
    return _pair_select(_pv(p.astype(BF16), v_segs) * il)


def _na_attend_full(q, k_segs, v_segs, bound=None):
    return jnp.concatenate(
        [_na_tile_attend(q[:, t * LANES:(t + 1) * LANES], [_seg_tile(k, t) for k in k_segs],
                         [_seg_tile(v, t) for v in v_segs], bound=bound) for t in range(NA_HEADS // 2)], axis=1)


def _df_attend(q, k_segs, v_segs, lam, g_sub, out_scale, bound=None):
    outs = []
    lane = _lane_iota((1, LANES))
    tq = q.shape[0]
    for t in range(DF_HEADS // 2):
        kt = [_seg_tile(k, t) for k in k_segs]
        vt = [_seg_tile(v, t) for v in v_segs]
        p, il = _softmax_parts(_scores(_lane_groups(q[:, t * LANES:(t + 1) * LANES], DF_QK), kt), bound)
        pn = []
        for hh in range(2):
            r1, r2 = 2 * hh * tq, (2 * hh + 1) * tq
            pn.append((p[r1:r1 + tq] * il[r1:r1 + tq] - p[r2:r2 + tq] * (lam * il[r2:r2 + tq])).astype(BF16))
        o = _pair_select(_pv(jnp.concatenate(pn, axis=0), vt))
        o2 = o * o
        ms_e = jnp.sum(jnp.where(lane < DF_HD, o2, 0.0), axis=-1, keepdims=True)
        ms_o = jnp.sum(jnp.where(lane >= DF_HD, o2, 0.0), axis=-1, keepdims=True)
        r = lax.rsqrt(jnp.where(lane < DF_HD, ms_e, ms_o) * (1.0 / DF_HD) + EPS)
        outs.append(o * r * (g_sub[:, t * LANES:(t + 1) * LANES] * out_scale))
    return jnp.concatenate(outs, axis=1)


def _mod_kernel(c_ref, w_ref, b_ref, o_ref):
    c = c_ref[...]
    s = c * jax.nn.sigmoid(c)
    o_ref[...] = _dot(s.astype(BF16), w_ref[...].astype(BF16)) + b_ref[...]


def _ctx_kernel(lam_init, n_alias, x_ref, mod_ref, vec_ref, wa_ref, wb_ref, w_uq_ref, w_ukv_ref, bd64_ref, bd32_ref,
                *rest):
    ffn_f32, rest = rest[:N_FFN_W], rest[N_FFN_W + n_alias:]
    mix_ref, *cache_refs = rest[:1 + N_CACHE]
    for src, dst in zip(ffn_f32, rest[1 + N_CACHE:]):
        dst[...] = src[...].astype(BF16)
    if n_alias == 0:
        for ref in cache_refs:
            ref[:, 1:] = jnp.zeros((ref.shape[0], ref.shape[1] - 1) + ref.shape[2:], F32)
        cache_refs = [ref.at[:, 0] for ref in cache_refs]
    ckv_ref, kr_ref, nak_ref, nav_ref, dfk_ref, dfv_ref = cache_refs
    lam = _diff_lambda(vec_ref, lam_init)
    mod = mod_ref[...]
    bf = lambda a: a.astype(BF16)
    seg = lambda a: [(a.astype(BF16), False)]
    for s in range(CTX_SEQS_PER_STEP):
        rows = slice(s * SEQ, (s + 1) * SEQ)
        q_mla, ckv_n, kr_tile, q_na, k_na, v_na, q_df, k_df, v_df = _mixer_front(
            x_ref[rows, :], mod, vec_ref, wa_ref, wb_ref, w_uq_ref, bd64_ref, bd32_ref)
        k_mla, v_mla = _mla_kv(ckv_n, kr_tile, vec_ref, w_ukv_ref)
        ckv_ref[s] = ckv_n
        kr_ref[s] = kr_tile.T[0:MLA_ROPE]
        for ref, a in ((nak_ref, k_na), (nav_ref, v_na), (dfk_ref, k_df), (dfv_ref, v_df)):
            at = a.T
            for h in range(ref.shape[1]):
                ref[s, h] = at[h * NA_HD:(h + 1) * NA_HD]
        o_mla = _mla_attend(bf(q_mla), seg(k_mla), seg(v_mla))
        o_na = _na_attend_full(bf(q_na), seg(k_na), seg(v_na))
        o_df = _df_attend(bf(q_df), seg(k_df), seg(v_df), lam, vec_ref[V_GDS:V_GDS + 1, 0:DF_W], 1.0 - lam_init)
        mix_ref[rows, :] = jnp.concatenate([o_mla, o_na, o_df], axis=1).astype(BF16)


def _finish_kernel(n_ctx_blocks, xc_ref, mixc_ref, xl_ref, mixl_ref, mod_ref, vec_ref, w_out_ref, w_gate_ref, w_up_ref,
                   w_down_ref, yc_ref, yl_ref):
    def block(x_ref, mix_ref, y_ref):
        mod = mod_ref[...]
        gate_m = mod[:, 2 * D_MODEL:3 * D_MODEL]
        sh = mod[:, 3 * D_MODEL:4 * D_MODEL]
        sc = mod[:, 4 * D_MODEL:5 * D_MODEL]
        gate_f = mod[:, 5 * D_MODEL:6 * D_MODEL]
        x1 = x_ref[...] + gate_m * _dot(mix_ref[...], w_out_ref[...])
        h = (_rms_rows(x1, vec_ref[V_GFFN:V_GFFN + 1, :]) * (1.0 + sc) + sh).astype(BF16)
        g = _dot(h, w_gate_ref[...])
        u = _dot(h, w_up_ref[...])
        a = (g * jax.nn.sigmoid(g) * u).astype(BF16)
        y_ref[...] = x1 + gate_f * _dot(a, w_down_ref[...])

    is_ctx = pl.program_id(0) < n_ctx_blocks
    pl.when(is_ctx)(lambda: block(xc_ref, mixc_ref, yc_ref))
    pl.when(jnp.logical_not(is_ctx))(lambda: block(xl_ref, mixl_ref, yl_ref))


def _lat_front_kernel(x_ref, mod_ref, vec_ref, rope_ref, wa_ref, wb_ref, w_uq_ref, w_ukv_ref, bd64_ref, bd32_ref,
                      q_ref, k_ref, v_ref):
    q_mla, ckv_n, kr_tile, q_na, k_na, v_na, q_df, k_df, v_df = _mixer_front(
        x_ref[...], mod_ref[...], vec_ref, wa_ref, wb_ref, w_uq_ref, bd64_ref, bd32_ref)
    k_mla, v_mla = _mla_kv(ckv_n, kr_tile, vec_ref, w_ukv_ref)
    cm, sam, sbm = rope_ref[0], rope_ref[1], rope_ref[2]
    cd, sad, sbd = rope_ref[3], rope_ref[4], rope_ref[5]
    q_ref[...] = jnp.concatenate(
        [_rope_tiles(q_mla, cm, sam, sbm), q_na, _rope_tiles(q_df, cd, sad, sbd)], axis=1).astype(BF16)
    k_ref[...] = jnp.concatenate(
        [_rope_tiles(k_mla, cm, sam, sbm), k_na, _rope_tiles(k_df, cd, sad, sbd)], axis=1).astype(BF16)
    v_ref[...] = jnp.concatenate([v_mla, v_na, v_df], axis=1).astype(BF16)


def _ctx_kv_kernel(ckv_ref, krt_ref, vec_ref, w_ukv_ref, k_ref, v_ref):
    krt = jnp.concatenate([jnp.zeros((MLA_NOPE, PAST_LEN), F32), krt_ref[...],
                           jnp.zeros((LANES - MLA_QK, PAST_LEN), F32)], axis=0)
    k_mla, v_mla = _mla_kv(ckv_ref[...], krt.T, vec_ref, w_ukv_ref)
    k_ref[...] = k_mla.astype(BF16)
    v_ref[...] = v_mla.astype(BF16)


def _na_lat_kernel(q_ref, kl_ref, vl_ref, kct_ref, vct_ref, g_ref, vec_ref, o_ref, bias_ref):
    b = pl.program_id(0)
    j = pl.program_id(1)

    @pl.when((b == 0) & (j == 0))
    def _build_bias():
        c = lax.broadcasted_iota(jnp.int32, (GRID_W, LANES), 0)
        kc = _lane_iota((GRID_W, LANES)) % GRID_W
        start = jnp.clip(c - NA_KW // 2, 0, GRID_W - NA_KW)
        in_win = (kc >= start) & (kc < start + NA_KW)

        def body(i, carry):
            row = jnp.broadcast_to(g_ref[pl.ds(i, 1), :], (GRID_W, LANES))
            toep = pltpu.roll(row, 0, 1, stride=1, stride_axis=0)
            bias_ref[i] = jnp.where(in_win, toep * LOG2E, NEG)
            return carry

        lax.fori_loop(0, NA_HEADS * NA_PAIR_TILES, body, 0)

    n_win = NA_KR * GRID_W
    kct_f32 = kct_ref[...]
    kct = kct_f32.astype(BF16)
    vct = vct_ref[...].astype(BF16)

    def win_start(r):
        return jnp.clip(r - NA_KR // 2, 0, GRID_ROWS - NA_KR)

    def attend(grid_rows, rs, bound):
        a0, n = grid_rows[0], len(grid_rows)
        row0 = pl.multiple_of(rs * GRID_W, GRID_W)
        outs = []
        for t in range(NA_HEADS // 2):
            lanes = slice(t * LANES, (t + 1) * LANES)
            bias = jnp.concatenate(
                [jnp.concatenate([bias_ref[h * NA_PAIR_TILES + (rs - (j * NA_ROWS_PER_STEP + a) + NA_KR - 1) + 2 * m]
                                  for m in range(NA_KR // 2)], axis=1)
                 for h in (2 * t, 2 * t + 1) for a in grid_rows], axis=0)
            outs.append(_na_tile_attend(
                q_ref[a0 * GRID_W:(a0 + n) * GRID_W, lanes],
                [(kl_ref[pl.ds(row0, n_win), lanes], False), (kct[lanes, :], True)],
                [(vl_ref[pl.ds(row0, n_win), lanes], False), (vct[lanes, :], True)], bias, bound))
        return jnp.concatenate(outs, axis=1).astype(BF16)

    first_r, last_r = j * NA_ROWS_PER_STEP, (j + 1) * NA_ROWS_PER_STEP - 1
    shared = win_start(first_r) == win_start(last_r)

    def attend_step(bounds):
        bound = bounds[1]

        @pl.when(shared)
        def _shared_window():
            o_ref[...] = attend(tuple(range(NA_ROWS_PER_STEP)), win_start(first_r), bound)

        @pl.when(jnp.logical_not(shared))
        def _per_row_windows():
            for a in range(NA_ROWS_PER_STEP):
                o_ref[a * GRID_W:(a + 1) * GRID_W, :] = attend((a,), win_start(first_r + a), bound)

    bias_max = jnp.max(jnp.max(jnp.abs(g_ref[...]), axis=-1, keepdims=True), axis=0, keepdims=True)
    _with_score_bounds(vec_ref, attend_step, bias_max=bias_max, na_key_sq=_max_sq_norm(kct_f32, NA_HD))


def _lat_attn_kernel(lam_init, q_ref, kl_ref, vl_ref, kc_ref, vc_ref, kct_ref, vct_ref, ona_ref, vec_ref, mix_ref):
    lam = _diff_lambda(vec_ref, lam_init)
    kct = kct_ref[...]

    def attend(bounds):
        b_mla, _, b_df = bounds
        o_mla = _mla_attend(q_ref[:, P_MLA:P_MLA + W_MLA_P],
                            [(kc_ref[...], False), (kl_ref[:, P_MLA:P_MLA + W_MLA_P], False)],
                            [(vc_ref[...], False), (vl_ref[:, PV_MLA:PV_MLA + MLA_W], False)], b_mla)
        o_df = _df_attend(q_ref[:, P_DF:P_DF + DF_W],
                          [(kct.astype(BF16), True), (kl_ref[:, P_DF:P_DF + DF_W], False)],
                          [(vct_ref[...].astype(BF16), True), (vl_ref[:, PV_DF:PV_DF + DF_W], False)],
                          lam, vec_ref[V_GDS:V_GDS + 1, 0:DF_W], 1.0 - lam_init, b_df)
        mix_ref[...] = jnp.concatenate([o_mla.astype(BF16), ona_ref[...], o_df.astype(BF16)], axis=1)

    _with_score_bounds(vec_ref, attend, df_key_sq=_max_sq_norm(kct, DF_QK))


def _const_spec(shape):
    nd = len(shape)
    return pl.BlockSpec(shape, lambda *_: (0,) * nd, pipeline_mode=pl.Buffered(1))


def _layer_spec(l, shape):
    nd = len(shape)
    return pl.BlockSpec((None,) + tuple(shape), lambda *_: (l,) + (0,) * nd, pipeline_mode=pl.Buffered(1))


def _params(n_axes):
    return pltpu.CompilerParams(dimension_semantics=("arbitrary",) * n_axes, vmem_limit_bytes=VMEM_LIMIT)


def _block_diag(group):
    i = np.arange(MXU_DIM) // group
    return jnp.asarray((i[:, None] == i[None, :]).astype(np.float32), dtype=BF16)


def _rope_tables():
    t = np.arange(DEC_SEQ)
    row = (t // GRID_W).astype(np.float64)
    col = (t % GRID_W).astype(np.float64)
    n = MLA_ROPE // 4
    inv = 1.0 / (ROPE_BASE ** (np.arange(n, dtype=np.float64) * 2.0 / (MLA_ROPE // 2)))
    ar = row[:, None] * inv
    ac = col[:, None] * inv
    ang = np.concatenate([ar, ar, ac, ac], axis=-1)
    cos32, sin32 = np.cos(ang), np.sin(ang)
    first = (np.arange(MLA_ROPE) % (2 * n)) < n
    sa32 = np.where(first, -sin32, 0.0)
    sb32 = np.where(first, 0.0, sin32)

    def mla_tile(v32, fill):
        out = np.full((DEC_SEQ, LANES), fill)
        out[:, MLA_NOPE:MLA_QK] = v32
        return out

    tabs = [mla_tile(cos32, 1.0), mla_tile(sa32, 0.0), mla_tile(sb32, 0.0),
            np.tile(cos32, (1, LANES // DF_QK)), np.tile(sa32, (1, LANES // DF_QK)), np.tile(sb32, (1, LANES // DF_QK))]
    return jnp.asarray(np.stack(tabs).astype(np.float32))


def _feature_major(a):
    a = jnp.swapaxes(a, -1, -2)
    return a.reshape(a.shape[:-3] + (a.shape[-3] * a.shape[-2], a.shape[-1]))


def _tables(p):
    w_in_t = jnp.swapaxes(p['w_in'], 1, 2)
    kr = w_in_t[:, Q_LORA + KV_LORA:B_SRC]
    z32 = jnp.zeros_like(kr)
    wa = jnp.concatenate([w_in_t[:, :Q_LORA + KV_LORA], kr, z32, kr, z32], axis=1).astype(BF16)
    wb = w_in_t[:, B_SRC:].astype(BF16)
    w_uq_t = jnp.swapaxes(p['w_uq'], 1, 2).reshape(DEPTH, MLA_HEADS, MLA_QK, Q_LORA)
    w_uq_p = jnp.pad(w_uq_t, ((0, 0), (0, 0), (0, MLA_PAD - MLA_QK), (0, 0))).reshape(DEPTH, W_MLA_P, Q_LORA).astype(BF16)
    w_ukv = p['w_ukv'].reshape(DEPTH, KV_LORA, MLA_HEADS, MLA_NOPE + MLA_V)
    wk = jnp.pad(w_ukv[..., :MLA_NOPE], ((0, 0), (0, 0), (0, 0), (0, MLA_PAD - MLA_NOPE))).reshape(DEPTH, KV_LORA, W_MLA_P)
    wv = w_ukv[..., MLA_NOPE:].reshape(DEPTH, KV_LORA, MLA_W)
    w_ukv_r = jnp.concatenate([wk, wv], axis=2).astype(BF16)

    def row(v, reps=1):
        v = jnp.tile(v, (1, reps)) if reps > 1 else v
        return [v, jnp.zeros((DEPTH, D_MODEL - v.shape[1]), F32)] if v.shape[1] < D_MODEL else [v]

    pad_head = lambda g: jnp.pad(g, ((0, 0), (0, MLA_PAD - MLA_QK)))
    pieces = (row(p['g_mix']) + row(p['g_ffn']) + row(p['g_qa']) + row(p['g_kva'])
              + row(pad_head(p['g_mla_q']), MLA_HEADS) + row(pad_head(p['g_mla_k']), MLA_HEADS)
              + row(p['g_na_q'], NA_HEADS) + row(p['g_na_k'], NA_HEADS)
              + row(p['g_df_q'], 2 * DF_HEADS) + row(p['g_df_k'], 2 * DF_HEADS) + row(p['g_df_sub'], DF_HEADS)
              + row(p['df_lq1']) + row(p['df_lk1']) + row(p['df_lq2']) + row(p['df_lk2'])
              + [jnp.zeros((DEPTH, D_MODEL), F32)])
    vec = jnp.concatenate(pieces, axis=1).reshape(DEPTH, N_VEC, D_MODEL)
    f = p['na_rpb']
    zpad = jnp.zeros((DEPTH, NA_HEADS, NA_PAIR_TILES, 33), F32)
    g_rows = jnp.concatenate([f[:, :, :-1, NA_KW - 1:], zpad, f[:, :, 1:, :], zpad, f[:, :, :-1, :NA_KW - 1]], axis=-1)
    g_rows = g_rows.reshape(DEPTH, NA_HEADS * NA_PAIR_TILES, LANES)
    return dict(wa=wa, wb=wb, w_uq=w_uq_p, w_ukv=w_ukv_r, vec=vec, g_rows=g_rows)


def _modulation(c_all, w_mod, b_mod):
    tn = 2048
    return pl.pallas_call(
        _mod_kernel,
        grid=(DEPTH, 6 * D_MODEL // tn),
        in_specs=[pl.BlockSpec((N_MOD, D_MODEL), lambda l, j: (0, 0)),
                  pl.BlockSpec((None, D_MODEL, tn), lambda l, j: (l, 0, j)),
                  pl.BlockSpec((None, 1, tn), lambda l, j: (l, 0, j))],
        out_specs=pl.BlockSpec((None, N_MOD, tn), lambda l, j: (l, 0, j)),
        out_shape=jax.ShapeDtypeStruct((DEPTH, N_MOD, 6 * D_MODEL), F32),
        compiler_params=_params(2),
        name="modulation",
    )(c_all, w_mod, b_mod.reshape(DEPTH, 1, 6 * D_MODEL))


def _front_weight_specs(l):
    return [_layer_spec(l, (N_VEC, D_MODEL)), _layer_spec(l, (A_COLS, D_MODEL)), _layer_spec(l, (B_COLS, D_MODEL)),
            _layer_spec(l, (W_MLA_P, Q_LORA)), _layer_spec(l, (KV_LORA, W_MLA_P + MLA_W)),
            _const_spec((MXU_DIM, MXU_DIM)), _const_spec((MXU_DIM, MXU_DIM))]


def _front_weights(t, consts):
    return (t['vec'], t['wa'], t['wb'], t['w_uq'], t['w_ukv'], consts['bd64'], consts['bd32'])


def _context_mixer(l, lam_init, x, mods, t, consts, ffn_f32, prev_caches):
    n_tok = BATCH * SEQ
    n_alias = len(prev_caches)
    tok = lambda w: pl.BlockSpec((CTX_SEQS_PER_STEP * SEQ, w), lambda b: (b, 0))
    if n_alias == 0:
        assert l == 0
        lay = lambda *s: pl.BlockSpec((CTX_SEQS_PER_STEP, DEPTH) + s, lambda b: (b, 0) + (0,) * len(s))
    else:
        lay = lambda *s: pl.BlockSpec((CTX_SEQS_PER_STEP, None) + s, lambda b: (b, l) + (0,) * len(s))
    cache_shapes = [(SEQ, KV_LORA), (MLA_ROPE, SEQ), (NA_HEADS, NA_HD, SEQ), (NA_HEADS, NA_HD, SEQ),
                    (DF_HEADS, DF_HD, SEQ), (DF_HEADS, DF_HD, SEQ)]
    weights = _front_weights(t, consts)
    steps = BATCH // CTX_SEQS_PER_STEP
    ffn_chunks = [(w.shape[1] // steps, w.shape[2]) for w in ffn_f32]
    n_in = 2 + len(weights) + len(ffn_f32)
    mix, *outs = pl.pallas_call(
        functools.partial(_ctx_kernel, lam_init, n_alias),
        grid=(steps,),
        in_specs=[tok(D_MODEL), pl.BlockSpec((None, 1, 6 * D_MODEL), lambda b: (l * N_MOD, 0, 0))]
        + _front_weight_specs(l) + [pl.BlockSpec((None,) + c, lambda b: (l, b, 0)) for c in ffn_chunks]
        + [pl.BlockSpec(memory_space=pl.ANY)] * n_alias,
        out_specs=[tok(D_MODEL)] + [lay(*s) for s in cache_shapes] + [pl.BlockSpec(c, lambda b: (b, 0)) for c in ffn_chunks],
        out_shape=[jax.ShapeDtypeStruct((n_tok, D_MODEL), BF16)]
        + [jax.ShapeDtypeStruct((BATCH, DEPTH) + s, F32) for s in cache_shapes]
        + [jax.ShapeDtypeStruct(w.shape[1:], BF16) for w in ffn_f32],
        input_output_aliases={n_in + i: 1 + i for i in range(n_alias)},
        compiler_params=_params(1),
        name="context_mixer",
    )(x, mods, *weights, *ffn_f32, *prev_caches)
    return mix, outs[:N_CACHE], outs[N_CACHE:]


def _finish(l, xc, mixc, xl, mixl, mods, t, ffn_bf16):
    nc, nl = xc.shape[0] // TM_FINISH, xl.shape[0] // TM_FINISH
    blocks_per_seq = DEC_SEQ // TM_FINISH
    ctx_tok = pl.BlockSpec((TM_FINISH, D_MODEL), lambda i: (jnp.minimum(i, nc - 1), 0))
    lat_tok = pl.BlockSpec((TM_FINISH, D_MODEL), lambda i: (jnp.maximum(i - nc, 0), 0))
    mod_row = lambda i: jnp.where(i < nc, 0, 1 + jnp.maximum(i - nc, 0) // blocks_per_seq)
    return pl.pallas_call(
        functools.partial(_finish_kernel, nc),
        grid=(nc + nl,),
        in_specs=[ctx_tok, ctx_tok, lat_tok, lat_tok,
                  pl.BlockSpec((None, 1, 6 * D_MODEL), lambda i: (l * N_MOD + mod_row(i), 0, 0)),
                  _layer_spec(l, (N_VEC, D_MODEL))] + [_const_spec(w.shape) for w in ffn_bf16],
        out_specs=[ctx_tok, lat_tok],
        out_shape=[jax.ShapeDtypeStruct(xc.shape, F32), jax.ShapeDtypeStruct(xl.shape, F32)],
        compiler_params=_params(1),
        name="finish",
    )(xc, mixc, xl, mixl, mods, t['vec'], *ffn_bf16)


def _latent_front(l, x, mods, t, consts):
    n_tok = DEC_BATCH * DEC_SEQ
    tm = TM_LAT_FRONT
    blocks_per_seq = DEC_SEQ // tm
    tok = lambda w: pl.BlockSpec((tm, w), lambda i: (i, 0))
    wspecs = _front_weight_specs(l)
    weights = _front_weights(t, consts)
    return pl.pallas_call(
        _lat_front_kernel,
        grid=(n_tok // tm,),
        in_specs=[tok(D_MODEL),
                  pl.BlockSpec((None, 1, 6 * D_MODEL), lambda i: (l * N_MOD + 1 + i // blocks_per_seq, 0, 0)),
                  wspecs[0], pl.BlockSpec((6, tm, LANES), lambda i: (0, i % blocks_per_seq, 0))] + wspecs[1:],
        out_specs=[tok(QK_PACK), tok(QK_PACK), tok(V_PACK)],
        out_shape=[jax.ShapeDtypeStruct((n_tok, QK_PACK), BF16), jax.ShapeDtypeStruct((n_tok, QK_PACK), BF16),
                   jax.ShapeDtypeStruct((n_tok, V_PACK), BF16)],
        compiler_params=_params(1),
        name="latent_front",
    )(x, mods, weights[0], consts['rope'], *weights[1:])


def _context_kv(l, ckv, krope_t, t):
    lay = lambda r, w: pl.BlockSpec((None, None, r, w), lambda b: (b, l, 0, 0))
    out = lambda w: pl.BlockSpec((None, PAST_LEN, w), lambda b: (b, 0, 0))
    return pl.pallas_call(
        _ctx_kv_kernel,
        grid=(DEC_BATCH,),
        in_specs=[lay(PAST_LEN, KV_LORA), lay(MLA_ROPE, PAST_LEN),
                  _layer_spec(l, (N_VEC, D_MODEL)), _layer_spec(l, (KV_LORA, W_MLA_P + MLA_W))],
        out_specs=[out(W_MLA_P), out(MLA_W)],
        out_shape=[jax.ShapeDtypeStruct((DEC_BATCH, PAST_LEN, W_MLA_P), BF16),
                   jax.ShapeDtypeStruct((DEC_BATCH, PAST_LEN, MLA_W), BF16)],
        compiler_params=_params(1),
        name="context_kv",
    )(ckv, krope_t, t['vec'], t['w_ukv'])


def _latent_na(l, q, k, v, kct, vct, g_rows, vec):
    n_tok = DEC_BATCH * DEC_SEQ
    na_blk = P_NA // NA_W
    steps = GRID_ROWS // NA_ROWS_PER_STEP
    tq = NA_ROWS_PER_STEP * GRID_W
    ctx = pl.BlockSpec((None, None, NA_W, PAST_LEN), lambda b, j: (b, l, 0, 0))
    return pl.pallas_call(
        _na_lat_kernel,
        grid=(DEC_BATCH, steps),
        in_specs=[pl.BlockSpec((tq, NA_W), lambda b, j: (b * steps + j, na_blk)),
                  pl.BlockSpec((DEC_SEQ, NA_W), lambda b, j: (b, na_blk)),
                  pl.BlockSpec((DEC_SEQ, NA_W), lambda b, j: (b, PV_NA // NA_W)),
                  ctx, ctx, _layer_spec(l, (NA_HEADS * NA_PAIR_TILES, LANES)), _layer_spec(l, (N_VEC, D_MODEL))],
        out_specs=pl.BlockSpec((tq, NA_W), lambda b, j: (b * steps + j, 0)),
        out_shape=jax.ShapeDtypeStruct((n_tok, NA_W), BF16),
        scratch_shapes=[pltpu.VMEM((NA_HEADS * NA_PAIR_TILES, GRID_W, LANES), F32)],
        compiler_params=_params(2),
        name="latent_neighbourhood",
    )(q, k, v, kct, vct, g_rows, vec)


def _latent_attention(l, lam_init, q, k, v, kc, vc, kct, vct, o_na, t):
    n_tok = DEC_BATCH * DEC_SEQ
    nq = DEC_SEQ // TM
    ctx = pl.BlockSpec((None, None, DF_W, PAST_LEN), lambda b, j: (b, l, 0, 0))
    return pl.pallas_call(
        functools.partial(_lat_attn_kernel, lam_init),
        grid=(DEC_BATCH, nq),
        in_specs=[pl.BlockSpec((TM, QK_PACK), lambda b, j: (b * nq + j, 0)),
                  pl.BlockSpec((DEC_SEQ, QK_PACK), lambda b, j: (b, 0)),
                  pl.BlockSpec((DEC_SEQ, V_PACK), lambda b, j: (b, 0)),
                  pl.BlockSpec((None, PAST_LEN, W_MLA_P), lambda b, j: (b, 0, 0)),
                  pl.BlockSpec((None, PAST_LEN, MLA_W), lambda b, j: (b, 0, 0)),
                  ctx, ctx,
                  pl.BlockSpec((TM, NA_W), lambda b, j: (b * nq + j, 0)),
                  _layer_spec(l, (N_VEC, D_MODEL))],
        out_specs=pl.BlockSpec((TM, D_MODEL), lambda b, j: (b * nq + j, 0)),
        out_shape=jax.ShapeDtypeStruct((n_tok, D_MODEL), BF16),
        compiler_params=_params(2),
        name="latent_attention",
    )(q, k, v, kc, vc, kct, vct, o_na, t['vec'])


def kernel(x_prompt, x_sample, cache_mla_ckv, cache_mla_krope, cache_na_k, cache_na_v, cache_df_k, cache_df_v, c, c_ctx, w_mod, b_mod, g_mix, w_in, g_qa, w_uq, g_kva, w_ukv, g_mla_q, g_mla_k, g_na_q, g_na_k, na_rpb, g_df_q, g_df_k, df_lq1, df_lk1, df_lq2, df_lk2, g_df_sub, w_out, g_ffn, w_gate, w_up, w_down):
    p = dict(g_mix=g_mix, w_in=w_in, g_qa=g_qa, w_uq=w_uq, g_kva=g_kva, w_ukv=w_ukv, g_mla_q=g_mla_q, g_mla_k=g_mla_k,
             g_na_q=g_na_q, g_na_k=g_na_k, na_rpb=na_rpb, g_df_q=g_df_q, g_df_k=g_df_k, df_lq1=df_lq1, df_lk1=df_lk1,
             df_lq2=df_lq2, df_lk2=df_lk2, g_df_sub=g_df_sub, w_out=w_out, g_ffn=g_ffn, w_gate=w_gate, w_up=w_up,
             w_down=w_down)
    consts = dict(bd64=_block_diag(NA_HD), bd32=_block_diag(DF_QK), rope=_rope_tables())
    t = _tables(p)

    c_all = jnp.concatenate([c_ctx[None, :], c, jnp.zeros((N_MOD - 1 - DEC_BATCH, D_MODEL), F32)], axis=0)
    mods = _modulation(c_all, w_mod, b_mod).reshape(DEPTH * N_MOD, 1, 6 * D_MODEL)

    krope_t = jnp.swapaxes(cache_mla_krope, -1, -2)
    na_kt, na_vt, df_kt, df_vt = (_feature_major(a) for a in (cache_na_k, cache_na_v, cache_df_k, cache_df_v))

    xp = x_prompt.reshape(BATCH * SEQ, D_MODEL)
    xs = x_sample.reshape(DEC_BATCH * DEC_SEQ, D_MODEL)
    new_caches = ()
    ffn_f32 = (w_out, w_gate, w_up, w_down)
    for l in range(DEPTH):
        lam_init = 0.8 - 0.6 * math.exp(-0.3 * l)
        q, k, v = _latent_front(l, xs, mods, t, consts)
        kc, vc = _context_kv(l, cache_mla_ckv, krope_t, t)
        o_na = _latent_na(l, q, k, v, na_kt, na_vt, t['g_rows'], t['vec'])
        mix_s = _latent_attention(l, lam_init, q, k, v, kc, vc, df_kt, df_vt, o_na, t)
        mix, new_caches, ffn_bf16 = _context_mixer(l, lam_init, xp, mods, t, consts, ffn_f32, new_caches)
        xp, xs = _finish(l, xp, mix, xs, mix_s, mods, t, ffn_bf16)
    ckv_new, *narrow = new_caches
    return (xp.reshape(BATCH, SEQ, D_MODEL), xs.reshape(DEC_BATCH, DEC_SEQ, D_MODEL), ckv_new,
            *(jnp.swapaxes(a, -1, -2) for a in narrow))
```

```python
import functools
import math

import numpy as np
import jax
import jax.numpy as jnp
from jax import lax
from jax.experimental import pallas as pl
from jax.experimental.pallas import tpu as pltpu

F32 = jnp.float32
BF16 = jnp.bfloat16

D_MODEL = 1024
BATCH = 32
SEQ = 256
DEPTH = 2
DEC_BATCH = 2
DEC_SEQ = 1024
PAST_LEN = 256
GRID_W = 64
GRID_ROWS = DEC_SEQ // GRID_W
MLA_HEADS = 6
MLA_NOPE = 64
MLA_ROPE = 32
MLA_QK = MLA_NOPE + MLA_ROPE
MLA_V = 64
MLA_PAD = 128
Q_LORA = 256
KV_LORA = 128
NA_HEADS = 6
NA_HD = 64
NA_KR = 8
NA_KW = 16
DF_HEADS = 4
DF_HD = 64
DF_QK = 32
MLA_W = MLA_HEADS * MLA_V
NA_W = NA_HEADS * NA_HD
DF_W = DF_HEADS * DF_HD
D_FF = -(-8 * D_MODEL // (3 * 256)) * 256
ROPE_BASE = 10000.0
EPS = 1e-6
NEG = -1e30
LOG2E = math.log2(math.e)
MAX_FREE_SOFTMAX_BOUND = 60.0

LANES = 128
MXU_DIM = 256

A_CQ = 0
A_CKV = A_CQ + Q_LORA
A_KR = A_CKV + KV_LORA
A_COLS = A_KR + LANES
B_SRC = Q_LORA + KV_LORA + MLA_ROPE
B_NAQ = 0
B_NAK = B_NAQ + NA_W
B_NAV = B_NAK + NA_W
B_DFQ = B_NAV + NA_W
B_DFK = B_DFQ + DF_W
B_DFV = B_DFK + DF_W
B_COLS = B_DFV + DF_W

W_MLA_P = MLA_HEADS * MLA_PAD
QK_PACK = W_MLA_P + NA_W + DF_W
V_PACK = MLA_W + NA_W + DF_W
P_MLA = 0
P_NA = W_MLA_P
P_DF = P_NA + NA_W
PV_MLA = 0
PV_NA = MLA_W
PV_DF = MLA_W + NA_W

(V_GMIX, V_GFFN, V_GQA, V_GKVA, V_GMQ, V_GMK, V_GNQ, V_GNK, V_GDQ, V_GDK, V_GDS,
 V_LQ1, V_LK1, V_LQ2, V_LK2) = range(15)
N_VEC = 16

N_MOD = 8
TM = 512
TM_LAT_FRONT = 256
TM_FINISH = 512
CTX_SEQS_PER_STEP = 2
NA_ROWS_PER_STEP = 4
VMEM_LIMIT = 56 * 1024 * 1024

NA_PAIR_TILES = 2 * NA_KR - 2
N_CACHE = 6
N_FFN_W = 4


def _dot(a, b):
    return jnp.dot(a, b, preferred_element_type=F32)


def _dot_nt(a, b):
    return lax.dot_general(a, b, (((1,), (1,)), ((), ())), preferred_element_type=F32)


def _lane_iota(shape):
    return lax.broadcasted_iota(jnp.int32, shape, len(shape) - 1)


def _rms_rows(x, g):
    ms = jnp.mean(x * x, axis=-1, keepdims=True)
    return x * lax.rsqrt(ms + EPS) * g


def _tile_rms(x, g, n_real):
    outs = []
    for c0 in range(0, x.shape[1], LANES):
        xt = x[:, c0:c0 + LANES]
        ms = jnp.sum(xt * xt, axis=-1, keepdims=True) * (1.0 / n_real)
        outs.append(xt * lax.rsqrt(ms + EPS) * g[:, c0:c0 + LANES])
    return jnp.concatenate(outs, axis=1)


def _seg_rms(x, bd_ref, g, group):
    width = x.shape[1]
    sq = (x * x).astype(BF16)
    parts = []
    for c0 in range(0, width, MXU_DIM):
        w = min(MXU_DIM, width - c0)
        parts.append(_dot(sq[:, c0:c0 + w], bd_ref[0:w, 0:w]))
    ss = parts[0] if len(parts) == 1 else jnp.concatenate(parts, axis=1)
    return x * lax.rsqrt(ss * (1.0 / group) + EPS) * g


def _rope_tiles(x, cos, sa, sb):
    outs = []
    for t in range(x.shape[1] // LANES):
        xt = x[:, t * LANES:(t + 1) * LANES]
        up = pltpu.roll(xt, LANES - MLA_ROPE // 4, 1)
        dn = pltpu.roll(xt, MLA_ROPE // 4, 1)
        outs.append(xt * cos + up * sa + dn * sb)
    return outs[0] if len(outs) == 1 else jnp.concatenate(outs, axis=1)


def _diff_lambda(vec_ref, lam_init):
    a = jnp.sum(vec_ref[V_LQ1:V_LQ1 + 1, 0:DF_QK] * vec_ref[V_LK1:V_LK1 + 1, 0:DF_QK], axis=-1, keepdims=True)
    b = jnp.sum(vec_ref[V_LQ2:V_LQ2 + 1, 0:DF_QK] * vec_ref[V_LK2:V_LK2 + 1, 0:DF_QK], axis=-1, keepdims=True)
    return jnp.exp(a) - jnp.exp(b) + lam_init


def _mixer_front(x, mod, vec_ref, wa_ref, wb_ref, w_uq_ref, bd64_ref, bd32_ref):
    sh = mod[:, 0:D_MODEL]
    sc = mod[:, D_MODEL:2 * D_MODEL]
    h = (_rms_rows(x, vec_ref[V_GMIX:V_GMIX + 1, :]) * (1.0 + sc) + sh).astype(BF16)
    za = _dot_nt(h, wa_ref[...])
    zb = _dot_nt(h, wb_ref[...])
    cqn = _rms_rows(za[:, A_CQ:A_CQ + Q_LORA], vec_ref[V_GQA:V_GQA + 1, 0:Q_LORA])
    q_raw = _dot_nt(cqn.astype(BF16), w_uq_ref[...])
    q_mla = _tile_rms(q_raw, vec_ref[V_GMQ:V_GMQ + 1, 0:W_MLA_P] * (MLA_QK ** -0.5 * LOG2E), MLA_QK)
    ckv_n = _rms_rows(za[:, A_CKV:A_CKV + KV_LORA], vec_ref[V_GKVA:V_GKVA + 1, 0:KV_LORA])
    kr_tile = za[:, A_KR:A_KR + LANES]
    q_na = _seg_rms(zb[:, B_NAQ:B_NAQ + NA_W], bd64_ref, vec_ref[V_GNQ:V_GNQ + 1, 0:NA_W] * (NA_HD ** -0.5 * LOG2E), NA_HD)
    k_na = _seg_rms(zb[:, B_NAK:B_NAK + NA_W], bd64_ref, vec_ref[V_GNK:V_GNK + 1, 0:NA_W], NA_HD)
    v_na = zb[:, B_NAV:B_NAV + NA_W]
    q_df = _seg_rms(zb[:, B_DFQ:B_DFQ + DF_W], bd32_ref, vec_ref[V_GDQ:V_GDQ + 1, 0:DF_W] * (DF_QK ** -0.5 * LOG2E), DF_QK)
    k_df = _seg_rms(zb[:, B_DFK:B_DFK + DF_W], bd32_ref, vec_ref[V_GDK:V_GDK + 1, 0:DF_W], DF_QK)
    v_df = zb[:, B_DFV:B_DFV + DF_W]
    return q_mla, ckv_n, kr_tile, q_na, k_na, v_na, q_df, k_df, v_df


def _mla_kv(ckv_n, kr_tile, vec_ref, w_ukv_ref):
    kv = _dot(ckv_n.astype(BF16), w_ukv_ref[...])
    lane = _lane_iota((1, LANES))
    kr = jnp.where((lane >= MLA_NOPE) & (lane < MLA_QK), kr_tile, 0.0)
    k_pre = kv[:, 0:W_MLA_P] + jnp.concatenate([kr] * MLA_HEADS, axis=1)
    k = _tile_rms(k_pre, vec_ref[V_GMK:V_GMK + 1, 0:W_MLA_P], MLA_QK)
    return k, kv[:, W_MLA_P:W_MLA_P + MLA_W]


def _softmax_parts(s, bound=None):
    m = jnp.max(s, axis=-1, keepdims=True) if bound is None else bound
    p = jnp.exp2(s - m)
    return p, 1.0 / jnp.sum(p, axis=-1, keepdims=True)


def _max_sq_norm(kt, group):
    sq = kt * kt
    best = None
    for r0 in range(0, kt.shape[0], group):
        n2 = jnp.sum(sq[r0:r0 + group], axis=0, keepdims=True)
        best = n2 if best is None else jnp.maximum(best, n2)
    return jnp.max(best, axis=-1, keepdims=True)


def _score_bounds(vec_ref, bias_max=None, na_key_sq=None, df_key_sq=None):
    gmax = lambda row, w: jnp.max(jnp.abs(vec_ref[row:row + 1, 0:w]), axis=-1, keepdims=True)
    slack = LOG2E * (1.0 + 2.0 ** -6)

    def key_norm(row, w, d, measured_sq):
        k = gmax(row, w) * d ** 0.5
        return k if measured_sq is None else jnp.maximum(k, jnp.sqrt(measured_sq))

    b_mla = gmax(V_GMQ, W_MLA_P) * key_norm(V_GMK, W_MLA_P, MLA_QK, None) * slack
    b_na = gmax(V_GNQ, NA_W) * key_norm(V_GNK, NA_W, NA_HD, na_key_sq) * slack
    if bias_max is not None:
        b_na = b_na + bias_max * LOG2E
    b_df = gmax(V_GDQ, DF_W) * key_norm(V_GDK, DF_W, DF_QK, df_key_sq) * slack
    worst = jnp.maximum(b_mla, jnp.maximum(b_na, b_df))
    return (b_mla, b_na, b_df), worst[0, 0] <= MAX_FREE_SOFTMAX_BOUND


def _with_score_bounds(vec_ref, attend, **measured):
    bounds, ok = _score_bounds(vec_ref, **measured)
    pl.when(ok)(lambda: attend(bounds))
    pl.when(jnp.logical_not(ok))(lambda: attend((None, None, None)))


def _scores(q, k_segs):
    parts = [_dot(q, k) if feature_major else _dot_nt(q, k) for k, feature_major in k_segs]
    return parts[0] if len(parts) == 1 else jnp.concatenate(parts, axis=1)


def _pv(p, v_segs):
    out = None
    c0 = 0
    for v, feature_major in v_segs:
        n = v.shape[1] if feature_major else v.shape[0]
        o = _dot_nt(p[:, c0:c0 + n], v) if feature_major else _dot(p[:, c0:c0 + n], v)
        out = o if out is None else out + o
        c0 += n
    return out


def _seg_tile(seg, t):
    a, feature_major = seg
    return (a[t * LANES:(t + 1) * LANES, :] if feature_major else a[:, t * LANES:(t + 1) * LANES]), feature_major


def _lane_groups(qt, width):
    lane = _lane_iota((1, LANES))
    zero = jnp.zeros_like(qt)
    return jnp.concatenate(
        [jnp.where((lane >= g * width) & (lane < (g + 1) * width), qt, zero) for g in range(LANES // width)], axis=0)


def _pair_select(o2):
    tq = o2.shape[0] // 2
    return jnp.where(_lane_iota((1, LANES)) < NA_HD, o2[0:tq], o2[tq:2 * tq])


def _mla_attend(q, k_segs, v_segs, bound=None):
    outs = []
    for t in range(MLA_HEADS // 2):
        vt = [_seg_tile(v, t) for v in v_segs]
        halves = []
        for h in (2 * t, 2 * t + 1):
            p, il = _softmax_parts(_scores(q[:, h * MLA_PAD:(h + 1) * MLA_PAD], [_seg_tile(k, h) for k in k_segs]),
                                   bound)
            halves.append(_pv(p.astype(BF16), vt) * il)
        outs.append(jnp.where(_lane_iota((1, LANES)) < MLA_V, halves[0], halves[1]))
    return jnp.concatenate(outs, axis=1)


def _na_tile_attend(qt, k_segs, v_segs, bias=None, bound=None):
    s = _scores(_lane_groups(qt, NA_HD), k_segs)
    if bias is not None:
        nb = bias.shape[1]
        s = jnp.concatenate([s[:, 0:nb] + bias, s[:, nb:]], axis=1)
    p, il = _softmax_parts(s, bound)
    return _pair_select(_pv(p.astype(BF16), v_segs) * il)


def _na_attend_full(q, k_segs, v_segs, bound=None):
    return jnp.concatenate(
        [_na_tile_attend(q[:, t * LANES:(t + 1) * LANES], [_seg_tile(k, t) for k in k_segs],
                         [_seg_tile(v, t) for v in v_segs], bound=bound) for t in range(NA_HEADS // 2)], axis=1)


def _df_attend(q, k_segs, v_segs, lam, g_sub, out_scale, bound=None):
    outs = []
    lane = _lane_iota((1, LANES))
    tq = q.shape[0]
    for t in range(DF_HEADS // 2):
        kt = [_seg_tile(k, t) for k in k_segs]
        vt = [_seg_tile(v, t) for v in v_segs]
        p, il = _softmax_parts(_scores(_lane_groups(q[:, t * LANES:(t + 1) * LANES], DF_QK), kt), bound)
        pn = []
        for hh in range(2):
            r1, r2 = 2 * hh * tq, (2 * hh + 1) * tq
            pn.append((p[r1:r1 + tq] * il[r1:r1 + tq] - p[r2:r2 + tq] * (lam * il[r2:r2 + tq])).astype(BF16))
        o = _pair_select(_pv(jnp.concatenate(pn, axis=0), vt))
        o2 = o * o
        ms_e = jnp.sum(jnp.where(lane < DF_HD, o2, 0.0), axis=-1, keepdims=True)
        ms_o = jnp.sum(jnp.where(lane >= DF_HD, o2, 0.0), axis=-1, keepdims=True)
        r = lax.rsqrt(jnp.where(lane < DF_HD, ms_e, ms_o) * (1.0 / DF_HD) + EPS)
        outs.append(o * r * (g_sub[:, t * LANES:(t + 1) * LANES] * out_scale))
    return jnp.concatenate(outs, axis=1)


def _mod_kernel(c_ref, w_ref, b_ref, o_ref):
    c = c_ref[...]
    s = c * jax.nn.sigmoid(c)
    o_ref[...] = _dot(s.astype(BF16), w_ref[...].astype(BF16)) + b_ref[...]


def _ctx_kernel(lam_init, n_alias, n_after, x_ref, mod_ref, vec_ref, wa_ref, wb_ref, w_uq_ref, w_ukv_ref, bd64_ref,
                bd32_ref, *rest):
    ffn_f32, rest = rest[:N_FFN_W], rest[N_FFN_W + n_alias + n_after:]
    mix_ref, *cache_refs = rest[:1 + N_CACHE]
    for src, dst in zip(ffn_f32, rest[1 + N_CACHE:]):
        dst[...] = src[...].astype(BF16)
    if n_alias == 0:
        for ref in cache_refs:
            ref[:, 1:] = jnp.zeros((ref.shape[0], ref.shape[1] - 1) + ref.shape[2:], F32)
        cache_refs = [ref.at[:, 0] for ref in cache_refs]
    ckv_ref, kr_ref, nak_ref, nav_ref, dfk_ref, dfv_ref = cache_refs
    lam = _diff_lambda(vec_ref, lam_init)
    mod = mod_ref[...]
    bf = lambda a: a.astype(BF16)
    seg = lambda a: [(a.astype(BF16), False)]
    for s in range(CTX_SEQS_PER_STEP):
        rows = slice(s * SEQ, (s + 1) * SEQ)
        q_mla, ckv_n, kr_tile, q_na, k_na, v_na, q_df, k_df, v_df = _mixer_front(
            x_ref[rows, :], mod, vec_ref, wa_ref, wb_ref, w_uq_ref, bd64_ref, bd32_ref)
        k_mla, v_mla = _mla_kv(ckv_n, kr_tile, vec_ref, w_ukv_ref)
        ckv_ref[s] = ckv_n
        kr_ref[s] = kr_tile.T[0:MLA_ROPE]
        for ref, a in ((nak_ref, k_na), (nav_ref, v_na), (dfk_ref, k_df), (dfv_ref, v_df)):
            at = a.T
            for h in range(ref.shape[1]):
                ref[s, h] = at[h * NA_HD:(h + 1) * NA_HD]
        o_mla = _mla_attend(bf(q_mla), seg(k_mla), seg(v_mla))
        o_na = _na_attend_full(bf(q_na), seg(k_na), seg(v_na))
        o_df = _df_attend(bf(q_df), seg(k_df), seg(v_df), lam, vec_ref[V_GDS:V_GDS + 1, 0:DF_W], 1.0 - lam_init)
        mix_ref[rows, :] = jnp.concatenate([o_mla, o_na, o_df], axis=1).astype(BF16)


def _finish_kernel(n_ctx_blocks, xc_ref, mixc_ref, xl_ref, mixl_ref, mod_ref, vec_ref, w_out_ref, w_gate_ref, w_up_ref,
                   w_down_ref, yc_ref, yl_ref):
    def block(x_ref, mix_ref, y_ref):
        mod = mod_ref[...]
        gate_m = mod[:, 2 * D_MODEL:3 * D_MODEL]
        sh = mod[:, 3 * D_MODEL:4 * D_MODEL]
        sc = mod[:, 4 * D_MODEL:5 * D_MODEL]
        gate_f = mod[:, 5 * D_MODEL:6 * D_MODEL]
        x1 = x_ref[...] + gate_m * _dot(mix_ref[...], w_out_ref[...])
        h = (_rms_rows(x1, vec_ref[V_GFFN:V_GFFN + 1, :]) * (1.0 + sc) + sh).astype(BF16)
        g = _dot(h, w_gate_ref[...])
        u = _dot(h, w_up_ref[...])
        a = (g * jax.nn.sigmoid(g) * u).astype(BF16)
        y_ref[...] = x1 + gate_f * _dot(a, w_down_ref[...])

    is_ctx = pl.program_id(0) < n_ctx_blocks
    pl.when(is_ctx)(lambda: block(xc_ref, mixc_ref, yc_ref))
    pl.when(jnp.logical_not(is_ctx))(lambda: block(xl_ref, mixl_ref, yl_ref))


def _lat_front_kernel(x_ref, mod_ref, vec_ref, rope_ref, wa_ref, wb_ref, w_uq_ref, w_ukv_ref, bd64_ref, bd32_ref,
                      q_ref, k_ref, v_ref):
    q_mla, ckv_n, kr_tile, q_na, k_na, v_na, q_df, k_df, v_df = _mixer_front(
        x_ref[...], mod_ref[...], vec_ref, wa_ref, wb_ref, w_uq_ref, bd64_ref, bd32_ref)
    k_mla, v_mla = _mla_kv(ckv_n, kr_tile, vec_ref, w_ukv_ref)
    cm, sam, sbm = rope_ref[0], rope_ref[1], rope_ref[2]
    cd, sad, sbd = rope_ref[3], rope_ref[4], rope_ref[5]
    q_ref[...] = jnp.concatenate(
        [_rope_tiles(q_mla, cm, sam, sbm), q_na, _rope_tiles(q_df, cd, sad, sbd)], axis=1).astype(BF16)
    k_ref[...] = jnp.concatenate(
        [_rope_tiles(k_mla, cm, sam, sbm), k_na, _rope_tiles(k_df, cd, sad, sbd)], axis=1).astype(BF16)
    v_ref[...] = jnp.concatenate([v_mla, v_na, v_df], axis=1).astype(BF16)


def _ctx_kv_kernel(ckv_ref, krt_ref, vec_ref, w_ukv_ref, k_ref, v_ref):
    krt = jnp.concatenate([jnp.zeros((MLA_NOPE, PAST_LEN), F32), krt_ref[...],
                           jnp.zeros((LANES - MLA_QK, PAST_LEN), F32)], axis=0)
    k_mla, v_mla = _mla_kv(ckv_ref[...], krt.T, vec_ref, w_ukv_ref)
    k_ref[...] = k_mla.astype(BF16)
    v_ref[...] = v_mla.astype(BF16)


def _na_lat_kernel(q_ref, kl_ref, vl_ref, kct_ref, vct_ref, g_ref, vec_ref, o_ref, bias_ref):
    b = pl.program_id(0)
    j = pl.program_id(1)

    @pl.when((b == 0) & (j == 0))
    def _build_bias():
        c = lax.broadcasted_iota(jnp.int32, (GRID_W, LANES), 0)
        kc = _lane_iota((GRID_W, LANES)) % GRID_W
        start = jnp.clip(c - NA_KW // 2, 0, GRID_W - NA_KW)
        in_win = (kc >= start) & (kc < start + NA_KW)

        def body(i, carry):
            row = jnp.broadcast_to(g_ref[pl.ds(i, 1), :], (GRID_W, LANES))
            toep = pltpu.roll(row, 0, 1, stride=1, stride_axis=0)
            bias_ref[i] = jnp.where(in_win, toep * LOG2E, NEG)
            return carry

        lax.fori_loop(0, NA_HEADS * NA_PAIR_TILES, body, 0)

    n_win = NA_KR * GRID_W
    kct_f32 = kct_ref[...]
    kct = kct_f32.astype(BF16)
    vct = vct_ref[...].astype(BF16)

    def win_start(r):
        return jnp.clip(r - NA_KR // 2, 0, GRID_ROWS - NA_KR)

    def attend(grid_rows, rs, bound):
        a0, n = grid_rows[0], len(grid_rows)
        row0 = pl.multiple_of(rs * GRID_W, GRID_W)
        outs = []
        for t in range(NA_HEADS // 2):
            lanes = slice(t * LANES, (t + 1) * LANES)
            bias = jnp.concatenate(
                [jnp.concatenate([bias_ref[h * NA_PAIR_TILES + (rs - (j * NA_ROWS_PER_STEP + a) + NA_KR - 1) + 2 * m]
                                  for m in range(NA_KR // 2)], axis=1)
                 for h in (2 * t, 2 * t + 1) for a in grid_rows], axis=0)
            outs.append(_na_tile_attend(
                q_ref[a0 * GRID_W:(a0 + n) * GRID_W, lanes],
                [(kl_ref[pl.ds(row0, n_win), lanes], False), (kct[lanes, :], True)],
                [(vl_ref[pl.ds(row0, n_win), lanes], False), (vct[lanes, :], True)], bias, bound))
        return jnp.concatenate(outs, axis=1).astype(BF16)

    first_r, last_r = j * NA_ROWS_PER_STEP, (j + 1) * NA_ROWS_PER_STEP - 1
    shared = win_start(first_r) == win_start(last_r)

    def attend_step(bounds):
        bound = bounds[1]

        @pl.when(shared)
        def _shared_window():
            o_ref[...] = attend(tuple(range(NA_ROWS_PER_STEP)), win_start(first_r), bound)

        @pl.when(jnp.logical_not(shared))
        def _per_row_windows():
            for a in range(NA_ROWS_PER_STEP):
                o_ref[a * GRID_W:(a + 1) * GRID_W, :] = attend((a,), win_start(first_r + a), bound)

    bias_max = jnp.max(jnp.max(jnp.abs(g_ref[...]), axis=-1, keepdims=True), axis=0, keepdims=True)
    _with_score_bounds(vec_ref, attend_step, bias_max=bias_max, na_key_sq=_max_sq_norm(kct_f32, NA_HD))


def _lat_attn_kernel(lam_init, q_ref, kl_ref, vl_ref, kc_ref, vc_ref, kct_ref, vct_ref, ona_ref, vec_ref, mix_ref):
    lam = _diff_lambda(vec_ref, lam_init)
    kct = kct_ref[...]

    def attend(bounds):
        b_mla, _, b_df = bounds
        o_mla = _mla_attend(q_ref[:, P_MLA:P_MLA + W_MLA_P],
                            [(kc_ref[...], False), (kl_ref[:, P_MLA:P_MLA + W_MLA_P], False)],
                            [(vc_ref[...], False), (vl_ref[:, PV_MLA:PV_MLA + MLA_W], False)], b_mla)
        o_df = _df_attend(q_ref[:, P_DF:P_DF + DF_W],
                          [(kct.astype(BF16), True), (kl_ref[:, P_DF:P_DF + DF_W], False)],
                          [(vct_ref[...].astype(BF16), True), (vl_ref[:, PV_DF:PV_DF + DF_W], False)],
                          lam, vec_ref[V_GDS:V_GDS + 1, 0:DF_W], 1.0 - lam_init, b_df)
        mix_ref[...] = jnp.concatenate([o_mla.astype(BF16), ona_ref[...], o_df.astype(BF16)], axis=1)

    _with_score_bounds(vec_ref, attend, df_key_sq=_max_sq_norm(kct, DF_QK))


def _const_spec(shape):
    nd = len(shape)
    return pl.BlockSpec(shape, lambda *_: (0,) * nd, pipeline_mode=pl.Buffered(1))


def _layer_spec(l, shape):
    nd = len(shape)
    return pl.BlockSpec((None,) + tuple(shape), lambda *_: (l,) + (0,) * nd, pipeline_mode=pl.Buffered(1))


def _params(n_axes):
    return pltpu.CompilerParams(dimension_semantics=("arbitrary",) * n_axes, vmem_limit_bytes=VMEM_LIMIT)


def _block_diag(group):
    i = np.arange(MXU_DIM) // group
    return jnp.asarray((i[:, None] == i[None, :]).astype(np.float32), dtype=BF16)


def _rope_tables():
    t = np.arange(DEC_SEQ)
    row = (t // GRID_W).astype(np.float64)
    col = (t % GRID_W).astype(np.float64)
    n = MLA_ROPE // 4
    inv = 1.0 / (ROPE_BASE ** (np.arange(n, dtype=np.float64) * 2.0 / (MLA_ROPE // 2)))
    ar = row[:, None] * inv
    ac = col[:, None] * inv
    ang = np.concatenate([ar, ar, ac, ac], axis=-1)
    cos32, sin32 = np.cos(ang), np.sin(ang)
    first = (np.arange(MLA_ROPE) % (2 * n)) < n
    sa32 = np.where(first, -sin32, 0.0)
    sb32 = np.where(first, 0.0, sin32)

    def mla_tile(v32, fill):
        out = np.full((DEC_SEQ, LANES), fill)
        out[:, MLA_NOPE:MLA_QK] = v32
        return out

    tabs = [mla_tile(cos32, 1.0), mla_tile(sa32, 0.0), mla_tile(sb32, 0.0),
            np.tile(cos32, (1, LANES // DF_QK)), np.tile(sa32, (1, LANES // DF_QK)), np.tile(sb32, (1, LANES // DF_QK))]
    return jnp.asarray(np.stack(tabs).astype(np.float32))


def _feature_major(a):
    a = jnp.swapaxes(a, -1, -2)
    return a.reshape(a.shape[:-3] + (a.shape[-3] * a.shape[-2], a.shape[-1]))


def _tables(p):
    w_in_t = jnp.swapaxes(p['w_in'], 1, 2)
    kr = w_in_t[:, Q_LORA + KV_LORA:B_SRC]
    z32 = jnp.zeros_like(kr)
    wa = jnp.concatenate([w_in_t[:, :Q_LORA + KV_LORA], kr, z32, kr, z32], axis=1).astype(BF16)
    wb = w_in_t[:, B_SRC:].astype(BF16)
    w_uq_t = jnp.swapaxes(p['w_uq'], 1, 2).reshape(DEPTH, MLA_HEADS, MLA_QK, Q_LORA)
    w_uq_p = jnp.pad(w_uq_t, ((0, 0), (0, 0), (0, MLA_PAD - MLA_QK), (0, 0))).reshape(DEPTH, W_MLA_P, Q_LORA).astype(BF16)
    w_ukv = p['w_ukv'].reshape(DEPTH, KV_LORA, MLA_HEADS, MLA_NOPE + MLA_V)
    wk = jnp.pad(w_ukv[..., :MLA_NOPE], ((0, 0), (0, 0), (0, 0), (0, MLA_PAD - MLA_NOPE))).reshape(DEPTH, KV_LORA, W_MLA_P)
    wv = w_ukv[..., MLA_NOPE:].reshape(DEPTH, KV_LORA, MLA_W)
    w_ukv_r = jnp.concatenate([wk, wv], axis=2).astype(BF16)

    def row(v, reps=1):
        v = jnp.tile(v, (1, reps)) if reps > 1 else v
        return [v, jnp.zeros((DEPTH, D_MODEL - v.shape[1]), F32)] if v.shape[1] < D_MODEL else [v]

    pad_head = lambda g: jnp.pad(g, ((0, 0), (0, MLA_PAD - MLA_QK)))
    pieces = (row(p['g_mix']) + row(p['g_ffn']) + row(p['g_qa']) + row(p['g_kva'])
              + row(pad_head(p['g_mla_q']), MLA_HEADS) + row(pad_head(p['g_mla_k']), MLA_HEADS)
              + row(p['g_na_q'], NA_HEADS) + row(p['g_na_k'], NA_HEADS)
              + row(p['g_df_q'], 2 * DF_HEADS) + row(p['g_df_k'], 2 * DF_HEADS) + row(p['g_df_sub'], DF_HEADS)
              + row(p['df_lq1']) + row(p['df_lk1']) + row(p['df_lq2']) + row(p['df_lk2'])
              + [jnp.zeros((DEPTH, D_MODEL), F32)])
    vec = jnp.concatenate(pieces, axis=1).reshape(DEPTH, N_VEC, D_MODEL)
    f = p['na_rpb']
    zpad = jnp.zeros((DEPTH, NA_HEADS, NA_PAIR_TILES, 33), F32)
    g_rows = jnp.concatenate([f[:, :, :-1, NA_KW - 1:], zpad, f[:, :, 1:, :], zpad, f[:, :, :-1, :NA_KW - 1]], axis=-1)
    g_rows = g_rows.reshape(DEPTH, NA_HEADS * NA_PAIR_TILES, LANES)
    return dict(wa=wa, wb=wb, w_uq=w_uq_p, w_ukv=w_ukv_r, vec=vec, g_rows=g_rows)


def _modulation(c_all, w_mod, b_mod):
    tn = 2048
    return pl.pallas_call(
        _mod_kernel,
        grid=(DEPTH, 6 * D_MODEL // tn),
        in_specs=[pl.BlockSpec((N_MOD, D_MODEL), lambda l, j: (0, 0)),
                  pl.BlockSpec((None, D_MODEL, tn), lambda l, j: (l, 0, j)),
                  pl.BlockSpec((None, 1, tn), lambda l, j: (l, 0, j))],
        out_specs=pl.BlockSpec((None, N_MOD, tn), lambda l, j: (l, 0, j)),
        out_shape=jax.ShapeDtypeStruct((DEPTH, N_MOD, 6 * D_MODEL), F32),
        compiler_params=_params(2),
        name="modulation",
    )(c_all, w_mod, b_mod.reshape(DEPTH, 1, 6 * D_MODEL))


def _front_weight_specs(l):
    return [_layer_spec(l, (N_VEC, D_MODEL)), _layer_spec(l, (A_COLS, D_MODEL)), _layer_spec(l, (B_COLS, D_MODEL)),
            _layer_spec(l, (W_MLA_P, Q_LORA)), _layer_spec(l, (KV_LORA, W_MLA_P + MLA_W)),
            _const_spec((MXU_DIM, MXU_DIM)), _const_spec((MXU_DIM, MXU_DIM))]


def _front_weights(t, consts):
    return (t['vec'], t['wa'], t['wb'], t['w_uq'], t['w_ukv'], consts['bd64'], consts['bd32'])


def _context_mixer(l, lam_init, x, mods, t, consts, ffn_f32, prev_caches, after=()):
    n_tok = BATCH * SEQ
    n_alias = len(prev_caches)
    tok = lambda w: pl.BlockSpec((CTX_SEQS_PER_STEP * SEQ, w), lambda b: (b, 0))
    if n_alias == 0:
        assert l == 0
        lay = lambda *s: pl.BlockSpec((CTX_SEQS_PER_STEP, DEPTH) + s, lambda b: (b, 0) + (0,) * len(s))
    else:
        lay = lambda *s: pl.BlockSpec((CTX_SEQS_PER_STEP, None) + s, lambda b: (b, l) + (0,) * len(s))
    cache_shapes = [(SEQ, KV_LORA), (MLA_ROPE, SEQ), (NA_HEADS, NA_HD, SEQ), (NA_HEADS, NA_HD, SEQ),
                    (DF_HEADS, DF_HD, SEQ), (DF_HEADS, DF_HD, SEQ)]
    weights = _front_weights(t, consts)
    steps = BATCH // CTX_SEQS_PER_STEP
    ffn_chunks = [(w.shape[1] // steps, w.shape[2]) for w in ffn_f32]
    n_in = 2 + len(weights) + len(ffn_f32)
    mix, *outs = pl.pallas_call(
        functools.partial(_ctx_kernel, lam_init, n_alias, len(after)),
        grid=(steps,),
        in_specs=[tok(D_MODEL), pl.BlockSpec((None, 1, 6 * D_MODEL), lambda b: (l * N_MOD, 0, 0))]
        + _front_weight_specs(l) + [pl.BlockSpec((None,) + c, lambda b: (l, b, 0)) for c in ffn_chunks]
        + [pl.BlockSpec(memory_space=pl.ANY)] * (n_alias + len(after)),
        out_specs=[tok(D_MODEL)] + [lay(*s) for s in cache_shapes] + [pl.BlockSpec(c, lambda b: (b, 0)) for c in ffn_chunks],
        out_shape=[jax.ShapeDtypeStruct((n_tok, D_MODEL), BF16)]
        + [jax.ShapeDtypeStruct((BATCH, DEPTH) + s, F32) for s in cache_shapes]
        + [jax.ShapeDtypeStruct(w.shape[1:], BF16) for w in ffn_f32],
        input_output_aliases={n_in + i: 1 + i for i in range(n_alias)},
        compiler_params=_params(1),
        name="context_mixer",
    )(x, mods, *weights, *ffn_f32, *prev_caches, *after)
    return mix, outs[:N_CACHE], outs[N_CACHE:]


def _finish(l, xc, mixc, xl, mixl, mods, t, ffn_bf16):
    nc, nl = xc.shape[0] // TM_FINISH, xl.shape[0] // TM_FINISH
    blocks_per_seq = DEC_SEQ // TM_FINISH
    ctx_tok = pl.BlockSpec((TM_FINISH, D_MODEL), lambda i: (jnp.minimum(i, nc - 1), 0))
    lat_tok = pl.BlockSpec((TM_FINISH, D_MODEL), lambda i: (jnp.maximum(i - nc, 0), 0))
    mod_row = lambda i: jnp.where(i < nc, 0, 1 + jnp.maximum(i - nc, 0) // blocks_per_seq)
    return pl.pallas_call(
        functools.partial(_finish_kernel, nc),
        grid=(nc + nl,),
        in_specs=[ctx_tok, ctx_tok, lat_tok, lat_tok,
                  pl.BlockSpec((None, 1, 6 * D_MODEL), lambda i: (l * N_MOD + mod_row(i), 0, 0)),
                  _layer_spec(l, (N_VEC, D_MODEL))] + [_const_spec(w.shape) for w in ffn_bf16],
        out_specs=[ctx_tok, lat_tok],
        out_shape=[jax.ShapeDtypeStruct(xc.shape, F32), jax.ShapeDtypeStruct(xl.shape, F32)],
        compiler_params=_params(1),
        name="finish",
    )(xc, mixc, xl, mixl, mods, t['vec'], *ffn_bf16)


def _latent_front(l, x, mods, t, consts):
    n_tok = DEC_BATCH * DEC_SEQ
    tm = TM_LAT_FRONT
    blocks_per_seq = DEC_SEQ // tm
    tok = lambda w: pl.BlockSpec((tm, w), lambda i: (i, 0))
    wspecs = _front_weight_specs(l)
    weights = _front_weights(t, consts)
    return pl.pallas_call(
        _lat_front_kernel,
        grid=(n_tok // tm,),
        in_specs=[tok(D_MODEL),
                  pl.BlockSpec((None, 1, 6 * D_MODEL), lambda i: (l * N_MOD + 1 + i // blocks_per_seq, 0, 0)),
                  wspecs[0], pl.BlockSpec((6, tm, LANES), lambda i: (0, i % blocks_per_seq, 0))] + wspecs[1:],
        out_specs=[tok(QK_PACK), tok(QK_PACK), tok(V_PACK)],
        out_shape=[jax.ShapeDtypeStruct((n_tok, QK_PACK), BF16), jax.ShapeDtypeStruct((n_tok, QK_PACK), BF16),
                   jax.ShapeDtypeStruct((n_tok, V_PACK), BF16)],
        compiler_params=_params(1),
        name="latent_front",
    )(x, mods, weights[0], consts['rope'], *weights[1:])


def _context_kv(l, ckv, krope_t, t):
    lay = lambda r, w: pl.BlockSpec((None, None, r, w), lambda b: (b, l, 0, 0))
    out = lambda w: pl.BlockSpec((None, PAST_LEN, w), lambda b: (b, 0, 0))
    return pl.pallas_call(
        _ctx_kv_kernel,
        grid=(DEC_BATCH,),
        in_specs=[lay(PAST_LEN, KV_LORA), lay(MLA_ROPE, PAST_LEN),
                  _layer_spec(l, (N_VEC, D_MODEL)), _layer_spec(l, (KV_LORA, W_MLA_P + MLA_W))],
        out_specs=[out(W_MLA_P), out(MLA_W)],
        out_shape=[jax.ShapeDtypeStruct((DEC_BATCH, PAST_LEN, W_MLA_P), BF16),
                   jax.ShapeDtypeStruct((DEC_BATCH, PAST_LEN, MLA_W), BF16)],
        compiler_params=_params(1),
        name="context_kv",
    )(ckv, krope_t, t['vec'], t['w_ukv'])


def _latent_na(l, q, k, v, kct, vct, g_rows, vec):
    n_tok = DEC_BATCH * DEC_SEQ
    na_blk = P_NA // NA_W
    steps = GRID_ROWS // NA_ROWS_PER_STEP
    tq = NA_ROWS_PER_STEP * GRID_W
    ctx = pl.BlockSpec((None, None, NA_W, PAST_LEN), lambda b, j: (b, l, 0, 0))
    return pl.pallas_call(
        _na_lat_kernel,
        grid=(DEC_BATCH, steps),
        in_specs=[pl.BlockSpec((tq, NA_W), lambda b, j: (b * steps + j, na_blk)),
                  pl.BlockSpec((DEC_SEQ, NA_W), lambda b, j: (b, na_blk)),
                  pl.BlockSpec((DEC_SEQ, NA_W), lambda b, j: (b, PV_NA // NA_W)),
                  ctx, ctx, _layer_spec(l, (NA_HEADS * NA_PAIR_TILES, LANES)), _layer_spec(l, (N_VEC, D_MODEL))],
        out_specs=pl.BlockSpec((tq, NA_W), lambda b, j: (b * steps + j, 0)),
        out_shape=jax.ShapeDtypeStruct((n_tok, NA_W), BF16),
        scratch_shapes=[pltpu.VMEM((NA_HEADS * NA_PAIR_TILES, GRID_W, LANES), F32)],
        compiler_params=_params(2),
        name="latent_neighbourhood",
    )(q, k, v, kct, vct, g_rows, vec)


def _latent_attention(l, lam_init, q, k, v, kc, vc, kct, vct, o_na, t):
    n_tok = DEC_BATCH * DEC_SEQ
    nq = DEC_SEQ // TM
    ctx = pl.BlockSpec((None, None, DF_W, PAST_LEN), lambda b, j: (b, l, 0, 0))
    return pl.pallas_call(
        functools.partial(_lat_attn_kernel, lam_init),
        grid=(DEC_BATCH, nq),
        in_specs=[pl.BlockSpec((TM, QK_PACK), lambda b, j: (b * nq + j, 0)),
                  pl.BlockSpec((DEC_SEQ, QK_PACK), lambda b, j: (b, 0)),
                  pl.BlockSpec((DEC_SEQ, V_PACK), lambda b, j: (b, 0)),
                  pl.BlockSpec((None, PAST_LEN, W_MLA_P), lambda b, j: (b, 0, 0)),
                  pl.BlockSpec((None, PAST_LEN, MLA_W), lambda b, j: (b, 0, 0)),
                  ctx, ctx,
                  pl.BlockSpec((TM, NA_W), lambda b, j: (b * nq + j, 0)),
                  _layer_spec(l, (N_VEC, D_MODEL))],
        out_specs=pl.BlockSpec((TM, D_MODEL), lambda b, j: (b * nq + j, 0)),
        out_shape=jax.ShapeDtypeStruct((n_tok, D_MODEL), BF16),
        compiler_params=_params(2),
        name="latent_attention",
    )(q, k, v, kc, vc, kct, vct, o_na, t['vec'])


def kernel(x_prompt, x_sample, cache_mla_ckv, cache_mla_krope, cache_na_k, cache_na_v, cache_df_k, cache_df_v, c, c_ctx, w_mod, b_mod, g_mix, w_in, g_qa, w_uq, g_kva, w_ukv, g_mla_q, g_mla_k, g_na_q, g_na_k, na_rpb, g_df_q, g_df_k, df_lq1, df_lk1, df_lq2, df_lk2, g_df_sub, w_out, g_ffn, w_gate, w_up, w_down):
    p = dict(g_mix=g_mix, w_in=w_in, g_qa=g_qa, w_uq=w_uq, g_kva=g_kva, w_ukv=w_ukv, g_mla_q=g_mla_q, g_mla_k=g_mla_k,
             g_na_q=g_na_q, g_na_k=g_na_k, na_rpb=na_rpb, g_df_q=g_df_q, g_df_k=g_df_k, df_lq1=df_lq1, df_lk1=df_lk1,
             df_lq2=df_lq2, df_lk2=df_lk2, g_df_sub=g_df_sub, w_out=w_out, g_ffn=g_ffn, w_gate=w_gate, w_up=w_up,
             w_down=w_down)
    consts = dict(bd64=_block_diag(NA_HD), bd32=_block_diag(DF_QK), rope=_rope_tables())
    t = _tables(p)

    c_all = jnp.concatenate([c_ctx[None, :], c, jnp.zeros((N_MOD - 1 - DEC_BATCH, D_MODEL), F32)], axis=0)
    mods = _modulation(c_all, w_mod, b_mod).reshape(DEPTH * N_MOD, 1, 6 * D_MODEL)

    krope_t = jnp.swapaxes(cache_mla_krope, -1, -2)
    na_kt, na_vt, df_kt, df_vt = (_feature_major(a) for a in (cache_na_k, cache_na_v, cache_df_k, cache_df_v))

    xp = x_prompt.reshape(BATCH * SEQ, D_MODEL)
    xs = x_sample.reshape(DEC_BATCH * DEC_SEQ, D_MODEL)
    new_caches = ()
    ffn_f32 = (w_out, w_gate, w_up, w_down)
    for l in range(DEPTH):
        lam_init = 0.8 - 0.6 * math.exp(-0.3 * l)
        q, k, v = _latent_front(l, xs, mods, t, consts)
        kc, vc = _context_kv(l, cache_mla_ckv, krope_t, t)
        o_na = _latent_na(l, q, k, v, na_kt, na_vt, t['g_rows'], t['vec'])
        mix_s = _latent_attention(l, lam_init, q, k, v, kc, vc, df_kt, df_vt, o_na, t)
        mix, new_caches, ffn_bf16 = _context_mixer(l, lam_init, xp, mods, t, consts, ffn_f32, new_caches, after=(mix_s,))
        xp, xs = _finish(l, xp, mix, xs, mix_s, mods, t, ffn_bf16)
    ckv_new, *narrow = new_caches
    return (xp.reshape(BATCH, SEQ, D_MODEL), xs.reshape(DEC_BATCH, DEC_SEQ, D_MODEL), ckv_new,
            *(jnp.swapaxes(a, -1, -2) for a in narrow))
```

```python
import functools
import math

import numpy as np
import jax
import jax.numpy as jnp
from jax import lax
from jax.experimental import pallas as pl
from jax.experimental.pallas import tpu as pltpu

F32 = jnp.float32
BF16 = jnp.bfloat16

D_MODEL = 1024
BATCH = 32
SEQ = 256
DEPTH = 2
DEC_BATCH = 2
DEC_SEQ = 1024
PAST_LEN = 256
GRID_W = 64
GRID_ROWS = DEC_SEQ // GRID_W
MLA_HEADS = 6
MLA_NOPE = 64
MLA_ROPE = 32
MLA_QK = MLA_NOPE + MLA_ROPE
MLA_V = 64
MLA_PAD = 128
Q_LORA = 256
KV_LORA = 128
NA_HEADS = 6
NA_HD = 64
NA_KR = 8
NA_KW = 16
DF_HEADS = 4
DF_HD = 64
DF_QK = 32
MLA_W = MLA_HEADS * MLA_V
NA_W = NA_HEADS * NA_HD
DF_W = DF_HEADS * DF_HD
D_FF = -(-8 * D_MODEL // (3 * 256)) * 256
ROPE_BASE = 10000.0
EPS = 1e-6
NEG = -1e30
LOG2E = math.log2(math.e)
MAX_FREE_SOFTMAX_BOUND = 60.0

LANES = 128
MXU_DIM = 256

A_CQ = 0
A_CKV = A_CQ + Q_LORA
A_KR = A_CKV + KV_LORA
A_COLS = A_KR + LANES
B_SRC = Q_LORA + KV_LORA + MLA_ROPE
B_NAQ = 0
B_NAK = B_NAQ + NA_W
B_NAV = B_NAK + NA_W
B_DFQ = B_NAV + NA_W
B_DFK = B_DFQ + DF_W
B_DFV = B_DFK + DF_W
B_COLS = B_DFV + DF_W

W_MLA_P = MLA_HEADS * MLA_PAD
QK_PACK = W_MLA_P + NA_W + DF_W
V_PACK = MLA_W + NA_W + DF_W
P_MLA = 0
P_NA = W_MLA_P
P_DF = P_NA + NA_W
PV_MLA = 0
PV_NA = MLA_W
PV_DF = MLA_W + NA_W

(V_GMIX, V_GFFN, V_GQA, V_GKVA, V_GMQ, V_GMK, V_GNQ, V_GNK, V_GDQ, V_GDK, V_GDS,
 V_LQ1, V_LK1, V_LQ2, V_LK2) = range(15)
N_VEC = 16

N_MOD = 8
TM = 512
TM_LAT_FRONT = 256
TM_FINISH = 512
CTX_SEQS_PER_STEP = 2
NA_ROWS_PER_STEP = 4
VMEM_LIMIT = 56 * 1024 * 1024

NA_PAIR_TILES = 2 * NA_KR - 2
N_CACHE = 6
N_FFN_W = 4


def _dot(a, b):
    return jnp.dot(a, b, preferred_element_type=F32)


def _dot_nt(a, b):
    return lax.dot_general(a, b, (((1,), (1,)), ((), ())), preferred_element_type=F32)


def _lane_iota(shape):
    return lax.broadcasted_iota(jnp.int32, shape, len(shape) - 1)


def _rms_rows(x, g):
    ms = jnp.mean(x * x, axis=-1, keepdims=True)
    return x * lax.rsqrt(ms + EPS) * g


def _tile_rms(x, g, n_real):
    outs = []
    for c0 in range(0, x.shape[1], LANES):
        xt = x[:, c0:c0 + LANES]
        ms = jnp.sum(xt * xt, axis=-1, keepdims=True) * (1.0 / n_real)
        outs.append(xt * lax.rsqrt(ms + EPS) * g[:, c0:c0 + LANES])
    return jnp.concatenate(outs, axis=1)


def _seg_rms(x, bd_ref, g, group):
    width = x.shape[1]
    sq = (x * x).astype(BF16)
    parts = []
    for c0 in range(0, width, MXU_DIM):
        w = min(MXU_DIM, width - c0)
        parts.append(_dot(sq[:, c0:c0 + w], bd_ref[0:w, 0:w]))
    ss = parts[0] if len(parts) == 1 else jnp.concatenate(parts, axis=1)
    return x * lax.rsqrt(ss * (1.0 / group) + EPS) * g


def _rope_tiles(x, cos, sa, sb):
    outs = []
    for t in range(x.shape[1] // LANES):
        xt = x[:, t * LANES:(t + 1) * LANES]
        up = pltpu.roll(xt, LANES - MLA_ROPE // 4, 1)
        dn = pltpu.roll(xt, MLA_ROPE // 4, 1)
        outs.append(xt * cos + up * sa + dn * sb)
    return outs[0] if len(outs) == 1 else jnp.concatenate(outs, axis=1)


def _diff_lambda(vec_ref, lam_init):
    a = jnp.sum(vec_ref[V_LQ1:V_LQ1 + 1, 0:DF_QK] * vec_ref[V_LK1:V_LK1 + 1, 0:DF_QK], axis=-1, keepdims=True)
    b = jnp.sum(vec_ref[V_LQ2:V_LQ2 + 1, 0:DF_QK] * vec_ref[V_LK2:V_LK2 + 1, 0:DF_QK], axis=-1, keepdims=True)
    return jnp.exp(a) - jnp.exp(b) + lam_init


def _mixer_front(x, mod, vec_ref, wa_ref, wb_ref, w_uq_ref, bd64_ref, bd32_ref):
    sh = mod[:, 0:D_MODEL]
    sc = mod[:, D_MODEL:2 * D_MODEL]
    h = (_rms_rows(x, vec_ref[V_GMIX:V_GMIX + 1, :]) * (1.0 + sc) + sh).astype(BF16)
    za = _dot_nt(h, wa_ref[...])
    zb = _dot_nt(h, wb_ref[...])
    cqn = _rms_rows(za[:, A_CQ:A_CQ + Q_LORA], vec_ref[V_GQA:V_GQA + 1, 0:Q_LORA])
    q_raw = _dot_nt(cqn.astype(BF16), w_uq_ref[...])
    q_mla = _tile_rms(q_raw, vec_ref[V_GMQ:V_GMQ + 1, 0:W_MLA_P] * (MLA_QK ** -0.5 * LOG2E), MLA_QK)
    ckv_n = _rms_rows(za[:, A_CKV:A_CKV + KV_LORA], vec_ref[V_GKVA:V_GKVA + 1, 0:KV_LORA])
    kr_tile = za[:, A_KR:A_KR + LANES]
    q_na = _seg_rms(zb[:, B_NAQ:B_NAQ + NA_W], bd64_ref, vec_ref[V_GNQ:V_GNQ + 1, 0:NA_W] * (NA_HD ** -0.5 * LOG2E), NA_HD)
    k_na = _seg_rms(zb[:, B_NAK:B_NAK + NA_W], bd64_ref, vec_ref[V_GNK:V_GNK + 1, 0:NA_W], NA_HD)
    v_na = zb[:, B_NAV:B_NAV + NA_W]
    q_df = _seg_rms(zb[:, B_DFQ:B_DFQ + DF_W], bd32_ref, vec_ref[V_GDQ:V_GDQ + 1, 0:DF_W] * (DF_QK ** -0.5 * LOG2E), DF_QK)
    k_df = _seg_rms(zb[:, B_DFK:B_DFK + DF_W], bd32_ref, vec_ref[V_GDK:V_GDK + 1, 0:DF_W], DF_QK)
    v_df = zb[:, B_DFV:B_DFV + DF_W]
    return q_mla, ckv_n, kr_tile, q_na, k_na, v_na, q_df, k_df, v_df


def _mla_kv(ckv_n, kr_tile, vec_ref, w_ukv_ref):
    kv = _dot(ckv_n.astype(BF16), w_ukv_ref[...])
    lane = _lane_iota((1, LANES))
    kr = jnp.where((lane >= MLA_NOPE) & (lane < MLA_QK), kr_tile, 0.0)
    k_pre = kv[:, 0:W_MLA_P] + jnp.concatenate([kr] * MLA_HEADS, axis=1)
    k = _tile_rms(k_pre, vec_ref[V_GMK:V_GMK + 1, 0:W_MLA_P], MLA_QK)
    return k, kv[:, W_MLA_P:W_MLA_P + MLA_W]


def _softmax_parts(s, bound=None):
    m = jnp.max(s, axis=-1, keepdims=True) if bound is None else bound
    p = jnp.exp2(s - m)
    return p, 1.0 / jnp.sum(p, axis=-1, keepdims=True)


def _max_sq_norm(kt, group):
    sq = kt * kt
    best = None
    for r0 in range(0, kt.shape[0], group):
        n2 = jnp.sum(sq[r0:r0 + group], axis=0, keepdims=True)
        best = n2 if best is None else jnp.maximum(best, n2)
    return jnp.max(best, axis=-1, keepdims=True)


def _score_bounds(vec_ref, bias_max=None, na_key_sq=None, df_key_sq=None):
    gmax = lambda row, w: jnp.max(jnp.abs(vec_ref[row:row + 1, 0:w]), axis=-1, keepdims=True)
    slack = LOG2E * (1.0 + 2.0 ** -6)

    def key_norm(row, w, d, measured_sq):
        k = gmax(row, w) * d ** 0.5
        return k if measured_sq is None else jnp.maximum(k, jnp.sqrt(measured_sq))

    b_mla = gmax(V_GMQ, W_MLA_P) * key_norm(V_GMK, W_MLA_P, MLA_QK, None) * slack
    b_na = gmax(V_GNQ, NA_W) * key_norm(V_GNK, NA_W, NA_HD, na_key_sq) * slack
    if bias_max is not None:
        b_na = b_na + bias_max * LOG2E
    b_df = gmax(V_GDQ, DF_W) * key_norm(V_GDK, DF_W, DF_QK, df_key_sq) * slack
    worst = jnp.maximum(b_mla, jnp.maximum(b_na, b_df))
    return (b_mla, b_na, b_df), worst[0, 0] <= MAX_FREE_SOFTMAX_BOUND


def _with_score_bounds(vec_ref, attend, **measured):
    bounds, ok = _score_bounds(vec_ref, **measured)
    pl.when(ok)(lambda: attend(bounds))
    pl.when(jnp.logical_not(ok))(lambda: attend((None, None, None)))


def _scores(q, k_segs):
    parts = [_dot(q, k) if feature_major else _dot_nt(q, k) for k, feature_major in k_segs]
    return parts[0] if len(parts) == 1 else jnp.concatenate(parts, axis=1)


def _pv(p, v_segs):
    out = None
    c0 = 0
    for v, feature_major in v_segs:
        n = v.shape[1] if feature_major else v.shape[0]
        o = _dot_nt(p[:, c0:c0 + n], v) if feature_major else _dot(p[:, c0:c0 + n], v)
        out = o if out is None else out + o
        c0 += n
    return out


def _seg_tile(seg, t):
    a, feature_major = seg
    return (a[t * LANES:(t + 1) * LANES, :] if feature_major else a[:, t * LANES:(t + 1) * LANES]), feature_major


def _lane_groups(qt, width):
    lane = _lane_iota((1, LANES))
    zero = jnp.zeros_like(qt)
    return jnp.concatenate(
        [jnp.where((lane >= g * width) & (lane < (g + 1) * width), qt, zero) for g in range(LANES // width)], axis=0)


def _pair_select(o2):
    tq = o2.shape[0] // 2
    return jnp.where(_lane_iota((1, LANES)) < NA_HD, o2[0:tq], o2[tq:2 * tq])


def _mla_attend(q, k_segs, v_segs, bound=None):
    outs = []
    for t in range(MLA_HEADS // 2):
        vt = [_seg_tile(v, t) for v in v_segs]
        halves = []
        for h in (2 * t, 2 * t + 1):
            p, il = _softmax_parts(_scores(q[:, h * MLA_PAD:(h + 1) * MLA_PAD], [_seg_tile(k, h) for k in k_segs]),
                                   bound)
            halves.append(_pv(p.astype(BF16), vt) * il)
        outs.append(jnp.where(_lane_iota((1, LANES)) < MLA_V, halves[0], halves[1]))
    return jnp.concatenate(outs, axis=1)


def _na_tile_attend(qt, k_segs, v_segs, bias=None, bound=None):
    s = _scores(_lane_groups(qt, NA_HD), k_segs)
    if bias is not None:
        nb = bias.shape[1]
        s = jnp.concatenate([s[:, 0:nb] + bias, s[:, nb:]], axis=1)
    p, il = _softmax_parts(s, bound)
    return _pair_select(_pv(p.astype(BF16), v_segs) * il)


def _na_attend_full(q, k_segs, v_segs, bound=None):
    return jnp.concatenate(
        [_na_tile_attend(q[:, t * LANES:(t + 1) * LANES], [_seg_tile(k, t) for k in k_segs],
                         [_seg_tile(v, t) for v in v_segs], bound=bound) for t in range(NA_HEADS // 2)], axis=1)


def _df_attend(q, k_segs, v_segs, lam, g_sub, out_scale, bound=None):
    outs = []
    lane = _lane_iota((1, LANES))
    tq = q.shape[0]
    for t in range(DF_HEADS // 2):
        kt = [_seg_tile(k, t) for k in k_segs]
        vt = [_seg_tile(v, t) for v in v_segs]
        p, il = _softmax_parts(_scores(_lane_groups(q[:, t * LANES:(t + 1) * LANES], DF_QK), kt), bound)
        pn = []
        for hh in range(2):
            r1, r2 = 2 * hh * tq, (2 * hh + 1) * tq
            pn.append((p[r1:r1 + tq] * il[r1:r1 + tq] - p[r2:r2 + tq] * (lam * il[r2:r2 + tq])).astype(BF16))
        o = _pair_select(_pv(jnp.concatenate(pn, axis=0), vt))
        o2 = o * o
        ms_e = jnp.sum(jnp.where(lane < DF_HD, o2, 0.0), axis=-1, keepdims=True)
        ms_o = jnp.sum(jnp.where(lane >= DF_HD, o2, 0.0), axis=-1, keepdims=True)
        r = lax.rsqrt(jnp.where(lane < DF_HD, ms_e, ms_o) * (1.0 / DF_HD) + EPS)
        outs.append(o * r * (g_sub[:, t * LANES:(t + 1) * LANES] * out_scale))
    return jnp.concatenate(outs, axis=1)


def _mod_kernel(c_ref, w_ref, b_ref, o_ref):
    c = c_ref[...]
    s = c * jax.nn.sigmoid(c)
    o_ref[...] = _dot(s.astype(BF16), w_ref[...].astype(BF16)) + b_ref[...]


def _ctx_kernel(lam_init, n_alias, n_after, x_ref, mod_ref, vec_ref, wa_ref, wb_ref, w_uq_ref, w_ukv_ref, bd64_ref,
                bd32_ref, *rest):
    ffn_f32, rest = rest[:N_FFN_W], rest[N_FFN_W + n_alias + n_after:]
    mix_ref, *cache_refs = rest[:1 + N_CACHE]
    for src, dst in zip(ffn_f32, rest[1 + N_CACHE:]):
        dst[...] = src[...].astype(BF16)
    if n_alias == 0:
        for ref in cache_refs:
            ref[:, 1:] = jnp.zeros((ref.shape[0], ref.shape[1] - 1) + ref.shape[2:], F32)
        cache_refs = [ref.at[:, 0] for ref in cache_refs]
    ckv_ref, kr_ref, nak_ref, nav_ref, dfk_ref, dfv_ref = cache_refs
    lam = _diff_lambda(vec_ref, lam_init)
    mod = mod_ref[...]
    bf = lambda a: a.astype(BF16)
    seg = lambda a: [(a.astype(BF16), False)]
    def step(bounds):
        b_mla, b_na, b_df = bounds
        for s in range(CTX_SEQS_PER_STEP):
            rows = slice(s * SEQ, (s + 1) * SEQ)
            q_mla, ckv_n, kr_tile, q_na, k_na, v_na, q_df, k_df, v_df = _mixer_front(
                x_ref[rows, :], mod, vec_ref, wa_ref, wb_ref, w_uq_ref, bd64_ref, bd32_ref)
            k_mla, v_mla = _mla_kv(ckv_n, kr_tile, vec_ref, w_ukv_ref)
            ckv_ref[s] = ckv_n
            kr_ref[s] = kr_tile.T[0:MLA_ROPE]
            for ref, a in ((nak_ref, k_na), (nav_ref, v_na), (dfk_ref, k_df), (dfv_ref, v_df)):
                at = a.T
                for h in range(ref.shape[1]):
                    ref[s, h] = at[h * NA_HD:(h + 1) * NA_HD]
            o_mla = _mla_attend(bf(q_mla), seg(k_mla), seg(v_mla), b_mla)
            o_na = _na_attend_full(bf(q_na), seg(k_na), seg(v_na), b_na)
            o_df = _df_attend(bf(q_df), seg(k_df), seg(v_df), lam, vec_ref[V_GDS:V_GDS + 1, 0:DF_W], 1.0 - lam_init,
                              b_df)
            mix_ref[rows, :] = jnp.concatenate([o_mla, o_na, o_df], axis=1).astype(BF16)

    _with_score_bounds(vec_ref, step)


def _finish_kernel(n_ctx_blocks, xc_ref, mixc_ref, xl_ref, mixl_ref, mod_ref, vec_ref, w_out_ref, w_gate_ref, w_up_ref,
                   w_down_ref, yc_ref, yl_ref):
    def block(x_ref, mix_ref, y_ref):
        mod = mod_ref[...]
        gate_m = mod[:, 2 * D_MODEL:3 * D_MODEL]
        sh = mod[:, 3 * D_MODEL:4 * D_MODEL]
        sc = mod[:, 4 * D_MODEL:5 * D_MODEL]
        gate_f = mod[:, 5 * D_MODEL:6 * D_MODEL]
        x1 = x_ref[...] + gate_m * _dot(mix_ref[...], w_out_ref[...])
        h = (_rms_rows(x1, vec_ref[V_GFFN:V_GFFN + 1, :]) * (1.0 + sc) + sh).astype(BF16)
        g = _dot(h, w_gate_ref[...])
        u = _dot(h, w_up_ref[...])
        a = (g * jax.nn.sigmoid(g) * u).astype(BF16)
        y_ref[...] = x1 + gate_f * _dot(a, w_down_ref[...])

    is_ctx = pl.program_id(0) < n_ctx_blocks
    pl.when(is_ctx)(lambda: block(xc_ref, mixc_ref, yc_ref))
    pl.when(jnp.logical_not(is_ctx))(lambda: block(xl_ref, mixl_ref, yl_ref))


def _lat_front_kernel(x_ref, mod_ref, vec_ref, rope_ref, wa_ref, wb_ref, w_uq_ref, w_ukv_ref, bd64_ref, bd32_ref,
                      q_ref, k_ref, v_ref):
    q_mla, ckv_n, kr_tile, q_na, k_na, v_na, q_df, k_df, v_df = _mixer_front(
        x_ref[...], mod_ref[...], vec_ref, wa_ref, wb_ref, w_uq_ref, bd64_ref, bd32_ref)
    k_mla, v_mla = _mla_kv(ckv_n, kr_tile, vec_ref, w_ukv_ref)
    cm, sam, sbm = rope_ref[0], rope_ref[1], rope_ref[2]
    cd, sad, sbd = rope_ref[3], rope_ref[4], rope_ref[5]
    q_ref[...] = jnp.concatenate(
        [_rope_tiles(q_mla, cm, sam, sbm), q_na, _rope_tiles(q_df, cd, sad, sbd)], axis=1).astype(BF16)
    k_ref[...] = jnp.concatenate(
        [_rope_tiles(k_mla, cm, sam, sbm), k_na, _rope_tiles(k_df, cd, sad, sbd)], axis=1).astype(BF16)
    v_ref[...] = jnp.concatenate([v_mla, v_na, v_df], axis=1).astype(BF16)


def _ctx_kv_kernel(ckv_ref, krt_ref, vec_ref, w_ukv_ref, k_ref, v_ref):
    krt = jnp.concatenate([jnp.zeros((MLA_NOPE, PAST_LEN), F32), krt_ref[...],
                           jnp.zeros((LANES - MLA_QK, PAST_LEN), F32)], axis=0)
    k_mla, v_mla = _mla_kv(ckv_ref[...], krt.T, vec_ref, w_ukv_ref)
    k_ref[...] = k_mla.astype(BF16)
    v_ref[...] = v_mla.astype(BF16)


def _na_lat_kernel(q_ref, kl_ref, vl_ref, kct_ref, vct_ref, g_ref, vec_ref, o_ref, bias_ref):
    b = pl.program_id(0)
    j = pl.program_id(1)

    @pl.when((b == 0) & (j == 0))
    def _build_bias():
        c = lax.broadcasted_iota(jnp.int32, (GRID_W, LANES), 0)
        kc = _lane_iota((GRID_W, LANES)) % GRID_W
        start = jnp.clip(c - NA_KW // 2, 0, GRID_W - NA_KW)
        in_win = (kc >= start) & (kc < start + NA_KW)

        def body(i, carry):
            row = jnp.broadcast_to(g_ref[pl.ds(i, 1), :], (GRID_W, LANES))
            toep = pltpu.roll(row, 0, 1, stride=1, stride_axis=0)
            bias_ref[i] = jnp.where(in_win, toep * LOG2E, NEG)
            return carry

        lax.fori_loop(0, NA_HEADS * NA_PAIR_TILES, body, 0)

    n_win = NA_KR * GRID_W
    kct_f32 = kct_ref[...]
    kct = kct_f32.astype(BF16)
    vct = vct_ref[...].astype(BF16)

    def win_start(r):
        return jnp.clip(r - NA_KR // 2, 0, GRID_ROWS - NA_KR)

    def attend(grid_rows, rs, bound):
        a0, n = grid_rows[0], len(grid_rows)
        row0 = pl.multiple_of(rs * GRID_W, GRID_W)
        outs = []
        for t in range(NA_HEADS // 2):
            lanes = slice(t * LANES, (t + 1) * LANES)
            bias = jnp.concatenate(
                [jnp.concatenate([bias_ref[h * NA_PAIR_TILES + (rs - (j * NA_ROWS_PER_STEP + a) + NA_KR - 1) + 2 * m]
                                  for m in range(NA_KR // 2)], axis=1)
                 for h in (2 * t, 2 * t + 1) for a in grid_rows], axis=0)
            outs.append(_na_tile_attend(
                q_ref[a0 * GRID_W:(a0 + n) * GRID_W, lanes],
                [(kl_ref[pl.ds(row0, n_win), lanes], False), (kct[lanes, :], True)],
                [(vl_ref[pl.ds(row0, n_win), lanes], False), (vct[lanes, :], True)], bias, bound))
        return jnp.concatenate(outs, axis=1).astype(BF16)

    first_r, last_r = j * NA_ROWS_PER_STEP, (j + 1) * NA_ROWS_PER_STEP - 1
    shared = win_start(first_r) == win_start(last_r)

    def attend_step(bounds):
        bound = bounds[1]

        @pl.when(shared)
        def _shared_window():
            o_ref[...] = attend(tuple(range(NA_ROWS_PER_STEP)), win_start(first_r), bound)

        @pl.when(jnp.logical_not(shared))
        def _per_row_windows():
            for a in range(NA_ROWS_PER_STEP):
                o_ref[a * GRID_W:(a + 1) * GRID_W, :] = attend((a,), win_start(first_r + a), bound)

    bias_max = jnp.max(jnp.max(jnp.abs(g_ref[...]), axis=-1, keepdims=True), axis=0, keepdims=True)
    _with_score_bounds(vec_ref, attend_step, bias_max=bias_max, na_key_sq=_max_sq_norm(kct_f32, NA_HD))


def _lat_attn_kernel(lam_init, q_ref, kl_ref, vl_ref, kc_ref, vc_ref, kct_ref, vct_ref, ona_ref, vec_ref, mix_ref):
    lam = _diff_lambda(vec_ref, lam_init)
    kct = kct_ref[...]

    def attend(bounds):
        b_mla, _, b_df = bounds
        o_mla = _mla_attend(q_ref[:, P_MLA:P_MLA + W_MLA_P],
                            [(kc_ref[...], False), (kl_ref[:, P_MLA:P_MLA + W_MLA_P], False)],
                            [(vc_ref[...], False), (vl_ref[:, PV_MLA:PV_MLA + MLA_W], False)], b_mla)
        o_df = _df_attend(q_ref[:, P_DF:P_DF + DF_W],
                          [(kct.astype(BF16), True), (kl_ref[:, P_DF:P_DF + DF_W], False)],
                          [(vct_ref[...].astype(BF16), True), (vl_ref[:, PV_DF:PV_DF + DF_W], False)],
                          lam, vec_ref[V_GDS:V_GDS + 1, 0:DF_W], 1.0 - lam_init, b_df)
        mix_ref[...] = jnp.concatenate([o_mla.astype(BF16), ona_ref[...], o_df.astype(BF16)], axis=1)

    _with_score_bounds(vec_ref, attend, df_key_sq=_max_sq_norm(kct, DF_QK))


def _const_spec(shape):
    nd = len(shape)
    return pl.BlockSpec(shape, lambda *_: (0,) * nd, pipeline_mode=pl.Buffered(1))


def _layer_spec(l, shape):
    nd = len(shape)
    return pl.BlockSpec((None,) + tuple(shape), lambda *_: (l,) + (0,) * nd, pipeline_mode=pl.Buffered(1))


def _params(n_axes):
    return pltpu.CompilerParams(dimension_semantics=("arbitrary",) * n_axes, vmem_limit_bytes=VMEM_LIMIT)


def _block_diag(group):
    i = np.arange(MXU_DIM) // group
    return jnp.asarray((i[:, None] == i[None, :]).astype(np.float32), dtype=BF16)


def _rope_tables():
    t = np.arange(DEC_SEQ)
    row = (t // GRID_W).astype(np.float64)
    col = (t % GRID_W).astype(np.float64)
    n = MLA_ROPE // 4
    inv = 1.0 / (ROPE_BASE ** (np.arange(n, dtype=np.float64) * 2.0 / (MLA_ROPE // 2)))
    ar = row[:, None] * inv
    ac = col[:, None] * inv
    ang = np.concatenate([ar, ar, ac, ac], axis=-1)
    cos32, sin32 = np.cos(ang), np.sin(ang)
    first = (np.arange(MLA_ROPE) % (2 * n)) < n
    sa32 = np.where(first, -sin32, 0.0)
    sb32 = np.where(first, 0.0, sin32)

    def mla_tile(v32, fill):
        out = np.full((DEC_SEQ, LANES), fill)
        out[:, MLA_NOPE:MLA_QK] = v32
        return out

    tabs = [mla_tile(cos32, 1.0), mla_tile(sa32, 0.0), mla_tile(sb32, 0.0),
            np.tile(cos32, (1, LANES // DF_QK)), np.tile(sa32, (1, LANES // DF_QK)), np.tile(sb32, (1, LANES // DF_QK))]
    return jnp.asarray(np.stack(tabs).astype(np.float32))


def _feature_major(a):
    a = jnp.swapaxes(a, -1, -2)
    return a.reshape(a.shape[:-3] + (a.shape[-3] * a.shape[-2], a.shape[-1]))


def _tables(p):
    w_in_t = jnp.swapaxes(p['w_in'], 1, 2)
    kr = w_in_t[:, Q_LORA + KV_LORA:B_SRC]
    z32 = jnp.zeros_like(kr)
    wa = jnp.concatenate([w_in_t[:, :Q_LORA + KV_LORA], kr, z32, kr, z32], axis=1).astype(BF16)
    wb = w_in_t[:, B_SRC:].astype(BF16)
    w_uq_t = jnp.swapaxes(p['w_uq'], 1, 2).reshape(DEPTH, MLA_HEADS, MLA_QK, Q_LORA)
    w_uq_p = jnp.pad(w_uq_t, ((0, 0), (0, 0), (0, MLA_PAD - MLA_QK), (0, 0))).reshape(DEPTH, W_MLA_P, Q_LORA).astype(BF16)
    w_ukv = p['w_ukv'].reshape(DEPTH, KV_LORA, MLA_HEADS, MLA_NOPE + MLA_V)
    wk = jnp.pad(w_ukv[..., :MLA_NOPE], ((0, 0), (0, 0), (0, 0), (0, MLA_PAD - MLA_NOPE))).reshape(DEPTH, KV_LORA, W_MLA_P)
    wv = w_ukv[..., MLA_NOPE:].reshape(DEPTH, KV_LORA, MLA_W)
    w_ukv_r = jnp.concatenate([wk, wv], axis=2).astype(BF16)

    def row(v, reps=1):
        v = jnp.tile(v, (1, reps)) if reps > 1 else v
        return [v, jnp.zeros((DEPTH, D_MODEL - v.shape[1]), F32)] if v.shape[1] < D_MODEL else [v]

    pad_head = lambda g: jnp.pad(g, ((0, 0), (0, MLA_PAD - MLA_QK)))
    pieces = (row(p['g_mix']) + row(p['g_ffn']) + row(p['g_qa']) + row(p['g_kva'])
              + row(pad_head(p['g_mla_q']), MLA_HEADS) + row(pad_head(p['g_mla_k']), MLA_HEADS)
              + row(p['g_na_q'], NA_HEADS) + row(p['g_na_k'], NA_HEADS)
              + row(p['g_df_q'], 2 * DF_HEADS) + row(p['g_df_k'], 2 * DF_HEADS) + row(p['g_df_sub'], DF_HEADS)
              + row(p['df_lq1']) + row(p['df_lk1']) + row(p['df_lq2']) + row(p['df_lk2'])
              + [jnp.zeros((DEPTH, D_MODEL), F32)])
    vec = jnp.concatenate(pieces, axis=1).reshape(DEPTH, N_VEC, D_MODEL)
    f = p['na_rpb']
    zpad = jnp.zeros((DEPTH, NA_HEADS, NA_PAIR_TILES, 33), F32)
    g_rows = jnp.concatenate([f[:, :, :-1, NA_KW - 1:], zpad, f[:, :, 1:, :], zpad, f[:, :, :-1, :NA_KW - 1]], axis=-1)
    g_rows = g_rows.reshape(DEPTH, NA_HEADS * NA_PAIR_TILES, LANES)
    return dict(wa=wa, wb=wb, w_uq=w_uq_p, w_ukv=w_ukv_r, vec=vec, g_rows=g_rows)


def _modulation(c_all, w_mod, b_mod):
    tn = 2048
    return pl.pallas_call(
        _mod_kernel,
        grid=(DEPTH, 6 * D_MODEL // tn),
        in_specs=[pl.BlockSpec((N_MOD, D_MODEL), lambda l, j: (0, 0)),
                  pl.BlockSpec((None, D_MODEL, tn), lambda l, j: (l, 0, j)),
                  pl.BlockSpec((None, 1, tn), lambda l, j: (l, 0, j))],
        out_specs=pl.BlockSpec((None, N_MOD, tn), lambda l, j: (l, 0, j)),
        out_shape=jax.ShapeDtypeStruct((DEPTH, N_MOD, 6 * D_MODEL), F32),
        compiler_params=_params(2),
        name="modulation",
    )(c_all, w_mod, b_mod.reshape(DEPTH, 1, 6 * D_MODEL))


def _front_weight_specs(l):
    return [_layer_spec(l, (N_VEC, D_MODEL)), _layer_spec(l, (A_COLS, D_MODEL)), _layer_spec(l, (B_COLS, D_MODEL)),
            _layer_spec(l, (W_MLA_P, Q_LORA)), _layer_spec(l, (KV_LORA, W_MLA_P + MLA_W)),
            _const_spec((MXU_DIM, MXU_DIM)), _const_spec((MXU_DIM, MXU_DIM))]


def _front_weights(t, consts):
    return (t['vec'], t['wa'], t['wb'], t['w_uq'], t['w_ukv'], consts['bd64'], consts['bd32'])


def _context_mixer(l, lam_init, x, mods, t, consts, ffn_f32, prev_caches, after=()):
    n_tok = BATCH * SEQ
    n_alias = len(prev_caches)
    tok = lambda w: pl.BlockSpec((CTX_SEQS_PER_STEP * SEQ, w), lambda b: (b, 0))
    if n_alias == 0:
        assert l == 0
        lay = lambda *s: pl.BlockSpec((CTX_SEQS_PER_STEP, DEPTH) + s, lambda b: (b, 0) + (0,) * len(s))
    else:
        lay = lambda *s: pl.BlockSpec((CTX_SEQS_PER_STEP, None) + s, lambda b: (b, l) + (0,) * len(s))
    cache_shapes = [(SEQ, KV_LORA), (MLA_ROPE, SEQ), (NA_HEADS, NA_HD, SEQ), (NA_HEADS, NA_HD, SEQ),
                    (DF_HEADS, DF_HD, SEQ), (DF_HEADS, DF_HD, SEQ)]
    weights = _front_weights(t, consts)
    steps = BATCH // CTX_SEQS_PER_STEP
    ffn_chunks = [(w.shape[1] // steps, w.shape[2]) for w in ffn_f32]
    n_in = 2 + len(weights) + len(ffn_f32)
    mix, *outs = pl.pallas_call(
        functools.partial(_ctx_kernel, lam_init, n_alias, len(after)),
        grid=(steps,),
        in_specs=[tok(D_MODEL), pl.BlockSpec((None, 1, 6 * D_MODEL), lambda b: (l * N_MOD, 0, 0))]
        + _front_weight_specs(l) + [pl.BlockSpec((None,) + c, lambda b: (l, b, 0)) for c in ffn_chunks]
        + [pl.BlockSpec(memory_space=pl.ANY)] * (n_alias + len(after)),
        out_specs=[tok(D_MODEL)] + [lay(*s) for s in cache_shapes] + [pl.BlockSpec(c, lambda b: (b, 0)) for c in ffn_chunks],
        out_shape=[jax.ShapeDtypeStruct((n_tok, D_MODEL), BF16)]
        + [jax.ShapeDtypeStruct((BATCH, DEPTH) + s, F32) for s in cache_shapes]
        + [jax.ShapeDtypeStruct(w.shape[1:], BF16) for w in ffn_f32],
        input_output_aliases={n_in + i: 1 + i for i in range(n_alias)},
        compiler_params=_params(1),
        name="context_mixer",
    )(x, mods, *weights, *ffn_f32, *prev_caches, *after)
    return mix, outs[:N_CACHE], outs[N_CACHE:]


def _finish(l, xc, mixc, xl, mixl, mods, t, ffn_bf16):
    nc, nl = xc.shape[0] // TM_FINISH, xl.shape[0] // TM_FINISH
    blocks_per_seq = DEC_SEQ // TM_FINISH
    ctx_tok = pl.BlockSpec((TM_FINISH, D_MODEL), lambda i: (jnp.minimum(i, nc - 1), 0))
    lat_tok = pl.BlockSpec((TM_FINISH, D_MODEL), lambda i: (jnp.maximum(i - nc, 0), 0))
    mod_row = lambda i: jnp.where(i < nc, 0, 1 + jnp.maximum(i - nc, 0) // blocks_per_seq)
    return pl.pallas_call(
        functools.partial(_finish_kernel, nc),
        grid=(nc + nl,),
        in_specs=[ctx_tok, ctx_tok, lat_tok, lat_tok,
                  pl.BlockSpec((None, 1, 6 * D_MODEL), lambda i: (l * N_MOD + mod_row(i), 0, 0)),
                  _layer_spec(l, (N_VEC, D_MODEL))] + [_const_spec(w.shape) for w in ffn_bf16],
        out_specs=[ctx_tok, lat_tok],
        out_shape=[jax.ShapeDtypeStruct(xc.shape, F32), jax.ShapeDtypeStruct(xl.shape, F32)],
        compiler_params=_params(1),
        name="finish",
    )(xc, mixc, xl, mixl, mods, t['vec'], *ffn_bf16)


def _latent_front(l, x, mods, t, consts):
    n_tok = DEC_BATCH * DEC_SEQ
    tm = TM_LAT_FRONT
    blocks_per_seq = DEC_SEQ // tm
    tok = lambda w: pl.BlockSpec((tm, w), lambda i: (i, 0))
    wspecs = _front_weight_specs(l)
    weights = _front_weights(t, consts)
    return pl.pallas_call(
        _lat_front_kernel,
        grid=(n_tok // tm,),
        in_specs=[tok(D_MODEL),
                  pl.BlockSpec((None, 1, 6 * D_MODEL), lambda i: (l * N_MOD + 1 + i // blocks_per_seq, 0, 0)),
                  wspecs[0], pl.BlockSpec((6, tm, LANES), lambda i: (0, i % blocks_per_seq, 0))] + wspecs[1:],
        out_specs=[tok(QK_PACK), tok(QK_PACK), tok(V_PACK)],
        out_shape=[jax.ShapeDtypeStruct((n_tok, QK_PACK), BF16), jax.ShapeDtypeStruct((n_tok, QK_PACK), BF16),
                   jax.ShapeDtypeStruct((n_tok, V_PACK), BF16)],
        compiler_params=_params(1),
        name="latent_front",
    )(x, mods, weights[0], consts['rope'], *weights[1:])


def _context_kv(l, ckv, krope_t, t):
    lay = lambda r, w: pl.BlockSpec((None, None, r, w), lambda b: (b, l, 0, 0))
    out = lambda w: pl.BlockSpec((None, PAST_LEN, w), lambda b: (b, 0, 0))
    return pl.pallas_call(
        _ctx_kv_kernel,
        grid=(DEC_BATCH,),
        in_specs=[lay(PAST_LEN, KV_LORA), lay(MLA_ROPE, PAST_LEN),
                  _layer_spec(l, (N_VEC, D_MODEL)), _layer_spec(l, (KV_LORA, W_MLA_P + MLA_W))],
        out_specs=[out(W_MLA_P), out(MLA_W)],
        out_shape=[jax.ShapeDtypeStruct((DEC_BATCH, PAST_LEN, W_MLA_P), BF16),
                   jax.ShapeDtypeStruct((DEC_BATCH, PAST_LEN, MLA_W), BF16)],
        compiler_params=_params(1),
        name="context_kv",
    )(ckv, krope_t, t['vec'], t['w_ukv'])


def _latent_na(l, q, k, v, kct, vct, g_rows, vec):
    n_tok = DEC_BATCH * DEC_SEQ
    na_blk = P_NA // NA_W
    steps = GRID_ROWS // NA_ROWS_PER_STEP
    tq = NA_ROWS_PER_STEP * GRID_W
    ctx = pl.BlockSpec((None, None, NA_W, PAST_LEN), lambda b, j: (b, l, 0, 0))
    return pl.pallas_call(
        _na_lat_kernel,
        grid=(DEC_BATCH, steps),
        in_specs=[pl.BlockSpec((tq, NA_W), lambda b, j: (b * steps + j, na_blk)),
                  pl.BlockSpec((DEC_SEQ, NA_W), lambda b, j: (b, na_blk)),
                  pl.BlockSpec((DEC_SEQ, NA_W), lambda b, j: (b, PV_NA // NA_W)),
                  ctx, ctx, _layer_spec(l, (NA_HEADS * NA_PAIR_TILES, LANES)), _layer_spec(l, (N_VEC, D_MODEL))],
        out_specs=pl.BlockSpec((tq, NA_W), lambda b, j: (b * steps + j, 0)),
        out_shape=jax.ShapeDtypeStruct((n_tok, NA_W), BF16),
        scratch_shapes=[pltpu.VMEM((NA_HEADS * NA_PAIR_TILES, GRID_W, LANES), F32)],
        compiler_params=_params(2),
        name="latent_neighbourhood",
    )(q, k, v, kct, vct, g_rows, vec)


def _latent_attention(l, lam_init, q, k, v, kc, vc, kct, vct, o_na, t):
    n_tok = DEC_BATCH * DEC_SEQ
    nq = DEC_SEQ // TM
    ctx = pl.BlockSpec((None, None, DF_W, PAST_LEN), lambda b, j: (b, l, 0, 0))
    return pl.pallas_call(
        functools.partial(_lat_attn_kernel, lam_init),
        grid=(DEC_BATCH, nq),
        in_specs=[pl.BlockSpec((TM, QK_PACK), lambda b, j: (b * nq + j, 0)),
                  pl.BlockSpec((DEC_SEQ, QK_PACK), lambda b, j: (b, 0)),
                  pl.BlockSpec((DEC_SEQ, V_PACK), lambda b, j: (b, 0)),
                  pl.BlockSpec((None, PAST_LEN, W_MLA_P), lambda b, j: (b, 0, 0)),
                  pl.BlockSpec((None, PAST_LEN, MLA_W), lambda b, j: (b, 0, 0)),
                  ctx, ctx,
                  pl.BlockSpec((TM, NA_W), lambda b, j: (b * nq + j, 0)),
                  _layer_spec(l, (N_VEC, D_MODEL))],
        out_specs=pl.BlockSpec((TM, D_MODEL), lambda b, j: (b * nq + j, 0)),
        out_shape=jax.ShapeDtypeStruct((n_tok, D_MODEL), BF16),
        compiler_params=_params(2),
        name="latent_attention",
    )(q, k, v, kc, vc, kct, vct, o_na, t['vec'])


def kernel(x_prompt, x_sample, cache_mla_ckv, cache_mla_krope, cache_na_k, cache_na_v, cache_df_k, cache_df_v, c, c_ctx, w_mod, b_mod, g_mix, w_in, g_qa, w_uq, g_kva, w_ukv, g_mla_q, g_mla_k, g_na_q, g_na_k, na_rpb, g_df_q, g_df_k, df_lq1, df_lk1, df_lq2, df_lk2, g_df_sub, w_out, g_ffn, w_gate, w_up, w_down):
    p = dict(g_mix=g_mix, w_in=w_in, g_qa=g_qa, w_uq=w_uq, g_kva=g_kva, w_ukv=w_ukv, g_mla_q=g_mla_q, g_mla_k=g_mla_k,
             g_na_q=g_na_q, g_na_k=g_na_k, na_rpb=na_rpb, g_df_q=g_df_q, g_df_k=g_df_k, df_lq1=df_lq1, df_lk1=df_lk1,
             df_lq2=df_lq2, df_lk2=df_lk2, g_df_sub=g_df_sub, w_out=w_out, g_ffn=g_ffn, w_gate=w_gate, w_up=w_up,
             w_down=w_down)
    consts = dict(bd64=_block_diag(NA_HD), bd32=_block_diag(DF_QK), rope=_rope_tables())
    t = _tables(p)

    c_all = jnp.concatenate([c_ctx[None, :], c, jnp.zeros((N_MOD - 1 - DEC_BATCH, D_MODEL), F32)], axis=0)
    mods = _modulation(c_all, w_mod, b_mod).reshape(DEPTH * N_MOD, 1, 6 * D_MODEL)

    krope_t = jnp.swapaxes(cache_mla_krope, -1, -2)
    na_kt, na_vt, df_kt, df_vt = (_feature_major(a) for a in (cache_na_k, cache_na_v, cache_df_k, cache_df_v))

    xp = x_prompt.reshape(BATCH * SEQ, D_MODEL)
    xs = x_sample.reshape(DEC_BATCH * DEC_SEQ, D_MODEL)
    new_caches = ()
    ffn_f32 = (w_out, w_gate, w_up, w_down)
    for l in range(DEPTH):
        lam_init = 0.8 - 0.6 * math.exp(-0.3 * l)
        q, k, v = _latent_front(l, xs, mods, t, consts)
        kc, vc = _context_kv(l, cache_mla_ckv, krope_t, t)
        o_na = _latent_na(l, q, k, v, na_kt, na_vt, t['g_rows'], t['vec'])
        mix_s = _latent_attention(l, lam_init, q, k, v, kc, vc, df_kt, df_vt, o_na, t)
        mix, new_caches, ffn_bf16 = _context_mixer(l, lam_init, xp, mods, t, consts, ffn_f32, new_caches, after=(mix_s,))
        xp, xs = _finish(l, xp, mix, xs, mix_s, mods, t, ffn_bf16)
    ckv_new, *narrow = new_caches
    return (xp.reshape(BATCH, SEQ, D_MODEL), xs.reshape(DEC_BATCH, DEC_SEQ, D_MODEL), ckv_new,
            *(jnp.swapaxes(a, -1, -2) for a in narrow))
```

```python
import functools
import math

import numpy as np
import jax
import jax.numpy as jnp
from jax import lax
from jax.experimental import pallas as pl
from jax.experimental.pallas import tpu as pltpu

F32 = jnp.float32
BF16 = jnp.bfloat16

D_MODEL = 1024
BATCH = 32
SEQ = 256
DEPTH = 2
DEC_BATCH = 2
DEC_SEQ = 1024
PAST_LEN = 256
GRID_W = 64
GRID_ROWS = DEC_SEQ // GRID_W
MLA_HEADS = 6
MLA_NOPE = 64
MLA_ROPE = 32
MLA_QK = MLA_NOPE + MLA_ROPE
MLA_V = 64
MLA_PAD = 128
Q_LORA = 256
KV_LORA = 128
NA_HEADS = 6
NA_HD = 64
NA_KR = 8
NA_KW = 16
DF_HEADS = 4
DF_HD = 64
DF_QK = 32
MLA_W = MLA_HEADS * MLA_V
NA_W = NA_HEADS * NA_HD
DF_W = DF_HEADS * DF_HD
D_FF = -(-8 * D_MODEL // (3 * 256)) * 256
ROPE_BASE = 10000.0
EPS = 1e-6
NEG = -1e30
LOG2E = math.log2(math.e)
MAX_FREE_SOFTMAX_BOUND = 60.0

LANES = 128
MXU_DIM = 256

A_CQ = 0
A_CKV = A_CQ + Q_LORA
A_KR = A_CKV + KV_LORA
A_COLS = A_KR + LANES
B_SRC = Q_LORA + KV_LORA + MLA_ROPE
B_NAQ = 0
B_NAK = B_NAQ + NA_W
B_NAV = B_NAK + NA_W
B_DFQ = B_NAV + NA_W
B_DFK = B_DFQ + DF_W
B_DFV = B_DFK + DF_W
B_COLS = B_DFV + DF_W

W_MLA_P = MLA_HEADS * MLA_PAD
QK_PACK = W_MLA_P + NA_W + DF_W
V_PACK = MLA_W + NA_W + DF_W
P_MLA = 0
P_NA = W_MLA_P
P_DF = P_NA + NA_W
PV_MLA = 0
PV_NA = MLA_W
PV_DF = MLA_W + NA_W

(V_GMIX, V_GFFN, V_GQA, V_GKVA, V_GMQ, V_GMK, V_GNQ, V_GNK, V_GDQ, V_GDK, V_GDS,
 V_LQ1, V_LK1, V_LQ2, V_LK2) = range(15)
N_VEC = 16

N_MOD = 8
TM = 512
EXACT_ROWS = 128
TM_LAT_FRONT = 256
TM_FINISH = 512
CTX_SEQS_PER_STEP = 2
NA_ROWS_PER_STEP = 4
VMEM_LIMIT = 56 * 1024 * 1024

NA_PAIR_TILES = 2 * NA_KR - 2
N_CACHE = 6
N_FFN_W = 4


def _dot(a, b):
    return jnp.dot(a, b, preferred_element_type=F32)


def _dot_nt(a, b):
    return lax.dot_general(a, b, (((1,), (1,)), ((), ())), preferred_element_type=F32)


def _lane_iota(shape):
    return lax.broadcasted_iota(jnp.int32, shape, len(shape) - 1)


def _rms_rows(x, g):
    ms = jnp.mean(x * x, axis=-1, keepdims=True)
    return x * lax.rsqrt(ms + EPS) * g


def _tile_rms(x, g, n_real):
    outs = []
    for c0 in range(0, x.shape[1], LANES):
        xt = x[:, c0:c0 + LANES]
        ms = jnp.sum(xt * xt, axis=-1, keepdims=True) * (1.0 / n_real)
        outs.append(xt * lax.rsqrt(ms + EPS) * g[:, c0:c0 + LANES])
    return jnp.concatenate(outs, axis=1)


def _seg_rms(x, bd_ref, g, group):
    width = x.shape[1]
    sq = (x * x).astype(BF16)
    parts = []
    for c0 in range(0, width, MXU_DIM):
        w = min(MXU_DIM, width - c0)
        parts.append(_dot(sq[:, c0:c0 + w], bd_ref[0:w, 0:w]))
    ss = parts[0] if len(parts) == 1 else jnp.concatenate(parts, axis=1)
    return x * lax.rsqrt(ss * (1.0 / group) + EPS) * g


def _rope_tiles(x, cos, sa, sb):
    outs = []
    for t in range(x.shape[1] // LANES):
        xt = x[:, t * LANES:(t + 1) * LANES]
        up = pltpu.roll(xt, LANES - MLA_ROPE // 4, 1)
        dn = pltpu.roll(xt, MLA_ROPE // 4, 1)
        outs.append(xt * cos + up * sa + dn * sb)
    return outs[0] if len(outs) == 1 else jnp.concatenate(outs, axis=1)


def _diff_lambda(vec_ref, lam_init):
    a = jnp.sum(vec_ref[V_LQ1:V_LQ1 + 1, 0:DF_QK] * vec_ref[V_LK1:V_LK1 + 1, 0:DF_QK], axis=-1, keepdims=True)
    b = jnp.sum(vec_ref[V_LQ2:V_LQ2 + 1, 0:DF_QK] * vec_ref[V_LK2:V_LK2 + 1, 0:DF_QK], axis=-1, keepdims=True)
    return jnp.exp(a) - jnp.exp(b) + lam_init


def _mixer_front(x, mod, vec_ref, wa_ref, wb_ref, w_uq_ref, bd64_ref, bd32_ref):
    sh = mod[:, 0:D_MODEL]
    sc = mod[:, D_MODEL:2 * D_MODEL]
    h = (_rms_rows(x, vec_ref[V_GMIX:V_GMIX + 1, :]) * (1.0 + sc) + sh).astype(BF16)
    za = _dot_nt(h, wa_ref[...])
    zb = _dot_nt(h, wb_ref[...])
    cqn = _rms_rows(za[:, A_CQ:A_CQ + Q_LORA], vec_ref[V_GQA:V_GQA + 1, 0:Q_LORA])
    q_raw = _dot_nt(cqn.astype(BF16), w_uq_ref[...])
    q_mla = _tile_rms(q_raw, vec_ref[V_GMQ:V_GMQ + 1, 0:W_MLA_P] * (MLA_QK ** -0.5 * LOG2E), MLA_QK)
    ckv_n = _rms_rows(za[:, A_CKV:A_CKV + KV_LORA], vec_ref[V_GKVA:V_GKVA + 1, 0:KV_LORA])
    kr_tile = za[:, A_KR:A_KR + LANES]
    q_na = _seg_rms(zb[:, B_NAQ:B_NAQ + NA_W], bd64_ref, vec_ref[V_GNQ:V_GNQ + 1, 0:NA_W] * (NA_HD ** -0.5 * LOG2E), NA_HD)
    k_na = _seg_rms(zb[:, B_NAK:B_NAK + NA_W], bd64_ref, vec_ref[V_GNK:V_GNK + 1, 0:NA_W], NA_HD)
    v_na = zb[:, B_NAV:B_NAV + NA_W]
    q_df = _seg_rms(zb[:, B_DFQ:B_DFQ + DF_W], bd32_ref, vec_ref[V_GDQ:V_GDQ + 1, 0:DF_W] * (DF_QK ** -0.5 * LOG2E), DF_QK)
    k_df = _seg_rms(zb[:, B_DFK:B_DFK + DF_W], bd32_ref, vec_ref[V_GDK:V_GDK + 1, 0:DF_W], DF_QK)
    v_df = zb[:, B_DFV:B_DFV + DF_W]
    return q_mla, ckv_n, kr_tile, q_na, k_na, v_na, q_df, k_df, v_df


def _mla_kv(ckv_n, kr_tile, vec_ref, w_ukv_ref):
    kv = _dot(ckv_n.astype(BF16), w_ukv_ref[...])
    lane = _lane_iota((1, LANES))
    kr = jnp.where((lane >= MLA_NOPE) & (lane < MLA_QK), kr_tile, 0.0)
    k_pre = kv[:, 0:W_MLA_P] + jnp.concatenate([kr] * MLA_HEADS, axis=1)
    k = _tile_rms(k_pre, vec_ref[V_GMK:V_GMK + 1, 0:W_MLA_P], MLA_QK)
    return k, kv[:, W_MLA_P:W_MLA_P + MLA_W]


def _softmax_parts(s, bound=None):
    m = jnp.max(s, axis=-1, keepdims=True) if bound is None else bound
    p = jnp.exp2(s - m)
    return p, 1.0 / jnp.sum(p, axis=-1, keepdims=True)


def _max_sq_norm(kt, group):
    sq = kt * kt
    best = None
    for r0 in range(0, kt.shape[0], group):
        n2 = jnp.sum(sq[r0:r0 + group], axis=0, keepdims=True)
        best = n2 if best is None else jnp.maximum(best, n2)
    return jnp.max(best, axis=-1, keepdims=True)


def _score_bounds(vec_ref, bias_max=None, na_key_sq=None, df_key_sq=None):
    gmax = lambda row, w: jnp.max(jnp.abs(vec_ref[row:row + 1, 0:w]), axis=-1, keepdims=True)
    slack = LOG2E * (1.0 + 2.0 ** -6)

    def key_norm(row, w, d, measured_sq):
        k = gmax(row, w) * d ** 0.5
        return k if measured_sq is None else jnp.maximum(k, jnp.sqrt(measured_sq))

    b_mla = gmax(V_GMQ, W_MLA_P) * key_norm(V_GMK, W_MLA_P, MLA_QK, None) * slack
    b_na = gmax(V_GNQ, NA_W) * key_norm(V_GNK, NA_W, NA_HD, na_key_sq) * slack
    if bias_max is not None:
        b_na = b_na + bias_max * LOG2E
    b_df = gmax(V_GDQ, DF_W) * key_norm(V_GDK, DF_W, DF_QK, df_key_sq) * slack
    worst = jnp.maximum(b_mla, jnp.maximum(b_na, b_df))
    return (b_mla, b_na, b_df), worst[0, 0] <= MAX_FREE_SOFTMAX_BOUND


EXACT_MAX = (None, None, None)


def _with_score_bounds(vec_ref, bounded, exact, **measured):
    bounds, ok = _score_bounds(vec_ref, **measured)
    pl.when(ok)(lambda: bounded(bounds))
    pl.when(jnp.logical_not(ok))(exact)


def _row_block(i, n):
    return slice(i * n, (i + 1) * n) if isinstance(i, int) else pl.ds(pl.multiple_of(i * n, n), n)


def _rolled(n, body):
    def step(i, carry):
        body(i)
        return carry

    lax.fori_loop(0, n, step, 0)


def _scores(q, k_segs):
    parts = [_dot(q, k) if feature_major else _dot_nt(q, k) for k, feature_major in k_segs]
    return parts[0] if len(parts) == 1 else jnp.concatenate(parts, axis=1)


def _pv(p, v_segs):
    out = None
    c0 = 0
    for v, feature_major in v_segs:
        n = v.shape[1] if feature_major else v.shape[0]
        o = _dot_nt(p[:, c0:c0 + n], v) if feature_major else _dot(p[:, c0:c0 + n], v)
        out = o if out is None else out + o
        c0 += n
    return out


def _seg_tile(seg, t):
    a, feature_major = seg
    return (a[t * LANES:(t + 1) * LANES, :] if feature_major else a[:, t * LANES:(t + 1) * LANES]), feature_major


def _lane_groups(qt, width):
    lane = _lane_iota((1, LANES))
    zero = jnp.zeros_like(qt)
    return jnp.concatenate(
        [jnp.where((lane >= g * width) & (lane < (g + 1) * width), qt, zero) for g in range(LANES // width)], axis=0)


def _pair_select(o2):
    tq = o2.shape[0] // 2
    return jnp.where(_lane_iota((1, LANES)) < NA_HD, o2[0:tq], o2[tq:2 * tq])


def _mla_attend(q, k_segs, v_segs, bound=None):
    outs = []
    for t in range(MLA_HEADS // 2):
        vt = [_seg_tile(v, t) for v in v_segs]
        halves = []
        for h in (2 * t, 2 * t + 1):
            p, il = _softmax_parts(_scores(q[:, h * MLA_PAD:(h + 1) * MLA_PAD], [_seg_tile(k, h) for k in k_segs]),
                                   bound)
            halves.append(_pv(p.astype(BF16), vt) * il)
        outs.append(jnp.where(_lane_iota((1, LANES)) < MLA_V, halves[0], halves[1]))
    return jnp.concatenate(outs, axis=1)


def _na_tile_attend(qt, k_segs, v_segs, bias=None, bound=None):
    s = _scores(_lane_groups(qt, NA_HD), k_segs)
    if bias is not None:
        nb = bias.shape[1]
        s = jnp.concatenate([s[:, 0:nb] + bias, s[:, nb:]], axis=1)
    p, il = _softmax_parts(s, bound)
    return _pair_select(_pv(p.astype(BF16), v_segs) * il)


def _na_attend_full(q, k_segs, v_segs, bound=None):
    return jnp.concatenate(
        [_na_tile_attend(q[:, t * LANES:(t + 1) * LANES], [_seg_tile(k, t) for k in k_segs],
                         [_seg_tile(v, t) for v in v_segs], bound=bound) for t in range(NA_HEADS // 2)], axis=1)


def _df_attend(q, k_segs, v_segs, lam, g_sub, out_scale, bound=None):
    outs = []
    lane = _lane_iota((1, LANES))
    tq = q.shape[0]
    for t in range(DF_HEADS // 2):
        kt = [_seg_tile(k, t) for k in k_segs]
        vt = [_seg_tile(v, t) for v in v_segs]
        p, il = _softmax_parts(_scores(_lane_groups(q[:, t * LANES:(t + 1) * LANES], DF_QK), kt), bound)
        pn = []
        for hh in range(2):
            r1, r2 = 2 * hh * tq, (2 * hh + 1) * tq
            pn.append((p[r1:r1 + tq] * il[r1:r1 + tq] - p[r2:r2 + tq] * (lam * il[r2:r2 + tq])).astype(BF16))
        o = _pair_select(_pv(jnp.concatenate(pn, axis=0), vt))
        o2 = o * o
        ms_e = jnp.sum(jnp.where(lane < DF_HD, o2, 0.0), axis=-1, keepdims=True)
        ms_o = jnp.sum(jnp.where(lane >= DF_HD, o2, 0.0), axis=-1, keepdims=True)
        r = lax.rsqrt(jnp.where(lane < DF_HD, ms_e, ms_o) * (1.0 / DF_HD) + EPS)
        outs.append(o * r * (g_sub[:, t * LANES:(t + 1) * LANES] * out_scale))
    return jnp.concatenate(outs, axis=1)


def _mod_kernel(c_ref, w_ref, b_ref, o_ref):
    c = c_ref[...]
    s = c * jax.nn.sigmoid(c)
    o_ref[...] = _dot(s.astype(BF16), w_ref[...].astype(BF16)) + b_ref[...]


def _ctx_kernel(lam_init, n_alias, n_after, x_ref, mod_ref, vec_ref, wa_ref, wb_ref, w_uq_ref, w_ukv_ref, bd64_ref,
                bd32_ref, *rest):
    ffn_f32, rest = rest[:N_FFN_W], rest[N_FFN_W + n_alias + n_after:]
    mix_ref, *cache_refs = rest[:1 + N_CACHE]
    for src, dst in zip(ffn_f32, rest[1 + N_CACHE:]):
        dst[...] = src[...].astype(BF16)
    if n_alias == 0:
        for ref in cache_refs:
            ref[:, 1:] = jnp.zeros((ref.shape[0], ref.shape[1] - 1) + ref.shape[2:], F32)
        cache_refs = [ref.at[:, 0] for ref in cache_refs]
    ckv_ref, kr_ref, nak_ref, nav_ref, dfk_ref, dfv_ref = cache_refs
    lam = _diff_lambda(vec_ref, lam_init)
    mod = mod_ref[...]
    bf = lambda a: a.astype(BF16)
    seg = lambda a: [(a.astype(BF16), False)]
    def sequence(s, bounds):
        b_mla, b_na, b_df = bounds
        rows = _row_block(s, SEQ)
        q_mla, ckv_n, kr_tile, q_na, k_na, v_na, q_df, k_df, v_df = _mixer_front(
            x_ref[rows, :], mod, vec_ref, wa_ref, wb_ref, w_uq_ref, bd64_ref, bd32_ref)
        k_mla, v_mla = _mla_kv(ckv_n, kr_tile, vec_ref, w_ukv_ref)
        ckv_ref[s] = ckv_n
        kr_ref[s] = kr_tile.T[0:MLA_ROPE]
        for ref, a in ((nak_ref, k_na), (nav_ref, v_na), (dfk_ref, k_df), (dfv_ref, v_df)):
            at = a.T
            for h in range(ref.shape[1]):
                ref[s, h] = at[h * NA_HD:(h + 1) * NA_HD]
        o_mla = _mla_attend(bf(q_mla), seg(k_mla), seg(v_mla), b_mla)
        o_na = _na_attend_full(bf(q_na), seg(k_na), seg(v_na), b_na)
        o_df = _df_attend(bf(q_df), seg(k_df), seg(v_df), lam, vec_ref[V_GDS:V_GDS + 1, 0:DF_W], 1.0 - lam_init, b_df)
        mix_ref[rows, :] = jnp.concatenate([o_mla, o_na, o_df], axis=1).astype(BF16)

    def bounded(bounds):
        for s in range(CTX_SEQS_PER_STEP):
            sequence(s, bounds)

    _with_score_bounds(vec_ref, bounded, lambda: _rolled(CTX_SEQS_PER_STEP, lambda s: sequence(s, EXACT_MAX)))


def _finish_kernel(n_ctx_blocks, xc_ref, mixc_ref, xl_ref, mixl_ref, mod_ref, vec_ref, w_out_ref, w_gate_ref, w_up_ref,
                   w_down_ref, yc_ref, yl_ref):
    def block(x_ref, mix_ref, y_ref):
        mod = mod_ref[...]
        gate_m = mod[:, 2 * D_MODEL:3 * D_MODEL]
        sh = mod[:, 3 * D_MODEL:4 * D_MODEL]
        sc = mod[:, 4 * D_MODEL:5 * D_MODEL]
        gate_f = mod[:, 5 * D_MODEL:6 * D_MODEL]
        x1 = x_ref[...] + gate_m * _dot(mix_ref[...], w_out_ref[...])
        h = (_rms_rows(x1, vec_ref[V_GFFN:V_GFFN + 1, :]) * (1.0 + sc) + sh).astype(BF16)
        g = _dot(h, w_gate_ref[...])
        u = _dot(h, w_up_ref[...])
        a = (g * jax.nn.sigmoid(g) * u).astype(BF16)
        y_ref[...] = x1 + gate_f * _dot(a, w_down_ref[...])

    is_ctx = pl.program_id(0) < n_ctx_blocks
    pl.when(is_ctx)(lambda: block(xc_ref, mixc_ref, yc_ref))
    pl.when(jnp.logical_not(is_ctx))(lambda: block(xl_ref, mixl_ref, yl_ref))


def _lat_front_kernel(x_ref, mod_ref, vec_ref, rope_ref, wa_ref, wb_ref, w_uq_ref, w_ukv_ref, bd64_ref, bd32_ref,
                      q_ref, k_ref, v_ref):
    q_mla, ckv_n, kr_tile, q_na, k_na, v_na, q_df, k_df, v_df = _mixer_front(
        x_ref[...], mod_ref[...], vec_ref, wa_ref, wb_ref, w_uq_ref, bd64_ref, bd32_ref)
    k_mla, v_mla = _mla_kv(ckv_n, kr_tile, vec_ref, w_ukv_ref)
    cm, sam, sbm = rope_ref[0], rope_ref[1], rope_ref[2]
    cd, sad, sbd = rope_ref[3], rope_ref[4], rope_ref[5]
    q_ref[...] = jnp.concatenate(
        [_rope_tiles(q_mla, cm, sam, sbm), q_na, _rope_tiles(q_df, cd, sad, sbd)], axis=1).astype(BF16)
    k_ref[...] = jnp.concatenate(
        [_rope_tiles(k_mla, cm, sam, sbm), k_na, _rope_tiles(k_df, cd, sad, sbd)], axis=1).astype(BF16)
    v_ref[...] = jnp.concatenate([v_mla, v_na, v_df], axis=1).astype(BF16)


def _ctx_kv_kernel(ckv_ref, krt_ref, vec_ref, w_ukv_ref, k_ref, v_ref):
    krt = jnp.concatenate([jnp.zeros((MLA_NOPE, PAST_LEN), F32), krt_ref[...],
                           jnp.zeros((LANES - MLA_QK, PAST_LEN), F32)], axis=0)
    k_mla, v_mla = _mla_kv(ckv_ref[...], krt.T, vec_ref, w_ukv_ref)
    k_ref[...] = k_mla.astype(BF16)
    v_ref[...] = v_mla.astype(BF16)


def _na_lat_kernel(q_ref, kl_ref, vl_ref, kct_ref, vct_ref, g_ref, vec_ref, o_ref, bias_ref):
    b = pl.program_id(0)
    j = pl.program_id(1)

    @pl.when((b == 0) & (j == 0))
    def _build_bias():
        c = lax.broadcasted_iota(jnp.int32, (GRID_W, LANES), 0)
        kc = _lane_iota((GRID_W, LANES)) % GRID_W
        start = jnp.clip(c - NA_KW // 2, 0, GRID_W - NA_KW)
        in_win = (kc >= start) & (kc < start + NA_KW)

        def body(i, carry):
            row = jnp.broadcast_to(g_ref[pl.ds(i, 1), :], (GRID_W, LANES))
            toep = pltpu.roll(row, 0, 1, stride=1, stride_axis=0)
            bias_ref[i] = jnp.where(in_win, toep * LOG2E, NEG)
            return carry

        lax.fori_loop(0, NA_HEADS * NA_PAIR_TILES, body, 0)

    n_win = NA_KR * GRID_W
    kct_f32 = kct_ref[...]
    kct = kct_f32.astype(BF16)
    vct = vct_ref[...].astype(BF16)

    def win_start(r):
        return jnp.clip(r - NA_KR // 2, 0, GRID_ROWS - NA_KR)

    def attend(grid_rows, rs, bound):
        a0, n = grid_rows[0], len(grid_rows)
        row0 = pl.multiple_of(rs * GRID_W, GRID_W)
        outs = []
        for t in range(NA_HEADS // 2):
            lanes = slice(t * LANES, (t + 1) * LANES)
            bias = jnp.concatenate(
                [jnp.concatenate([bias_ref[h * NA_PAIR_TILES + (rs - (j * NA_ROWS_PER_STEP + a) + NA_KR - 1) + 2 * m]
                                  for m in range(NA_KR // 2)], axis=1)
                 for h in (2 * t, 2 * t + 1) for a in grid_rows], axis=0)
            outs.append(_na_tile_attend(
                q_ref[_row_block(a0, GRID_W) if n == 1 else slice(a0 * GRID_W, (a0 + n) * GRID_W), lanes],
                [(kl_ref[pl.ds(row0, n_win), lanes], False), (kct[lanes, :], True)],
                [(vl_ref[pl.ds(row0, n_win), lanes], False), (vct[lanes, :], True)], bias, bound))
        return jnp.concatenate(outs, axis=1).astype(BF16)

    first_r, last_r = j * NA_ROWS_PER_STEP, (j + 1) * NA_ROWS_PER_STEP - 1
    shared = win_start(first_r) == win_start(last_r)

    def attend_step(bounds):
        bound = bounds[1]

        @pl.when(shared)
        def _shared_window():
            o_ref[...] = attend(tuple(range(NA_ROWS_PER_STEP)), win_start(first_r), bound)

        @pl.when(jnp.logical_not(shared))
        def _per_row_windows():
            for a in range(NA_ROWS_PER_STEP):
                o_ref[a * GRID_W:(a + 1) * GRID_W, :] = attend((a,), win_start(first_r + a), bound)

    def exact_row(a):
        o_ref[_row_block(a, GRID_W), :] = attend((a,), win_start(first_r + a), None)

    bias_max = jnp.max(jnp.max(jnp.abs(g_ref[...]), axis=-1, keepdims=True), axis=0, keepdims=True)
    _with_score_bounds(vec_ref, attend_step, lambda: _rolled(NA_ROWS_PER_STEP, exact_row),
                       bias_max=bias_max, na_key_sq=_max_sq_norm(kct_f32, NA_HD))


def _lat_attn_kernel(lam_init, q_ref, kl_ref, vl_ref, kc_ref, vc_ref, kct_ref, vct_ref, ona_ref, vec_ref, mix_ref):
    lam = _diff_lambda(vec_ref, lam_init)
    kct = kct_ref[...]

    def attend(bounds, rows):
        b_mla, _, b_df = bounds
        o_mla = _mla_attend(q_ref[rows, P_MLA:P_MLA + W_MLA_P],
                            [(kc_ref[...], False), (kl_ref[:, P_MLA:P_MLA + W_MLA_P], False)],
                            [(vc_ref[...], False), (vl_ref[:, PV_MLA:PV_MLA + MLA_W], False)], b_mla)
        o_df = _df_attend(q_ref[rows, P_DF:P_DF + DF_W],
                          [(kct.astype(BF16), True), (kl_ref[:, P_DF:P_DF + DF_W], False)],
                          [(vct_ref[...].astype(BF16), True), (vl_ref[:, PV_DF:PV_DF + DF_W], False)],
                          lam, vec_ref[V_GDS:V_GDS + 1, 0:DF_W], 1.0 - lam_init, b_df)
        mix_ref[rows, :] = jnp.concatenate([o_mla.astype(BF16), ona_ref[rows, :], o_df.astype(BF16)], axis=1)

    def exact():
        _rolled(TM // EXACT_ROWS, lambda i: attend(EXACT_MAX, _row_block(i, EXACT_ROWS)))

    _with_score_bounds(vec_ref, lambda bounds: attend(bounds, slice(None)), exact,
                       df_key_sq=_max_sq_norm(kct, DF_QK))


def _const_spec(shape):
    nd = len(shape)
    return pl.BlockSpec(shape, lambda *_: (0,) * nd, pipeline_mode=pl.Buffered(1))


def _layer_spec(l, shape):
    nd = len(shape)
    return pl.BlockSpec((None,) + tuple(shape), lambda *_: (l,) + (0,) * nd, pipeline_mode=pl.Buffered(1))


def _params(n_axes):
    return pltpu.CompilerParams(dimension_semantics=("arbitrary",) * n_axes, vmem_limit_bytes=VMEM_LIMIT)


def _block_diag(group):
    i = np.arange(MXU_DIM) // group
    return jnp.asarray((i[:, None] == i[None, :]).astype(np.float32), dtype=BF16)


def _rope_tables():
    t = np.arange(DEC_SEQ)
    row = (t // GRID_W).astype(np.float64)
    col = (t % GRID_W).astype(np.float64)
    n = MLA_ROPE // 4
    inv = 1.0 / (ROPE_BASE ** (np.arange(n, dtype=np.float64) * 2.0 / (MLA_ROPE // 2)))
    ar = row[:, None] * inv
    ac = col[:, None] * inv
    ang = np.concatenate([ar, ar, ac, ac], axis=-1)
    cos32, sin32 = np.cos(ang), np.sin(ang)
    first = (np.arange(MLA_ROPE) % (2 * n)) < n
    sa32 = np.where(first, -sin32, 0.0)
    sb32 = np.where(first, 0.0, sin32)

    def mla_tile(v32, fill):
        out = np.full((DEC_SEQ, LANES), fill)
        out[:, MLA_NOPE:MLA_QK] = v32
        return out

    tabs = [mla_tile(cos32, 1.0), mla_tile(sa32, 0.0), mla_tile(sb32, 0.0),
            np.tile(cos32, (1, LANES // DF_QK)), np.tile(sa32, (1, LANES // DF_QK)), np.tile(sb32, (1, LANES // DF_QK))]
    return jnp.asarray(np.stack(tabs).astype(np.float32))


def _feature_major(a):
    a = jnp.swapaxes(a, -1, -2)
    return a.reshape(a.shape[:-3] + (a.shape[-3] * a.shape[-2], a.shape[-1]))


def _tables(p):
    w_in_t = jnp.swapaxes(p['w_in'], 1, 2)
    kr = w_in_t[:, Q_LORA + KV_LORA:B_SRC]
    z32 = jnp.zeros_like(kr)
    wa = jnp.concatenate([w_in_t[:, :Q_LORA + KV_LORA], kr, z32, kr, z32], axis=1).astype(BF16)
    wb = w_in_t[:, B_SRC:].astype(BF16)
    w_uq_t = jnp.swapaxes(p['w_uq'], 1, 2).reshape(DEPTH, MLA_HEADS, MLA_QK, Q_LORA)
    w_uq_p = jnp.pad(w_uq_t, ((0, 0), (0, 0), (0, MLA_PAD - MLA_QK), (0, 0))).reshape(DEPTH, W_MLA_P, Q_LORA).astype(BF16)
    w_ukv = p['w_ukv'].reshape(DEPTH, KV_LORA, MLA_HEADS, MLA_NOPE + MLA_V)
    wk = jnp.pad(w_ukv[..., :MLA_NOPE], ((0, 0), (0, 0), (0, 0), (0, MLA_PAD - MLA_NOPE))).reshape(DEPTH, KV_LORA, W_MLA_P)
    wv = w_ukv[..., MLA_NOPE:].reshape(DEPTH, KV_LORA, MLA_W)
    w_ukv_r = jnp.concatenate([wk, wv], axis=2).astype(BF16)

    def row(v, reps=1):
        v = jnp.tile(v, (1, reps)) if reps > 1 else v
        return [v, jnp.zeros((DEPTH, D_MODEL - v.shape[1]), F32)] if v.shape[1] < D_MODEL else [v]

    pad_head = lambda g: jnp.pad(g, ((0, 0), (0, MLA_PAD - MLA_QK)))
    pieces = (row(p['g_mix']) + row(p['g_ffn']) + row(p['g_qa']) + row(p['g_kva'])
              + row(pad_head(p['g_mla_q']), MLA_HEADS) + row(pad_head(p['g_mla_k']), MLA_HEADS)
              + row(p['g_na_q'], NA_HEADS) + row(p['g_na_k'], NA_HEADS)
              + row(p['g_df_q'], 2 * DF_HEADS) + row(p['g_df_k'], 2 * DF_HEADS) + row(p['g_df_sub'], DF_HEADS)
              + row(p['df_lq1']) + row(p['df_lk1']) + row(p['df_lq2']) + row(p['df_lk2'])
              + [jnp.zeros((DEPTH, D_MODEL), F32)])
    vec = jnp.concatenate(pieces, axis=1).reshape(DEPTH, N_VEC, D_MODEL)
    f = p['na_rpb']
    zpad = jnp.zeros((DEPTH, NA_HEADS, NA_PAIR_TILES, 33), F32)
    g_rows = jnp.concatenate([f[:, :, :-1, NA_KW - 1:], zpad, f[:, :, 1:, :], zpad, f[:, :, :-1, :NA_KW - 1]], axis=-1)
    g_rows = g_rows.reshape(DEPTH, NA_HEADS * NA_PAIR_TILES, LANES)
    return dict(wa=wa, wb=wb, w_uq=w_uq_p, w_ukv=w_ukv_r, vec=vec, g_rows=g_rows)


def _modulation(c_all, w_mod, b_mod):
    tn = 2048
    return pl.pallas_call(
        _mod_kernel,
        grid=(DEPTH, 6 * D_MODEL // tn),
        in_specs=[pl.BlockSpec((N_MOD, D_MODEL), lambda l, j: (0, 0)),
                  pl.BlockSpec((None, D_MODEL, tn), lambda l, j: (l, 0, j)),
                  pl.BlockSpec((None, 1, tn), lambda l, j: (l, 0, j))],
        out_specs=pl.BlockSpec((None, N_MOD, tn), lambda l, j: (l, 0, j)),
        out_shape=jax.ShapeDtypeStruct((DEPTH, N_MOD, 6 * D_MODEL), F32),
        compiler_params=_params(2),
        name="modulation",
    )(c_all, w_mod, b_mod.reshape(DEPTH, 1, 6 * D_MODEL))


def _front_weight_specs(l):
    return [_layer_spec(l, (N_VEC, D_MODEL)), _layer_spec(l, (A_COLS, D_MODEL)), _layer_spec(l, (B_COLS, D_MODEL)),
            _layer_spec(l, (W_MLA_P, Q_LORA)), _layer_spec(l, (KV_LORA, W_MLA_P + MLA_W)),
            _const_spec((MXU_DIM, MXU_DIM)), _const_spec((MXU_DIM, MXU_DIM))]


def _front_weights(t, consts):
    return (t['vec'], t['wa'], t['wb'], t['w_uq'], t['w_ukv'], consts['bd64'], consts['bd32'])


def _context_mixer(l, lam_init, x, mods, t, consts, ffn_f32, prev_caches, after=()):
    n_tok = BATCH * SEQ
    n_alias = len(prev_caches)
    tok = lambda w: pl.BlockSpec((CTX_SEQS_PER_STEP * SEQ, w), lambda b: (b, 0))
    if n_alias == 0:
        assert l == 0
        lay = lambda *s: pl.BlockSpec((CTX_SEQS_PER_STEP, DEPTH) + s, lambda b: (b, 0) + (0,) * len(s))
    else:
        lay = lambda *s: pl.BlockSpec((CTX_SEQS_PER_STEP, None) + s, lambda b: (b, l) + (0,) * len(s))
    cache_shapes = [(SEQ, KV_LORA), (MLA_ROPE, SEQ), (NA_HEADS, NA_HD, SEQ), (NA_HEADS, NA_HD, SEQ),
                    (DF_HEADS, DF_HD, SEQ), (DF_HEADS, DF_HD, SEQ)]
    weights = _front_weights(t, consts)
    steps = BATCH // CTX_SEQS_PER_STEP
    ffn_chunks = [(w.shape[1] // steps, w.shape[2]) for w in ffn_f32]
    n_in = 2 + len(weights) + len(ffn_f32)
    mix, *outs = pl.pallas_call(
        functools.partial(_ctx_kernel, lam_init, n_alias, len(after)),
        grid=(steps,),
        in_specs=[tok(D_MODEL), pl.BlockSpec((None, 1, 6 * D_MODEL), lambda b: (l * N_MOD, 0, 0))]
        + _front_weight_specs(l) + [pl.BlockSpec((None,) + c, lambda b: (l, b, 0)) for c in ffn_chunks]
        + [pl.BlockSpec(memory_space=pl.ANY)] * (n_alias + len(after)),
        out_specs=[tok(D_MODEL)] + [lay(*s) for s in cache_shapes] + [pl.BlockSpec(c, lambda b: (b, 0)) for c in ffn_chunks],
        out_shape=[jax.ShapeDtypeStruct((n_tok, D_MODEL), BF16)]
        + [jax.ShapeDtypeStruct((BATCH, DEPTH) + s, F32) for s in cache_shapes]
        + [jax.ShapeDtypeStruct(w.shape[1:], BF16) for w in ffn_f32],
        input_output_aliases={n_in + i: 1 + i for i in range(n_alias)},
        compiler_params=_params(1),
        name="context_mixer",
    )(x, mods, *weights, *ffn_f32, *prev_caches, *after)
    return mix, outs[:N_CACHE], outs[N_CACHE:]


def _finish(l, xc, mixc, xl, mixl, mods, t, ffn_bf16):
    nc, nl = xc.shape[0] // TM_FINISH, xl.shape[0] // TM_FINISH
    blocks_per_seq = DEC_SEQ // TM_FINISH
    ctx_tok = pl.BlockSpec((TM_FINISH, D_MODEL), lambda i: (jnp.minimum(i, nc - 1), 0))
    lat_tok = pl.BlockSpec((TM_FINISH, D_MODEL), lambda i: (jnp.maximum(i - nc, 0), 0))
    mod_row = lambda i: jnp.where(i < nc, 0, 1 + jnp.maximum(i - nc, 0) // blocks_per_seq)
    return pl.pallas_call(
        functools.partial(_finish_kernel, nc),
        grid=(nc + nl,),
        in_specs=[ctx_tok, ctx_tok, lat_tok, lat_tok,
                  pl.BlockSpec((None, 1, 6 * D_MODEL), lambda i: (l * N_MOD + mod_row(i), 0, 0)),
                  _layer_spec(l, (N_VEC, D_MODEL))] + [_const_spec(w.shape) for w in ffn_bf16],
        out_specs=[ctx_tok, lat_tok],
        out_shape=[jax.ShapeDtypeStruct(xc.shape, F32), jax.ShapeDtypeStruct(xl.shape, F32)],
        compiler_params=_params(1),
        name="finish",
    )(xc, mixc, xl, mixl, mods, t['vec'], *ffn_bf16)


def _latent_front(l, x, mods, t, consts):
    n_tok = DEC_BATCH * DEC_SEQ
    tm = TM_LAT_FRONT
    blocks_per_seq = DEC_SEQ // tm
    tok = lambda w: pl.BlockSpec((tm, w), lambda i: (i, 0))
    wspecs = _front_weight_specs(l)
    weights = _front_weights(t, consts)
    return pl.pallas_call(
        _lat_front_kernel,
        grid=(n_tok // tm,),
        in_specs=[tok(D_MODEL),
                  pl.BlockSpec((None, 1, 6 * D_MODEL), lambda i: (l * N_MOD + 1 + i // blocks_per_seq, 0, 0)),
                  wspecs[0], pl.BlockSpec((6, tm, LANES), lambda i: (0, i % blocks_per_seq, 0))] + wspecs[1:],
        out_specs=[tok(QK_PACK), tok(QK_PACK), tok(V_PACK)],
        out_shape=[jax.ShapeDtypeStruct((n_tok, QK_PACK), BF16), jax.ShapeDtypeStruct((n_tok, QK_PACK), BF16),
                   jax.ShapeDtypeStruct((n_tok, V_PACK), BF16)],
        compiler_params=_params(1),
        name="latent_front",
    )(x, mods, weights[0], consts['rope'], *weights[1:])


def _context_kv(l, ckv, krope_t, t):
    lay = lambda r, w: pl.BlockSpec((None, None, r, w), lambda b: (b, l, 0, 0))
    out = lambda w: pl.BlockSpec((None, PAST_LEN, w), lambda b: (b, 0, 0))
    return pl.pallas_call(
        _ctx_kv_kernel,
        grid=(DEC_BATCH,),
        in_specs=[lay(PAST_LEN, KV_LORA), lay(MLA_ROPE, PAST_LEN),
                  _layer_spec(l, (N_VEC, D_MODEL)), _layer_spec(l, (KV_LORA, W_MLA_P + MLA_W))],
        out_specs=[out(W_MLA_P), out(MLA_W)],
        out_shape=[jax.ShapeDtypeStruct((DEC_BATCH, PAST_LEN, W_MLA_P), BF16),
                   jax.ShapeDtypeStruct((DEC_BATCH, PAST_LEN, MLA_W), BF16)],
        compiler_params=_params(1),
        name="context_kv",
    )(ckv, krope_t, t['vec'], t['w_ukv'])


def _latent_na(l, q, k, v, kct, vct, g_rows, vec):
    n_tok = DEC_BATCH * DEC_SEQ
    na_blk = P_NA // NA_W
    steps = GRID_ROWS // NA_ROWS_PER_STEP
    tq = NA_ROWS_PER_STEP * GRID_W
    ctx = pl.BlockSpec((None, None, NA_W, PAST_LEN), lambda b, j: (b, l, 0, 0))
    return pl.pallas_call(
        _na_lat_kernel,
        grid=(DEC_BATCH, steps),
        in_specs=[pl.BlockSpec((tq, NA_W), lambda b, j: (b * steps + j, na_blk)),
                  pl.BlockSpec((DEC_SEQ, NA_W), lambda b, j: (b, na_blk)),
                  pl.BlockSpec((DEC_SEQ, NA_W), lambda b, j: (b, PV_NA // NA_W)),
                  ctx, ctx, _layer_spec(l, (NA_HEADS * NA_PAIR_TILES, LANES)), _layer_spec(l, (N_VEC, D_MODEL))],
        out_specs=pl.BlockSpec((tq, NA_W), lambda b, j: (b * steps + j, 0)),
        out_shape=jax.ShapeDtypeStruct((n_tok, NA_W), BF16),
        scratch_shapes=[pltpu.VMEM((NA_HEADS * NA_PAIR_TILES, GRID_W, LANES), F32)],
        compiler_params=_params(2),
        name="latent_neighbourhood",
    )(q, k, v, kct, vct, g_rows, vec)


def _latent_attention(l, lam_init, q, k, v, kc, vc, kct, vct, o_na, t):
    n_tok = DEC_BATCH * DEC_SEQ
    nq = DEC_SEQ // TM
    ctx = pl.BlockSpec((None, None, DF_W, PAST_LEN), lambda b, j: (b, l, 0, 0))
    return pl.pallas_call(
        functools.partial(_lat_attn_kernel, lam_init),
        grid=(DEC_BATCH, nq),
        in_specs=[pl.BlockSpec((TM, QK_PACK), lambda b, j: (b * nq + j, 0)),
                  pl.BlockSpec((DEC_SEQ, QK_PACK), lambda b, j: (b, 0)),
                  pl.BlockSpec((DEC_SEQ, V_PACK), lambda b, j: (b, 0)),
                  pl.BlockSpec((None, PAST_LEN, W_MLA_P), lambda b, j: (b, 0, 0)),
                  pl.BlockSpec((None, PAST_LEN, MLA_W), lambda b, j: (b, 0, 0)),
                  ctx, ctx,
                  pl.BlockSpec((TM, NA_W), lambda b, j: (b * nq + j, 0)),
                  _layer_spec(l, (N_VEC, D_MODEL))],
        out_specs=pl.BlockSpec((TM, D_MODEL), lambda b, j: (b * nq + j, 0)),
        out_shape=jax.ShapeDtypeStruct((n_tok, D_MODEL), BF16),
        compiler_params=_params(2),
        name="latent_attention",
    )(q, k, v, kc, vc, kct, vct, o_na, t['vec'])


def kernel(x_prompt, x_sample, cache_mla_ckv, cache_mla_krope, cache_na_k, cache_na_v, cache_df_k, cache_df_v, c, c_ctx, w_mod, b_mod, g_mix, w_in, g_qa, w_uq, g_kva, w_ukv, g_mla_q, g_mla_k, g_na_q, g_na_k, na_rpb, g_df_q, g_df_k, df_lq1, df_lk1, df_lq2, df_lk2, g_df_sub, w_out, g_ffn, w_gate, w_up, w_down):
    p = dict(g_mix=g_mix, w_in=w_in, g_qa=g_qa, w_uq=w_uq, g_kva=g_kva, w_ukv=w_ukv, g_mla_q=g_mla_q, g_mla_k=g_mla_k,
             g_na_q=g_na_q, g_na_k=g_na_k, na_rpb=na_rpb, g_df_q=g_df_q, g_df_k=g_df_k, df_lq1=df_lq1, df_lk1=df_lk1,
             df_lq2=df_lq2, df_lk2=df_lk2, g_df_sub=g_df_sub, w_out=w_out, g_ffn=g_ffn, w_gate=w_gate, w_up=w_up,
             w_down=w_down)
    consts = dict(bd64=_block_diag(NA_HD), bd32=_block_diag(DF_QK), rope=_rope_tables())
    t = _tables(p)

    c_all = jnp.concatenate([c_ctx[None, :], c, jnp.zeros((N_MOD - 1 - DEC_BATCH, D_MODEL), F32)], axis=0)
    mods = _modulation(c_all, w_mod, b_mod).reshape(DEPTH * N_MOD, 1, 6 * D_MODEL)

    krope_t = jnp.swapaxes(cache_mla_krope, -1, -2)
    na_kt, na_vt, df_kt, df_vt = (_feature_major(a) for a in (cache_na_k, cache_na_v, cache_df_k, cache_df_v))

    xp = x_prompt.reshape(BATCH * SEQ, D_MODEL)
    xs = x_sample.reshape(DEC_BATCH * DEC_SEQ, D_MODEL)
    new_caches = ()
    ffn_f32 = (w_out, w_gate, w_up, w_down)
    for l in range(DEPTH):
        lam_init = 0.8 - 0.6 * math.exp(-0.3 * l)
        q, k, v = _latent_front(l, xs, mods, t, consts)
        kc, vc = _context_kv(l, cache_mla_ckv, krope_t, t)
        o_na = _latent_na(l, q, k, v, na_kt, na_vt, t['g_rows'], t['vec'])
        mix_s = _latent_attention(l, lam_init, q, k, v, kc, vc, df_kt, df_vt, o_na, t)
        mix, new_caches, ffn_bf16 = _context_mixer(l, lam_init, xp, mods, t, consts, ffn_f32, new_caches, after=(mix_s,))
        xp, xs = _finish(l, xp, mix, xs, mix_s, mods, t, ffn_bf16)
    ckv_new, *narrow = new_caches
    return (xp.reshape(BATCH, SEQ, D_MODEL), xs.reshape(DEC_BATCH, DEC_SEQ, D_MODEL), ckv_new,
            *(jnp.swapaxes(a, -1, -2) for a in narrow))
```

```python
import functools
import math

import numpy as np
import jax
import jax.numpy as jnp
from jax import lax
from jax.experimental import pallas as pl
from jax.experimental.pallas import tpu as pltpu

F32 = jnp.float32
BF16 = jnp.bfloat16

D_MODEL = 1024
BATCH = 32
SEQ = 256
DEPTH = 2
DEC_BATCH = 2
DEC_SEQ = 1024
PAST_LEN = 256
GRID_W = 64
GRID_ROWS = DEC_SEQ // GRID_W
MLA_HEADS = 6
MLA_NOPE = 64
MLA_ROPE = 32
MLA_QK = MLA_NOPE + MLA_ROPE
MLA_V = 64
MLA_PAD = 128
Q_LORA = 256
KV_LORA = 128
NA_HEADS = 6
NA_HD = 64
NA_KR = 8
NA_KW = 16
DF_HEADS = 4
DF_HD = 64
DF_QK = 32
MLA_W = MLA_HEADS * MLA_V
NA_W = NA_HEADS * NA_HD
DF_W = DF_HEADS * DF_HD
D_FF = -(-8 * D_MODEL // (3 * 256)) * 256
ROPE_BASE = 10000.0
EPS = 1e-6
NEG = -1e30
LOG2E = math.log2(math.e)
MAX_FREE_SOFTMAX_BOUND = 60.0

LANES = 128
MXU_DIM = 256

A_CQ = 0
A_CKV = A_CQ + Q_LORA
A_KR = A_CKV + KV_LORA
A_COLS = A_KR + LANES
B_SRC = Q_LORA + KV_LORA + MLA_ROPE
B_NAQ = 0
B_NAK = B_NAQ + NA_W
B_NAV = B_NAK + NA_W
B_DFQ = B_NAV + NA_W
B_DFK = B_DFQ + DF_W
B_DFV = B_DFK + DF_W
B_COLS = B_DFV + DF_W

W_MLA_P = MLA_HEADS * MLA_PAD
QK_PACK = W_MLA_P + NA_W + DF_W
V_PACK = MLA_W + NA_W + DF_W
P_MLA = 0
P_NA = W_MLA_P
P_DF = P_NA + NA_W
PV_MLA = 0
PV_NA = MLA_W
PV_DF = MLA_W + NA_W

(V_GMIX, V_GFFN, V_GQA, V_GKVA, V_GMQ, V_GMK, V_GNQ, V_GNK, V_GDQ, V_GDK, V_GDS,
 V_LQ1, V_LK1, V_LQ2, V_LK2) = range(15)
N_VEC = 16

N_MOD = 8
TM = 512
EXACT_ROWS = 128
TM_LAT_FRONT = 512
LAT_FRONT_CHUNK = 256
TM_FINISH = 512
CTX_SEQS_PER_STEP = 2
NA_ROWS_PER_STEP = 4
VMEM_LIMIT = 56 * 1024 * 1024

NA_PAIR_TILES = 2 * NA_KR - 2
N_CACHE = 6
N_FFN_W = 4


def _dot(a, b):
    return jnp.dot(a, b, preferred_element_type=F32)


def _dot_nt(a, b):
    return lax.dot_general(a, b, (((1,), (1,)), ((), ())), preferred_element_type=F32)


def _lane_iota(shape):
    return lax.broadcasted_iota(jnp.int32, shape, len(shape) - 1)


def _rms_rows(x, g):
    ms = jnp.mean(x * x, axis=-1, keepdims=True)
    return x * lax.rsqrt(ms + EPS) * g


def _tile_rms(x, g, n_real):
    outs = []
    for c0 in range(0, x.shape[1], LANES):
        xt = x[:, c0:c0 + LANES]
        ms = jnp.sum(xt * xt, axis=-1, keepdims=True) * (1.0 / n_real)
        outs.append(xt * lax.rsqrt(ms + EPS) * g[:, c0:c0 + LANES])
    return jnp.concatenate(outs, axis=1)


def _seg_rms(x, bd_ref, g, group):
    width = x.shape[1]
    sq = (x * x).astype(BF16)
    parts = []
    for c0 in range(0, width, MXU_DIM):
        w = min(MXU_DIM, width - c0)
        parts.append(_dot(sq[:, c0:c0 + w], bd_ref[0:w, 0:w]))
    ss = parts[0] if len(parts) == 1 else jnp.concatenate(parts, axis=1)
    return x * lax.rsqrt(ss * (1.0 / group) + EPS) * g


def _rope_tiles(x, cos, sa, sb):
    outs = []
    for t in range(x.shape[1] // LANES):
        xt = x[:, t * LANES:(t + 1) * LANES]
        up = pltpu.roll(xt, LANES - MLA_ROPE // 4, 1)
        dn = pltpu.roll(xt, MLA_ROPE // 4, 1)
        outs.append(xt * cos + up * sa + dn * sb)
    return outs[0] if len(outs) == 1 else jnp.concatenate(outs, axis=1)


def _diff_lambda(vec_ref, lam_init):
    a = jnp.sum(vec_ref[V_LQ1:V_LQ1 + 1, 0:DF_QK] * vec_ref[V_LK1:V_LK1 + 1, 0:DF_QK], axis=-1, keepdims=True)
    b = jnp.sum(vec_ref[V_LQ2:V_LQ2 + 1, 0:DF_QK] * vec_ref[V_LK2:V_LK2 + 1, 0:DF_QK], axis=-1, keepdims=True)
    return jnp.exp(a) - jnp.exp(b) + lam_init


def _mixer_front(x, mod, vec_ref, wa_ref, wb_ref, w_uq_ref, bd64_ref, bd32_ref):
    sh = mod[:, 0:D_MODEL]
    sc = mod[:, D_MODEL:2 * D_MODEL]
    h = (_rms_rows(x, vec_ref[V_GMIX:V_GMIX + 1, :]) * (1.0 + sc) + sh).astype(BF16)
    za = _dot_nt(h, wa_ref[...])
    zb = _dot_nt(h, wb_ref[B_SRC:B_SRC + B_COLS, :])
    cqn = _rms_rows(za[:, A_CQ:A_CQ + Q_LORA], vec_ref[V_GQA:V_GQA + 1, 0:Q_LORA])
    q_raw = _dot_nt(cqn.astype(BF16), w_uq_ref[...])
    q_mla = _tile_rms(q_raw, vec_ref[V_GMQ:V_GMQ + 1, 0:W_MLA_P] * (MLA_QK ** -0.5 * LOG2E), MLA_QK)
    ckv_n = _rms_rows(za[:, A_CKV:A_CKV + KV_LORA], vec_ref[V_GKVA:V_GKVA + 1, 0:KV_LORA])
    kr_tile = za[:, A_KR:A_KR + LANES]
    q_na = _seg_rms(zb[:, B_NAQ:B_NAQ + NA_W], bd64_ref, vec_ref[V_GNQ:V_GNQ + 1, 0:NA_W] * (NA_HD ** -0.5 * LOG2E), NA_HD)
    k_na = _seg_rms(zb[:, B_NAK:B_NAK + NA_W], bd64_ref, vec_ref[V_GNK:V_GNK + 1, 0:NA_W], NA_HD)
    v_na = zb[:, B_NAV:B_NAV + NA_W]
    q_df = _seg_rms(zb[:, B_DFQ:B_DFQ + DF_W], bd32_ref, vec_ref[V_GDQ:V_GDQ + 1, 0:DF_W] * (DF_QK ** -0.5 * LOG2E), DF_QK)
    k_df = _seg_rms(zb[:, B_DFK:B_DFK + DF_W], bd32_ref, vec_ref[V_GDK:V_GDK + 1, 0:DF_W], DF_QK)
    v_df = zb[:, B_DFV:B_DFV + DF_W]
    return q_mla, ckv_n, kr_tile, q_na, k_na, v_na, q_df, k_df, v_df


def _mla_kv(ckv_n, kr_tile, vec_ref, w_ukv_ref):
    kv = _dot(ckv_n.astype(BF16), w_ukv_ref[...])
    lane = _lane_iota((1, LANES))
    kr = jnp.where((lane >= MLA_NOPE) & (lane < MLA_QK), kr_tile, 0.0)
    k_pre = kv[:, 0:W_MLA_P] + jnp.concatenate([kr] * MLA_HEADS, axis=1)
    k = _tile_rms(k_pre, vec_ref[V_GMK:V_GMK + 1, 0:W_MLA_P], MLA_QK)
    return k, kv[:, W_MLA_P:W_MLA_P + MLA_W]


def _softmax_parts(s, bound=None):
    m = jnp.max(s, axis=-1, keepdims=True) if bound is None else bound
    p = jnp.exp2(s - m)
    return p, 1.0 / jnp.sum(p, axis=-1, keepdims=True)


def _max_sq_norm(kt, group):
    sq = kt * kt
    best = None
    for r0 in range(0, kt.shape[0], group):
        n2 = jnp.sum(sq[r0:r0 + group], axis=0, keepdims=True)
        best = n2 if best is None else jnp.maximum(best, n2)
    return jnp.max(best, axis=-1, keepdims=True)


def _score_bounds(vec_ref, bias_max=None, na_key_sq=None, df_key_sq=None):
    gmax = lambda row, w: jnp.max(jnp.abs(vec_ref[row:row + 1, 0:w]), axis=-1, keepdims=True)
    slack = LOG2E * (1.0 + 2.0 ** -6)

    def key_norm(row, w, d, measured_sq):
        k = gmax(row, w) * d ** 0.5
        return k if measured_sq is None else jnp.maximum(k, jnp.sqrt(measured_sq))

    b_mla = gmax(V_GMQ, W_MLA_P) * key_norm(V_GMK, W_MLA_P, MLA_QK, None) * slack
    b_na = gmax(V_GNQ, NA_W) * key_norm(V_GNK, NA_W, NA_HD, na_key_sq) * slack
    if bias_max is not None:
        b_na = b_na + bias_max * LOG2E
    b_df = gmax(V_GDQ, DF_W) * key_norm(V_GDK, DF_W, DF_QK, df_key_sq) * slack
    worst = jnp.maximum(b_mla, jnp.maximum(b_na, b_df))
    return (b_mla, b_na, b_df), worst[0, 0] <= MAX_FREE_SOFTMAX_BOUND


EXACT_MAX = (None, None, None)


def _with_score_bounds(vec_ref, bounded, exact, **measured):
    bounds, ok = _score_bounds(vec_ref, **measured)
    pl.when(ok)(lambda: bounded(bounds))
    pl.when(jnp.logical_not(ok))(exact)


def _row_block(i, n):
    return slice(i * n, (i + 1) * n) if isinstance(i, int) else pl.ds(pl.multiple_of(i * n, n), n)


def _rolled(n, body):
    def step(i, carry):
        body(i)
        return carry

    lax.fori_loop(0, n, step, 0)


def _scores(q, k_segs):
    parts = [_dot(q, k) if feature_major else _dot_nt(q, k) for k, feature_major in k_segs]
    return parts[0] if len(parts) == 1 else jnp.concatenate(parts, axis=1)


def _pv(p, v_segs):
    out = None
    c0 = 0
    for v, feature_major in v_segs:
        n = v.shape[1] if feature_major else v.shape[0]
        o = _dot_nt(p[:, c0:c0 + n], v) if feature_major else _dot(p[:, c0:c0 + n], v)
        out = o if out is None else out + o
        c0 += n
    return out


def _seg_tile(seg, t):
    a, feature_major = seg
    return (a[t * LANES:(t + 1) * LANES, :] if feature_major else a[:, t * LANES:(t + 1) * LANES]), feature_major


def _lane_groups(qt, width):
    lane = _lane_iota((1, LANES))
    zero = jnp.zeros_like(qt)
    return jnp.concatenate(
        [jnp.where((lane >= g * width) & (lane < (g + 1) * width), qt, zero) for g in range(LANES // width)], axis=0)


def _pair_select(o2):
    tq = o2.shape[0] // 2
    return jnp.where(_lane_iota((1, LANES)) < NA_HD, o2[0:tq], o2[tq:2 * tq])


def _mla_attend(q, k_segs, v_segs, bound=None):
    outs = []
    for t in range(MLA_HEADS // 2):
        vt = [_seg_tile(v, t) for v in v_segs]
        halves = []
        for h in (2 * t, 2 * t + 1):
            p, il = _softmax_parts(_scores(q[:, h * MLA_PAD:(h + 1) * MLA_PAD], [_seg_tile(k, h) for k in k_segs]),
                                   bound)
            halves.append(_pv(p.astype(BF16), vt) * il)
        outs.append(jnp.where(_lane_iota((1, LANES)) < MLA_V, halves[0], halves[1]))
    return jnp.concatenate(outs, axis=1)


def _na_tile_attend(qt, k_segs, v_segs, bias=None, bound=None):
    s = _scores(_lane_groups(qt, NA_HD), k_segs)
    if bias is not None:
        nb = bias.shape[1]
        s = jnp.concatenate([s[:, 0:nb] + bias, s[:, nb:]], axis=1)
    p, il = _softmax_parts(s, bound)
    return _pair_select(_pv(p.astype(BF16), v_segs) * il)


def _na_attend_full(q, k_segs, v_segs, bound=None):
    return jnp.concatenate(
        [_na_tile_attend(q[:, t * LANES:(t + 1) * LANES], [_seg_tile(k, t) for k in k_segs],
                         [_seg_tile(v, t) for v in v_segs], bound=bound) for t in range(NA_HEADS // 2)], axis=1)


def _df_attend(q, k_segs, v_segs, lam, g_sub, out_scale, bound=None):
    outs = []
    lane = _lane_iota((1, LANES))
    tq = q.shape[0]
    for t in range(DF_HEADS // 2):
        kt = [_seg_tile(k, t) for k in k_segs]
        vt = [_seg_tile(v, t) for v in v_segs]
        p, il = _softmax_parts(_scores(_lane_groups(q[:, t * LANES:(t + 1) * LANES], DF_QK), kt), bound)
        pn = []
        for hh in range(2):
            r1, r2 = 2 * hh * tq, (2 * hh + 1) * tq
            pn.append((p[r1:r1 + tq] * il[r1:r1 + tq] - p[r2:r2 + tq] * (lam * il[r2:r2 + tq])).astype(BF16))
        o = _pair_select(_pv(jnp.concatenate(pn, axis=0), vt))
        o2 = o * o
        ms_e = jnp.sum(jnp.where(lane < DF_HD, o2, 0.0), axis=-1, keepdims=True)
        ms_o = jnp.sum(jnp.where(lane >= DF_HD, o2, 0.0), axis=-1, keepdims=True)
        r = lax.rsqrt(jnp.where(lane < DF_HD, ms_e, ms_o) * (1.0 / DF_HD) + EPS)
        outs.append(o * r * (g_sub[:, t * LANES:(t + 1) * LANES] * out_scale))
    return jnp.concatenate(outs, axis=1)


def _mod_kernel(c_ref, w_ref, b_ref, o_ref):
    c = c_ref[...]
    s = c * jax.nn.sigmoid(c)
    o_ref[...] = _dot(s.astype(BF16), w_ref[...].astype(BF16)) + b_ref[pl.ds(pl.program_id(0), 1), :]


def _ctx_kernel(lam_init, n_alias, n_after, x_ref, mod_ref, vec_ref, wa_ref, wb_ref, w_uq_ref, w_ukv_ref, bd64_ref,
                bd32_ref, *rest):
    ffn_f32, rest = rest[:N_FFN_W], rest[N_FFN_W + n_alias + n_after:]
    mix_ref, *cache_refs = rest[:1 + N_CACHE]
    for src, dst in zip(ffn_f32, rest[1 + N_CACHE:]):
        dst[...] = src[...].astype(BF16)
    if n_alias == 0:
        for ref in cache_refs:
            ref[:, 1:] = jnp.zeros((ref.shape[0], ref.shape[1] - 1) + ref.shape[2:], F32)
        cache_refs = [ref.at[:, 0] for ref in cache_refs]
    ckv_ref, kr_ref, nak_ref, nav_ref, dfk_ref, dfv_ref = cache_refs
    lam = _diff_lambda(vec_ref, lam_init)
    mod = mod_ref[0:1, :]
    bf = lambda a: a.astype(BF16)
    seg = lambda a: [(a.astype(BF16), False)]
    def sequence(s, bounds):
        b_mla, b_na, b_df = bounds
        rows = _row_block(s, SEQ)
        q_mla, ckv_n, kr_tile, q_na, k_na, v_na, q_df, k_df, v_df = _mixer_front(
            x_ref[rows, :], mod, vec_ref, wa_ref, wb_ref, w_uq_ref, bd64_ref, bd32_ref)
        k_mla, v_mla = _mla_kv(ckv_n, kr_tile, vec_ref, w_ukv_ref)
        ckv_ref[s] = ckv_n
        kr_ref[s] = kr_tile.T[0:MLA_ROPE]
        for ref, a in ((nak_ref, k_na), (nav_ref, v_na), (dfk_ref, k_df), (dfv_ref, v_df)):
            at = a.T
            for h in range(ref.shape[1]):
                ref[s, h] = at[h * NA_HD:(h + 1) * NA_HD]
        o_mla = _mla_attend(bf(q_mla), seg(k_mla), seg(v_mla), b_mla)
        o_na = _na_attend_full(bf(q_na), seg(k_na), seg(v_na), b_na)
        o_df = _df_attend(bf(q_df), seg(k_df), seg(v_df), lam, vec_ref[V_GDS:V_GDS + 1, 0:DF_W], 1.0 - lam_init, b_df)
        mix_ref[rows, :] = jnp.concatenate([o_mla, o_na, o_df], axis=1).astype(BF16)

    def bounded(bounds):
        for s in range(CTX_SEQS_PER_STEP):
            sequence(s, bounds)

    _with_score_bounds(vec_ref, bounded, lambda: _rolled(CTX_SEQS_PER_STEP, lambda s: sequence(s, EXACT_MAX)))


def _finish_kernel(n_ctx_blocks, xc_ref, mixc_ref, xl_ref, mixl_ref, mod_ref, vec_ref, w_out_ref, w_gate_ref, w_up_ref,
                   w_down_ref, yc_ref, yl_ref):
    def block(x_ref, mix_ref, y_ref, mod_row):
        mod = mod_ref[pl.ds(mod_row, 1), :]
        gate_m = mod[:, 2 * D_MODEL:3 * D_MODEL]
        sh = mod[:, 3 * D_MODEL:4 * D_MODEL]
        sc = mod[:, 4 * D_MODEL:5 * D_MODEL]
        gate_f = mod[:, 5 * D_MODEL:6 * D_MODEL]
        x1 = x_ref[...] + gate_m * _dot(mix_ref[...], w_out_ref[...])
        h = (_rms_rows(x1, vec_ref[V_GFFN:V_GFFN + 1, :]) * (1.0 + sc) + sh).astype(BF16)
        g = _dot(h, w_gate_ref[...])
        u = _dot(h, w_up_ref[...])
        a = (g * jax.nn.sigmoid(g) * u).astype(BF16)
        y_ref[...] = x1 + gate_f * _dot(a, w_down_ref[...])

    is_ctx = pl.program_id(0) < n_ctx_blocks
    lat_row = 1 + jnp.maximum(pl.program_id(0) - n_ctx_blocks, 0) // (DEC_SEQ // TM_FINISH)
    pl.when(is_ctx)(lambda: block(xc_ref, mixc_ref, yc_ref, 0))
    pl.when(jnp.logical_not(is_ctx))(lambda: block(xl_ref, mixl_ref, yl_ref, lat_row))


def _lat_front_kernel(x_ref, mod_ref, vec_ref, rope_ref, wa_ref, wb_ref, w_uq_ref, w_ukv_ref, bd64_ref, bd32_ref,
                      q_ref, k_ref, v_ref):
    mod = mod_ref[pl.ds(1 + pl.program_id(0) // (DEC_SEQ // TM_LAT_FRONT), 1), :]
    for c in range(TM_LAT_FRONT // LAT_FRONT_CHUNK):
        rows = slice(c * LAT_FRONT_CHUNK, (c + 1) * LAT_FRONT_CHUNK)
        q_mla, ckv_n, kr_tile, q_na, k_na, v_na, q_df, k_df, v_df = _mixer_front(
            x_ref[rows, :], mod, vec_ref, wa_ref, wb_ref, w_uq_ref, bd64_ref, bd32_ref)
        k_mla, v_mla = _mla_kv(ckv_n, kr_tile, vec_ref, w_ukv_ref)
        cm, sam, sbm = rope_ref[0, rows], rope_ref[1, rows], rope_ref[2, rows]
        cd, sad, sbd = rope_ref[3, rows], rope_ref[4, rows], rope_ref[5, rows]
        q_ref[rows, :] = jnp.concatenate(
            [_rope_tiles(q_mla, cm, sam, sbm), q_na, _rope_tiles(q_df, cd, sad, sbd)], axis=1).astype(BF16)
        k_ref[rows, :] = jnp.concatenate(
            [_rope_tiles(k_mla, cm, sam, sbm), k_na, _rope_tiles(k_df, cd, sad, sbd)], axis=1).astype(BF16)
        v_ref[rows, :] = jnp.concatenate([v_mla, v_na, v_df], axis=1).astype(BF16)


def _ctx_kv_kernel(ckv_ref, krt_ref, vec_ref, w_ukv_ref, k_ref, v_ref):
    krt = jnp.concatenate([jnp.zeros((MLA_NOPE, PAST_LEN), F32), krt_ref[...],
                           jnp.zeros((LANES - MLA_QK, PAST_LEN), F32)], axis=0)
    k_mla, v_mla = _mla_kv(ckv_ref[...], krt.T, vec_ref, w_ukv_ref)
    k_ref[...] = k_mla.astype(BF16)
    v_ref[...] = v_mla.astype(BF16)


def _na_lat_kernel(q_ref, kl_ref, vl_ref, kct_ref, vct_ref, g_ref, vec_ref, o_ref, bias_ref):
    b = pl.program_id(0)
    j = pl.program_id(1)

    @pl.when((b == 0) & (j == 0))
    def _build_bias():
        c = lax.broadcasted_iota(jnp.int32, (GRID_W, LANES), 0)
        kc = _lane_iota((GRID_W, LANES)) % GRID_W
        start = jnp.clip(c - NA_KW // 2, 0, GRID_W - NA_KW)
        in_win = (kc >= start) & (kc < start + NA_KW)

        def body(i, carry):
            row = jnp.broadcast_to(g_ref[pl.ds(i, 1), :], (GRID_W, LANES))
            toep = pltpu.roll(row, 0, 1, stride=1, stride_axis=0)
            bias_ref[i] = jnp.where(in_win, toep * LOG2E, NEG)
            return carry

        lax.fori_loop(0, NA_HEADS * NA_PAIR_TILES, body, 0)

    n_win = NA_KR * GRID_W
    kct_f32 = kct_ref[...]
    kct = kct_f32.astype(BF16)
    vct = vct_ref[...].astype(BF16)

    def win_start(r):
        return jnp.clip(r - NA_KR // 2, 0, GRID_ROWS - NA_KR)

    def attend(grid_rows, rs, bound):
        a0, n = grid_rows[0], len(grid_rows)
        row0 = pl.multiple_of(rs * GRID_W, GRID_W)
        outs = []
        for t in range(NA_HEADS // 2):
            lanes = slice(t * LANES, (t + 1) * LANES)
            bias = jnp.concatenate(
                [jnp.concatenate([bias_ref[h * NA_PAIR_TILES + (rs - (j * NA_ROWS_PER_STEP + a) + NA_KR - 1) + 2 * m]
                                  for m in range(NA_KR // 2)], axis=1)
                 for h in (2 * t, 2 * t + 1) for a in grid_rows], axis=0)
            outs.append(_na_tile_attend(
                q_ref[_row_block(a0, GRID_W) if n == 1 else slice(a0 * GRID_W, (a0 + n) * GRID_W), lanes],
                [(kl_ref[pl.ds(row0, n_win), lanes], False), (kct[lanes, :], True)],
                [(vl_ref[pl.ds(row0, n_win), lanes], False), (vct[lanes, :], True)], bias, bound))
        return jnp.concatenate(outs, axis=1).astype(BF16)

    first_r, last_r = j * NA_ROWS_PER_STEP, (j + 1) * NA_ROWS_PER_STEP - 1
    shared = win_start(first_r) == win_start(last_r)

    def attend_step(bounds):
        bound = bounds[1]

        @pl.when(shared)
        def _shared_window():
            o_ref[...] = attend(tuple(range(NA_ROWS_PER_STEP)), win_start(first_r), bound)

        @pl.when(jnp.logical_not(shared))
        def _per_row_windows():
            for a in range(NA_ROWS_PER_STEP):
                o_ref[a * GRID_W:(a + 1) * GRID_W, :] = attend((a,), win_start(first_r + a), bound)

    def exact_row(a):
        o_ref[_row_block(a, GRID_W), :] = attend((a,), win_start(first_r + a), None)

    bias_max = jnp.max(jnp.max(jnp.abs(g_ref[...]), axis=-1, keepdims=True), axis=0, keepdims=True)
    _with_score_bounds(vec_ref, attend_step, lambda: _rolled(NA_ROWS_PER_STEP, exact_row),
                       bias_max=bias_max, na_key_sq=_max_sq_norm(kct_f32, NA_HD))


def _lat_attn_kernel(lam_init, q_ref, kl_ref, vl_ref, kc_ref, vc_ref, kct_ref, vct_ref, ona_ref, vec_ref, mix_ref):
    lam = _diff_lambda(vec_ref, lam_init)
    kct = kct_ref[...]

    def attend(bounds, rows):
        b_mla, _, b_df = bounds
        o_mla = _mla_attend(q_ref[rows, P_MLA:P_MLA + W_MLA_P],
                            [(kc_ref[...], False), (kl_ref[:, P_MLA:P_MLA + W_MLA_P], False)],
                            [(vc_ref[...], False), (vl_ref[:, PV_MLA:PV_MLA + MLA_W], False)], b_mla)
        o_df = _df_attend(q_ref[rows, P_DF:P_DF + DF_W],
                          [(kct.astype(BF16), True), (kl_ref[:, P_DF:P_DF + DF_W], False)],
                          [(vct_ref[...].astype(BF16), True), (vl_ref[:, PV_DF:PV_DF + DF_W], False)],
                          lam, vec_ref[V_GDS:V_GDS + 1, 0:DF_W], 1.0 - lam_init, b_df)
        mix_ref[rows, :] = jnp.concatenate([o_mla.astype(BF16), ona_ref[rows, :], o_df.astype(BF16)], axis=1)

    def exact():
        _rolled(TM // EXACT_ROWS, lambda i: attend(EXACT_MAX, _row_block(i, EXACT_ROWS)))

    _with_score_bounds(vec_ref, lambda bounds: attend(bounds, slice(None)), exact,
                       df_key_sq=_max_sq_norm(kct, DF_QK))


def _const_spec(shape):
    nd = len(shape)
    return pl.BlockSpec(shape, lambda *_: (0,) * nd, pipeline_mode=pl.Buffered(1))


def _layer_spec(l, shape):
    nd = len(shape)
    return pl.BlockSpec((None,) + tuple(shape), lambda *_: (l,) + (0,) * nd, pipeline_mode=pl.Buffered(1))


def _params(n_axes):
    return pltpu.CompilerParams(dimension_semantics=("arbitrary",) * n_axes, vmem_limit_bytes=VMEM_LIMIT)


def _block_diag(group):
    i = np.arange(MXU_DIM) // group
    return jnp.asarray((i[:, None] == i[None, :]).astype(np.float32), dtype=BF16)


def _rope_tables():
    t = np.arange(DEC_SEQ)
    row = (t // GRID_W).astype(np.float64)
    col = (t % GRID_W).astype(np.float64)
    n = MLA_ROPE // 4
    inv = 1.0 / (ROPE_BASE ** (np.arange(n, dtype=np.float64) * 2.0 / (MLA_ROPE // 2)))
    ar = row[:, None] * inv
    ac = col[:, None] * inv
    ang = np.concatenate([ar, ar, ac, ac], axis=-1)
    cos32, sin32 = np.cos(ang), np.sin(ang)
    first = (np.arange(MLA_ROPE) % (2 * n)) < n
    sa32 = np.where(first, -sin32, 0.0)
    sb32 = np.where(first, 0.0, sin32)

    def mla_tile(v32, fill):
        out = np.full((DEC_SEQ, LANES), fill)
        out[:, MLA_NOPE:MLA_QK] = v32
        return out

    tabs = [mla_tile(cos32, 1.0), mla_tile(sa32, 0.0), mla_tile(sb32, 0.0),
            np.tile(cos32, (1, LANES // DF_QK)), np.tile(sa32, (1, LANES // DF_QK)), np.tile(sb32, (1, LANES // DF_QK))]
    return jnp.asarray(np.stack(tabs).astype(np.float32))


def _feature_major(a):
    a = jnp.swapaxes(a, -1, -2)
    return a.reshape(a.shape[:-3] + (a.shape[-3] * a.shape[-2], a.shape[-1]))


def _tables(p):
    wb = jnp.swapaxes(p['w_in'], 1, 2).astype(BF16)
    kr = wb[:, Q_LORA + KV_LORA:B_SRC]
    z32 = jnp.zeros_like(kr)
    wa = jnp.concatenate([wb[:, :Q_LORA + KV_LORA], kr, z32, kr, z32], axis=1)
    w_uq_t = jnp.swapaxes(p['w_uq'], 1, 2).reshape(DEPTH, MLA_HEADS, MLA_QK, Q_LORA)
    w_uq_p = jnp.pad(w_uq_t, ((0, 0), (0, 0), (0, MLA_PAD - MLA_QK), (0, 0))).reshape(DEPTH, W_MLA_P, Q_LORA).astype(BF16)
    w_ukv = p['w_ukv'].reshape(DEPTH, KV_LORA, MLA_HEADS, MLA_NOPE + MLA_V)
    wk = jnp.pad(w_ukv[..., :MLA_NOPE], ((0, 0), (0, 0), (0, 0), (0, MLA_PAD - MLA_NOPE))).reshape(DEPTH, KV_LORA, W_MLA_P)
    wv = w_ukv[..., MLA_NOPE:].reshape(DEPTH, KV_LORA, MLA_W)
    w_ukv_r = jnp.concatenate([wk, wv], axis=2).astype(BF16)

    def row(v, reps=1):
        v = jnp.tile(v, (1, reps)) if reps > 1 else v
        return [v, jnp.zeros((DEPTH, D_MODEL - v.shape[1]), F32)] if v.shape[1] < D_MODEL else [v]

    pad_head = lambda g: jnp.pad(g, ((0, 0), (0, MLA_PAD - MLA_QK)))
    pieces = (row(p['g_mix']) + row(p['g_ffn']) + row(p['g_qa']) + row(p['g_kva'])
              + row(pad_head(p['g_mla_q']), MLA_HEADS) + row(pad_head(p['g_mla_k']), MLA_HEADS)
              + row(p['g_na_q'], NA_HEADS) + row(p['g_na_k'], NA_HEADS)
              + row(p['g_df_q'], 2 * DF_HEADS) + row(p['g_df_k'], 2 * DF_HEADS) + row(p['g_df_sub'], DF_HEADS)
              + row(p['df_lq1']) + row(p['df_lk1']) + row(p['df_lq2']) + row(p['df_lk2'])
              + [jnp.zeros((DEPTH, D_MODEL), F32)])
    vec = jnp.concatenate(pieces, axis=1).reshape(DEPTH, N_VEC, D_MODEL)
    f = p['na_rpb']
    zpad = jnp.zeros((DEPTH, NA_HEADS, NA_PAIR_TILES, 33), F32)
    g_rows = jnp.concatenate([f[:, :, :-1, NA_KW - 1:], zpad, f[:, :, 1:, :], zpad, f[:, :, :-1, :NA_KW - 1]], axis=-1)
    g_rows = g_rows.reshape(DEPTH, NA_HEADS * NA_PAIR_TILES, LANES)
    return dict(wa=wa, wb=wb, w_uq=w_uq_p, w_ukv=w_ukv_r, vec=vec, g_rows=g_rows)


def _modulation(c_all, w_mod, b_mod):
    tn = 2048
    return pl.pallas_call(
        _mod_kernel,
        grid=(DEPTH, 6 * D_MODEL // tn),
        in_specs=[pl.BlockSpec((N_MOD, D_MODEL), lambda l, j: (0, 0)),
                  pl.BlockSpec((None, D_MODEL, tn), lambda l, j: (l, 0, j)),
                  pl.BlockSpec((DEPTH, tn), lambda l, j: (0, j))],
        out_specs=pl.BlockSpec((None, N_MOD, tn), lambda l, j: (l, 0, j)),
        out_shape=jax.ShapeDtypeStruct((DEPTH, N_MOD, 6 * D_MODEL), F32),
        compiler_params=_params(2),
        name="modulation",
    )(c_all, w_mod, b_mod)


def _mod_spec(l):
    return pl.BlockSpec((None, N_MOD, 6 * D_MODEL), lambda *_: (l, 0, 0))


def _front_weight_specs(l):
    return [_layer_spec(l, (N_VEC, D_MODEL)), _layer_spec(l, (A_COLS, D_MODEL)), _layer_spec(l, (B_SRC + B_COLS, D_MODEL)),
            _layer_spec(l, (W_MLA_P, Q_LORA)), _layer_spec(l, (KV_LORA, W_MLA_P + MLA_W)),
            _const_spec((MXU_DIM, MXU_DIM)), _const_spec((MXU_DIM, MXU_DIM))]


def _front_weights(t, consts):
    return (t['vec'], t['wa'], t['wb'], t['w_uq'], t['w_ukv'], consts['bd64'], consts['bd32'])


def _context_mixer(l, lam_init, x, mods, t, consts, ffn_f32, prev_caches, after=()):
    n_tok = BATCH * SEQ
    n_alias = len(prev_caches)
    tok = lambda w: pl.BlockSpec((CTX_SEQS_PER_STEP * SEQ, w), lambda b: (b, 0))
    if n_alias == 0:
        assert l == 0
        lay = lambda *s: pl.BlockSpec((CTX_SEQS_PER_STEP, DEPTH) + s, lambda b: (b, 0) + (0,) * len(s))
    else:
        lay = lambda *s: pl.BlockSpec((CTX_SEQS_PER_STEP, None) + s, lambda b: (b, l) + (0,) * len(s))
    cache_shapes = [(SEQ, KV_LORA), (MLA_ROPE, SEQ), (NA_HEADS, NA_HD, SEQ), (NA_HEADS, NA_HD, SEQ),
                    (DF_HEADS, DF_HD, SEQ), (DF_HEADS, DF_HD, SEQ)]
    weights = _front_weights(t, consts)
    steps = BATCH // CTX_SEQS_PER_STEP
    ffn_chunks = [(w.shape[1] // steps, w.shape[2]) for w in ffn_f32]
    n_in = 2 + len(weights) + len(ffn_f32)
    mix, *outs = pl.pallas_call(
        functools.partial(_ctx_kernel, lam_init, n_alias, len(after)),
        grid=(steps,),
        in_specs=[tok(D_MODEL), _mod_spec(l)]
        + _front_weight_specs(l) + [pl.BlockSpec((None,) + c, lambda b: (l, b, 0)) for c in ffn_chunks]
        + [pl.BlockSpec(memory_space=pl.ANY)] * (n_alias + len(after)),
        out_specs=[tok(D_MODEL)] + [lay(*s) for s in cache_shapes] + [pl.BlockSpec(c, lambda b: (b, 0)) for c in ffn_chunks],
        out_shape=[jax.ShapeDtypeStruct((n_tok, D_MODEL), BF16)]
        + [jax.ShapeDtypeStruct((BATCH, DEPTH) + s, F32) for s in cache_shapes]
        + [jax.ShapeDtypeStruct(w.shape[1:], BF16) for w in ffn_f32],
        input_output_aliases={n_in + i: 1 + i for i in range(n_alias)},
        compiler_params=_params(1),
        name="context_mixer",
    )(x, mods, *weights, *ffn_f32, *prev_caches, *after)
    return mix, outs[:N_CACHE], outs[N_CACHE:]


def _finish(l, xc, mixc, xl, mixl, mods, t, ffn_bf16):
    nc, nl = xc.shape[0] // TM_FINISH, xl.shape[0] // TM_FINISH
    ctx_tok = pl.BlockSpec((TM_FINISH, D_MODEL), lambda i: (jnp.minimum(i, nc - 1), 0))
    lat_tok = pl.BlockSpec((TM_FINISH, D_MODEL), lambda i: (jnp.maximum(i - nc, 0), 0))
    return pl.pallas_call(
        functools.partial(_finish_kernel, nc),
        grid=(nc + nl,),
        in_specs=[ctx_tok, ctx_tok, lat_tok, lat_tok,
                  _mod_spec(l),
                  _layer_spec(l, (N_VEC, D_MODEL))] + [_const_spec(w.shape) for w in ffn_bf16],
        out_specs=[ctx_tok, lat_tok],
        out_shape=[jax.ShapeDtypeStruct(xc.shape, F32), jax.ShapeDtypeStruct(xl.shape, F32)],
        compiler_params=_params(1),
        name="finish",
    )(xc, mixc, xl, mixl, mods, t['vec'], *ffn_bf16)


def _latent_front(l, x, mods, t, consts):
    n_tok = DEC_BATCH * DEC_SEQ
    tm = TM_LAT_FRONT
    blocks_per_seq = DEC_SEQ // tm
    tok = lambda w: pl.BlockSpec((tm, w), lambda i: (i, 0))
    wspecs = _front_weight_specs(l)
    weights = _front_weights(t, consts)
    return pl.pallas_call(
        _lat_front_kernel,
        grid=(n_tok // tm,),
        in_specs=[tok(D_MODEL),
                  _mod_spec(l),
                  wspecs[0], pl.BlockSpec((6, tm, LANES), lambda i: (0, i % blocks_per_seq, 0))] + wspecs[1:],
        out_specs=[tok(QK_PACK), tok(QK_PACK), tok(V_PACK)],
        out_shape=[jax.ShapeDtypeStruct((n_tok, QK_PACK), BF16), jax.ShapeDtypeStruct((n_tok, QK_PACK), BF16),
                   jax.ShapeDtypeStruct((n_tok, V_PACK), BF16)],
        compiler_params=_params(1),
        name="latent_front",
    )(x, mods, weights[0], consts['rope'], *weights[1:])


def _context_kv(l, ckv, krope_t, t):
    lay = lambda r, w: pl.BlockSpec((None, None, r, w), lambda b: (b, l, 0, 0))
    out = lambda w: pl.BlockSpec((None, PAST_LEN, w), lambda b: (b, 0, 0))
    return pl.pallas_call(
        _ctx_kv_kernel,
        grid=(DEC_BATCH,),
        in_specs=[lay(PAST_LEN, KV_LORA), lay(MLA_ROPE, PAST_LEN),
                  _layer_spec(l, (N_VEC, D_MODEL)), _layer_spec(l, (KV_LORA, W_MLA_P + MLA_W))],
        out_specs=[out(W_MLA_P), out(MLA_W)],
        out_shape=[jax.ShapeDtypeStruct((DEC_BATCH, PAST_LEN, W_MLA_P), BF16),
                   jax.ShapeDtypeStruct((DEC_BATCH, PAST_LEN, MLA_W), BF16)],
        compiler_params=_params(1),
        name="context_kv",
    )(ckv, krope_t, t['vec'], t['w_ukv'])


def _latent_na(l, q, k, v, kct, vct, g_rows, vec):
    n_tok = DEC_BATCH * DEC_SEQ
    na_blk = P_NA // NA_W
    steps = GRID_ROWS // NA_ROWS_PER_STEP
    tq = NA_ROWS_PER_STEP * GRID_W
    ctx = pl.BlockSpec((None, None, NA_W, PAST_LEN), lambda b, j: (b, l, 0, 0))
    return pl.pallas_call(
        _na_lat_kernel,
        grid=(DEC_BATCH, steps),
        in_specs=[pl.BlockSpec((tq, NA_W), lambda b, j: (b * steps + j, na_blk)),
                  pl.BlockSpec((DEC_SEQ, NA_W), lambda b, j: (b, na_blk)),
                  pl.BlockSpec((DEC_SEQ, NA_W), lambda b, j: (b, PV_NA // NA_W)),
                  ctx, ctx, _layer_spec(l, (NA_HEADS * NA_PAIR_TILES, LANES)), _layer_spec(l, (N_VEC, D_MODEL))],
        out_specs=pl.BlockSpec((tq, NA_W), lambda b, j: (b * steps + j, 0)),
        out_shape=jax.ShapeDtypeStruct((n_tok, NA_W), BF16),
        scratch_shapes=[pltpu.VMEM((NA_HEADS * NA_PAIR_TILES, GRID_W, LANES), F32)],
        compiler_params=_params(2),
        name="latent_neighbourhood",
    )(q, k, v, kct, vct, g_rows, vec)


def _latent_attention(l, lam_init, q, k, v, kc, vc, kct, vct, o_na, t):
    n_tok = DEC_BATCH * DEC_SEQ
    nq = DEC_SEQ // TM
    ctx = pl.BlockSpec((None, None, DF_W, PAST_LEN), lambda b, j: (b, l, 0, 0))
    return pl.pallas_call(
        functools.partial(_lat_attn_kernel, lam_init),
        grid=(DEC_BATCH, nq),
        in_specs=[pl.BlockSpec((TM, QK_PACK), lambda b, j: (b * nq + j, 0)),
                  pl.BlockSpec((DEC_SEQ, QK_PACK), lambda b, j: (b, 0)),
                  pl.BlockSpec((DEC_SEQ, V_PACK), lambda b, j: (b, 0)),
                  pl.BlockSpec((None, PAST_LEN, W_MLA_P), lambda b, j: (b, 0, 0)),
                  pl.BlockSpec((None, PAST_LEN, MLA_W), lambda b, j: (b, 0, 0)),
                  ctx, ctx,
                  pl.BlockSpec((TM, NA_W), lambda b, j: (b * nq + j, 0)),
                  _layer_spec(l, (N_VEC, D_MODEL))],
        out_specs=pl.BlockSpec((TM, D_MODEL), lambda b, j: (b * nq + j, 0)),
        out_shape=jax.ShapeDtypeStruct((n_tok, D_MODEL), BF16),
        compiler_params=_params(2),
        name="latent_attention",
    )(q, k, v, kc, vc, kct, vct, o_na, t['vec'])


def kernel(x_prompt, x_sample, cache_mla_ckv, cache_mla_krope, cache_na_k, cache_na_v, cache_df_k, cache_df_v, c, c_ctx, w_mod, b_mod, g_mix, w_in, g_qa, w_uq, g_kva, w_ukv, g_mla_q, g_mla_k, g_na_q, g_na_k, na_rpb, g_df_q, g_df_k, df_lq1, df_lk1, df_lq2, df_lk2, g_df_sub, w_out, g_ffn, w_gate, w_up, w_down):
    p = dict(g_mix=g_mix, w_in=w_in, g_qa=g_qa, w_uq=w_uq, g_kva=g_kva, w_ukv=w_ukv, g_mla_q=g_mla_q, g_mla_k=g_mla_k,
             g_na_q=g_na_q, g_na_k=g_na_k, na_rpb=na_rpb, g_df_q=g_df_q, g_df_k=g_df_k, df_lq1=df_lq1, df_lk1=df_lk1,
             df_lq2=df_lq2, df_lk2=df_lk2, g_df_sub=g_df_sub, w_out=w_out, g_ffn=g_ffn, w_gate=w_gate, w_up=w_up,
             w_down=w_down)
    consts = dict(bd64=_block_diag(NA_HD), bd32=_block_diag(DF_QK), rope=_rope_tables())
    t = _tables(p)

    c_all = jnp.concatenate([c_ctx[None, :], c, jnp.zeros((N_MOD - 1 - DEC_BATCH, D_MODEL), F32)], axis=0)
    mods = _modulation(c_all, w_mod, b_mod)

    krope_t = jnp.swapaxes(cache_mla_krope, -1, -2)
    na_kt, na_vt, df_kt, df_vt = (_feature_major(a) for a in (cache_na_k, cache_na_v, cache_df_k, cache_df_v))

    xp = x_prompt.reshape(BATCH * SEQ, D_MODEL)
    xs = x_sample.reshape(DEC_BATCH * DEC_SEQ, D_MODEL)
    new_caches = ()
    ffn_f32 = (w_out, w_gate, w_up, w_down)
    for l in range(DEPTH):
        lam_init = 0.8 - 0.6 * math.exp(-0.3 * l)
        q, k, v = _latent_front(l, xs, mods, t, consts)
        kc, vc = _context_kv(l, cache_mla_ckv, krope_t, t)
        o_na = _latent_na(l, q, k, v, na_kt, na_vt, t['g_rows'], t['vec'])
        mix_s = _latent_attention(l, lam_init, q, k, v, kc, vc, df_kt, df_vt, o_na, t)
        mix, new_caches, ffn_bf16 = _context_mixer(l, lam_init, xp, mods, t, consts, ffn_f32, new_caches, after=(mix_s,))
        xp, xs = _finish(l, xp, mix, xs, mix_s, mods, t, ffn_bf16)
    ckv_new, *narrow = new_caches
    return (xp.reshape(BATCH, SEQ, D_MODEL), xs.reshape(DEC_BATCH, DEC_SEQ, D_MODEL), ckv_new,
            *(jnp.swapaxes(a, -1, -2) for a in narrow))
```

```python
import functools
import math

import numpy as np
import jax
import jax.numpy as jnp
from jax import lax
from jax.experimental import pallas as pl
from jax.experimental.pallas import tpu as pltpu

F32 = jnp.float32
BF16 = jnp.bfloat16

D_MODEL = 1024
BATCH = 32
SEQ = 256
DEPTH = 2
DEC_BATCH = 2
DEC_SEQ = 1024
PAST_LEN = 256
GRID_W = 64
GRID_ROWS = DEC_SEQ // GRID_W
MLA_HEADS = 6
MLA_NOPE = 64
MLA_ROPE = 32
MLA_QK = MLA_NOPE + MLA_ROPE
MLA_V = 64
MLA_PAD = 128
Q_LORA = 256
KV_LORA = 128
NA_HEADS = 6
NA_HD = 64
NA_KR = 8
NA_KW = 16
DF_HEADS = 4
DF_HD = 64
DF_QK = 32
MLA_W = MLA_HEADS * MLA_V
NA_W = NA_HEADS * NA_HD
DF_W = DF_HEADS * DF_HD
D_FF = -(-8 * D_MODEL // (3 * 256)) * 256
ROPE_BASE = 10000.0
EPS = 1e-6
NEG = -1e30
LOG2E = math.log2(math.e)
MAX_FREE_SOFTMAX_BOUND = 60.0

LANES = 128
MXU_DIM = 256

A_CQ = 0
A_CKV = A_CQ + Q_LORA
A_KR = A_CKV + KV_LORA
A_COLS = A_KR + LANES
B_SRC = Q_LORA + KV_LORA + MLA_ROPE
B_NAQ = 0
B_NAK = B_NAQ + NA_W
B_NAV = B_NAK + NA_W
B_DFQ = B_NAV + NA_W
B_DFK = B_DFQ + DF_W
B_DFV = B_DFK + DF_W
B_COLS = B_DFV + DF_W

W_MLA_P = MLA_HEADS * MLA_PAD
QK_PACK = W_MLA_P + NA_W + DF_W
V_PACK = MLA_W + NA_W + DF_W
P_MLA = 0
P_NA = W_MLA_P
P_DF = P_NA + NA_W
PV_MLA = 0
PV_NA = MLA_W
PV_DF = MLA_W + NA_W

(V_GMIX, V_GFFN, V_GQA, V_GKVA, V_GMQ, V_GMK, V_GNQ, V_GNK, V_GDQ, V_GDK, V_GDS,
 V_LQ1, V_LK1, V_LQ2, V_LK2) = range(15)
N_VEC = 16

N_MOD = 8
MOD_COLS_PER_STEP = 3072
TM = 512
EXACT_ROWS = 128
TM_LAT_FRONT = 1024
LAT_FRONT_CHUNK = 256
TM_FINISH = 512
CTX_SEQS_PER_STEP = 2
NA_ROWS_PER_STEP = 4
VMEM_LIMIT = 56 * 1024 * 1024

NA_PAIR_TILES = 2 * NA_KR - 2
N_CACHE = 6
N_FFN_W = 4


def _dot(a, b):
    return jnp.dot(a, b, preferred_element_type=F32)


def _dot_nt(a, b):
    return lax.dot_general(a, b, (((1,), (1,)), ((), ())), preferred_element_type=F32)


def _lane_iota(shape):
    return lax.broadcasted_iota(jnp.int32, shape, len(shape) - 1)


def _rms_rows(x, g):
    ms = jnp.mean(x * x, axis=-1, keepdims=True)
    return x * lax.rsqrt(ms + EPS) * g


def _tile_rms(x, g, n_real):
    outs = []
    for c0 in range(0, x.shape[1], LANES):
        xt = x[:, c0:c0 + LANES]
        ms = jnp.sum(xt * xt, axis=-1, keepdims=True) * (1.0 / n_real)
        outs.append(xt * lax.rsqrt(ms + EPS) * g[:, c0:c0 + LANES])
    return jnp.concatenate(outs, axis=1)


def _seg_rms(x, bd_ref, g, group):
    width = x.shape[1]
    sq = (x * x).astype(BF16)
    parts = []
    for c0 in range(0, width, MXU_DIM):
        w = min(MXU_DIM, width - c0)
        parts.append(_dot(sq[:, c0:c0 + w], bd_ref[0:w, 0:w]))
    ss = parts[0] if len(parts) == 1 else jnp.concatenate(parts, axis=1)
    return x * lax.rsqrt(ss * (1.0 / group) + EPS) * g


def _rope_tiles(x, cos, sa, sb):
    outs = []
    for t in range(x.shape[1] // LANES):
        xt = x[:, t * LANES:(t + 1) * LANES]
        up = pltpu.roll(xt, LANES - MLA_ROPE // 4, 1)
        dn = pltpu.roll(xt, MLA_ROPE // 4, 1)
        outs.append(xt * cos + up * sa + dn * sb)
    return outs[0] if len(outs) == 1 else jnp.concatenate(outs, axis=1)


def _diff_lambda(vec_ref, lam_init):
    a = jnp.sum(vec_ref[V_LQ1:V_LQ1 + 1, 0:DF_QK] * vec_ref[V_LK1:V_LK1 + 1, 0:DF_QK], axis=-1, keepdims=True)
    b = jnp.sum(vec_ref[V_LQ2:V_LQ2 + 1, 0:DF_QK] * vec_ref[V_LK2:V_LK2 + 1, 0:DF_QK], axis=-1, keepdims=True)
    return jnp.exp(a) - jnp.exp(b) + lam_init


def _mixer_front(x, mod, vec_ref, wa_ref, wb_ref, w_uq_ref, bd64_ref, bd32_ref):
    sh = mod[:, 0:D_MODEL]
    sc = mod[:, D_MODEL:2 * D_MODEL]
    h = (_rms_rows(x, vec_ref[V_GMIX:V_GMIX + 1, :]) * (1.0 + sc) + sh).astype(BF16)
    za = _dot_nt(h, wa_ref[...])
    zb = _dot_nt(h, wb_ref[B_SRC:B_SRC + B_COLS, :])
    cqn = _rms_rows(za[:, A_CQ:A_CQ + Q_LORA], vec_ref[V_GQA:V_GQA + 1, 0:Q_LORA])
    q_raw = _dot_nt(cqn.astype(BF16), w_uq_ref[...])
    q_mla = _tile_rms(q_raw, vec_ref[V_GMQ:V_GMQ + 1, 0:W_MLA_P] * (MLA_QK ** -0.5 * LOG2E), MLA_QK)
    ckv_n = _rms_rows(za[:, A_CKV:A_CKV + KV_LORA], vec_ref[V_GKVA:V_GKVA + 1, 0:KV_LORA])
    kr_tile = za[:, A_KR:A_KR + LANES]
    q_na = _seg_rms(zb[:, B_NAQ:B_NAQ + NA_W], bd64_ref, vec_ref[V_GNQ:V_GNQ + 1, 0:NA_W] * (NA_HD ** -0.5 * LOG2E), NA_HD)
    k_na = _seg_rms(zb[:, B_NAK:B_NAK + NA_W], bd64_ref, vec_ref[V_GNK:V_GNK + 1, 0:NA_W], NA_HD)
    v_na = zb[:, B_NAV:B_NAV + NA_W]
    q_df = _seg_rms(zb[:, B_DFQ:B_DFQ + DF_W], bd32_ref, vec_ref[V_GDQ:V_GDQ + 1, 0:DF_W] * (DF_QK ** -0.5 * LOG2E), DF_QK)
    k_df = _seg_rms(zb[:, B_DFK:B_DFK + DF_W], bd32_ref, vec_ref[V_GDK:V_GDK + 1, 0:DF_W], DF_QK)
    v_df = zb[:, B_DFV:B_DFV + DF_W]
    return q_mla, ckv_n, kr_tile, q_na, k_na, v_na, q_df, k_df, v_df


def _mla_kv(ckv_n, kr_tile, vec_ref, w_ukv_ref):
    kv = _dot(ckv_n.astype(BF16), w_ukv_ref[...])
    lane = _lane_iota((1, LANES))
    kr = jnp.where((lane >= MLA_NOPE) & (lane < MLA_QK), kr_tile, 0.0)
    k_pre = kv[:, 0:W_MLA_P] + jnp.concatenate([kr] * MLA_HEADS, axis=1)
    k = _tile_rms(k_pre, vec_ref[V_GMK:V_GMK + 1, 0:W_MLA_P], MLA_QK)
    return k, kv[:, W_MLA_P:W_MLA_P + MLA_W]


def _softmax_parts(s, bound=None):
    m = jnp.max(s, axis=-1, keepdims=True) if bound is None else bound
    p = jnp.exp2(s - m)
    return p, 1.0 / jnp.sum(p, axis=-1, keepdims=True)


def _max_sq_norm(kt, group):
    sq = kt * kt
    best = None
    for r0 in range(0, kt.shape[0], group):
        n2 = jnp.sum(sq[r0:r0 + group], axis=0, keepdims=True)
        best = n2 if best is None else jnp.maximum(best, n2)
    return jnp.max(best, axis=-1, keepdims=True)


def _score_bounds(vec_ref, bias_max=None, na_key_sq=None, df_key_sq=None):
    gmax = lambda row, w: jnp.max(jnp.abs(vec_ref[row:row + 1, 0:w]), axis=-1, keepdims=True)
    slack = LOG2E * (1.0 + 2.0 ** -6)

    def key_norm(row, w, d, measured_sq):
        k = gmax(row, w) * d ** 0.5
        return k if measured_sq is None else jnp.maximum(k, jnp.sqrt(measured_sq))

    b_mla = gmax(V_GMQ, W_MLA_P) * key_norm(V_GMK, W_MLA_P, MLA_QK, None) * slack
    b_na = gmax(V_GNQ, NA_W) * key_norm(V_GNK, NA_W, NA_HD, na_key_sq) * slack
    if bias_max is not None:
        b_na = b_na + bias_max * LOG2E
    b_df = gmax(V_GDQ, DF_W) * key_norm(V_GDK, DF_W, DF_QK, df_key_sq) * slack
    worst = jnp.maximum(b_mla, jnp.maximum(b_na, b_df))
    return (b_mla, b_na, b_df), worst[0, 0] <= MAX_FREE_SOFTMAX_BOUND


EXACT_MAX = (None, None, None)


def _with_score_bounds(vec_ref, bounded, exact, **measured):
    bounds, ok = _score_bounds(vec_ref, **measured)
    pl.when(ok)(lambda: bounded(bounds))
    pl.when(jnp.logical_not(ok))(exact)


def _row_block(i, n):
    return slice(i * n, (i + 1) * n) if isinstance(i, int) else pl.ds(pl.multiple_of(i * n, n), n)


def _rolled(n, body):
    def step(i, carry):
        body(i)
        return carry

    lax.fori_loop(0, n, step, 0)


def _scores(q, k_segs):
    parts = [_dot(q, k) if feature_major else _dot_nt(q, k) for k, feature_major in k_segs]
    return parts[0] if len(parts) == 1 else jnp.concatenate(parts, axis=1)


def _pv(p, v_segs):
    out = None
    c0 = 0
    for v, feature_major in v_segs:
        n = v.shape[1] if feature_major else v.shape[0]
        o = _dot_nt(p[:, c0:c0 + n], v) if feature_major else _dot(p[:, c0:c0 + n], v)
        out = o if out is None else out + o
        c0 += n
    return out


def _seg_tile(seg, t):
    a, feature_major = seg
    return (a[t * LANES:(t + 1) * LANES, :] if feature_major else a[:, t * LANES:(t + 1) * LANES]), feature_major


def _lane_groups(qt, width):
    lane = _lane_iota((1, LANES))
    zero = jnp.zeros_like(qt)
    return jnp.concatenate(
        [jnp.where((lane >= g * width) & (lane < (g + 1) * width), qt, zero) for g in range(LANES // width)], axis=0)


def _pair_select(o2):
    tq = o2.shape[0] // 2
    return jnp.where(_lane_iota((1, LANES)) < NA_HD, o2[0:tq], o2[tq:2 * tq])


def _mla_attend(q, k_segs, v_segs, bound=None):
    outs = []
    for t in range(MLA_HEADS // 2):
        vt = [_seg_tile(v, t) for v in v_segs]
        halves = []
        for h in (2 * t, 2 * t + 1):
            p, il = _softmax_parts(_scores(q[:, h * MLA_PAD:(h + 1) * MLA_PAD], [_seg_tile(k, h) for k in k_segs]),
                                   bound)
            halves.append(_pv(p.astype(BF16), vt) * il)
        outs.append(jnp.where(_lane_iota((1, LANES)) < MLA_V, halves[0], halves[1]))
    return jnp.concatenate(outs, axis=1)


def _na_tile_attend(qt, k_segs, v_segs, bias=None, bound=None):
    s = _scores(_lane_groups(qt, NA_HD), k_segs)
    if bias is not None:
        nb = bias.shape[1]
        s = jnp.concatenate([s[:, 0:nb] + bias, s[:, nb:]], axis=1)
    p, il = _softmax_parts(s, bound)
    return _pair_select(_pv(p.astype(BF16), v_segs) * il)


def _na_attend_full(q, k_segs, v_segs, bound=None):
    return jnp.concatenate(
        [_na_tile_attend(q[:, t * LANES:(t + 1) * LANES], [_seg_tile(k, t) for k in k_segs],
                         [_seg_tile(v, t) for v in v_segs], bound=bound) for t in range(NA_HEADS // 2)], axis=1)


def _df_attend(q, k_segs, v_segs, lam, g_sub, out_scale, bound=None):
    outs = []
    lane = _lane_iota((1, LANES))
    tq = q.shape[0]
    for t in range(DF_HEADS // 2):
        kt = [_seg_tile(k, t) for k in k_segs]
        vt = [_seg_tile(v, t) for v in v_segs]
        p, il = _softmax_parts(_scores(_lane_groups(q[:, t * LANES:(t + 1) * LANES], DF_QK), kt), bound)
        pn = []
        for hh in range(2):
            r1, r2 = 2 * hh * tq, (2 * hh + 1) * tq
            pn.append((p[r1:r1 + tq] * il[r1:r1 + tq] - p[r2:r2 + tq] * (lam * il[r2:r2 + tq])).astype(BF16))
        o = _pair_select(_pv(jnp.concatenate(pn, axis=0), vt))
        o2 = o * o
        ms_e = jnp.sum(jnp.where(lane < DF_HD, o2, 0.0), axis=-1, keepdims=True)
        ms_o = jnp.sum(jnp.where(lane >= DF_HD, o2, 0.0), axis=-1, keepdims=True)
        r = lax.rsqrt(jnp.where(lane < DF_HD, ms_e, ms_o) * (1.0 / DF_HD) + EPS)
        outs.append(o * r * (g_sub[:, t * LANES:(t + 1) * LANES] * out_scale))
    return jnp.concatenate(outs, axis=1)


def _mod_kernel(c_ref, w_ref, b_ref, o_ref):
    c = c_ref[...]
    s = c * jax.nn.sigmoid(c)
    o_ref[...] = _dot(s.astype(BF16), w_ref[...].astype(BF16)) + b_ref[pl.ds(pl.program_id(0), 1), :]


def _ctx_kernel(lam_init, n_alias, n_after, x_ref, mod_ref, vec_ref, wa_ref, wb_ref, w_uq_ref, w_ukv_ref, bd64_ref,
                bd32_ref, *rest):
    ffn_f32, rest = rest[:N_FFN_W], rest[N_FFN_W + n_alias + n_after:]
    mix_ref, *cache_refs = rest[:1 + N_CACHE]
    for src, dst in zip(ffn_f32, rest[1 + N_CACHE:]):
        dst[...] = src[...].astype(BF16)
    if n_alias == 0:
        for ref in cache_refs:
            ref[:, 1:] = jnp.zeros((ref.shape[0], ref.shape[1] - 1) + ref.shape[2:], F32)
        cache_refs = [ref.at[:, 0] for ref in cache_refs]
    ckv_ref, kr_ref, nak_ref, nav_ref, dfk_ref, dfv_ref = cache_refs
    lam = _diff_lambda(vec_ref, lam_init)
    mod = mod_ref[0:1, :]
    bf = lambda a: a.astype(BF16)
    seg = lambda a: [(a.astype(BF16), False)]
    def sequence(s, bounds):
        b_mla, b_na, b_df = bounds
        rows = _row_block(s, SEQ)
        q_mla, ckv_n, kr_tile, q_na, k_na, v_na, q_df, k_df, v_df = _mixer_front(
            x_ref[rows, :], mod, vec_ref, wa_ref, wb_ref, w_uq_ref, bd64_ref, bd32_ref)
        k_mla, v_mla = _mla_kv(ckv_n, kr_tile, vec_ref, w_ukv_ref)
        ckv_ref[s] = ckv_n
        kr_ref[s] = kr_tile.T[0:MLA_ROPE]
        for ref, a in ((nak_ref, k_na), (nav_ref, v_na), (dfk_ref, k_df), (dfv_ref, v_df)):
            at = a.T
            for h in range(ref.shape[1]):
                ref[s, h] = at[h * NA_HD:(h + 1) * NA_HD]
        o_mla = _mla_attend(bf(q_mla), seg(k_mla), seg(v_mla), b_mla)
        o_na = _na_attend_full(bf(q_na), seg(k_na), seg(v_na), b_na)
        o_df = _df_attend(bf(q_df), seg(k_df), seg(v_df), lam, vec_ref[V_GDS:V_GDS + 1, 0:DF_W], 1.0 - lam_init, b_df)
        mix_ref[rows, :] = jnp.concatenate([o_mla, o_na, o_df], axis=1).astype(BF16)

    def bounded(bounds):
        for s in range(CTX_SEQS_PER_STEP):
            sequence(s, bounds)

    _with_score_bounds(vec_ref, bounded, lambda: _rolled(CTX_SEQS_PER_STEP, lambda s: sequence(s, EXACT_MAX)))


def _finish_kernel(n_ctx_blocks, xc_ref, mixc_ref, xl_ref, mixl_ref, mod_ref, vec_ref, w_out_ref, w_gate_ref, w_up_ref,
                   w_down_ref, yc_ref, yl_ref):
    def block(x_ref, mix_ref, y_ref, mod_row):
        mod = mod_ref[pl.ds(mod_row, 1), :]
        gate_m = mod[:, 2 * D_MODEL:3 * D_MODEL]
        sh = mod[:, 3 * D_MODEL:4 * D_MODEL]
        sc = mod[:, 4 * D_MODEL:5 * D_MODEL]
        gate_f = mod[:, 5 * D_MODEL:6 * D_MODEL]
        x1 = x_ref[...] + gate_m * _dot(mix_ref[...], w_out_ref[...])
        h = (_rms_rows(x1, vec_ref[V_GFFN:V_GFFN + 1, :]) * (1.0 + sc) + sh).astype(BF16)
        g = _dot(h, w_gate_ref[...])
        u = _dot(h, w_up_ref[...])
        a = (g * jax.nn.sigmoid(g) * u).astype(BF16)
        y_ref[...] = x1 + gate_f * _dot(a, w_down_ref[...])

    is_ctx = pl.program_id(0) < n_ctx_blocks
    lat_row = 1 + jnp.maximum(pl.program_id(0) - n_ctx_blocks, 0) // (DEC_SEQ // TM_FINISH)
    pl.when(is_ctx)(lambda: block(xc_ref, mixc_ref, yc_ref, 0))
    pl.when(jnp.logical_not(is_ctx))(lambda: block(xl_ref, mixl_ref, yl_ref, lat_row))


def _lat_front_kernel(x_ref, mod_ref, vec_ref, rope_ref, wa_ref, wb_ref, w_uq_ref, w_ukv_ref, bd64_ref, bd32_ref,
                      q_ref, k_ref, v_ref):
    mod = mod_ref[pl.ds(1 + pl.program_id(0) // (DEC_SEQ // TM_LAT_FRONT), 1), :]
    for c in range(TM_LAT_FRONT // LAT_FRONT_CHUNK):
        rows = slice(c * LAT_FRONT_CHUNK, (c + 1) * LAT_FRONT_CHUNK)
        q_mla, ckv_n, kr_tile, q_na, k_na, v_na, q_df, k_df, v_df = _mixer_front(
            x_ref[rows, :], mod, vec_ref, wa_ref, wb_ref, w_uq_ref, bd64_ref, bd32_ref)
        k_mla, v_mla = _mla_kv(ckv_n, kr_tile, vec_ref, w_ukv_ref)
        cm, sam, sbm = rope_ref[0, rows], rope_ref[1, rows], rope_ref[2, rows]
        cd, sad, sbd = rope_ref[3, rows], rope_ref[4, rows], rope_ref[5, rows]
        q_ref[rows, :] = jnp.concatenate(
            [_rope_tiles(q_mla, cm, sam, sbm), q_na, _rope_tiles(q_df, cd, sad, sbd)], axis=1).astype(BF16)
        k_ref[rows, :] = jnp.concatenate(
            [_rope_tiles(k_mla, cm, sam, sbm), k_na, _rope_tiles(k_df, cd, sad, sbd)], axis=1).astype(BF16)
        v_ref[rows, :] = jnp.concatenate([v_mla, v_na, v_df], axis=1).astype(BF16)


def _na_lat_kernel(q_ref, kl_ref, vl_ref, kct_ref, vct_ref, g_ref, vec_ref, ckv_ref, krt_ref, w_ukv_ref,
                   o_ref, kc_mla_ref, vc_mla_ref, bias_ref):
    b = pl.program_id(0)
    j = pl.program_id(1)

    @pl.when(j == 0)
    def _context_mla_kv():
        krt = jnp.concatenate([jnp.zeros((MLA_NOPE, PAST_LEN), F32), krt_ref[...],
                               jnp.zeros((LANES - MLA_QK, PAST_LEN), F32)], axis=0)
        k_mla, v_mla = _mla_kv(ckv_ref[...], krt.T, vec_ref, w_ukv_ref)
        kc_mla_ref[...] = k_mla.astype(BF16)
        vc_mla_ref[...] = v_mla.astype(BF16)

    @pl.when((b == 0) & (j == 0))
    def _build_bias():
        c = lax.broadcasted_iota(jnp.int32, (GRID_W, LANES), 0)
        kc = _lane_iota((GRID_W, LANES)) % GRID_W
        start = jnp.clip(c - NA_KW // 2, 0, GRID_W - NA_KW)
        in_win = (kc >= start) & (kc < start + NA_KW)

        def body(i, carry):
            row = jnp.broadcast_to(g_ref[pl.ds(i, 1), :], (GRID_W, LANES))
            toep = pltpu.roll(row, 0, 1, stride=1, stride_axis=0)
            bias_ref[i] = jnp.where(in_win, toep * LOG2E, NEG)
            return carry

        lax.fori_loop(0, NA_HEADS * NA_PAIR_TILES, body, 0)

    n_win = NA_KR * GRID_W
    kct_f32 = kct_ref[...]
    kct = kct_f32.astype(BF16)
    vct = vct_ref[...].astype(BF16)

    def win_start(r):
        return jnp.clip(r - NA_KR // 2, 0, GRID_ROWS - NA_KR)

    def attend(grid_rows, rs, bound):
        a0, n = grid_rows[0], len(grid_rows)
        row0 = pl.multiple_of(rs * GRID_W, GRID_W)
        outs = []
        for t in range(NA_HEADS // 2):
            lanes = slice(t * LANES, (t + 1) * LANES)
            bias = jnp.concatenate(
                [jnp.concatenate([bias_ref[h * NA_PAIR_TILES + (rs - (j * NA_ROWS_PER_STEP + a) + NA_KR - 1) + 2 * m]
                                  for m in range(NA_KR // 2)], axis=1)
                 for h in (2 * t, 2 * t + 1) for a in grid_rows], axis=0)
            outs.append(_na_tile_attend(
                q_ref[_row_block(a0, GRID_W) if n == 1 else slice(a0 * GRID_W, (a0 + n) * GRID_W), lanes],
                [(kl_ref[pl.ds(row0, n_win), lanes], False), (kct[lanes, :], True)],
                [(vl_ref[pl.ds(row0, n_win), lanes], False), (vct[lanes, :], True)], bias, bound))
        return jnp.concatenate(outs, axis=1).astype(BF16)

    first_r, last_r = j * NA_ROWS_PER_STEP, (j + 1) * NA_ROWS_PER_STEP - 1
    shared = win_start(first_r) == win_start(last_r)

    def attend_step(bounds):
        bound = bounds[1]

        @pl.when(shared)
        def _shared_window():
            o_ref[...] = attend(tuple(range(NA_ROWS_PER_STEP)), win_start(first_r), bound)

        @pl.when(jnp.logical_not(shared))
        def _per_row_windows():
            for a in range(NA_ROWS_PER_STEP):
                o_ref[a * GRID_W:(a + 1) * GRID_W, :] = attend((a,), win_start(first_r + a), bound)

    def exact_row(a):
        o_ref[_row_block(a, GRID_W), :] = attend((a,), win_start(first_r + a), None)

    bias_max = jnp.max(jnp.max(jnp.abs(g_ref[...]), axis=-1, keepdims=True), axis=0, keepdims=True)
    _with_score_bounds(vec_ref, attend_step, lambda: _rolled(NA_ROWS_PER_STEP, exact_row),
                       bias_max=bias_max, na_key_sq=_max_sq_norm(kct_f32, NA_HD))


def _lat_attn_kernel(lam_init, q_ref, kl_ref, vl_ref, kc_ref, vc_ref, kct_ref, vct_ref, ona_ref, vec_ref, mix_ref):
    lam = _diff_lambda(vec_ref, lam_init)
    kct = kct_ref[...]

    def attend(bounds, rows):
        b_mla, _, b_df = bounds
        o_mla = _mla_attend(q_ref[rows, P_MLA:P_MLA + W_MLA_P],
                            [(kc_ref[...], False), (kl_ref[:, P_MLA:P_MLA + W_MLA_P], False)],
                            [(vc_ref[...], False), (vl_ref[:, PV_MLA:PV_MLA + MLA_W], False)], b_mla)
        o_df = _df_attend(q_ref[rows, P_DF:P_DF + DF_W],
                          [(kct.astype(BF16), True), (kl_ref[:, P_DF:P_DF + DF_W], False)],
                          [(vct_ref[...].astype(BF16), True), (vl_ref[:, PV_DF:PV_DF + DF_W], False)],
                          lam, vec_ref[V_GDS:V_GDS + 1, 0:DF_W], 1.0 - lam_init, b_df)
        mix_ref[rows, :] = jnp.concatenate([o_mla.astype(BF16), ona_ref[rows, :], o_df.astype(BF16)], axis=1)

    def exact():
        _rolled(TM // EXACT_ROWS, lambda i: attend(EXACT_MAX, _row_block(i, EXACT_ROWS)))

    _with_score_bounds(vec_ref, lambda bounds: attend(bounds, slice(None)), exact,
                       df_key_sq=_max_sq_norm(kct, DF_QK))


def _const_spec(shape):
    nd = len(shape)
    return pl.BlockSpec(shape, lambda *_: (0,) * nd, pipeline_mode=pl.Buffered(1))


def _layer_spec(l, shape):
    nd = len(shape)
    return pl.BlockSpec((None,) + tuple(shape), lambda *_: (l,) + (0,) * nd, pipeline_mode=pl.Buffered(1))


def _params(n_axes):
    return pltpu.CompilerParams(dimension_semantics=("arbitrary",) * n_axes, vmem_limit_bytes=VMEM_LIMIT)


def _block_diag(group):
    i = np.arange(MXU_DIM) // group
    return jnp.asarray((i[:, None] == i[None, :]).astype(np.float32), dtype=BF16)


def _rope_tables():
    t = np.arange(DEC_SEQ)
    row = (t // GRID_W).astype(np.float64)
    col = (t % GRID_W).astype(np.float64)
    n = MLA_ROPE // 4
    inv = 1.0 / (ROPE_BASE ** (np.arange(n, dtype=np.float64) * 2.0 / (MLA_ROPE // 2)))
    ar = row[:, None] * inv
    ac = col[:, None] * inv
    ang = np.concatenate([ar, ar, ac, ac], axis=-1)
    cos32, sin32 = np.cos(ang), np.sin(ang)
    first = (np.arange(MLA_ROPE) % (2 * n)) < n
    sa32 = np.where(first, -sin32, 0.0)
    sb32 = np.where(first, 0.0, sin32)

    def mla_tile(v32, fill):
        out = np.full((DEC_SEQ, LANES), fill)
        out[:, MLA_NOPE:MLA_QK] = v32
        return out

    tabs = [mla_tile(cos32, 1.0), mla_tile(sa32, 0.0), mla_tile(sb32, 0.0),
            np.tile(cos32, (1, LANES // DF_QK)), np.tile(sa32, (1, LANES // DF_QK)), np.tile(sb32, (1, LANES // DF_QK))]
    return jnp.asarray(np.stack(tabs).astype(np.float32))


def _feature_major(a):
    a = jnp.swapaxes(a, -1, -2)
    return a.reshape(a.shape[:-3] + (a.shape[-3] * a.shape[-2], a.shape[-1]))


def _tables(p):
    wb = jnp.swapaxes(p['w_in'], 1, 2).astype(BF16)
    kr = wb[:, Q_LORA + KV_LORA:B_SRC]
    z32 = jnp.zeros_like(kr)
    wa = jnp.concatenate([wb[:, :Q_LORA + KV_LORA], kr, z32, kr, z32], axis=1)
    w_uq_t = jnp.swapaxes(p['w_uq'], 1, 2).reshape(DEPTH, MLA_HEADS, MLA_QK, Q_LORA)
    w_uq_p = jnp.pad(w_uq_t, ((0, 0), (0, 0), (0, MLA_PAD - MLA_QK), (0, 0))).reshape(DEPTH, W_MLA_P, Q_LORA).astype(BF16)
    w_ukv = p['w_ukv'].reshape(DEPTH, KV_LORA, MLA_HEADS, MLA_NOPE + MLA_V)
    wk = jnp.pad(w_ukv[..., :MLA_NOPE], ((0, 0), (0, 0), (0, 0), (0, MLA_PAD - MLA_NOPE))).reshape(DEPTH, KV_LORA, W_MLA_P)
    wv = w_ukv[..., MLA_NOPE:].reshape(DEPTH, KV_LORA, MLA_W)
    w_ukv_r = jnp.concatenate([wk, wv], axis=2).astype(BF16)

    def row(v, reps=1):
        v = jnp.tile(v, (1, reps)) if reps > 1 else v
        return [v, jnp.zeros((DEPTH, D_MODEL - v.shape[1]), F32)] if v.shape[1] < D_MODEL else [v]

    pad_head = lambda g: jnp.pad(g, ((0, 0), (0, MLA_PAD - MLA_QK)))
    pieces = (row(p['g_mix']) + row(p['g_ffn']) + row(p['g_qa']) + row(p['g_kva'])
              + row(pad_head(p['g_mla_q']), MLA_HEADS) + row(pad_head(p['g_mla_k']), MLA_HEADS)
              + row(p['g_na_q'], NA_HEADS) + row(p['g_na_k'], NA_HEADS)
              + row(p['g_df_q'], 2 * DF_HEADS) + row(p['g_df_k'], 2 * DF_HEADS) + row(p['g_df_sub'], DF_HEADS)
              + row(p['df_lq1']) + row(p['df_lk1']) + row(p['df_lq2']) + row(p['df_lk2'])
              + [jnp.zeros((DEPTH, D_MODEL), F32)])
    vec = jnp.concatenate(pieces, axis=1).reshape(DEPTH, N_VEC, D_MODEL)
    f = p['na_rpb']
    n_rel = 2 * NA_KW - 1
    zpad = jnp.zeros((DEPTH, NA_HEADS, NA_PAIR_TILES, (LANES - 2 * n_rel) // 2), F32)
    g_rows = jnp.concatenate([f[:, :, :-1, NA_KW - 1:], zpad, f[:, :, 1:, :], zpad, f[:, :, :-1, :NA_KW - 1]], axis=-1)
    g_rows = g_rows.reshape(DEPTH, NA_HEADS * NA_PAIR_TILES, LANES)
    return dict(wa=wa, wb=wb, w_uq=w_uq_p, w_ukv=w_ukv_r, vec=vec, g_rows=g_rows)


def _modulation(c_all, w_mod, b_mod):
    tn = MOD_COLS_PER_STEP
    return pl.pallas_call(
        _mod_kernel,
        grid=(DEPTH, 6 * D_MODEL // tn),
        in_specs=[pl.BlockSpec((N_MOD, D_MODEL), lambda l, j: (0, 0)),
                  pl.BlockSpec((None, D_MODEL, tn), lambda l, j: (l, 0, j)),
                  pl.BlockSpec((DEPTH, tn), lambda l, j: (0, j))],
        out_specs=pl.BlockSpec((None, N_MOD, tn), lambda l, j: (l, 0, j)),
        out_shape=jax.ShapeDtypeStruct((DEPTH, N_MOD, 6 * D_MODEL), F32),
        compiler_params=_params(2),
        name="modulation",
    )(c_all, w_mod, b_mod)


def _mod_spec(l):
    return pl.BlockSpec((None, N_MOD, 6 * D_MODEL), lambda *_: (l, 0, 0))


def _front_weight_specs(l):
    return [_layer_spec(l, (N_VEC, D_MODEL)), _layer_spec(l, (A_COLS, D_MODEL)), _layer_spec(l, (B_SRC + B_COLS, D_MODEL)),
            _layer_spec(l, (W_MLA_P, Q_LORA)), _layer_spec(l, (KV_LORA, W_MLA_P + MLA_W)),
            _const_spec((MXU_DIM, MXU_DIM)), _const_spec((MXU_DIM, MXU_DIM))]


def _front_weights(t, consts):
    return (t['vec'], t['wa'], t['wb'], t['w_uq'], t['w_ukv'], consts['bd64'], consts['bd32'])


def _context_mixer(l, lam_init, x, mods, t, consts, ffn_f32, prev_caches, after=()):
    n_tok = BATCH * SEQ
    n_alias = len(prev_caches)
    tok = lambda w: pl.BlockSpec((CTX_SEQS_PER_STEP * SEQ, w), lambda b: (b, 0))
    if n_alias == 0:
        assert l == 0
        lay = lambda *s: pl.BlockSpec((CTX_SEQS_PER_STEP, DEPTH) + s, lambda b: (b, 0) + (0,) * len(s))
    else:
        lay = lambda *s: pl.BlockSpec((CTX_SEQS_PER_STEP, None) + s, lambda b: (b, l) + (0,) * len(s))
    cache_shapes = [(SEQ, KV_LORA), (MLA_ROPE, SEQ), (NA_HEADS, NA_HD, SEQ), (NA_HEADS, NA_HD, SEQ),
                    (DF_HEADS, DF_HD, SEQ), (DF_HEADS, DF_HD, SEQ)]
    weights = _front_weights(t, consts)
    steps = BATCH // CTX_SEQS_PER_STEP
    ffn_chunks = [(w.shape[1] // steps, w.shape[2]) for w in ffn_f32]
    n_in = 2 + len(weights) + len(ffn_f32)
    mix, *outs = pl.pallas_call(
        functools.partial(_ctx_kernel, lam_init, n_alias, len(after)),
        grid=(steps,),
        in_specs=[tok(D_MODEL), _mod_spec(l)]
        + _front_weight_specs(l) + [pl.BlockSpec((None,) + c, lambda b: (l, b, 0)) for c in ffn_chunks]
        + [pl.BlockSpec(memory_space=pl.ANY)] * (n_alias + len(after)),
        out_specs=[tok(D_MODEL)] + [lay(*s) for s in cache_shapes] + [pl.BlockSpec(c, lambda b: (b, 0)) for c in ffn_chunks],
        out_shape=[jax.ShapeDtypeStruct((n_tok, D_MODEL), BF16)]
        + [jax.ShapeDtypeStruct((BATCH, DEPTH) + s, F32) for s in cache_shapes]
        + [jax.ShapeDtypeStruct(w.shape[1:], BF16) for w in ffn_f32],
        input_output_aliases={n_in + i: 1 + i for i in range(n_alias)},
        compiler_params=_params(1),
        name="context_mixer",
    )(x, mods, *weights, *ffn_f32, *prev_caches, *after)
    return mix, outs[:N_CACHE], outs[N_CACHE:]


def _finish(l, xc, mixc, xl, mixl, mods, t, ffn_bf16):
    nc, nl = xc.shape[0] // TM_FINISH, xl.shape[0] // TM_FINISH
    ctx_tok = pl.BlockSpec((TM_FINISH, D_MODEL), lambda i: (jnp.minimum(i, nc - 1), 0))
    lat_tok = pl.BlockSpec((TM_FINISH, D_MODEL), lambda i: (jnp.maximum(i - nc, 0), 0))
    return pl.pallas_call(
        functools.partial(_finish_kernel, nc),
        grid=(nc + nl,),
        in_specs=[ctx_tok, ctx_tok, lat_tok, lat_tok,
                  _mod_spec(l),
                  _layer_spec(l, (N_VEC, D_MODEL))] + [_const_spec(w.shape) for w in ffn_bf16],
        out_specs=[ctx_tok, lat_tok],
        out_shape=[jax.ShapeDtypeStruct(xc.shape, F32), jax.ShapeDtypeStruct(xl.shape, F32)],
        compiler_params=_params(1),
        name="finish",
    )(xc, mixc, xl, mixl, mods, t['vec'], *ffn_bf16)


def _latent_front(l, x, mods, t, consts):
    n_tok = DEC_BATCH * DEC_SEQ
    tm = TM_LAT_FRONT
    blocks_per_seq = DEC_SEQ // tm
    tok = lambda w: pl.BlockSpec((tm, w), lambda i: (i, 0))
    wspecs = _front_weight_specs(l)
    weights = _front_weights(t, consts)
    return pl.pallas_call(
        _lat_front_kernel,
        grid=(n_tok // tm,),
        in_specs=[tok(D_MODEL),
                  _mod_spec(l),
                  wspecs[0], pl.BlockSpec((6, tm, LANES), lambda i: (0, i % blocks_per_seq, 0))] + wspecs[1:],
        out_specs=[tok(QK_PACK), tok(QK_PACK), tok(V_PACK)],
        out_shape=[jax.ShapeDtypeStruct((n_tok, QK_PACK), BF16), jax.ShapeDtypeStruct((n_tok, QK_PACK), BF16),
                   jax.ShapeDtypeStruct((n_tok, V_PACK), BF16)],
        compiler_params=_params(1),
        name="latent_front",
    )(x, mods, weights[0], consts['rope'], *weights[1:])


def _latent_na(l, q, k, v, kct, vct, g_rows, vec, ckv, krope_t, w_ukv):
    n_tok = DEC_BATCH * DEC_SEQ
    na_blk = P_NA // NA_W
    steps = GRID_ROWS // NA_ROWS_PER_STEP
    tq = NA_ROWS_PER_STEP * GRID_W
    ctx = pl.BlockSpec((None, None, NA_W, PAST_LEN), lambda b, j: (b, l, 0, 0))
    lay = lambda r, w: pl.BlockSpec((None, None, r, w), lambda b, j: (b, l, 0, 0))
    per_batch = lambda w: pl.BlockSpec((None, PAST_LEN, w), lambda b, j: (b, 0, 0))
    return pl.pallas_call(
        _na_lat_kernel,
        grid=(DEC_BATCH, steps),
        in_specs=[pl.BlockSpec((tq, NA_W), lambda b, j: (b * steps + j, na_blk)),
                  pl.BlockSpec((DEC_SEQ, NA_W), lambda b, j: (b, na_blk)),
                  pl.BlockSpec((DEC_SEQ, NA_W), lambda b, j: (b, PV_NA // NA_W)),
                  ctx, ctx, _layer_spec(l, (NA_HEADS * NA_PAIR_TILES, LANES)), _layer_spec(l, (N_VEC, D_MODEL)),
                  lay(PAST_LEN, KV_LORA), lay(MLA_ROPE, PAST_LEN), _layer_spec(l, (KV_LORA, W_MLA_P + MLA_W))],
        out_specs=[pl.BlockSpec((tq, NA_W), lambda b, j: (b * steps + j, 0)), per_batch(W_MLA_P), per_batch(MLA_W)],
        out_shape=[jax.ShapeDtypeStruct((n_tok, NA_W), BF16),
                   jax.ShapeDtypeStruct((DEC_BATCH, PAST_LEN, W_MLA_P), BF16),
                   jax.ShapeDtypeStruct((DEC_BATCH, PAST_LEN, MLA_W), BF16)],
        scratch_shapes=[pltpu.VMEM((NA_HEADS * NA_PAIR_TILES, GRID_W, LANES), F32)],
        compiler_params=_params(2),
        name="latent_neighbourhood",
    )(q, k, v, kct, vct, g_rows, vec, ckv, krope_t, w_ukv)


def _latent_attention(l, lam_init, q, k, v, kc, vc, kct, vct, o_na, t):
    n_tok = DEC_BATCH * DEC_SEQ
    nq = DEC_SEQ // TM
    ctx = pl.BlockSpec((None, None, DF_W, PAST_LEN), lambda b, j: (b, l, 0, 0))
    return pl.pallas_call(
        functools.partial(_lat_attn_kernel, lam_init),
        grid=(DEC_BATCH, nq),
        in_specs=[pl.BlockSpec((TM, QK_PACK), lambda b, j: (b * nq + j, 0)),
                  pl.BlockSpec((DEC_SEQ, QK_PACK), lambda b, j: (b, 0)),
                  pl.BlockSpec((DEC_SEQ, V_PACK), lambda b, j: (b, 0)),
                  pl.BlockSpec((None, PAST_LEN, W_MLA_P), lambda b, j: (b, 0, 0)),
                  pl.BlockSpec((None, PAST_LEN, MLA_W), lambda b, j: (b, 0, 0)),
                  ctx, ctx,
                  pl.BlockSpec((TM, NA_W), lambda b, j: (b * nq + j, 0)),
                  _layer_spec(l, (N_VEC, D_MODEL))],
        out_specs=pl.BlockSpec((TM, D_MODEL), lambda b, j: (b * nq + j, 0)),
        out_shape=jax.ShapeDtypeStruct((n_tok, D_MODEL), BF16),
        compiler_params=_params(2),
        name="latent_attention",
    )(q, k, v, kc, vc, kct, vct, o_na, t['vec'])


def kernel(x_prompt, x_sample, cache_mla_ckv, cache_mla_krope, cache_na_k, cache_na_v, cache_df_k, cache_df_v, c, c_ctx, w_mod, b_mod, g_mix, w_in, g_qa, w_uq, g_kva, w_ukv, g_mla_q, g_mla_k, g_na_q, g_na_k, na_rpb, g_df_q, g_df_k, df_lq1, df_lk1, df_lq2, df_lk2, g_df_sub, w_out, g_ffn, w_gate, w_up, w_down):
    p = dict(g_mix=g_mix, w_in=w_in, g_qa=g_qa, w_uq=w_uq, g_kva=g_kva, w_ukv=w_ukv, g_mla_q=g_mla_q, g_mla_k=g_mla_k,
             g_na_q=g_na_q, g_na_k=g_na_k, na_rpb=na_rpb, g_df_q=g_df_q, g_df_k=g_df_k, df_lq1=df_lq1, df_lk1=df_lk1,
             df_lq2=df_lq2, df_lk2=df_lk2, g_df_sub=g_df_sub, g_ffn=g_ffn)
    consts = dict(bd64=_block_diag(NA_HD), bd32=_block_diag(DF_QK), rope=_rope_tables())
    t = _tables(p)

    c_all = jnp.concatenate([c_ctx[None, :], c, jnp.zeros((N_MOD - 1 - DEC_BATCH, D_MODEL), F32)], axis=0)
    mods = _modulation(c_all, w_mod, b_mod)

    krope_t = jnp.swapaxes(cache_mla_krope, -1, -2)
    na_kt, na_vt, df_kt, df_vt = (_feature_major(a) for a in (cache_na_k, cache_na_v, cache_df_k, cache_df_v))

    xp = x_prompt.reshape(BATCH * SEQ, D_MODEL)
    xs = x_sample.reshape(DEC_BATCH * DEC_SEQ, D_MODEL)
    new_caches = ()
    ffn_f32 = (w_out, w_gate, w_up, w_down)
    for l in range(DEPTH):
        lam_init = 0.8 - 0.6 * math.exp(-0.3 * l)
        q, k, v = _latent_front(l, xs, mods, t, consts)
        o_na, kc, vc = _latent_na(l, q, k, v, na_kt, na_vt, t['g_rows'], t['vec'], cache_mla_ckv, krope_t, t['w_ukv'])
        mix_s = _latent_attention(l, lam_init, q, k, v, kc, vc, df_kt, df_vt, o_na, t)
        mix, new_caches, ffn_bf16 = _context_mixer(l, lam_init, xp, mods, t, consts, ffn_f32, new_caches, after=(mix_s,))
        xp, xs = _finish(l, xp, mix, xs, mix_s, mods, t, ffn_bf16)
    ckv_new, *narrow = new_caches
    return (xp.reshape(BATCH, SEQ, D_MODEL), xs.reshape(DEC_BATCH, DEC_SEQ, D_MODEL), ckv_new,
            *(jnp.swapaxes(a, -1, -2) for a in narrow))
```

```python
import functools
import math

import numpy as np
import jax
import jax.numpy as jnp
from jax import lax
from jax.experimental import pallas as pl
from jax.experimental.pallas import tpu as pltpu

F32 = jnp.float32
BF16 = jnp.bfloat16

D_MODEL = 1024
BATCH = 32
SEQ = 256
DEPTH = 2
DEC_BATCH = 2
DEC_SEQ = 1024
PAST_LEN = 256
GRID_W = 64
GRID_ROWS = DEC_SEQ // GRID_W
MLA_HEADS = 6
MLA_NOPE = 64
MLA_ROPE = 32
MLA_QK = MLA_NOPE + MLA_ROPE
MLA_V = 64
MLA_PAD = 128
Q_LORA = 256
KV_LORA = 128
NA_HEADS = 6
NA_HD = 64
NA_KR = 8
NA_KW = 16
DF_HEADS = 4
DF_HD = 64
DF_QK = 32
MLA_W = MLA_HEADS * MLA_V
NA_W = NA_HEADS * NA_HD
DF_W = DF_HEADS * DF_HD
D_FF = -(-8 * D_MODEL // (3 * 256)) * 256
ROPE_BASE = 10000.0
EPS = 1e-6
NEG = -1e30
LOG2E = math.log2(math.e)
MAX_FREE_SOFTMAX_BOUND = 60.0

LANES = 128
MXU_DIM = 256

A_CQ = 0
A_CKV = A_CQ + Q_LORA
A_KR = A_CKV + KV_LORA
A_COLS = A_KR + LANES
B_SRC = Q_LORA + KV_LORA + MLA_ROPE
B_NAQ = 0
B_NAK = B_NAQ + NA_W
B_NAV = B_NAK + NA_W
B_DFQ = B_NAV + NA_W
B_DFK = B_DFQ + DF_W
B_DFV = B_DFK + DF_W
B_COLS = B_DFV + DF_W

W_MLA_P = MLA_HEADS * MLA_PAD
QK_PACK = W_MLA_P + NA_W + DF_W
V_PACK = MLA_W + NA_W + DF_W
P_MLA = 0
P_NA = W_MLA_P
P_DF = P_NA + NA_W
PV_MLA = 0
PV_NA = MLA_W
PV_DF = MLA_W + NA_W

(V_GMIX, V_GFFN, V_GQA, V_GKVA, V_GMQ, V_GMK, V_GNQ, V_GNK, V_GDQ, V_GDK, V_GDS,
 V_LQ1, V_LK1, V_LQ2, V_LK2) = range(15)
N_VEC = 16

N_MOD = 8
MOD_COLS_PER_STEP = 3072
TM = 512
EXACT_ROWS = 128
TM_LAT_FRONT = 1024
LAT_FRONT_CHUNK = 256
TM_FINISH = 512
CTX_SEQS_PER_STEP = 2
NA_ROWS_PER_STEP = 4
VMEM_LIMIT = 56 * 1024 * 1024

NA_PAIR_TILES = 2 * NA_KR - 2
N_CACHE = 6
N_FFN_W = 4


def _dot(a, b):
    return jnp.dot(a, b, preferred_element_type=F32)


def _dot_nt(a, b):
    return lax.dot_general(a, b, (((1,), (1,)), ((), ())), preferred_element_type=F32)


def _lane_iota(shape):
    return lax.broadcasted_iota(jnp.int32, shape, len(shape) - 1)


def _rms_rows(x, g):
    ms = jnp.mean(x * x, axis=-1, keepdims=True)
    return x * lax.rsqrt(ms + EPS) * g


def _tile_rms(x, g, n_real):
    outs = []
    for c0 in range(0, x.shape[1], LANES):
        xt = x[:, c0:c0 + LANES]
        ms = jnp.sum(xt * xt, axis=-1, keepdims=True) * (1.0 / n_real)
        outs.append(xt * lax.rsqrt(ms + EPS) * g[:, c0:c0 + LANES])
    return jnp.concatenate(outs, axis=1)


def _seg_rms(x, bd_ref, g, group):
    width = x.shape[1]
    sq = (x * x).astype(BF16)
    parts = []
    for c0 in range(0, width, MXU_DIM):
        w = min(MXU_DIM, width - c0)
        parts.append(_dot(sq[:, c0:c0 + w], bd_ref[0:w, 0:w]))
    ss = parts[0] if len(parts) == 1 else jnp.concatenate(parts, axis=1)
    return x * lax.rsqrt(ss * (1.0 / group) + EPS) * g


def _rope_tiles(x, cos, sa, sb):
    outs = []
    for t in range(x.shape[1] // LANES):
        xt = x[:, t * LANES:(t + 1) * LANES]
        up = pltpu.roll(xt, LANES - MLA_ROPE // 4, 1)
        dn = pltpu.roll(xt, MLA_ROPE // 4, 1)
        outs.append(xt * cos + up * sa + dn * sb)
    return outs[0] if len(outs) == 1 else jnp.concatenate(outs, axis=1)


def _diff_lambda(vec_ref, lam_init):
    a = jnp.sum(vec_ref[V_LQ1:V_LQ1 + 1, 0:DF_QK] * vec_ref[V_LK1:V_LK1 + 1, 0:DF_QK], axis=-1, keepdims=True)
    b = jnp.sum(vec_ref[V_LQ2:V_LQ2 + 1, 0:DF_QK] * vec_ref[V_LK2:V_LK2 + 1, 0:DF_QK], axis=-1, keepdims=True)
    return jnp.exp(a) - jnp.exp(b) + lam_init


def _mixer_front(x, mod, vec_ref, wa_ref, wb_ref, w_uq_ref, bd64_ref, bd32_ref):
    sh = mod[:, 0:D_MODEL]
    sc = mod[:, D_MODEL:2 * D_MODEL]
    h = (_rms_rows(x, vec_ref[V_GMIX:V_GMIX + 1, :]) * (1.0 + sc) + sh).astype(BF16)
    za = _dot_nt(h, wa_ref[...])
    zb = _dot_nt(h, wb_ref[B_SRC:B_SRC + B_COLS, :])
    cqn = _rms_rows(za[:, A_CQ:A_CQ + Q_LORA], vec_ref[V_GQA:V_GQA + 1, 0:Q_LORA])
    q_raw = _dot_nt(cqn.astype(BF16), w_uq_ref[...])
    q_mla = _tile_rms(q_raw, vec_ref[V_GMQ:V_GMQ + 1, 0:W_MLA_P] * (MLA_QK ** -0.5 * LOG2E), MLA_QK)
    ckv_n = _rms_rows(za[:, A_CKV:A_CKV + KV_LORA], vec_ref[V_GKVA:V_GKVA + 1, 0:KV_LORA])
    kr_tile = za[:, A_KR:A_KR + LANES]
    q_na = _seg_rms(zb[:, B_NAQ:B_NAQ + NA_W], bd64_ref, vec_ref[V_GNQ:V_GNQ + 1, 0:NA_W] * (NA_HD ** -0.5 * LOG2E), NA_HD)
    k_na = _seg_rms(zb[:, B_NAK:B_NAK + NA_W], bd64_ref, vec_ref[V_GNK:V_GNK + 1, 0:NA_W], NA_HD)
    v_na = zb[:, B_NAV:B_NAV + NA_W]
    q_df = _seg_rms(zb[:, B_DFQ:B_DFQ + DF_W], bd32_ref, vec_ref[V_GDQ:V_GDQ + 1, 0:DF_W] * (DF_QK ** -0.5 * LOG2E), DF_QK)
    k_df = _seg_rms(zb[:, B_DFK:B_DFK + DF_W], bd32_ref, vec_ref[V_GDK:V_GDK + 1, 0:DF_W], DF_QK)
    v_df = zb[:, B_DFV:B_DFV + DF_W]
    return q_mla, ckv_n, kr_tile, q_na, k_na, v_na, q_df, k_df, v_df


def _mla_kv(ckv_n, kr_tile, vec_ref, w_ukv_ref):
    kv = _dot(ckv_n.astype(BF16), w_ukv_ref[...])
    lane = _lane_iota((1, LANES))
    kr = jnp.where((lane >= MLA_NOPE) & (lane < MLA_QK), kr_tile, 0.0)
    k_pre = kv[:, 0:W_MLA_P] + jnp.concatenate([kr] * MLA_HEADS, axis=1)
    k = _tile_rms(k_pre, vec_ref[V_GMK:V_GMK + 1, 0:W_MLA_P], MLA_QK)
    return k, kv[:, W_MLA_P:W_MLA_P + MLA_W]


def _softmax_parts(s, bound=None):
    m = jnp.max(s, axis=-1, keepdims=True) if bound is None else bound
    p = jnp.exp2(s - m)
    return p, 1.0 / jnp.sum(p, axis=-1, keepdims=True)


def _max_sq_norm(kt, group):
    sq = kt * kt
    best = None
    for r0 in range(0, kt.shape[0], group):
        n2 = jnp.sum(sq[r0:r0 + group], axis=0, keepdims=True)
        best = n2 if best is None else jnp.maximum(best, n2)
    return jnp.max(best, axis=-1, keepdims=True)


def _score_bounds(vec_ref, bias_max=None, na_key_sq=None, df_key_sq=None):
    gmax = lambda row, w: jnp.max(jnp.abs(vec_ref[row:row + 1, 0:w]), axis=-1, keepdims=True)
    slack = LOG2E * (1.0 + 2.0 ** -6)

    def key_norm(row, w, d, measured_sq):
        k = gmax(row, w) * d ** 0.5
        return k if measured_sq is None else jnp.maximum(k, jnp.sqrt(measured_sq))

    b_mla = gmax(V_GMQ, W_MLA_P) * key_norm(V_GMK, W_MLA_P, MLA_QK, None) * slack
    b_na = gmax(V_GNQ, NA_W) * key_norm(V_GNK, NA_W, NA_HD, na_key_sq) * slack
    if bias_max is not None:
        b_na = b_na + bias_max * LOG2E
    b_df = gmax(V_GDQ, DF_W) * key_norm(V_GDK, DF_W, DF_QK, df_key_sq) * slack
    worst = jnp.maximum(b_mla, jnp.maximum(b_na, b_df))
    return (b_mla, b_na, b_df), worst[0, 0] <= MAX_FREE_SOFTMAX_BOUND


EXACT_MAX = (None, None, None)


def _with_score_bounds(vec_ref, bounded, exact, **measured):
    bounds, ok = _score_bounds(vec_ref, **measured)
    pl.when(ok)(lambda: bounded(bounds))
    pl.when(jnp.logical_not(ok))(exact)


def _row_block(i, n):
    return slice(i * n, (i + 1) * n) if isinstance(i, int) else pl.ds(pl.multiple_of(i * n, n), n)


def _rolled(n, body):
    def step(i, carry):
        body(i)
        return carry

    lax.fori_loop(0, n, step, 0)


def _scores(q, k_segs):
    parts = [_dot(q, k) if feature_major else _dot_nt(q, k) for k, feature_major in k_segs]
    return parts[0] if len(parts) == 1 else jnp.concatenate(parts, axis=1)


def _pv(p, v_segs):
    out = None
    c0 = 0
    for v, feature_major in v_segs:
        n = v.shape[1] if feature_major else v.shape[0]
        o = _dot_nt(p[:, c0:c0 + n], v) if feature_major else _dot(p[:, c0:c0 + n], v)
        out = o if out is None else out + o
        c0 += n
    return out


def _seg_tile(seg, t):
    a, feature_major = seg
    return (a[t * LANES:(t + 1) * LANES, :] if feature_major else a[:, t * LANES:(t + 1) * LANES]), feature_major


def _lane_groups(qt, width):
    lane = _lane_iota((1, LANES))
    zero = jnp.zeros_like(qt)
    return jnp.concatenate(
        [jnp.where((lane >= g * width) & (lane < (g + 1) * width), qt, zero) for g in range(LANES // width)], axis=0)


def _pair_select(o2):
    tq = o2.shape[0] // 2
    return jnp.where(_lane_iota((1, LANES)) < NA_HD, o2[0:tq], o2[tq:2 * tq])


def _mla_attend(q, k_segs, v_segs, bound=None):
    outs = []
    for t in range(MLA_HEADS // 2):
        vt = [_seg_tile(v, t) for v in v_segs]
        halves = []
        for h in (2 * t, 2 * t + 1):
            p, il = _softmax_parts(_scores(q[:, h * MLA_PAD:(h + 1) * MLA_PAD], [_seg_tile(k, h) for k in k_segs]),
                                   bound)
            halves.append(_pv(p.astype(BF16), vt) * il)
        outs.append(jnp.where(_lane_iota((1, LANES)) < MLA_V, halves[0], halves[1]))
    return jnp.concatenate(outs, axis=1)


def _na_tile_attend(qt, k_segs, v_segs, bias=None, bound=None):
    s = _scores(_lane_groups(qt, NA_HD), k_segs)
    if bias is not None:
        nb = bias.shape[1]
        s = jnp.concatenate([s[:, 0:nb] + bias, s[:, nb:]], axis=1)
    p, il = _softmax_parts(s, bound)
    return _pair_select(_pv(p.astype(BF16), v_segs) * il)


def _na_attend_full(q, k_segs, v_segs, bound=None):
    return jnp.concatenate(
        [_na_tile_attend(q[:, t * LANES:(t + 1) * LANES], [_seg_tile(k, t) for k in k_segs],
                         [_seg_tile(v, t) for v in v_segs], bound=bound) for t in range(NA_HEADS // 2)], axis=1)


def _df_attend(q, k_segs, v_segs, lam, g_sub, out_scale, bound=None):
    outs = []
    lane = _lane_iota((1, LANES))
    tq = q.shape[0]
    for t in range(DF_HEADS // 2):
        kt = [_seg_tile(k, t) for k in k_segs]
        vt = [_seg_tile(v, t) for v in v_segs]
        p, il = _softmax_parts(_scores(_lane_groups(q[:, t * LANES:(t + 1) * LANES], DF_QK), kt), bound)
        pn = []
        for hh in range(2):
            r1, r2 = 2 * hh * tq, (2 * hh + 1) * tq
            pn.append((p[r1:r1 + tq] * il[r1:r1 + tq] - p[r2:r2 + tq] * (lam * il[r2:r2 + tq])).astype(BF16))
        o = _pair_select(_pv(jnp.concatenate(pn, axis=0), vt))
        o2 = o * o
        ms_e = jnp.sum(jnp.where(lane < DF_HD, o2, 0.0), axis=-1, keepdims=True)
        ms_o = jnp.sum(jnp.where(lane >= DF_HD, o2, 0.0), axis=-1, keepdims=True)
        r = lax.rsqrt(jnp.where(lane < DF_HD, ms_e, ms_o) * (1.0 / DF_HD) + EPS)
        outs.append(o * r * (g_sub[:, t * LANES:(t + 1) * LANES] * out_scale))
    return jnp.concatenate(outs, axis=1)


def _mod_kernel(c_ref, w_ref, b_ref, o_ref):
    c = c_ref[...]
    s = c * jax.nn.sigmoid(c)
    o_ref[...] = _dot(s.astype(BF16), w_ref[...].astype(BF16)) + b_ref[pl.ds(pl.program_id(0), 1), :]


def _ctx_kernel(lam_init, n_alias, n_after, x_ref, mod_ref, vec_ref, wa_ref, wb_ref, w_uq_ref, w_ukv_ref, bd64_ref,
                bd32_ref, *rest):
    ffn_f32, rest = rest[:N_FFN_W], rest[N_FFN_W + n_alias + n_after:]
    mix_ref, *cache_refs = rest[:1 + N_CACHE]
    for src, dst in zip(ffn_f32, rest[1 + N_CACHE:]):
        dst[...] = src[...].astype(BF16)
    if n_alias == 0:
        for ref in cache_refs:
            ref[:, 1:] = jnp.zeros((ref.shape[0], ref.shape[1] - 1) + ref.shape[2:], F32)
        cache_refs = [ref.at[:, 0] for ref in cache_refs]
    ckv_ref, kr_ref, nak_ref, nav_ref, dfk_ref, dfv_ref = cache_refs
    lam = _diff_lambda(vec_ref, lam_init)
    mod = mod_ref[0:1, :]
    bf = lambda a: a.astype(BF16)
    seg = lambda a: [(a.astype(BF16), False)]
    def sequence(s, bounds):
        b_mla, b_na, b_df = bounds
        rows = _row_block(s, SEQ)
        q_mla, ckv_n, kr_tile, q_na, k_na, v_na, q_df, k_df, v_df = _mixer_front(
            x_ref[rows, :], mod, vec_ref, wa_ref, wb_ref, w_uq_ref, bd64_ref, bd32_ref)
        k_mla, v_mla = _mla_kv(ckv_n, kr_tile, vec_ref, w_ukv_ref)
        ckv_ref[s] = ckv_n
        kr_ref[s] = kr_tile.T[0:MLA_ROPE]
        for ref, a in ((nak_ref, k_na), (nav_ref, v_na), (dfk_ref, k_df), (dfv_ref, v_df)):
            at = a.T
            for h in range(ref.shape[1]):
                ref[s, h] = at[h * NA_HD:(h + 1) * NA_HD]
        o_mla = _mla_attend(bf(q_mla), seg(k_mla), seg(v_mla), b_mla)
        o_na = _na_attend_full(bf(q_na), seg(k_na), seg(v_na), b_na)
        o_df = _df_attend(bf(q_df), seg(k_df), seg(v_df), lam, vec_ref[V_GDS:V_GDS + 1, 0:DF_W], 1.0 - lam_init, b_df)
        mix_ref[rows, :] = jnp.concatenate([o_mla, o_na, o_df], axis=1).astype(BF16)

    def bounded(bounds):
        for s in range(CTX_SEQS_PER_STEP):
            sequence(s, bounds)

    _with_score_bounds(vec_ref, bounded, lambda: _rolled(CTX_SEQS_PER_STEP, lambda s: sequence(s, EXACT_MAX)))


def _finish_kernel(n_ctx_blocks, xc_ref, mixc_ref, xl_ref, mixl_ref, mod_ref, vec_ref, w_out_ref, w_gate_ref, w_up_ref,
                   w_down_ref, yc_ref, yl_ref):
    def block(x_ref, mix_ref, y_ref, mod_row):
        mod = mod_ref[pl.ds(mod_row, 1), :]
        gate_m = mod[:, 2 * D_MODEL:3 * D_MODEL]
        sh = mod[:, 3 * D_MODEL:4 * D_MODEL]
        sc = mod[:, 4 * D_MODEL:5 * D_MODEL]
        gate_f = mod[:, 5 * D_MODEL:6 * D_MODEL]
        x1 = x_ref[...] + gate_m * _dot(mix_ref[...], w_out_ref[...])
        h = (_rms_rows(x1, vec_ref[V_GFFN:V_GFFN + 1, :]) * (1.0 + sc) + sh).astype(BF16)
        g = _dot(h, w_gate_ref[...])
        u = _dot(h, w_up_ref[...])
        a = (g * jax.nn.sigmoid(g) * u).astype(BF16)
        y_ref[...] = x1 + gate_f * _dot(a, w_down_ref[...])

    is_ctx = pl.program_id(0) < n_ctx_blocks
    lat_row = 1 + jnp.maximum(pl.program_id(0) - n_ctx_blocks, 0) // (DEC_SEQ // TM_FINISH)
    pl.when(is_ctx)(lambda: block(xc_ref, mixc_ref, yc_ref, 0))
    pl.when(jnp.logical_not(is_ctx))(lambda: block(xl_ref, mixl_ref, yl_ref, lat_row))


def _lat_front_kernel(x_ref, mod_ref, vec_ref, rope_ref, wa_ref, wb_ref, w_uq_ref, w_ukv_ref, bd64_ref, bd32_ref,
                      q_ref, k_ref, v_ref):
    mod = mod_ref[pl.ds(1 + pl.program_id(0) // (DEC_SEQ // TM_LAT_FRONT), 1), :]
    for c in range(TM_LAT_FRONT // LAT_FRONT_CHUNK):
        rows = slice(c * LAT_FRONT_CHUNK, (c + 1) * LAT_FRONT_CHUNK)
        q_mla, ckv_n, kr_tile, q_na, k_na, v_na, q_df, k_df, v_df = _mixer_front(
            x_ref[rows, :], mod, vec_ref, wa_ref, wb_ref, w_uq_ref, bd64_ref, bd32_ref)
        k_mla, v_mla = _mla_kv(ckv_n, kr_tile, vec_ref, w_ukv_ref)
        cm, sam, sbm = rope_ref[0, rows], rope_ref[1, rows], rope_ref[2, rows]
        cd, sad, sbd = rope_ref[3, rows], rope_ref[4, rows], rope_ref[5, rows]
        q_ref[rows, :] = jnp.concatenate(
            [_rope_tiles(q_mla, cm, sam, sbm), q_na, _rope_tiles(q_df, cd, sad, sbd)], axis=1).astype(BF16)
        k_ref[rows, :] = jnp.concatenate(
            [_rope_tiles(k_mla, cm, sam, sbm), k_na, _rope_tiles(k_df, cd, sad, sbd)], axis=1).astype(BF16)
        v_ref[rows, :] = jnp.concatenate([v_mla, v_na, v_df], axis=1).astype(BF16)


def _na_lat_kernel(q_ref, kl_ref, vl_ref, kct_ref, vct_ref, g_ref, vec_ref, ckv_ref, krt_ref, w_ukv_ref,
                   o_ref, kc_mla_ref, vc_mla_ref, bias_ref):
    b = pl.program_id(0)
    j = pl.program_id(1)

    @pl.when(j == 0)
    def _context_mla_kv():
        krt = jnp.concatenate([jnp.zeros((MLA_NOPE, PAST_LEN), F32), krt_ref[...],
                               jnp.zeros((LANES - MLA_QK, PAST_LEN), F32)], axis=0)
        k_mla, v_mla = _mla_kv(ckv_ref[...], krt.T, vec_ref, w_ukv_ref)
        kc_mla_ref[...] = k_mla.astype(BF16)
        vc_mla_ref[...] = v_mla.astype(BF16)

    @pl.when((b == 0) & (j == 0))
    def _build_bias():
        c = lax.broadcasted_iota(jnp.int32, (GRID_W, LANES), 0)
        kc = _lane_iota((GRID_W, LANES)) % GRID_W
        start = jnp.clip(c - NA_KW // 2, 0, GRID_W - NA_KW)
        in_win = (kc >= start) & (kc < start + NA_KW)

        def body(i, carry):
            row = jnp.broadcast_to(g_ref[pl.ds(i, 1), :], (GRID_W, LANES))
            toep = pltpu.roll(row, 0, 1, stride=1, stride_axis=0)
            bias_ref[i] = jnp.where(in_win, toep * LOG2E, NEG)
            return carry

        lax.fori_loop(0, NA_HEADS * NA_PAIR_TILES, body, 0, unroll=NA_HEADS)

    n_win = NA_KR * GRID_W
    kct_f32 = kct_ref[...]
    kct = kct_f32.astype(BF16)
    vct = vct_ref[...].astype(BF16)

    def win_start(r):
        return jnp.clip(r - NA_KR // 2, 0, GRID_ROWS - NA_KR)

    def attend(grid_rows, rs, bound):
        a0, n = grid_rows[0], len(grid_rows)
        row0 = pl.multiple_of(rs * GRID_W, GRID_W)
        outs = []
        for t in range(NA_HEADS // 2):
            lanes = slice(t * LANES, (t + 1) * LANES)
            bias = jnp.concatenate(
                [jnp.concatenate([bias_ref[h * NA_PAIR_TILES + (rs - (j * NA_ROWS_PER_STEP + a) + NA_KR - 1) + 2 * m]
                                  for m in range(NA_KR // 2)], axis=1)
                 for h in (2 * t, 2 * t + 1) for a in grid_rows], axis=0)
            outs.append(_na_tile_attend(
                q_ref[_row_block(a0, GRID_W) if n == 1 else slice(a0 * GRID_W, (a0 + n) * GRID_W), lanes],
                [(kl_ref[pl.ds(row0, n_win), lanes], False), (kct[lanes, :], True)],
                [(vl_ref[pl.ds(row0, n_win), lanes], False), (vct[lanes, :], True)], bias, bound))
        return jnp.concatenate(outs, axis=1).astype(BF16)

    first_r, last_r = j * NA_ROWS_PER_STEP, (j + 1) * NA_ROWS_PER_STEP - 1
    shared = win_start(first_r) == win_start(last_r)

    def attend_step(bounds):
        bound = bounds[1]

        @pl.when(shared)
        def _shared_window():
            o_ref[...] = attend(tuple(range(NA_ROWS_PER_STEP)), win_start(first_r), bound)

        @pl.when(jnp.logical_not(shared))
        def _per_row_windows():
            for a in range(NA_ROWS_PER_STEP):
                o_ref[a * GRID_W:(a + 1) * GRID_W, :] = attend((a,), win_start(first_r + a), bound)

    def exact_row(a):
        o_ref[_row_block(a, GRID_W), :] = attend((a,), win_start(first_r + a), None)

    bias_max = jnp.max(jnp.max(jnp.abs(g_ref[...]), axis=-1, keepdims=True), axis=0, keepdims=True)
    _with_score_bounds(vec_ref, attend_step, lambda: _rolled(NA_ROWS_PER_STEP, exact_row),
                       bias_max=bias_max, na_key_sq=_max_sq_norm(kct_f32, NA_HD))


def _lat_attn_kernel(lam_init, q_ref, kl_ref, vl_ref, kc_ref, vc_ref, kct_ref, vct_ref, ona_ref, vec_ref, mix_ref):
    lam = _diff_lambda(vec_ref, lam_init)
    kct = kct_ref[...]

    def attend(bounds, rows):
        b_mla, _, b_df = bounds
        o_mla = _mla_attend(q_ref[rows, P_MLA:P_MLA + W_MLA_P],
                            [(kc_ref[...], False), (kl_ref[:, P_MLA:P_MLA + W_MLA_P], False)],
                            [(vc_ref[...], False), (vl_ref[:, PV_MLA:PV_MLA + MLA_W], False)], b_mla)
        o_df = _df_attend(q_ref[rows, P_DF:P_DF + DF_W],
                          [(kct.astype(BF16), True), (kl_ref[:, P_DF:P_DF + DF_W], False)],
                          [(vct_ref[...].astype(BF16), True), (vl_ref[:, PV_DF:PV_DF + DF_W], False)],
                          lam, vec_ref[V_GDS:V_GDS + 1, 0:DF_W], 1.0 - lam_init, b_df)
        mix_ref[rows, :] = jnp.concatenate([o_mla.astype(BF16), ona_ref[rows, :], o_df.astype(BF16)], axis=1)

    def exact():
        _rolled(TM // EXACT_ROWS, lambda i: attend(EXACT_MAX, _row_block(i, EXACT_ROWS)))

    _with_score_bounds(vec_ref, lambda bounds: attend(bounds, slice(None)), exact,
                       df_key_sq=_max_sq_norm(kct, DF_QK))


def _const_spec(shape):
    nd = len(shape)
    return pl.BlockSpec(shape, lambda *_: (0,) * nd, pipeline_mode=pl.Buffered(1))


def _layer_spec(l, shape):
    nd = len(shape)
    return pl.BlockSpec((None,) + tuple(shape), lambda *_: (l,) + (0,) * nd, pipeline_mode=pl.Buffered(1))


def _params(n_axes):
    return pltpu.CompilerParams(dimension_semantics=("arbitrary",) * n_axes, vmem_limit_bytes=VMEM_LIMIT)


def _block_diag(group):
    i = np.arange(MXU_DIM) // group
    return jnp.asarray((i[:, None] == i[None, :]).astype(np.float32), dtype=BF16)


def _rope_tables():
    t = np.arange(DEC_SEQ)
    row = (t // GRID_W).astype(np.float64)
    col = (t % GRID_W).astype(np.float64)
    n = MLA_ROPE // 4
    inv = 1.0 / (ROPE_BASE ** (np.arange(n, dtype=np.float64) * 2.0 / (MLA_ROPE // 2)))
    ar = row[:, None] * inv
    ac = col[:, None] * inv
    ang = np.concatenate([ar, ar, ac, ac], axis=-1)
    cos32, sin32 = np.cos(ang), np.sin(ang)
    first = (np.arange(MLA_ROPE) % (2 * n)) < n
    sa32 = np.where(first, -sin32, 0.0)
    sb32 = np.where(first, 0.0, sin32)

    def mla_tile(v32, fill):
        out = np.full((DEC_SEQ, LANES), fill)
        out[:, MLA_NOPE:MLA_QK] = v32
        return out

    tabs = [mla_tile(cos32, 1.0), mla_tile(sa32, 0.0), mla_tile(sb32, 0.0),
            np.tile(cos32, (1, LANES // DF_QK)), np.tile(sa32, (1, LANES // DF_QK)), np.tile(sb32, (1, LANES // DF_QK))]
    return jnp.asarray(np.stack(tabs).astype(np.float32))


def _feature_major(a):
    a = jnp.swapaxes(a, -1, -2)
    return a.reshape(a.shape[:-3] + (a.shape[-3] * a.shape[-2], a.shape[-1]))


def _tables(p):
    wb = jnp.swapaxes(p['w_in'], 1, 2).astype(BF16)
    kr = wb[:, Q_LORA + KV_LORA:B_SRC]
    z32 = jnp.zeros_like(kr)
    wa = jnp.concatenate([wb[:, :Q_LORA + KV_LORA], kr, z32, kr, z32], axis=1)
    w_uq_t = jnp.swapaxes(p['w_uq'], 1, 2).reshape(DEPTH, MLA_HEADS, MLA_QK, Q_LORA)
    w_uq_p = jnp.pad(w_uq_t, ((0, 0), (0, 0), (0, MLA_PAD - MLA_QK), (0, 0))).reshape(DEPTH, W_MLA_P, Q_LORA).astype(BF16)
    w_ukv = p['w_ukv'].reshape(DEPTH, KV_LORA, MLA_HEADS, MLA_NOPE + MLA_V)
    wk = jnp.pad(w_ukv[..., :MLA_NOPE], ((0, 0), (0, 0), (0, 0), (0, MLA_PAD - MLA_NOPE))).reshape(DEPTH, KV_LORA, W_MLA_P)
    wv = w_ukv[..., MLA_NOPE:].reshape(DEPTH, KV_LORA, MLA_W)
    w_ukv_r = jnp.concatenate([wk, wv], axis=2).astype(BF16)

    def row(v, reps=1):
        v = jnp.tile(v, (1, reps)) if reps > 1 else v
        return [v, jnp.zeros((DEPTH, D_MODEL - v.shape[1]), F32)] if v.shape[1] < D_MODEL else [v]

    pad_head = lambda g: jnp.pad(g, ((0, 0), (0, MLA_PAD - MLA_QK)))
    pieces = (row(p['g_mix']) + row(p['g_ffn']) + row(p['g_qa']) + row(p['g_kva'])
              + row(pad_head(p['g_mla_q']), MLA_HEADS) + row(pad_head(p['g_mla_k']), MLA_HEADS)
              + row(p['g_na_q'], NA_HEADS) + row(p['g_na_k'], NA_HEADS)
              + row(p['g_df_q'], 2 * DF_HEADS) + row(p['g_df_k'], 2 * DF_HEADS) + row(p['g_df_sub'], DF_HEADS)
              + row(p['df_lq1']) + row(p['df_lk1']) + row(p['df_lq2']) + row(p['df_lk2'])
              + [jnp.zeros((DEPTH, D_MODEL), F32)])
    vec = jnp.concatenate(pieces, axis=1).reshape(DEPTH, N_VEC, D_MODEL)
    f = p['na_rpb']
    n_rel = 2 * NA_KW - 1
    zpad = jnp.zeros((DEPTH, NA_HEADS, NA_PAIR_TILES, (LANES - 2 * n_rel) // 2), F32)
    g_rows = jnp.concatenate([f[:, :, :-1, NA_KW - 1:], zpad, f[:, :, 1:, :], zpad, f[:, :, :-1, :NA_KW - 1]], axis=-1)
    g_rows = g_rows.reshape(DEPTH, NA_HEADS * NA_PAIR_TILES, LANES)
    return dict(wa=wa, wb=wb, w_uq=w_uq_p, w_ukv=w_ukv_r, vec=vec, g_rows=g_rows)


def _modulation(c_all, w_mod, b_mod):
    tn = MOD_COLS_PER_STEP
    return pl.pallas_call(
        _mod_kernel,
        grid=(DEPTH, 6 * D_MODEL // tn),
        in_specs=[pl.BlockSpec((N_MOD, D_MODEL), lambda l, j: (0, 0)),
                  pl.BlockSpec((None, D_MODEL, tn), lambda l, j: (l, 0, j)),
                  pl.BlockSpec((DEPTH, tn), lambda l, j: (0, j))],
        out_specs=pl.BlockSpec((None, N_MOD, tn), lambda l, j: (l, 0, j)),
        out_shape=jax.ShapeDtypeStruct((DEPTH, N_MOD, 6 * D_MODEL), F32),
        compiler_params=_params(2),
        name="modulation",
    )(c_all, w_mod, b_mod)


def _mod_spec(l):
    return pl.BlockSpec((None, N_MOD, 6 * D_MODEL), lambda *_: (l, 0, 0))


def _front_weight_specs(l):
    return [_layer_spec(l, (N_VEC, D_MODEL)), _layer_spec(l, (A_COLS, D_MODEL)), _layer_spec(l, (B_SRC + B_COLS, D_MODEL)),
            _layer_spec(l, (W_MLA_P, Q_LORA)), _layer_spec(l, (KV_LORA, W_MLA_P + MLA_W)),
            _const_spec((MXU_DIM, MXU_DIM)), _const_spec((MXU_DIM, MXU_DIM))]


def _front_weights(t, consts):
    return (t['vec'], t['wa'], t['wb'], t['w_uq'], t['w_ukv'], consts['bd64'], consts['bd32'])


def _context_mixer(l, lam_init, x, mods, t, consts, ffn_f32, prev_caches, after=()):
    n_tok = BATCH * SEQ
    n_alias = len(prev_caches)
    tok = lambda w: pl.BlockSpec((CTX_SEQS_PER_STEP * SEQ, w), lambda b: (b, 0))
    if n_alias == 0:
        assert l == 0
        lay = lambda *s: pl.BlockSpec((CTX_SEQS_PER_STEP, DEPTH) + s, lambda b: (b, 0) + (0,) * len(s))
    else:
        lay = lambda *s: pl.BlockSpec((CTX_SEQS_PER_STEP, None) + s, lambda b: (b, l) + (0,) * len(s))
    cache_shapes = [(SEQ, KV_LORA), (MLA_ROPE, SEQ), (NA_HEADS, NA_HD, SEQ), (NA_HEADS, NA_HD, SEQ),
                    (DF_HEADS, DF_HD, SEQ), (DF_HEADS, DF_HD, SEQ)]
    weights = _front_weights(t, consts)
    steps = BATCH // CTX_SEQS_PER_STEP
    ffn_chunks = [(w.shape[1] // steps, w.shape[2]) for w in ffn_f32]
    n_in = 2 + len(weights) + len(ffn_f32)
    mix, *outs = pl.pallas_call(
        functools.partial(_ctx_kernel, lam_init, n_alias, len(after)),
        grid=(steps,),
        in_specs=[tok(D_MODEL), _mod_spec(l)]
        + _front_weight_specs(l) + [pl.BlockSpec((None,) + c, lambda b: (l, b, 0)) for c in ffn_chunks]
        + [pl.BlockSpec(memory_space=pl.ANY)] * (n_alias + len(after)),
        out_specs=[tok(D_MODEL)] + [lay(*s) for s in cache_shapes] + [pl.BlockSpec(c, lambda b: (b, 0)) for c in ffn_chunks],
        out_shape=[jax.ShapeDtypeStruct((n_tok, D_MODEL), BF16)]
        + [jax.ShapeDtypeStruct((BATCH, DEPTH) + s, F32) for s in cache_shapes]
        + [jax.ShapeDtypeStruct(w.shape[1:], BF16) for w in ffn_f32],
        input_output_aliases={n_in + i: 1 + i for i in range(n_alias)},
        compiler_params=_params(1),
        name="context_mixer",
    )(x, mods, *weights, *ffn_f32, *prev_caches, *after)
    return mix, outs[:N_CACHE], outs[N_CACHE:]


def _finish(l, xc, mixc, xl, mixl, mods, t, ffn_bf16):
    nc, nl = xc.shape[0] // TM_FINISH, xl.shape[0] // TM_FINISH
    ctx_tok = pl.BlockSpec((TM_FINISH, D_MODEL), lambda i: (jnp.minimum(i, nc - 1), 0))
    lat_tok = pl.BlockSpec((TM_FINISH, D_MODEL), lambda i: (jnp.maximum(i - nc, 0), 0))
    return pl.pallas_call(
        functools.partial(_finish_kernel, nc),
        grid=(nc + nl,),
        in_specs=[ctx_tok, ctx_tok, lat_tok, lat_tok,
                  _mod_spec(l),
                  _layer_spec(l, (N_VEC, D_MODEL))] + [_const_spec(w.shape) for w in ffn_bf16],
        out_specs=[ctx_tok, lat_tok],
        out_shape=[jax.ShapeDtypeStruct(xc.shape, F32), jax.ShapeDtypeStruct(xl.shape, F32)],
        compiler_params=_params(1),
        name="finish",
    )(xc, mixc, xl, mixl, mods, t['vec'], *ffn_bf16)


def _latent_front(l, x, mods, t, consts):
    n_tok = DEC_BATCH * DEC_SEQ
    tm = TM_LAT_FRONT
    blocks_per_seq = DEC_SEQ // tm
    tok = lambda w: pl.BlockSpec((tm, w), lambda i: (i, 0))
    wspecs = _front_weight_specs(l)
    weights = _front_weights(t, consts)
    return pl.pallas_call(
        _lat_front_kernel,
        grid=(n_tok // tm,),
        in_specs=[tok(D_MODEL),
                  _mod_spec(l),
                  wspecs[0], pl.BlockSpec((6, tm, LANES), lambda i: (0, i % blocks_per_seq, 0))] + wspecs[1:],
        out_specs=[tok(QK_PACK), tok(QK_PACK), tok(V_PACK)],
        out_shape=[jax.ShapeDtypeStruct((n_tok, QK_PACK), BF16), jax.ShapeDtypeStruct((n_tok, QK_PACK), BF16),
                   jax.ShapeDtypeStruct((n_tok, V_PACK), BF16)],
        compiler_params=_params(1),
        name="latent_front",
    )(x, mods, weights[0], consts['rope'], *weights[1:])


def _latent_na(l, q, k, v, kct, vct, g_rows, vec, ckv, krope_t, w_ukv):
    n_tok = DEC_BATCH * DEC_SEQ
    na_blk = P_NA // NA_W
    steps = GRID_ROWS // NA_ROWS_PER_STEP
    tq = NA_ROWS_PER_STEP * GRID_W
    ctx = pl.BlockSpec((None, None, NA_W, PAST_LEN), lambda b, j: (b, l, 0, 0))
    lay = lambda r, w: pl.BlockSpec((None, None, r, w), lambda b, j: (b, l, 0, 0))
    per_batch = lambda w: pl.BlockSpec((None, PAST_LEN, w), lambda b, j: (b, 0, 0))
    return pl.pallas_call(
        _na_lat_kernel,
        grid=(DEC_BATCH, steps),
        in_specs=[pl.BlockSpec((tq, NA_W), lambda b, j: (b * steps + j, na_blk)),
                  pl.BlockSpec((DEC_SEQ, NA_W), lambda b, j: (b, na_blk)),
                  pl.BlockSpec((DEC_SEQ, NA_W), lambda b, j: (b, PV_NA // NA_W)),
                  ctx, ctx, _layer_spec(l, (NA_HEADS * NA_PAIR_TILES, LANES)), _layer_spec(l, (N_VEC, D_MODEL)),
                  lay(PAST_LEN, KV_LORA), lay(MLA_ROPE, PAST_LEN), _layer_spec(l, (KV_LORA, W_MLA_P + MLA_W))],
        out_specs=[pl.BlockSpec((tq, NA_W), lambda b, j: (b * steps + j, 0)), per_batch(W_MLA_P), per_batch(MLA_W)],
        out_shape=[jax.ShapeDtypeStruct((n_tok, NA_W), BF16),
                   jax.ShapeDtypeStruct((DEC_BATCH, PAST_LEN, W_MLA_P), BF16),
                   jax.ShapeDtypeStruct((DEC_BATCH, PAST_LEN, MLA_W), BF16)],
        scratch_shapes=[pltpu.VMEM((NA_HEADS * NA_PAIR_TILES, GRID_W, LANES), F32)],
        compiler_params=_params(2),
        name="latent_neighbourhood",
    )(q, k, v, kct, vct, g_rows, vec, ckv, krope_t, w_ukv)


def _latent_attention(l, lam_init, q, k, v, kc, vc, kct, vct, o_na, t):
    n_tok = DEC_BATCH * DEC_SEQ
    nq = DEC_SEQ // TM
    ctx = pl.BlockSpec((None, None, DF_W, PAST_LEN), lambda b, j: (b, l, 0, 0))
    return pl.pallas_call(
        functools.partial(_lat_attn_kernel, lam_init),
        grid=(DEC_BATCH, nq),
        in_specs=[pl.BlockSpec((TM, QK_PACK), lambda b, j: (b * nq + j, 0)),
                  pl.BlockSpec((DEC_SEQ, QK_PACK), lambda b, j: (b, 0)),
                  pl.BlockSpec((DEC_SEQ, V_PACK), lambda b, j: (b, 0)),
                  pl.BlockSpec((None, PAST_LEN, W_MLA_P), lambda b, j: (b, 0, 0)),
                  pl.BlockSpec((None, PAST_LEN, MLA_W), lambda b, j: (b, 0, 0)),
                  ctx, ctx,
                  pl.BlockSpec((TM, NA_W), lambda b, j: (b * nq + j, 0)),
                  _layer_spec(l, (N_VEC, D_MODEL))],
        out_specs=pl.BlockSpec((TM, D_MODEL), lambda b, j: (b * nq + j, 0)),
        out_shape=jax.ShapeDtypeStruct((n_tok, D_MODEL), BF16),
        compiler_params=_params(2),
        name="latent_attention",
    )(q, k, v, kc, vc, kct, vct, o_na, t['vec'])


def kernel(x_prompt, x_sample, cache_mla_ckv, cache_mla_krope, cache_na_k, cache_na_v, cache_df_k, cache_df_v, c, c_ctx, w_mod, b_mod, g_mix, w_in, g_qa, w_uq, g_kva, w_ukv, g_mla_q, g_mla_k, g_na_q, g_na_k, na_rpb, g_df_q, g_df_k, df_lq1, df_lk1, df_lq2, df_lk2, g_df_sub, w_out, g_ffn, w_gate, w_up, w_down):
    p = dict(g_mix=g_mix, w_in=w_in, g_qa=g_qa, w_uq=w_uq, g_kva=g_kva, w_ukv=w_ukv, g_mla_q=g_mla_q, g_mla_k=g_mla_k,
             g_na_q=g_na_q, g_na_k=g_na_k, na_rpb=na_rpb, g_df_q=g_df_q, g_df_k=g_df_k, df_lq1=df_lq1, df_lk1=df_lk1,
             df_lq2=df_lq2, df_lk2=df_lk2, g_df_sub=g_df_sub, g_ffn=g_ffn)
    consts = dict(bd64=_block_diag(NA_HD), bd32=_block_diag(DF_QK), rope=_rope_tables())
    t = _tables(p)

    c_all = jnp.concatenate([c_ctx[None, :], c, jnp.zeros((N_MOD - 1 - DEC_BATCH, D_MODEL), F32)], axis=0)
    mods = _modulation(c_all, w_mod, b_mod)

    krope_t = jnp.swapaxes(cache_mla_krope, -1, -2)
    na_kt, na_vt, df_kt, df_vt = (_feature_major(a) for a in (cache_na_k, cache_na_v, cache_df_k, cache_df_v))

    xp = x_prompt.reshape(BATCH * SEQ, D_MODEL)
    xs = x_sample.reshape(DEC_BATCH * DEC_SEQ, D_MODEL)
    new_caches = ()
    ffn_f32 = (w_out, w_gate, w_up, w_down)
    for l in range(DEPTH):
        lam_init = 0.8 - 0.6 * math.exp(-0.3 * l)
        q, k, v = _latent_front(l, xs, mods, t, consts)
        o_na, kc, vc = _latent_na(l, q, k, v, na_kt, na_vt, t['g_rows'], t['vec'], cache_mla_ckv, krope_t, t['w_ukv'])
        mix_s = _latent_attention(l, lam_init, q, k, v, kc, vc, df_kt, df_vt, o_na, t)
        mix, new_caches, ffn_bf16 = _context_mixer(l, lam_init, xp, mods, t, consts, ffn_f32, new_caches, after=(mix_s,))
        xp, xs = _finish(l, xp, mix, xs, mix_s, mods, t, ffn_bf16)
    ckv_new, *narrow = new_caches
    return (xp.reshape(BATCH, SEQ, D_MODEL), xs.reshape(DEC_BATCH, DEC_SEQ, D_MODEL), ckv_new,
            *(jnp.swapaxes(a, -1, -2) for a in narrow))
```

```python
import functools
import math

import numpy as np
import jax
import jax.numpy as jnp
from jax import lax
from jax.experimental import pallas as pl
from jax.experimental.pallas import tpu as pltpu

F32 = jnp.float32
BF16 = jnp.bfloat16

D_MODEL = 1024
BATCH = 32
SEQ = 256
DEPTH = 2
DEC_BATCH = 2
DEC_SEQ = 1024
PAST_LEN = 256
GRID_W = 64
GRID_ROWS = DEC_SEQ // GRID_W
MLA_HEADS = 6
MLA_NOPE = 64
MLA_ROPE = 32
MLA_QK = MLA_NOPE + MLA_ROPE
MLA_V = 64
MLA_PAD = 128
Q_LORA = 256
KV_LORA = 128
NA_HEADS = 6
NA_HD = 64
NA_KR = 8
NA_KW = 16
DF_HEADS = 4
DF_HD = 64
DF_QK = 32
MLA_W = MLA_HEADS * MLA_V
NA_W = NA_HEADS * NA_HD
DF_W = DF_HEADS * DF_HD
D_FF = -(-8 * D_MODEL // (3 * 256)) * 256
ROPE_BASE = 10000.0
EPS = 1e-6
NEG = -1e30
LOG2E = math.log2(math.e)
MAX_FREE_SOFTMAX_BOUND = 60.0

LANES = 128
MXU_DIM = 256

A_CQ = 0
A_CKV = A_CQ + Q_LORA
A_KR = A_CKV + KV_LORA
A_COLS = A_KR + LANES
B_SRC = Q_LORA + KV_LORA + MLA_ROPE
B_NAQ = 0
B_NAK = B_NAQ + NA_W
B_NAV = B_NAK + NA_W
B_DFQ = B_NAV + NA_W
B_DFK = B_DFQ + DF_W
B_DFV = B_DFK + DF_W
B_COLS = B_DFV + DF_W

W_MLA_P = MLA_HEADS * MLA_PAD
QK_PACK = W_MLA_P + NA_W + DF_W
V_PACK = MLA_W + NA_W + DF_W
P_MLA = 0
P_NA = W_MLA_P
P_DF = P_NA + NA_W
PV_MLA = 0
PV_NA = MLA_W
PV_DF = MLA_W + NA_W

(V_GMIX, V_GFFN, V_GQA, V_GKVA, V_GMQ, V_GMK, V_GNQ, V_GNK, V_GDQ, V_GDK, V_GDS,
 V_LQ1, V_LK1, V_LQ2, V_LK2) = range(15)
N_VEC = 16

N_MOD = 8
MOD_COLS_PER_STEP = 3072
TM = 512
EXACT_ROWS = 128
TM_LAT_FRONT = 1024
LAT_FRONT_CHUNK = 256
TM_FINISH = 512
CTX_SEQS_PER_STEP = 2
NA_ROWS_PER_STEP = 4
VMEM_LIMIT = 56 * 1024 * 1024

NA_PAIR_TILES = 2 * NA_KR - 2
N_CACHE = 6
N_FFN_W = 4


def _dot(a, b):
    return jnp.dot(a, b, preferred_element_type=F32)


def _dot_nt(a, b):
    return lax.dot_general(a, b, (((1,), (1,)), ((), ())), preferred_element_type=F32)


def _lane_iota(shape):
    return lax.broadcasted_iota(jnp.int32, shape, len(shape) - 1)


def _rms_rows(x, g):
    ms = jnp.mean(x * x, axis=-1, keepdims=True)
    return x * lax.rsqrt(ms + EPS) * g


def _tile_rms(x, g, n_real):
    outs = []
    for c0 in range(0, x.shape[1], LANES):
        xt = x[:, c0:c0 + LANES]
        ms = jnp.sum(xt * xt, axis=-1, keepdims=True) * (1.0 / n_real)
        outs.append(xt * lax.rsqrt(ms + EPS) * g[:, c0:c0 + LANES])
    return jnp.concatenate(outs, axis=1)


def _seg_rms(x, bd_ref, g, group):
    width = x.shape[1]
    sq = (x * x).astype(BF16)
    parts = []
    for c0 in range(0, width, MXU_DIM):
        w = min(MXU_DIM, width - c0)
        parts.append(_dot(sq[:, c0:c0 + w], bd_ref[0:w, 0:w]))
    ss = parts[0] if len(parts) == 1 else jnp.concatenate(parts, axis=1)
    return x * lax.rsqrt(ss * (1.0 / group) + EPS) * g


def _rope_tiles(x, cos, sa, sb):
    outs = []
    for t in range(x.shape[1] // LANES):
        xt = x[:, t * LANES:(t + 1) * LANES]
        up = pltpu.roll(xt, LANES - MLA_ROPE // 4, 1)
        dn = pltpu.roll(xt, MLA_ROPE // 4, 1)
        outs.append(xt * cos + up * sa + dn * sb)
    return outs[0] if len(outs) == 1 else jnp.concatenate(outs, axis=1)


def _diff_lambda(vec_ref, lam_init):
    a = jnp.sum(vec_ref[V_LQ1:V_LQ1 + 1, 0:DF_QK] * vec_ref[V_LK1:V_LK1 + 1, 0:DF_QK], axis=-1, keepdims=True)
    b = jnp.sum(vec_ref[V_LQ2:V_LQ2 + 1, 0:DF_QK] * vec_ref[V_LK2:V_LK2 + 1, 0:DF_QK], axis=-1, keepdims=True)
    return jnp.exp(a) - jnp.exp(b) + lam_init


def _mixer_front(x, mod, vec_ref, wa_ref, wb_ref, w_uq_ref, bd64_ref, bd32_ref):
    sh = mod[:, 0:D_MODEL]
    sc = mod[:, D_MODEL:2 * D_MODEL]
    h = (_rms_rows(x, vec_ref[V_GMIX:V_GMIX + 1, :]) * (1.0 + sc) + sh).astype(BF16)
    za = _dot_nt(h, wa_ref[...])
    zb = _dot_nt(h, wb_ref[B_SRC:B_SRC + B_COLS, :])
    cqn = _rms_rows(za[:, A_CQ:A_CQ + Q_LORA], vec_ref[V_GQA:V_GQA + 1, 0:Q_LORA])
    q_raw = _dot_nt(cqn.astype(BF16), w_uq_ref[...])
    q_mla = _tile_rms(q_raw, vec_ref[V_GMQ:V_GMQ + 1, 0:W_MLA_P] * (MLA_QK ** -0.5 * LOG2E), MLA_QK)
    ckv_n = _rms_rows(za[:, A_CKV:A_CKV + KV_LORA], vec_ref[V_GKVA:V_GKVA + 1, 0:KV_LORA])
    kr_tile = za[:, A_KR:A_KR + LANES]
    q_na = _seg_rms(zb[:, B_NAQ:B_NAQ + NA_W], bd64_ref, vec_ref[V_GNQ:V_GNQ + 1, 0:NA_W] * (NA_HD ** -0.5 * LOG2E), NA_HD)
    k_na = _seg_rms(zb[:, B_NAK:B_NAK + NA_W], bd64_ref, vec_ref[V_GNK:V_GNK + 1, 0:NA_W], NA_HD)
    v_na = zb[:, B_NAV:B_NAV + NA_W]
    q_df = _seg_rms(zb[:, B_DFQ:B_DFQ + DF_W], bd32_ref, vec_ref[V_GDQ:V_GDQ + 1, 0:DF_W] * (DF_QK ** -0.5 * LOG2E), DF_QK)
    k_df = _seg_rms(zb[:, B_DFK:B_DFK + DF_W], bd32_ref, vec_ref[V_GDK:V_GDK + 1, 0:DF_W], DF_QK)
    v_df = zb[:, B_DFV:B_DFV + DF_W]
    return q_mla, ckv_n, kr_tile, q_na, k_na, v_na, q_df, k_df, v_df


def _mla_kv(ckv_n, kr_tile, vec_ref, w_ukv_ref):
    kv = _dot(ckv_n.astype(BF16), w_ukv_ref[...])
    lane = _lane_iota((1, LANES))
    kr = jnp.where((lane >= MLA_NOPE) & (lane < MLA_QK), kr_tile, 0.0)
    k_pre = kv[:, 0:W_MLA_P] + jnp.concatenate([kr] * MLA_HEADS, axis=1)
    k = _tile_rms(k_pre, vec_ref[V_GMK:V_GMK + 1, 0:W_MLA_P], MLA_QK)
    return k, kv[:, W_MLA_P:W_MLA_P + MLA_W]


def _softmax_parts(s, bound=None):
    m = jnp.max(s, axis=-1, keepdims=True) if bound is None else bound
    p = jnp.exp2(s - m)
    return p, 1.0 / jnp.sum(p, axis=-1, keepdims=True)


def _max_sq_norm(kt, group):
    sq = kt * kt
    best = None
    for r0 in range(0, kt.shape[0], group):
        n2 = jnp.sum(sq[r0:r0 + group], axis=0, keepdims=True)
        best = n2 if best is None else jnp.maximum(best, n2)
    return jnp.max(best, axis=-1, keepdims=True)


def _score_bounds(vec_ref, bias_max=None, na_key_sq=None, df_key_sq=None):
    gmax = lambda row, w: jnp.max(jnp.abs(vec_ref[row:row + 1, 0:w]), axis=-1, keepdims=True)
    slack = LOG2E * (1.0 + 2.0 ** -6)

    def key_norm(row, w, d, measured_sq):
        k = gmax(row, w) * d ** 0.5
        return k if measured_sq is None else jnp.maximum(k, jnp.sqrt(measured_sq))

    b_mla = gmax(V_GMQ, W_MLA_P) * key_norm(V_GMK, W_MLA_P, MLA_QK, None) * slack
    b_na = gmax(V_GNQ, NA_W) * key_norm(V_GNK, NA_W, NA_HD, na_key_sq) * slack
    if bias_max is not None:
        b_na = b_na + bias_max * LOG2E
    b_df = gmax(V_GDQ, DF_W) * key_norm(V_GDK, DF_W, DF_QK, df_key_sq) * slack
    worst = jnp.maximum(b_mla, jnp.maximum(b_na, b_df))
    return (b_mla, b_na, b_df), worst[0, 0] <= MAX_FREE_SOFTMAX_BOUND


EXACT_MAX = (None, None, None)


def _with_score_bounds(vec_ref, bounded, exact, **measured):
    bounds, ok = _score_bounds(vec_ref, **measured)
    pl.when(ok)(lambda: bounded(bounds))
    pl.when(jnp.logical_not(ok))(exact)


def _row_block(i, n):
    return slice(i * n, (i + 1) * n) if isinstance(i, int) else pl.ds(pl.multiple_of(i * n, n), n)


def _rolled(n, body):
    def step(i, carry):
        body(i)
        return carry

    lax.fori_loop(0, n, step, 0)


def _scores(q, k_segs):
    parts = [_dot(q, k) if feature_major else _dot_nt(q, k) for k, feature_major in k_segs]
    return parts[0] if len(parts) == 1 else jnp.concatenate(parts, axis=1)


def _pv(p, v_segs):
    out = None
    c0 = 0
    for v, feature_major in v_segs:
        n = v.shape[1] if feature_major else v.shape[0]
        o = _dot_nt(p[:, c0:c0 + n], v) if feature_major else _dot(p[:, c0:c0 + n], v)
        out = o if out is None else out + o
        c0 += n
    return out


def _seg_tile(seg, t):
    a, feature_major = seg
    return (a[t * LANES:(t + 1) * LANES, :] if feature_major else a[:, t * LANES:(t + 1) * LANES]), feature_major


def _lane_groups(qt, width):
    lane = _lane_iota((1, LANES))
    zero = jnp.zeros_like(qt)
    return jnp.concatenate(
        [jnp.where((lane >= g * width) & (lane < (g + 1) * width), qt, zero) for g in range(LANES // width)], axis=0)


def _pair_select(o2):
    tq = o2.shape[0] // 2
    return jnp.where(_lane_iota((1, LANES)) < NA_HD, o2[0:tq], o2[tq:2 * tq])


def _mla_attend(q, k_segs, v_segs, bound=None):
    outs = []
    for t in range(MLA_HEADS // 2):
        vt = [_seg_tile(v, t) for v in v_segs]
        halves = []
        for h in (2 * t, 2 * t + 1):
            p, il = _softmax_parts(_scores(q[:, h * MLA_PAD:(h + 1) * MLA_PAD], [_seg_tile(k, h) for k in k_segs]),
                                   bound)
            halves.append(_pv(p.astype(BF16), vt) * il)
        outs.append(jnp.where(_lane_iota((1, LANES)) < MLA_V, halves[0], halves[1]))
    return jnp.concatenate(outs, axis=1)


def _na_tile_attend(qt, k_segs, v_segs, bias=None, bound=None):
    s = _scores(_lane_groups(qt, NA_HD), k_segs)
    if bias is not None:
        nb = bias.shape[1]
        s = jnp.concatenate([s[:, 0:nb] + bias, s[:, nb:]], axis=1)
    p, il = _softmax_parts(s, bound)
    return _pair_select(_pv(p.astype(BF16), v_segs) * il)


def _na_attend_full(q, k_segs, v_segs, bound=None):
    return jnp.concatenate(
        [_na_tile_attend(q[:, t * LANES:(t + 1) * LANES], [_seg_tile(k, t) for k in k_segs],
                         [_seg_tile(v, t) for v in v_segs], bound=bound) for t in range(NA_HEADS // 2)], axis=1)


def _df_attend(q, k_segs, v_segs, lam, g_sub, out_scale, bound=None):
    outs = []
    lane = _lane_iota((1, LANES))
    tq = q.shape[0]
    for t in range(DF_HEADS // 2):
        kt = [_seg_tile(k, t) for k in k_segs]
        vt = [_seg_tile(v, t) for v in v_segs]
        p, il = _softmax_parts(_scores(_lane_groups(q[:, t * LANES:(t + 1) * LANES], DF_QK), kt), bound)
        pn = []
        for hh in range(2):
            r1, r2 = 2 * hh * tq, (2 * hh + 1) * tq
            pn.append((p[r1:r1 + tq] * il[r1:r1 + tq] - p[r2:r2 + tq] * (lam * il[r2:r2 + tq])).astype(BF16))
        o = _pair_select(_pv(jnp.concatenate(pn, axis=0), vt))
        o2 = o * o
        ms_e = jnp.sum(jnp.where(lane < DF_HD, o2, 0.0), axis=-1, keepdims=True)
        ms_o = jnp.sum(jnp.where(lane >= DF_HD, o2, 0.0), axis=-1, keepdims=True)
        r = lax.rsqrt(jnp.where(lane < DF_HD, ms_e, ms_o) * (1.0 / DF_HD) + EPS)
        outs.append(o * r * (g_sub[:, t * LANES:(t + 1) * LANES] * out_scale))
    return jnp.concatenate(outs, axis=1)


def _mod_kernel(c_ref, w_ref, b_ref, o_ref):
    c = c_ref[...]
    s = c * jax.nn.sigmoid(c)
    o_ref[...] = _dot(s.astype(BF16), w_ref[...].astype(BF16)) + b_ref[pl.ds(pl.program_id(0), 1), :]


def _ctx_kernel(lam_init, n_alias, n_after, x_ref, mod_ref, vec_ref, wa_ref, wb_ref, w_uq_ref, w_ukv_ref, bd64_ref,
                bd32_ref, w_out_ref, *rest):
    ffn_f32, rest = rest[:N_FFN_W], rest[N_FFN_W + n_alias + n_after:]
    mix_ref, *cache_refs = rest[:1 + N_CACHE]
    for src, dst in zip(ffn_f32, rest[1 + N_CACHE:]):
        dst[...] = src[...].astype(BF16)
    if n_alias == 0:
        for ref in cache_refs:
            ref[:, 1:] = jnp.zeros((ref.shape[0], ref.shape[1] - 1) + ref.shape[2:], F32)
        cache_refs = [ref.at[:, 0] for ref in cache_refs]
    ckv_ref, kr_ref, nak_ref, nav_ref, dfk_ref, dfv_ref = cache_refs
    lam = _diff_lambda(vec_ref, lam_init)
    mod = mod_ref[0:1, :]
    w_out = w_out_ref[...].astype(BF16)
    bf = lambda a: a.astype(BF16)
    seg = lambda a: [(a.astype(BF16), False)]
    def sequence(s, bounds):
        b_mla, b_na, b_df = bounds
        rows = _row_block(s, SEQ)
        q_mla, ckv_n, kr_tile, q_na, k_na, v_na, q_df, k_df, v_df = _mixer_front(
            x_ref[rows, :], mod, vec_ref, wa_ref, wb_ref, w_uq_ref, bd64_ref, bd32_ref)
        k_mla, v_mla = _mla_kv(ckv_n, kr_tile, vec_ref, w_ukv_ref)
        ckv_ref[s] = ckv_n
        kr_ref[s] = kr_tile.T[0:MLA_ROPE]
        for ref, a in ((nak_ref, k_na), (nav_ref, v_na), (dfk_ref, k_df), (dfv_ref, v_df)):
            at = a.T
            for h in range(ref.shape[1]):
                ref[s, h] = at[h * NA_HD:(h + 1) * NA_HD]
        o_mla = _mla_attend(bf(q_mla), seg(k_mla), seg(v_mla), b_mla)
        o_na = _na_attend_full(bf(q_na), seg(k_na), seg(v_na), b_na)
        o_df = _df_attend(bf(q_df), seg(k_df), seg(v_df), lam, vec_ref[V_GDS:V_GDS + 1, 0:DF_W], 1.0 - lam_init, b_df)
        mix_ref[rows, :] = _dot(jnp.concatenate([o_mla, o_na, o_df], axis=1).astype(BF16), w_out)

    def bounded(bounds):
        for s in range(CTX_SEQS_PER_STEP):
            sequence(s, bounds)

    _with_score_bounds(vec_ref, bounded, lambda: _rolled(CTX_SEQS_PER_STEP, lambda s: sequence(s, EXACT_MAX)))


def _finish_kernel(n_ctx_blocks, xc_ref, mixc_ref, xl_ref, mixl_ref, mod_ref, vec_ref, w_out_ref, w_gate_ref, w_up_ref,
                   w_down_ref, yc_ref, yl_ref):
    def block(x_ref, mix_ref, y_ref, mod_row, projected):
        mod = mod_ref[pl.ds(mod_row, 1), :]
        gate_m = mod[:, 2 * D_MODEL:3 * D_MODEL]
        sh = mod[:, 3 * D_MODEL:4 * D_MODEL]
        sc = mod[:, 4 * D_MODEL:5 * D_MODEL]
        gate_f = mod[:, 5 * D_MODEL:6 * D_MODEL]
        x1 = x_ref[...] + gate_m * (mix_ref[...] if projected else _dot(mix_ref[...], w_out_ref[...]))
        h = (_rms_rows(x1, vec_ref[V_GFFN:V_GFFN + 1, :]) * (1.0 + sc) + sh).astype(BF16)
        g = _dot(h, w_gate_ref[...])
        u = _dot(h, w_up_ref[...])
        a = (g * jax.nn.sigmoid(g) * u).astype(BF16)
        y_ref[...] = x1 + gate_f * _dot(a, w_down_ref[...])

    is_ctx = pl.program_id(0) < n_ctx_blocks
    lat_row = 1 + jnp.maximum(pl.program_id(0) - n_ctx_blocks, 0) // (DEC_SEQ // TM_FINISH)
    pl.when(is_ctx)(lambda: block(xc_ref, mixc_ref, yc_ref, 0, True))
    pl.when(jnp.logical_not(is_ctx))(lambda: block(xl_ref, mixl_ref, yl_ref, lat_row, False))


def _lat_front_kernel(x_ref, mod_ref, vec_ref, rope_ref, wa_ref, wb_ref, w_uq_ref, w_ukv_ref, bd64_ref, bd32_ref,
                      q_ref, k_ref, v_ref):
    mod = mod_ref[pl.ds(1 + pl.program_id(0) // (DEC_SEQ // TM_LAT_FRONT), 1), :]
    for c in range(TM_LAT_FRONT // LAT_FRONT_CHUNK):
        rows = slice(c * LAT_FRONT_CHUNK, (c + 1) * LAT_FRONT_CHUNK)
        q_mla, ckv_n, kr_tile, q_na, k_na, v_na, q_df, k_df, v_df = _mixer_front(
            x_ref[rows, :], mod, vec_ref, wa_ref, wb_ref, w_uq_ref, bd64_ref, bd32_ref)
        k_mla, v_mla = _mla_kv(ckv_n, kr_tile, vec_ref, w_ukv_ref)
        cm, sam, sbm = rope_ref[0, rows], rope_ref[1, rows], rope_ref[2, rows]
        cd, sad, sbd = rope_ref[3, rows], rope_ref[4, rows], rope_ref[5, rows]
        q_ref[rows, :] = jnp.concatenate(
            [_rope_tiles(q_mla, cm, sam, sbm), q_na, _rope_tiles(q_df, cd, sad, sbd)], axis=1).astype(BF16)
        k_ref[rows, :] = jnp.concatenate(
            [_rope_tiles(k_mla, cm, sam, sbm), k_na, _rope_tiles(k_df, cd, sad, sbd)], axis=1).astype(BF16)
        v_ref[rows, :] = jnp.concatenate([v_mla, v_na, v_df], axis=1).astype(BF16)


def _na_lat_kernel(q_ref, kl_ref, vl_ref, kct_ref, vct_ref, g_ref, vec_ref, ckv_ref, krt_ref, w_ukv_ref,
                   o_ref, kc_mla_ref, vc_mla_ref, bias_ref):
    b = pl.program_id(0)
    j = pl.program_id(1)

    @pl.when(j == 0)
    def _context_mla_kv():
        krt = jnp.concatenate([jnp.zeros((MLA_NOPE, PAST_LEN), F32), krt_ref[...],
                               jnp.zeros((LANES - MLA_QK, PAST_LEN), F32)], axis=0)
        k_mla, v_mla = _mla_kv(ckv_ref[...], krt.T, vec_ref, w_ukv_ref)
        kc_mla_ref[...] = k_mla.astype(BF16)
        vc_mla_ref[...] = v_mla.astype(BF16)

    @pl.when((b == 0) & (j == 0))
    def _build_bias():
        c = lax.broadcasted_iota(jnp.int32, (GRID_W, LANES), 0)
        kc = _lane_iota((GRID_W, LANES)) % GRID_W
        start = jnp.clip(c - NA_KW // 2, 0, GRID_W - NA_KW)
        in_win = (kc >= start) & (kc < start + NA_KW)

        def body(i, carry):
            row = jnp.broadcast_to(g_ref[pl.ds(i, 1), :], (GRID_W, LANES))
            toep = pltpu.roll(row, 0, 1, stride=1, stride_axis=0)
            bias_ref[i] = jnp.where(in_win, toep * LOG2E, NEG)
            return carry

        lax.fori_loop(0, NA_HEADS * NA_PAIR_TILES, body, 0, unroll=NA_HEADS)

    n_win = NA_KR * GRID_W
    kct_f32 = kct_ref[...]
    kct = kct_f32.astype(BF16)
    vct = vct_ref[...].astype(BF16)

    def win_start(r):
        return jnp.clip(r - NA_KR // 2, 0, GRID_ROWS - NA_KR)

    def attend(grid_rows, rs, bound):
        a0, n = grid_rows[0], len(grid_rows)
        row0 = pl.multiple_of(rs * GRID_W, GRID_W)
        outs = []
        for t in range(NA_HEADS // 2):
            lanes = slice(t * LANES, (t + 1) * LANES)
            bias = jnp.concatenate(
                [jnp.concatenate([bias_ref[h * NA_PAIR_TILES + (rs - (j * NA_ROWS_PER_STEP + a) + NA_KR - 1) + 2 * m]
                                  for m in range(NA_KR // 2)], axis=1)
                 for h in (2 * t, 2 * t + 1) for a in grid_rows], axis=0)
            outs.append(_na_tile_attend(
                q_ref[_row_block(a0, GRID_W) if n == 1 else slice(a0 * GRID_W, (a0 + n) * GRID_W), lanes],
                [(kl_ref[pl.ds(row0, n_win), lanes], False), (kct[lanes, :], True)],
                [(vl_ref[pl.ds(row0, n_win), lanes], False), (vct[lanes, :], True)], bias, bound))
        return jnp.concatenate(outs, axis=1).astype(BF16)

    first_r, last_r = j * NA_ROWS_PER_STEP, (j + 1) * NA_ROWS_PER_STEP - 1
    shared = win_start(first_r) == win_start(last_r)

    def attend_step(bounds):
        bound = bounds[1]

        @pl.when(shared)
        def _shared_window():
            o_ref[...] = attend(tuple(range(NA_ROWS_PER_STEP)), win_start(first_r), bound)

        @pl.when(jnp.logical_not(shared))
        def _per_row_windows():
            for a in range(NA_ROWS_PER_STEP):
                o_ref[a * GRID_W:(a + 1) * GRID_W, :] = attend((a,), win_start(first_r + a), bound)

    def exact_row(a):
        o_ref[_row_block(a, GRID_W), :] = attend((a,), win_start(first_r + a), None)

    bias_max = jnp.max(jnp.max(jnp.abs(g_ref[...]), axis=-1, keepdims=True), axis=0, keepdims=True)
    _with_score_bounds(vec_ref, attend_step, lambda: _rolled(NA_ROWS_PER_STEP, exact_row),
                       bias_max=bias_max, na_key_sq=_max_sq_norm(kct_f32, NA_HD))


def _lat_attn_kernel(lam_init, q_ref, kl_ref, vl_ref, kc_ref, vc_ref, kct_ref, vct_ref, ona_ref, vec_ref, mix_ref):
    lam = _diff_lambda(vec_ref, lam_init)
    kct = kct_ref[...]

    def attend(bounds, rows):
        b_mla, _, b_df = bounds
        o_mla = _mla_attend(q_ref[rows, P_MLA:P_MLA + W_MLA_P],
                            [(kc_ref[...], False), (kl_ref[:, P_MLA:P_MLA + W_MLA_P], False)],
                            [(vc_ref[...], False), (vl_ref[:, PV_MLA:PV_MLA + MLA_W], False)], b_mla)
        o_df = _df_attend(q_ref[rows, P_DF:P_DF + DF_W],
                          [(kct.astype(BF16), True), (kl_ref[:, P_DF:P_DF + DF_W], False)],
                          [(vct_ref[...].astype(BF16), True), (vl_ref[:, PV_DF:PV_DF + DF_W], False)],
                          lam, vec_ref[V_GDS:V_GDS + 1, 0:DF_W], 1.0 - lam_init, b_df)
        mix_ref[rows, :] = jnp.concatenate([o_mla.astype(BF16), ona_ref[rows, :], o_df.astype(BF16)], axis=1)

    def exact():
        _rolled(TM // EXACT_ROWS, lambda i: attend(EXACT_MAX, _row_block(i, EXACT_ROWS)))

    _with_score_bounds(vec_ref, lambda bounds: attend(bounds, slice(None)), exact,
                       df_key_sq=_max_sq_norm(kct, DF_QK))


def _const_spec(shape):
    nd = len(shape)
    return pl.BlockSpec(shape, lambda *_: (0,) * nd, pipeline_mode=pl.Buffered(1))


def _layer_spec(l, shape):
    nd = len(shape)
    return pl.BlockSpec((None,) + tuple(shape), lambda *_: (l,) + (0,) * nd, pipeline_mode=pl.Buffered(1))


def _params(n_axes):
    return pltpu.CompilerParams(dimension_semantics=("arbitrary",) * n_axes, vmem_limit_bytes=VMEM_LIMIT)


def _block_diag(group):
    i = np.arange(MXU_DIM) // group
    return jnp.asarray((i[:, None] == i[None, :]).astype(np.float32), dtype=BF16)


def _rope_tables():
    t = np.arange(DEC_SEQ)
    row = (t // GRID_W).astype(np.float64)
    col = (t % GRID_W).astype(np.float64)
    n = MLA_ROPE // 4
    inv = 1.0 / (ROPE_BASE ** (np.arange(n, dtype=np.float64) * 2.0 / (MLA_ROPE // 2)))
    ar = row[:, None] * inv
    ac = col[:, None] * inv
    ang = np.concatenate([ar, ar, ac, ac], axis=-1)
    cos32, sin32 = np.cos(ang), np.sin(ang)
    first = (np.arange(MLA_ROPE) % (2 * n)) < n
    sa32 = np.where(first, -sin32, 0.0)
    sb32 = np.where(first, 0.0, sin32)

    def mla_tile(v32, fill):
        out = np.full((DEC_SEQ, LANES), fill)
        out[:, MLA_NOPE:MLA_QK] = v32
        return out

    tabs = [mla_tile(cos32, 1.0), mla_tile(sa32, 0.0), mla_tile(sb32, 0.0),
            np.tile(cos32, (1, LANES // DF_QK)), np.tile(sa32, (1, LANES // DF_QK)), np.tile(sb32, (1, LANES // DF_QK))]
    return jnp.asarray(np.stack(tabs).astype(np.float32))


def _feature_major(a):
    a = jnp.swapaxes(a, -1, -2)
    return a.reshape(a.shape[:-3] + (a.shape[-3] * a.shape[-2], a.shape[-1]))


def _tables(p):
    wb = jnp.swapaxes(p['w_in'], 1, 2).astype(BF16)
    kr = wb[:, Q_LORA + KV_LORA:B_SRC]
    z32 = jnp.zeros_like(kr)
    wa = jnp.concatenate([wb[:, :Q_LORA + KV_LORA], kr, z32, kr, z32], axis=1)
    w_uq_t = jnp.swapaxes(p['w_uq'], 1, 2).reshape(DEPTH, MLA_HEADS, MLA_QK, Q_LORA)
    w_uq_p = jnp.pad(w_uq_t, ((0, 0), (0, 0), (0, MLA_PAD - MLA_QK), (0, 0))).reshape(DEPTH, W_MLA_P, Q_LORA).astype(BF16)
    w_ukv = p['w_ukv'].reshape(DEPTH, KV_LORA, MLA_HEADS, MLA_NOPE + MLA_V)
    wk = jnp.pad(w_ukv[..., :MLA_NOPE], ((0, 0), (0, 0), (0, 0), (0, MLA_PAD - MLA_NOPE))).reshape(DEPTH, KV_LORA, W_MLA_P)
    wv = w_ukv[..., MLA_NOPE:].reshape(DEPTH, KV_LORA, MLA_W)
    w_ukv_r = jnp.concatenate([wk, wv], axis=2).astype(BF16)

    def row(v, reps=1):
        v = jnp.tile(v, (1, reps)) if reps > 1 else v
        return [v, jnp.zeros((DEPTH, D_MODEL - v.shape[1]), F32)] if v.shape[1] < D_MODEL else [v]

    pad_head = lambda g: jnp.pad(g, ((0, 0), (0, MLA_PAD - MLA_QK)))
    pieces = (row(p['g_mix']) + row(p['g_ffn']) + row(p['g_qa']) + row(p['g_kva'])
              + row(pad_head(p['g_mla_q']), MLA_HEADS) + row(pad_head(p['g_mla_k']), MLA_HEADS)
              + row(p['g_na_q'], NA_HEADS) + row(p['g_na_k'], NA_HEADS)
              + row(p['g_df_q'], 2 * DF_HEADS) + row(p['g_df_k'], 2 * DF_HEADS) + row(p['g_df_sub'], DF_HEADS)
              + row(p['df_lq1']) + row(p['df_lk1']) + row(p['df_lq2']) + row(p['df_lk2'])
              + [jnp.zeros((DEPTH, D_MODEL), F32)])
    vec = jnp.concatenate(pieces, axis=1).reshape(DEPTH, N_VEC, D_MODEL)
    f = p['na_rpb']
    n_rel = 2 * NA_KW - 1
    zpad = jnp.zeros((DEPTH, NA_HEADS, NA_PAIR_TILES, (LANES - 2 * n_rel) // 2), F32)
    g_rows = jnp.concatenate([f[:, :, :-1, NA_KW - 1:], zpad, f[:, :, 1:, :], zpad, f[:, :, :-1, :NA_KW - 1]], axis=-1)
    g_rows = g_rows.reshape(DEPTH, NA_HEADS * NA_PAIR_TILES, LANES)
    return dict(wa=wa, wb=wb, w_uq=w_uq_p, w_ukv=w_ukv_r, vec=vec, g_rows=g_rows)


def _modulation(c_all, w_mod, b_mod):
    tn = MOD_COLS_PER_STEP
    return pl.pallas_call(
        _mod_kernel,
        grid=(DEPTH, 6 * D_MODEL // tn),
        in_specs=[pl.BlockSpec((N_MOD, D_MODEL), lambda l, j: (0, 0)),
                  pl.BlockSpec((None, D_MODEL, tn), lambda l, j: (l, 0, j)),
                  pl.BlockSpec((DEPTH, tn), lambda l, j: (0, j))],
        out_specs=pl.BlockSpec((None, N_MOD, tn), lambda l, j: (l, 0, j)),
        out_shape=jax.ShapeDtypeStruct((DEPTH, N_MOD, 6 * D_MODEL), F32),
        compiler_params=_params(2),
        name="modulation",
    )(c_all, w_mod, b_mod)


def _mod_spec(l):
    return pl.BlockSpec((None, N_MOD, 6 * D_MODEL), lambda *_: (l, 0, 0))


def _front_weight_specs(l):
    return [_layer_spec(l, (N_VEC, D_MODEL)), _layer_spec(l, (A_COLS, D_MODEL)), _layer_spec(l, (B_SRC + B_COLS, D_MODEL)),
            _layer_spec(l, (W_MLA_P, Q_LORA)), _layer_spec(l, (KV_LORA, W_MLA_P + MLA_W)),
            _const_spec((MXU_DIM, MXU_DIM)), _const_spec((MXU_DIM, MXU_DIM))]


def _front_weights(t, consts):
    return (t['vec'], t['wa'], t['wb'], t['w_uq'], t['w_ukv'], consts['bd64'], consts['bd32'])


def _context_mixer(l, lam_init, x, mods, t, consts, ffn_f32, prev_caches, after=()):
    n_tok = BATCH * SEQ
    n_alias = len(prev_caches)
    tok = lambda w: pl.BlockSpec((CTX_SEQS_PER_STEP * SEQ, w), lambda b: (b, 0))
    if n_alias == 0:
        assert l == 0
        lay = lambda *s: pl.BlockSpec((CTX_SEQS_PER_STEP, DEPTH) + s, lambda b: (b, 0) + (0,) * len(s))
    else:
        lay = lambda *s: pl.BlockSpec((CTX_SEQS_PER_STEP, None) + s, lambda b: (b, l) + (0,) * len(s))
    cache_shapes = [(SEQ, KV_LORA), (MLA_ROPE, SEQ), (NA_HEADS, NA_HD, SEQ), (NA_HEADS, NA_HD, SEQ),
                    (DF_HEADS, DF_HD, SEQ), (DF_HEADS, DF_HD, SEQ)]
    weights = _front_weights(t, consts)
    steps = BATCH // CTX_SEQS_PER_STEP
    ffn_chunks = [(w.shape[1] // steps, w.shape[2]) for w in ffn_f32]
    n_in = 2 + len(weights) + 1 + len(ffn_f32)
    mix, *outs = pl.pallas_call(
        functools.partial(_ctx_kernel, lam_init, n_alias, len(after)),
        grid=(steps,),
        in_specs=[tok(D_MODEL), _mod_spec(l)]
        + _front_weight_specs(l) + [_layer_spec(l, ffn_f32[0].shape[1:])]
        + [pl.BlockSpec((None,) + c, lambda b: (l, b, 0)) for c in ffn_chunks]
        + [pl.BlockSpec(memory_space=pl.ANY)] * (n_alias + len(after)),
        out_specs=[tok(D_MODEL)] + [lay(*s) for s in cache_shapes] + [pl.BlockSpec(c, lambda b: (b, 0)) for c in ffn_chunks],
        out_shape=[jax.ShapeDtypeStruct((n_tok, D_MODEL), F32)]
        + [jax.ShapeDtypeStruct((BATCH, DEPTH) + s, F32) for s in cache_shapes]
        + [jax.ShapeDtypeStruct(w.shape[1:], BF16) for w in ffn_f32],
        input_output_aliases={n_in + i: 1 + i for i in range(n_alias)},
        compiler_params=_params(1),
        name="context_mixer",
    )(x, mods, *weights, ffn_f32[0], *ffn_f32, *prev_caches, *after)
    return mix, outs[:N_CACHE], outs[N_CACHE:]


def _finish(l, xc, mixc, xl, mixl, mods, t, ffn_bf16):
    nc, nl = xc.shape[0] // TM_FINISH, xl.shape[0] // TM_FINISH
    ctx_tok = pl.BlockSpec((TM_FINISH, D_MODEL), lambda i: (jnp.minimum(i, nc - 1), 0))
    lat_tok = pl.BlockSpec((TM_FINISH, D_MODEL), lambda i: (jnp.maximum(i - nc, 0), 0))
    return pl.pallas_call(
        functools.partial(_finish_kernel, nc),
        grid=(nc + nl,),
        in_specs=[ctx_tok, ctx_tok, lat_tok, lat_tok,
                  _mod_spec(l),
                  _layer_spec(l, (N_VEC, D_MODEL))] + [_const_spec(w.shape) for w in ffn_bf16],
        out_specs=[ctx_tok, lat_tok],
        out_shape=[jax.ShapeDtypeStruct(xc.shape, F32), jax.ShapeDtypeStruct(xl.shape, F32)],
        compiler_params=_params(1),
        name="finish",
    )(xc, mixc, xl, mixl, mods, t['vec'], *ffn_bf16)


def _latent_front(l, x, mods, t, consts):
    n_tok = DEC_BATCH * DEC_SEQ
    tm = TM_LAT_FRONT
    blocks_per_seq = DEC_SEQ // tm
    tok = lambda w: pl.BlockSpec((tm, w), lambda i: (i, 0))
    wspecs = _front_weight_specs(l)
    weights = _front_weights(t, consts)
    return pl.pallas_call(
        _lat_front_kernel,
        grid=(n_tok // tm,),
        in_specs=[tok(D_MODEL),
                  _mod_spec(l),
                  wspecs[0], pl.BlockSpec((6, tm, LANES), lambda i: (0, i % blocks_per_seq, 0))] + wspecs[1:],
        out_specs=[tok(QK_PACK), tok(QK_PACK), tok(V_PACK)],
        out_shape=[jax.ShapeDtypeStruct((n_tok, QK_PACK), BF16), jax.ShapeDtypeStruct((n_tok, QK_PACK), BF16),
                   jax.ShapeDtypeStruct((n_tok, V_PACK), BF16)],
        compiler_params=_params(1),
        name="latent_front",
    )(x, mods, weights[0], consts['rope'], *weights[1:])


def _latent_na(l, q, k, v, kct, vct, g_rows, vec, ckv, krope_t, w_ukv):
    n_tok = DEC_BATCH * DEC_SEQ
    na_blk = P_NA // NA_W
    steps = GRID_ROWS // NA_ROWS_PER_STEP
    tq = NA_ROWS_PER_STEP * GRID_W
    ctx = pl.BlockSpec((None, None, NA_W, PAST_LEN), lambda b, j: (b, l, 0, 0))
    lay = lambda r, w: pl.BlockSpec((None, None, r, w), lambda b, j: (b, l, 0, 0))
    per_batch = lambda w: pl.BlockSpec((None, PAST_LEN, w), lambda b, j: (b, 0, 0))
    return pl.pallas_call(
        _na_lat_kernel,
        grid=(DEC_BATCH, steps),
        in_specs=[pl.BlockSpec((tq, NA_W), lambda b, j: (b * steps + j, na_blk)),
                  pl.BlockSpec((DEC_SEQ, NA_W), lambda b, j: (b, na_blk)),
                  pl.BlockSpec((DEC_SEQ, NA_W), lambda b, j: (b, PV_NA // NA_W)),
                  ctx, ctx, _layer_spec(l, (NA_HEADS * NA_PAIR_TILES, LANES)), _layer_spec(l, (N_VEC, D_MODEL)),
                  lay(PAST_LEN, KV_LORA), lay(MLA_ROPE, PAST_LEN), _layer_spec(l, (KV_LORA, W_MLA_P + MLA_W))],
        out_specs=[pl.BlockSpec((tq, NA_W), lambda b, j: (b * steps + j, 0)), per_batch(W_MLA_P), per_batch(MLA_W)],
        out_shape=[jax.ShapeDtypeStruct((n_tok, NA_W), BF16),
                   jax.ShapeDtypeStruct((DEC_BATCH, PAST_LEN, W_MLA_P), BF16),
                   jax.ShapeDtypeStruct((DEC_BATCH, PAST_LEN, MLA_W), BF16)],
        scratch_shapes=[pltpu.VMEM((NA_HEADS * NA_PAIR_TILES, GRID_W, LANES), F32)],
        compiler_params=_params(2),
        name="latent_neighbourhood",
    )(q, k, v, kct, vct, g_rows, vec, ckv, krope_t, w_ukv)


def _latent_attention(l, lam_init, q, k, v, kc, vc, kct, vct, o_na, t):
    n_tok = DEC_BATCH * DEC_SEQ
    nq = DEC_SEQ // TM
    ctx = pl.BlockSpec((None, None, DF_W, PAST_LEN), lambda b, j: (b, l, 0, 0))
    return pl.pallas_call(
        functools.partial(_lat_attn_kernel, lam_init),
        grid=(DEC_BATCH, nq),
        in_specs=[pl.BlockSpec((TM, QK_PACK), lambda b, j: (b * nq + j, 0)),
                  pl.BlockSpec((DEC_SEQ, QK_PACK), lambda b, j: (b, 0)),
                  pl.BlockSpec((DEC_SEQ, V_PACK), lambda b, j: (b, 0)),
                  pl.BlockSpec((None, PAST_LEN, W_MLA_P), lambda b, j: (b, 0, 0)),
                  pl.BlockSpec((None, PAST_LEN, MLA_W), lambda b, j: (b, 0, 0)),
                  ctx, ctx,
                  pl.BlockSpec((TM, NA_W), lambda b, j: (b * nq + j, 0)),
                  _layer_spec(l, (N_VEC, D_MODEL))],
        out_specs=pl.BlockSpec((TM, D_MODEL), lambda b, j: (b * nq + j, 0)),
        out_shape=jax.ShapeDtypeStruct((n_tok, D_MODEL), BF16),
        compiler_params=_params(2),
        name="latent_attention",
    )(q, k, v, kc, vc, kct, vct, o_na, t['vec'])


def kernel(x_prompt, x_sample, cache_mla_ckv, cache_mla_krope, cache_na_k, cache_na_v, cache_df_k, cache_df_v, c, c_ctx, w_mod, b_mod, g_mix, w_in, g_qa, w_uq, g_kva, w_ukv, g_mla_q, g_mla_k, g_na_q, g_na_k, na_rpb, g_df_q, g_df_k, df_lq1, df_lk1, df_lq2, df_lk2, g_df_sub, w_out, g_ffn, w_gate, w_up, w_down):
    p = dict(g_mix=g_mix, w_in=w_in, g_qa=g_qa, w_uq=w_uq, g_kva=g_kva, w_ukv=w_ukv, g_mla_q=g_mla_q, g_mla_k=g_mla_k,
             g_na_q=g_na_q, g_na_k=g_na_k, na_rpb=na_rpb, g_df_q=g_df_q, g_df_k=g_df_k, df_lq1=df_lq1, df_lk1=df_lk1,
             df_lq2=df_lq2, df_lk2=df_lk2, g_df_sub=g_df_sub, g_ffn=g_ffn)
    consts = dict(bd64=_block_diag(NA_HD), bd32=_block_diag(DF_QK), rope=_rope_tables())
    t = _tables(p)

    c_all = jnp.concatenate([c_ctx[None, :], c, jnp.zeros((N_MOD - 1 - DEC_BATCH, D_MODEL), F32)], axis=0)
    mods = _modulation(c_all, w_mod, b_mod)

    krope_t = jnp.swapaxes(cache_mla_krope, -1, -2)
    na_kt, na_vt, df_kt, df_vt = (_feature_major(a) for a in (cache_na_k, cache_na_v, cache_df_k, cache_df_v))

    xp = x_prompt.reshape(BATCH * SEQ, D_MODEL)
    xs = x_sample.reshape(DEC_BATCH * DEC_SEQ, D_MODEL)
    new_caches = ()
    ffn_f32 = (w_out, w_gate, w_up, w_down)
    for l in range(DEPTH):
        lam_init = 0.8 - 0.6 * math.exp(-0.3 * l)
        q, k, v = _latent_front(l, xs, mods, t, consts)
        o_na, kc, vc = _latent_na(l, q, k, v, na_kt, na_vt, t['g_rows'], t['vec'], cache_mla_ckv, krope_t, t['w_ukv'])
        mix_s = _latent_attention(l, lam_init, q, k, v, kc, vc, df_kt, df_vt, o_na, t)
        mix, new_caches, ffn_bf16 = _context_mixer(l, lam_init, xp, mods, t, consts, ffn_f32, new_caches, after=(mix_s,))
        xp, xs = _finish(l, xp, mix, xs, mix_s, mods, t, ffn_bf16)
    ckv_new, *narrow = new_caches
    return (xp.reshape(BATCH, SEQ, D_MODEL), xs.reshape(DEC_BATCH, DEC_SEQ, D_MODEL), ckv_new,
            *(jnp.swapaxes(a, -1, -2) for a in narrow))
```

```python
import functools
import math

import numpy as np
import jax
import jax.numpy as jnp
from jax import lax
from jax.experimental import pallas as pl
from jax.experimental.pallas import tpu as pltpu

F32 = jnp.float32
BF16 = jnp.bfloat16

D_MODEL = 1024
BATCH = 32
SEQ = 256
DEPTH = 2
DEC_BATCH = 2
DEC_SEQ = 1024
PAST_LEN = 256
GRID_W = 64
GRID_ROWS = DEC_SEQ // GRID_W
MLA_HEADS = 6
MLA_NOPE = 64
MLA_ROPE = 32
MLA_QK = MLA_NOPE + MLA_ROPE
MLA_V = 64
MLA_PAD = 128
Q_LORA = 256
KV_LORA = 128
NA_HEADS = 6
NA_HD = 64
NA_KR = 8
NA_KW = 16
DF_HEADS = 4
DF_HD = 64
DF_QK = 32
MLA_W = MLA_HEADS * MLA_V
NA_W = NA_HEADS * NA_HD
DF_W = DF_HEADS * DF_HD
D_FF = -(-8 * D_MODEL // (3 * 256)) * 256
ROPE_BASE = 10000.0
EPS = 1e-6
NEG = -1e30
LOG2E = math.log2(math.e)
MAX_FREE_SOFTMAX_BOUND = 60.0

LANES = 128
MXU_DIM = 256

A_CQ = 0
A_CKV = A_CQ + Q_LORA
A_KR = A_CKV + KV_LORA
A_COLS = A_KR + LANES
B_SRC = Q_LORA + KV_LORA + MLA_ROPE
B_NAQ = 0
B_NAK = B_NAQ + NA_W
B_NAV = B_NAK + NA_W
B_DFQ = B_NAV + NA_W
B_DFK = B_DFQ + DF_W
B_DFV = B_DFK + DF_W
B_COLS = B_DFV + DF_W

W_MLA_P = MLA_HEADS * MLA_PAD
QK_PACK = W_MLA_P + NA_W + DF_W
V_PACK = MLA_W + NA_W + DF_W
P_MLA = 0
P_NA = W_MLA_P
P_DF = P_NA + NA_W
PV_MLA = 0
PV_NA = MLA_W
PV_DF = MLA_W + NA_W

(V_GMIX, V_GFFN, V_GQA, V_GKVA, V_GMQ, V_GMK, V_GNQ, V_GNK, V_GDQ, V_GDK, V_GDS,
 V_LQ1, V_LK1, V_LQ2, V_LK2) = range(15)
N_VEC = 16

N_MOD = 8
MOD_COLS_PER_STEP = 3072
TM = 512
EXACT_ROWS = 128
TM_LAT_FRONT = 1024
LAT_FRONT_CHUNK = 256
TM_FINISH = 512
CTX_SEQS_PER_STEP = 2
NA_ROWS_PER_STEP = 4
VMEM_LIMIT = 56 * 1024 * 1024

NA_PAIR_TILES = 2 * NA_KR - 2
N_CACHE = 6
N_FFN_W = 4


def _dot(a, b):
    return jnp.dot(a, b, preferred_element_type=F32)


def _dot_nt(a, b):
    return lax.dot_general(a, b, (((1,), (1,)), ((), ())), preferred_element_type=F32)


def _lane_iota(shape):
    return lax.broadcasted_iota(jnp.int32, shape, len(shape) - 1)


def _rms_rows(x, g):
    ms = jnp.mean(x * x, axis=-1, keepdims=True)
    return x * lax.rsqrt(ms + EPS) * g


def _tile_rms(x, g, n_real):
    outs = []
    for c0 in range(0, x.shape[1], LANES):
        xt = x[:, c0:c0 + LANES]
        ms = jnp.sum(xt * xt, axis=-1, keepdims=True) * (1.0 / n_real)
        outs.append(xt * lax.rsqrt(ms + EPS) * g[:, c0:c0 + LANES])
    return jnp.concatenate(outs, axis=1)


def _seg_rms(x, bd_ref, g, group):
    width = x.shape[1]
    sq = (x * x).astype(BF16)
    parts = []
    for c0 in range(0, width, MXU_DIM):
        w = min(MXU_DIM, width - c0)
        parts.append(_dot(sq[:, c0:c0 + w], bd_ref[0:w, 0:w]))
    ss = parts[0] if len(parts) == 1 else jnp.concatenate(parts, axis=1)
    return x * lax.rsqrt(ss * (1.0 / group) + EPS) * g


def _rope_tiles(x, cos, sa, sb):
    outs = []
    for t in range(x.shape[1] // LANES):
        xt = x[:, t * LANES:(t + 1) * LANES]
        up = pltpu.roll(xt, LANES - MLA_ROPE // 4, 1)
        dn = pltpu.roll(xt, MLA_ROPE // 4, 1)
        outs.append(xt * cos + up * sa + dn * sb)
    return outs[0] if len(outs) == 1 else jnp.concatenate(outs, axis=1)


def _diff_lambda(vec_ref, lam_init):
    a = jnp.sum(vec_ref[V_LQ1:V_LQ1 + 1, 0:DF_QK] * vec_ref[V_LK1:V_LK1 + 1, 0:DF_QK], axis=-1, keepdims=True)
    b = jnp.sum(vec_ref[V_LQ2:V_LQ2 + 1, 0:DF_QK] * vec_ref[V_LK2:V_LK2 + 1, 0:DF_QK], axis=-1, keepdims=True)
    return jnp.exp(a) - jnp.exp(b) + lam_init


def _mixer_front(x, mod, vec_ref, wa_ref, wb_ref, w_uq_ref, bd64_ref, bd32_ref):
    sh = mod[:, 0:D_MODEL]
    sc = mod[:, D_MODEL:2 * D_MODEL]
    h = (_rms_rows(x, vec_ref[V_GMIX:V_GMIX + 1, :]) * (1.0 + sc) + sh).astype(BF16)
    za = _dot_nt(h, wa_ref[...])
    zb = _dot_nt(h, wb_ref[B_SRC:B_SRC + B_COLS, :])
    cqn = _rms_rows(za[:, A_CQ:A_CQ + Q_LORA], vec_ref[V_GQA:V_GQA + 1, 0:Q_LORA])
    q_raw = _dot_nt(cqn.astype(BF16), w_uq_ref[...])
    q_mla = _tile_rms(q_raw, vec_ref[V_GMQ:V_GMQ + 1, 0:W_MLA_P] * (MLA_QK ** -0.5 * LOG2E), MLA_QK)
    ckv_n = _rms_rows(za[:, A_CKV:A_CKV + KV_LORA], vec_ref[V_GKVA:V_GKVA + 1, 0:KV_LORA])
    kr_tile = za[:, A_KR:A_KR + LANES]
    q_na = _seg_rms(zb[:, B_NAQ:B_NAQ + NA_W], bd64_ref, vec_ref[V_GNQ:V_GNQ + 1, 0:NA_W] * (NA_HD ** -0.5 * LOG2E), NA_HD)
    k_na = _seg_rms(zb[:, B_NAK:B_NAK + NA_W], bd64_ref, vec_ref[V_GNK:V_GNK + 1, 0:NA_W], NA_HD)
    v_na = zb[:, B_NAV:B_NAV + NA_W]
    q_df = _seg_rms(zb[:, B_DFQ:B_DFQ + DF_W], bd32_ref, vec_ref[V_GDQ:V_GDQ + 1, 0:DF_W] * (DF_QK ** -0.5 * LOG2E), DF_QK)
    k_df = _seg_rms(zb[:, B_DFK:B_DFK + DF_W], bd32_ref, vec_ref[V_GDK:V_GDK + 1, 0:DF_W], DF_QK)
    v_df = zb[:, B_DFV:B_DFV + DF_W]
    return q_mla, ckv_n, kr_tile, q_na, k_na, v_na, q_df, k_df, v_df


def _mla_kv(ckv_n, kr_tile, vec_ref, w_ukv_ref):
    kv = _dot(ckv_n.astype(BF16), w_ukv_ref[...])
    lane = _lane_iota((1, LANES))
    kr = jnp.where((lane >= MLA_NOPE) & (lane < MLA_QK), kr_tile, 0.0)
    k_pre = kv[:, 0:W_MLA_P] + jnp.concatenate([kr] * MLA_HEADS, axis=1)
    k = _tile_rms(k_pre, vec_ref[V_GMK:V_GMK + 1, 0:W_MLA_P], MLA_QK)
    return k, kv[:, W_MLA_P:W_MLA_P + MLA_W]


def _softmax_parts(s, bound=None):
    m = jnp.max(s, axis=-1, keepdims=True) if bound is None else bound
    p = jnp.exp2(s - m)
    return p, 1.0 / jnp.sum(p, axis=-1, keepdims=True)


def _max_sq_norm(kt, group):
    sq = kt * kt
    best = None
    for r0 in range(0, kt.shape[0], group):
        n2 = jnp.sum(sq[r0:r0 + group], axis=0, keepdims=True)
        best = n2 if best is None else jnp.maximum(best, n2)
    return jnp.max(best, axis=-1, keepdims=True)


def _score_bounds(vec_ref, bias_max=None, na_key_sq=None, df_key_sq=None):
    gmax = lambda row, w: jnp.max(jnp.abs(vec_ref[row:row + 1, 0:w]), axis=-1, keepdims=True)
    slack = LOG2E * (1.0 + 2.0 ** -6)

    def key_norm(row, w, d, measured_sq):
        k = gmax(row, w) * d ** 0.5
        return k if measured_sq is None else jnp.maximum(k, jnp.sqrt(measured_sq))

    b_mla = gmax(V_GMQ, W_MLA_P) * key_norm(V_GMK, W_MLA_P, MLA_QK, None) * slack
    b_na = gmax(V_GNQ, NA_W) * key_norm(V_GNK, NA_W, NA_HD, na_key_sq) * slack
    if bias_max is not None:
        b_na = b_na + bias_max * LOG2E
    b_df = gmax(V_GDQ, DF_W) * key_norm(V_GDK, DF_W, DF_QK, df_key_sq) * slack
    worst = jnp.maximum(b_mla, jnp.maximum(b_na, b_df))
    return (b_mla, b_na, b_df), worst[0, 0] <= MAX_FREE_SOFTMAX_BOUND


EXACT_MAX = (None, None, None)


def _with_score_bounds(vec_ref, bounded, exact, **measured):
    bounds, ok = _score_bounds(vec_ref, **measured)
    pl.when(ok)(lambda: bounded(bounds))
    pl.when(jnp.logical_not(ok))(exact)


def _row_block(i, n):
    return slice(i * n, (i + 1) * n) if isinstance(i, int) else pl.ds(pl.multiple_of(i * n, n), n)


def _rolled(n, body):
    def step(i, carry):
        body(i)
        return carry

    lax.fori_loop(0, n, step, 0)


def _scores(q, k_segs):
    parts = [_dot(q, k) if feature_major else _dot_nt(q, k) for k, feature_major in k_segs]
    return parts[0] if len(parts) == 1 else jnp.concatenate(parts, axis=1)


def _pv(p, v_segs):
    out = None
    c0 = 0
    for v, feature_major in v_segs:
        n = v.shape[1] if feature_major else v.shape[0]
        o = _dot_nt(p[:, c0:c0 + n], v) if feature_major else _dot(p[:, c0:c0 + n], v)
        out = o if out is None else out + o
        c0 += n
    return out


def _seg_tile(seg, t):
    a, feature_major = seg
    return (a[t * LANES:(t + 1) * LANES, :] if feature_major else a[:, t * LANES:(t + 1) * LANES]), feature_major


def _lane_groups(qt, width):
    lane = _lane_iota((1, LANES))
    zero = jnp.zeros_like(qt)
    return jnp.concatenate(
        [jnp.where((lane >= g * width) & (lane < (g + 1) * width), qt, zero) for g in range(LANES // width)], axis=0)


def _pair_select(o2):
    tq = o2.shape[0] // 2
    return jnp.where(_lane_iota((1, LANES)) < NA_HD, o2[0:tq], o2[tq:2 * tq])


def _mla_attend(q, k_segs, v_segs, bound=None):
    outs = []
    for t in range(MLA_HEADS // 2):
        vt = [_seg_tile(v, t) for v in v_segs]
        halves = []
        for h in (2 * t, 2 * t + 1):
            p, il = _softmax_parts(_scores(q[:, h * MLA_PAD:(h + 1) * MLA_PAD], [_seg_tile(k, h) for k in k_segs]),
                                   bound)
            halves.append(_pv(p.astype(BF16), vt) * il)
        outs.append(jnp.where(_lane_iota((1, LANES)) < MLA_V, halves[0], halves[1]))
    return jnp.concatenate(outs, axis=1)


def _na_tile_attend(qt, k_segs, v_segs, bias=None, bound=None):
    s = _scores(_lane_groups(qt, NA_HD), k_segs)
    if bias is not None:
        nb = bias.shape[1]
        s = jnp.concatenate([s[:, 0:nb] + bias, s[:, nb:]], axis=1)
    p, il = _softmax_parts(s, bound)
    return _pair_select(_pv(p.astype(BF16), v_segs) * il)


def _na_attend_full(q, k_segs, v_segs, bound=None):
    return jnp.concatenate(
        [_na_tile_attend(q[:, t * LANES:(t + 1) * LANES], [_seg_tile(k, t) for k in k_segs],
                         [_seg_tile(v, t) for v in v_segs], bound=bound) for t in range(NA_HEADS // 2)], axis=1)


def _df_attend(q, k_segs, v_segs, lam, g_sub, out_scale, bound=None):
    outs = []
    lane = _lane_iota((1, LANES))
    tq = q.shape[0]
    for t in range(DF_HEADS // 2):
        kt = [_seg_tile(k, t) for k in k_segs]
        vt = [_seg_tile(v, t) for v in v_segs]
        p, il = _softmax_parts(_scores(_lane_groups(q[:, t * LANES:(t + 1) * LANES], DF_QK), kt), bound)
        pn = []
        for hh in range(2):
            r1, r2 = 2 * hh * tq, (2 * hh + 1) * tq
            pn.append((p[r1:r1 + tq] * il[r1:r1 + tq] - p[r2:r2 + tq] * (lam * il[r2:r2 + tq])).astype(BF16))
        o = _pair_select(_pv(jnp.concatenate(pn, axis=0), vt))
        o2 = o * o
        ms_e = jnp.sum(jnp.where(lane < DF_HD, o2, 0.0), axis=-1, keepdims=True)
        ms_o = jnp.sum(jnp.where(lane >= DF_HD, o2, 0.0), axis=-1, keepdims=True)
        r = lax.rsqrt(jnp.where(lane < DF_HD, ms_e, ms_o) * (1.0 / DF_HD) + EPS)
        outs.append(o * r * (g_sub[:, t * LANES:(t + 1) * LANES] * out_scale))
    return jnp.concatenate(outs, axis=1)


def _mod_kernel(c_ref, w_ref, b_ref, o_ref):
    c = c_ref[...]
    s = c * jax.nn.sigmoid(c)
    o_ref[...] = _dot(s.astype(BF16), w_ref[...].astype(BF16)) + b_ref[pl.ds(pl.program_id(0), 1), :]


def _ctx_kernel(lam_init, n_alias, n_after, x_ref, mod_ref, vec_ref, wa_ref, wb_ref, w_uq_ref, w_ukv_ref, bd64_ref,
                bd32_ref, *rest):
    ffn_f32, rest = rest[:N_FFN_W], rest[N_FFN_W + n_alias + n_after:]
    mix_ref, *cache_refs = rest[:1 + N_CACHE]
    for src, dst in zip(ffn_f32, rest[1 + N_CACHE:]):
        dst[...] = src[...].astype(BF16)
    if n_alias == 0:
        for ref in cache_refs:
            ref[:, 1:] = jnp.zeros((ref.shape[0], ref.shape[1] - 1) + ref.shape[2:], F32)
        cache_refs = [ref.at[:, 0] for ref in cache_refs]
    ckv_ref, kr_ref, nak_ref, nav_ref, dfk_ref, dfv_ref = cache_refs
    lam = _diff_lambda(vec_ref, lam_init)
    mod = mod_ref[0:1, :]
    bf = lambda a: a.astype(BF16)
    seg = lambda a: [(a.astype(BF16), False)]
    def sequence(s, bounds):
        b_mla, b_na, b_df = bounds
        rows = _row_block(s, SEQ)
        q_mla, ckv_n, kr_tile, q_na, k_na, v_na, q_df, k_df, v_df = _mixer_front(
            x_ref[rows, :], mod, vec_ref, wa_ref, wb_ref, w_uq_ref, bd64_ref, bd32_ref)
        k_mla, v_mla = _mla_kv(ckv_n, kr_tile, vec_ref, w_ukv_ref)
        ckv_ref[s] = ckv_n
        kr_ref[s] = kr_tile.T[0:MLA_ROPE]
        for ref, a in ((nak_ref, k_na), (nav_ref, v_na), (dfk_ref, k_df), (dfv_ref, v_df)):
            at = a.T
            for h in range(ref.shape[1]):
                ref[s, h] = at[h * NA_HD:(h + 1) * NA_HD]
        o_mla = _mla_attend(bf(q_mla), seg(k_mla), seg(v_mla), b_mla)
        o_na = _na_attend_full(bf(q_na), seg(k_na), seg(v_na), b_na)
        o_df = _df_attend(bf(q_df), seg(k_df), seg(v_df), lam, vec_ref[V_GDS:V_GDS + 1, 0:DF_W], 1.0 - lam_init, b_df)
        mix_ref[rows, :] = jnp.concatenate([o_mla, o_na, o_df], axis=1).astype(BF16)

    def bounded(bounds):
        for s in range(CTX_SEQS_PER_STEP):
            sequence(s, bounds)

    _with_score_bounds(vec_ref, bounded, lambda: _rolled(CTX_SEQS_PER_STEP, lambda s: sequence(s, EXACT_MAX)))


def _finish_kernel(n_ctx_blocks, xc_ref, mixc_ref, xl_ref, mixl_ref, mod_ref, vec_ref, w_out_ref, w_gate_ref, w_up_ref,
                   w_down_ref, yc_ref, yl_ref):
    def block(x_ref, mix_ref, y_ref, mod_row):
        mod = mod_ref[pl.ds(mod_row, 1), :]
        gate_m = mod[:, 2 * D_MODEL:3 * D_MODEL]
        sh = mod[:, 3 * D_MODEL:4 * D_MODEL]
        sc = mod[:, 4 * D_MODEL:5 * D_MODEL]
        gate_f = mod[:, 5 * D_MODEL:6 * D_MODEL]
        x1 = x_ref[...] + gate_m * _dot(mix_ref[...], w_out_ref[...])
        h = (_rms_rows(x1, vec_ref[V_GFFN:V_GFFN + 1, :]) * (1.0 + sc) + sh).astype(BF16)
        g = _dot(h, w_gate_ref[...])
        u = _dot(h, w_up_ref[...])
        a = (g * jax.nn.sigmoid(g) * u).astype(BF16)
        y_ref[...] = x1 + gate_f * _dot(a, w_down_ref[...])

    is_ctx = pl.program_id(0) < n_ctx_blocks
    lat_row = 1 + jnp.maximum(pl.program_id(0) - n_ctx_blocks, 0) // (DEC_SEQ // TM_FINISH)
    pl.when(is_ctx)(lambda: block(xc_ref, mixc_ref, yc_ref, 0))
    pl.when(jnp.logical_not(is_ctx))(lambda: block(xl_ref, mixl_ref, yl_ref, lat_row))


def _lat_front_kernel(x_ref, mod_ref, vec_ref, rope_ref, wa_ref, wb_ref, w_uq_ref, w_ukv_ref, bd64_ref, bd32_ref,
                      q_ref, k_ref, v_ref):
    mod = mod_ref[pl.ds(1 + pl.program_id(0) // (DEC_SEQ // TM_LAT_FRONT), 1), :]
    for c in range(TM_LAT_FRONT // LAT_FRONT_CHUNK):
        rows = slice(c * LAT_FRONT_CHUNK, (c + 1) * LAT_FRONT_CHUNK)
        q_mla, ckv_n, kr_tile, q_na, k_na, v_na, q_df, k_df, v_df = _mixer_front(
            x_ref[rows, :], mod, vec_ref, wa_ref, wb_ref, w_uq_ref, bd64_ref, bd32_ref)
        k_mla, v_mla = _mla_kv(ckv_n, kr_tile, vec_ref, w_ukv_ref)
        cm, sam, sbm = rope_ref[0, rows], rope_ref[1, rows], rope_ref[2, rows]
        cd, sad, sbd = rope_ref[3, rows], rope_ref[4, rows], rope_ref[5, rows]
        q_ref[rows, :] = jnp.concatenate(
            [_rope_tiles(q_mla, cm, sam, sbm), q_na, _rope_tiles(q_df, cd, sad, sbd)], axis=1).astype(BF16)
        k_ref[rows, :] = jnp.concatenate(
            [_rope_tiles(k_mla, cm, sam, sbm), k_na, _rope_tiles(k_df, cd, sad, sbd)], axis=1).astype(BF16)
        v_ref[rows, :] = jnp.concatenate([v_mla, v_na, v_df], axis=1).astype(BF16)


def _na_lat_kernel(q_ref, kl_ref, vl_ref, kct_ref, vct_ref, g_ref, vec_ref, ckv_ref, krt_ref, w_ukv_ref,
                   o_ref, kc_mla_ref, vc_mla_ref, bias_ref):
    b = pl.program_id(0)
    j = pl.program_id(1)

    @pl.when(j == 0)
    def _context_mla_kv():
        krt = jnp.concatenate([jnp.zeros((MLA_NOPE, PAST_LEN), F32), krt_ref[...],
                               jnp.zeros((LANES - MLA_QK, PAST_LEN), F32)], axis=0)
        k_mla, v_mla = _mla_kv(ckv_ref[...], krt.T, vec_ref, w_ukv_ref)
        kc_mla_ref[...] = k_mla.astype(BF16)
        vc_mla_ref[...] = v_mla.astype(BF16)

    @pl.when((b == 0) & (j == 0))
    def _build_bias():
        c = lax.broadcasted_iota(jnp.int32, (GRID_W, LANES), 0)
        kc = _lane_iota((GRID_W, LANES)) % GRID_W
        start = jnp.clip(c - NA_KW // 2, 0, GRID_W - NA_KW)
        in_win = (kc >= start) & (kc < start + NA_KW)

        def body(i, carry):
            row = jnp.broadcast_to(g_ref[pl.ds(i, 1), :], (GRID_W, LANES))
            toep = pltpu.roll(row, 0, 1, stride=1, stride_axis=0)
            bias_ref[i] = jnp.where(in_win, toep * LOG2E, NEG)
            return carry

        lax.fori_loop(0, NA_HEADS * NA_PAIR_TILES, body, 0, unroll=NA_HEADS)

    n_win = NA_KR * GRID_W
    kct_f32 = kct_ref[...]
    kct = kct_f32.astype(BF16)
    vct = vct_ref[...].astype(BF16)

    def win_start(r):
        return jnp.clip(r - NA_KR // 2, 0, GRID_ROWS - NA_KR)

    def attend(grid_rows, rs, bound):
        a0, n = grid_rows[0], len(grid_rows)
        row0 = pl.multiple_of(rs * GRID_W, GRID_W)
        outs = []
        for t in range(NA_HEADS // 2):
            lanes = slice(t * LANES, (t + 1) * LANES)
            bias = jnp.concatenate(
                [jnp.concatenate([bias_ref[h * NA_PAIR_TILES + (rs - (j * NA_ROWS_PER_STEP + a) + NA_KR - 1) + 2 * m]
                                  for m in range(NA_KR // 2)], axis=1)
                 for h in (2 * t, 2 * t + 1) for a in grid_rows], axis=0)
            outs.append(_na_tile_attend(
                q_ref[_row_block(a0, GRID_W) if n == 1 else slice(a0 * GRID_W, (a0 + n) * GRID_W), lanes],
                [(kl_ref[pl.ds(row0, n_win), lanes], False), (kct[lanes, :], True)],
                [(vl_ref[pl.ds(row0, n_win), lanes], False), (vct[lanes, :], True)], bias, bound))
        return jnp.concatenate(outs, axis=1).astype(BF16)

    first_r, last_r = j * NA_ROWS_PER_STEP, (j + 1) * NA_ROWS_PER_STEP - 1
    shared = win_start(first_r) == win_start(last_r)

    def attend_step(bounds):
        bound = bounds[1]

        @pl.when(shared)
        def _shared_window():
            o_ref[...] = attend(tuple(range(NA_ROWS_PER_STEP)), win_start(first_r), bound)

        @pl.when(jnp.logical_not(shared))
        def _per_row_windows():
            for a in range(NA_ROWS_PER_STEP):
                o_ref[a * GRID_W:(a + 1) * GRID_W, :] = attend((a,), win_start(first_r + a), bound)

    def exact_row(a):
        o_ref[_row_block(a, GRID_W), :] = attend((a,), win_start(first_r + a), None)

    bias_max = jnp.max(jnp.max(jnp.abs(g_ref[...]), axis=-1, keepdims=True), axis=0, keepdims=True)
    _with_score_bounds(vec_ref, attend_step, lambda: _rolled(NA_ROWS_PER_STEP, exact_row),
                       bias_max=bias_max, na_key_sq=_max_sq_norm(kct_f32, NA_HD))


def _lat_attn_kernel(lam_init, q_ref, kl_ref, vl_ref, kc_ref, vc_ref, kct_ref, vct_ref, ona_ref, vec_ref, mix_ref):
    lam = _diff_lambda(vec_ref, lam_init)
    kct = kct_ref[...]

    def attend(bounds, rows):
        b_mla, _, b_df = bounds
        o_mla = _mla_attend(q_ref[rows, P_MLA:P_MLA + W_MLA_P],
                            [(kc_ref[...], False), (kl_ref[:, P_MLA:P_MLA + W_MLA_P], False)],
                            [(vc_ref[...], False), (vl_ref[:, PV_MLA:PV_MLA + MLA_W], False)], b_mla)
        o_df = _df_attend(q_ref[rows, P_DF:P_DF + DF_W],
                          [(kct.astype(BF16), True), (kl_ref[:, P_DF:P_DF + DF_W], False)],
                          [(vct_ref[...].astype(BF16), True), (vl_ref[:, PV_DF:PV_DF + DF_W], False)],
                          lam, vec_ref[V_GDS:V_GDS + 1, 0:DF_W], 1.0 - lam_init, b_df)
        mix_ref[rows, :] = jnp.concatenate([o_mla.astype(BF16), ona_ref[rows, :], o_df.astype(BF16)], axis=1)

    def exact():
        _rolled(TM // EXACT_ROWS, lambda i: attend(EXACT_MAX, _row_block(i, EXACT_ROWS)))

    _with_score_bounds(vec_ref, lambda bounds: attend(bounds, slice(None)), exact,
                       df_key_sq=_max_sq_norm(kct, DF_QK))


def _const_spec(shape):
    nd = len(shape)
    return pl.BlockSpec(shape, lambda *_: (0,) * nd, pipeline_mode=pl.Buffered(1))


def _layer_spec(l, shape):
    nd = len(shape)
    return pl.BlockSpec((None,) + tuple(shape), lambda *_: (l,) + (0,) * nd, pipeline_mode=pl.Buffered(1))


def _params(n_axes, independent=False):
    semantics = ("parallel" if independent else "arbitrary",) * n_axes
    return pltpu.CompilerParams(dimension_semantics=semantics, vmem_limit_bytes=VMEM_LIMIT)


def _block_diag(group):
    i = np.arange(MXU_DIM) // group
    return jnp.asarray((i[:, None] == i[None, :]).astype(np.float32), dtype=BF16)


def _rope_tables():
    t = np.arange(DEC_SEQ)
    row = (t // GRID_W).astype(np.float64)
    col = (t % GRID_W).astype(np.float64)
    n = MLA_ROPE // 4
    inv = 1.0 / (ROPE_BASE ** (np.arange(n, dtype=np.float64) * 2.0 / (MLA_ROPE // 2)))
    ar = row[:, None] * inv
    ac = col[:, None] * inv
    ang = np.concatenate([ar, ar, ac, ac], axis=-1)
    cos32, sin32 = np.cos(ang), np.sin(ang)
    first = (np.arange(MLA_ROPE) % (2 * n)) < n
    sa32 = np.where(first, -sin32, 0.0)
    sb32 = np.where(first, 0.0, sin32)

    def mla_tile(v32, fill):
        out = np.full((DEC_SEQ, LANES), fill)
        out[:, MLA_NOPE:MLA_QK] = v32
        return out

    tabs = [mla_tile(cos32, 1.0), mla_tile(sa32, 0.0), mla_tile(sb32, 0.0),
            np.tile(cos32, (1, LANES // DF_QK)), np.tile(sa32, (1, LANES // DF_QK)), np.tile(sb32, (1, LANES // DF_QK))]
    return jnp.asarray(np.stack(tabs).astype(np.float32))


def _feature_major(a):
    a = jnp.swapaxes(a, -1, -2)
    return a.reshape(a.shape[:-3] + (a.shape[-3] * a.shape[-2], a.shape[-1]))


def _tables(p):
    wb = jnp.swapaxes(p['w_in'], 1, 2).astype(BF16)
    kr = wb[:, Q_LORA + KV_LORA:B_SRC]
    z32 = jnp.zeros_like(kr)
    wa = jnp.concatenate([wb[:, :Q_LORA + KV_LORA], kr, z32, kr, z32], axis=1)
    w_uq_t = jnp.swapaxes(p['w_uq'], 1, 2).reshape(DEPTH, MLA_HEADS, MLA_QK, Q_LORA)
    w_uq_p = jnp.pad(w_uq_t, ((0, 0), (0, 0), (0, MLA_PAD - MLA_QK), (0, 0))).reshape(DEPTH, W_MLA_P, Q_LORA).astype(BF16)
    w_ukv = p['w_ukv'].reshape(DEPTH, KV_LORA, MLA_HEADS, MLA_NOPE + MLA_V)
    wk = jnp.pad(w_ukv[..., :MLA_NOPE], ((0, 0), (0, 0), (0, 0), (0, MLA_PAD - MLA_NOPE))).reshape(DEPTH, KV_LORA, W_MLA_P)
    wv = w_ukv[..., MLA_NOPE:].reshape(DEPTH, KV_LORA, MLA_W)
    w_ukv_r = jnp.concatenate([wk, wv], axis=2).astype(BF16)

    def row(v, reps=1):
        v = jnp.tile(v, (1, reps)) if reps > 1 else v
        return [v, jnp.zeros((DEPTH, D_MODEL - v.shape[1]), F32)] if v.shape[1] < D_MODEL else [v]

    pad_head = lambda g: jnp.pad(g, ((0, 0), (0, MLA_PAD - MLA_QK)))
    pieces = (row(p['g_mix']) + row(p['g_ffn']) + row(p['g_qa']) + row(p['g_kva'])
              + row(pad_head(p['g_mla_q']), MLA_HEADS) + row(pad_head(p['g_mla_k']), MLA_HEADS)
              + row(p['g_na_q'], NA_HEADS) + row(p['g_na_k'], NA_HEADS)
              + row(p['g_df_q'], 2 * DF_HEADS) + row(p['g_df_k'], 2 * DF_HEADS) + row(p['g_df_sub'], DF_HEADS)
              + row(p['df_lq1']) + row(p['df_lk1']) + row(p['df_lq2']) + row(p['df_lk2'])
              + [jnp.zeros((DEPTH, D_MODEL), F32)])
    vec = jnp.concatenate(pieces, axis=1).reshape(DEPTH, N_VEC, D_MODEL)
    f = p['na_rpb']
    n_rel = 2 * NA_KW - 1
    zpad = jnp.zeros((DEPTH, NA_HEADS, NA_PAIR_TILES, (LANES - 2 * n_rel) // 2), F32)
    g_rows = jnp.concatenate([f[:, :, :-1, NA_KW - 1:], zpad, f[:, :, 1:, :], zpad, f[:, :, :-1, :NA_KW - 1]], axis=-1)
    g_rows = g_rows.reshape(DEPTH, NA_HEADS * NA_PAIR_TILES, LANES)
    return dict(wa=wa, wb=wb, w_uq=w_uq_p, w_ukv=w_ukv_r, vec=vec, g_rows=g_rows)


def _modulation(c_all, w_mod, b_mod):
    tn = MOD_COLS_PER_STEP
    return pl.pallas_call(
        _mod_kernel,
        grid=(DEPTH, 6 * D_MODEL // tn),
        in_specs=[pl.BlockSpec((N_MOD, D_MODEL), lambda l, j: (0, 0)),
                  pl.BlockSpec((None, D_MODEL, tn), lambda l, j: (l, 0, j)),
                  pl.BlockSpec((DEPTH, tn), lambda l, j: (0, j))],
        out_specs=pl.BlockSpec((None, N_MOD, tn), lambda l, j: (l, 0, j)),
        out_shape=jax.ShapeDtypeStruct((DEPTH, N_MOD, 6 * D_MODEL), F32),
        compiler_params=_params(2),
        name="modulation",
    )(c_all, w_mod, b_mod)


def _mod_spec(l):
    return pl.BlockSpec((None, N_MOD, 6 * D_MODEL), lambda *_: (l, 0, 0))


def _front_weight_specs(l):
    return [_layer_spec(l, (N_VEC, D_MODEL)), _layer_spec(l, (A_COLS, D_MODEL)), _layer_spec(l, (B_SRC + B_COLS, D_MODEL)),
            _layer_spec(l, (W_MLA_P, Q_LORA)), _layer_spec(l, (KV_LORA, W_MLA_P + MLA_W)),
            _const_spec((MXU_DIM, MXU_DIM)), _const_spec((MXU_DIM, MXU_DIM))]


def _front_weights(t, consts):
    return (t['vec'], t['wa'], t['wb'], t['w_uq'], t['w_ukv'], consts['bd64'], consts['bd32'])


def _context_mixer(l, lam_init, x, mods, t, consts, ffn_f32, prev_caches, after=()):
    n_tok = BATCH * SEQ
    n_alias = len(prev_caches)
    tok = lambda w: pl.BlockSpec((CTX_SEQS_PER_STEP * SEQ, w), lambda b: (b, 0))
    if n_alias == 0:
        assert l == 0
        lay = lambda *s: pl.BlockSpec((CTX_SEQS_PER_STEP, DEPTH) + s, lambda b: (b, 0) + (0,) * len(s))
    else:
        lay = lambda *s: pl.BlockSpec((CTX_SEQS_PER_STEP, None) + s, lambda b: (b, l) + (0,) * len(s))
    cache_shapes = [(SEQ, KV_LORA), (MLA_ROPE, SEQ), (NA_HEADS, NA_HD, SEQ), (NA_HEADS, NA_HD, SEQ),
                    (DF_HEADS, DF_HD, SEQ), (DF_HEADS, DF_HD, SEQ)]
    weights = _front_weights(t, consts)
    steps = BATCH // CTX_SEQS_PER_STEP
    ffn_chunks = [(w.shape[1] // steps, w.shape[2]) for w in ffn_f32]
    n_in = 2 + len(weights) + len(ffn_f32)
    mix, *outs = pl.pallas_call(
        functools.partial(_ctx_kernel, lam_init, n_alias, len(after)),
        grid=(steps,),
        in_specs=[tok(D_MODEL), _mod_spec(l)]
        + _front_weight_specs(l) + [pl.BlockSpec((None,) + c, lambda b: (l, b, 0)) for c in ffn_chunks]
        + [pl.BlockSpec(memory_space=pl.ANY)] * (n_alias + len(after)),
        out_specs=[tok(D_MODEL)] + [lay(*s) for s in cache_shapes] + [pl.BlockSpec(c, lambda b: (b, 0)) for c in ffn_chunks],
        out_shape=[jax.ShapeDtypeStruct((n_tok, D_MODEL), BF16)]
        + [jax.ShapeDtypeStruct((BATCH, DEPTH) + s, F32) for s in cache_shapes]
        + [jax.ShapeDtypeStruct(w.shape[1:], BF16) for w in ffn_f32],
        input_output_aliases={n_in + i: 1 + i for i in range(n_alias)},
        compiler_params=_params(1, independent=True),
        name="context_mixer",
    )(x, mods, *weights, *ffn_f32, *prev_caches, *after)
    return mix, outs[:N_CACHE], outs[N_CACHE:]


def _finish(l, xc, mixc, xl, mixl, mods, t, ffn_bf16):
    nc, nl = xc.shape[0] // TM_FINISH, xl.shape[0] // TM_FINISH
    ctx_tok = pl.BlockSpec((TM_FINISH, D_MODEL), lambda i: (jnp.minimum(i, nc - 1), 0))
    lat_tok = pl.BlockSpec((TM_FINISH, D_MODEL), lambda i: (jnp.maximum(i - nc, 0), 0))
    return pl.pallas_call(
        functools.partial(_finish_kernel, nc),
        grid=(nc + nl,),
        in_specs=[ctx_tok, ctx_tok, lat_tok, lat_tok,
                  _mod_spec(l),
                  _layer_spec(l, (N_VEC, D_MODEL))] + [_const_spec(w.shape) for w in ffn_bf16],
        out_specs=[ctx_tok, lat_tok],
        out_shape=[jax.ShapeDtypeStruct(xc.shape, F32), jax.ShapeDtypeStruct(xl.shape, F32)],
        compiler_params=_params(1),
        name="finish",
    )(xc, mixc, xl, mixl, mods, t['vec'], *ffn_bf16)


def _latent_front(l, x, mods, t, consts):
    n_tok = DEC_BATCH * DEC_SEQ
    tm = TM_LAT_FRONT
    blocks_per_seq = DEC_SEQ // tm
    tok = lambda w: pl.BlockSpec((tm, w), lambda i: (i, 0))
    wspecs = _front_weight_specs(l)
    weights = _front_weights(t, consts)
    return pl.pallas_call(
        _lat_front_kernel,
        grid=(n_tok // tm,),
        in_specs=[tok(D_MODEL),
                  _mod_spec(l),
                  wspecs[0], pl.BlockSpec((6, tm, LANES), lambda i: (0, i % blocks_per_seq, 0))] + wspecs[1:],
        out_specs=[tok(QK_PACK), tok(QK_PACK), tok(V_PACK)],
        out_shape=[jax.ShapeDtypeStruct((n_tok, QK_PACK), BF16), jax.ShapeDtypeStruct((n_tok, QK_PACK), BF16),
                   jax.ShapeDtypeStruct((n_tok, V_PACK), BF16)],
        compiler_params=_params(1, independent=True),
        name="latent_front",
    )(x, mods, weights[0], consts['rope'], *weights[1:])


def _latent_na(l, q, k, v, kct, vct, g_rows, vec, ckv, krope_t, w_ukv):
    n_tok = DEC_BATCH * DEC_SEQ
    na_blk = P_NA // NA_W
    steps = GRID_ROWS // NA_ROWS_PER_STEP
    tq = NA_ROWS_PER_STEP * GRID_W
    ctx = pl.BlockSpec((None, None, NA_W, PAST_LEN), lambda b, j: (b, l, 0, 0))
    lay = lambda r, w: pl.BlockSpec((None, None, r, w), lambda b, j: (b, l, 0, 0))
    per_batch = lambda w: pl.BlockSpec((None, PAST_LEN, w), lambda b, j: (b, 0, 0))
    return pl.pallas_call(
        _na_lat_kernel,
        grid=(DEC_BATCH, steps),
        in_specs=[pl.BlockSpec((tq, NA_W), lambda b, j: (b * steps + j, na_blk)),
                  pl.BlockSpec((DEC_SEQ, NA_W), lambda b, j: (b, na_blk)),
                  pl.BlockSpec((DEC_SEQ, NA_W), lambda b, j: (b, PV_NA // NA_W)),
                  ctx, ctx, _layer_spec(l, (NA_HEADS * NA_PAIR_TILES, LANES)), _layer_spec(l, (N_VEC, D_MODEL)),
                  lay(PAST_LEN, KV_LORA), lay(MLA_ROPE, PAST_LEN), _layer_spec(l, (KV_LORA, W_MLA_P + MLA_W))],
        out_specs=[pl.BlockSpec((tq, NA_W), lambda b, j: (b * steps + j, 0)), per_batch(W_MLA_P), per_batch(MLA_W)],
        out_shape=[jax.ShapeDtypeStruct((n_tok, NA_W), BF16),
                   jax.ShapeDtypeStruct((DEC_BATCH, PAST_LEN, W_MLA_P), BF16),
                   jax.ShapeDtypeStruct((DEC_BATCH, PAST_LEN, MLA_W), BF16)],
        scratch_shapes=[pltpu.VMEM((NA_HEADS * NA_PAIR_TILES, GRID_W, LANES), F32)],
        compiler_params=_params(2),
        name="latent_neighbourhood",
    )(q, k, v, kct, vct, g_rows, vec, ckv, krope_t, w_ukv)


def _latent_attention(l, lam_init, q, k, v, kc, vc, kct, vct, o_na, t):
    n_tok = DEC_BATCH * DEC_SEQ
    nq = DEC_SEQ // TM
    ctx = pl.BlockSpec((None, None, DF_W, PAST_LEN), lambda b, j: (b, l, 0, 0))
    return pl.pallas_call(
        functools.partial(_lat_attn_kernel, lam_init),
        grid=(DEC_BATCH, nq),
        in_specs=[pl.BlockSpec((TM, QK_PACK), lambda b, j: (b * nq + j, 0)),
                  pl.BlockSpec((DEC_SEQ, QK_PACK), lambda b, j: (b, 0)),
                  pl.BlockSpec((DEC_SEQ, V_PACK), lambda b, j: (b, 0)),
                  pl.BlockSpec((None, PAST_LEN, W_MLA_P), lambda b, j: (b, 0, 0)),
                  pl.BlockSpec((None, PAST_LEN, MLA_W), lambda b, j: (b, 0, 0)),
                  ctx, ctx,
                  pl.BlockSpec((TM, NA_W), lambda b, j: (b * nq + j, 0)),
                  _layer_spec(l, (N_VEC, D_MODEL))],
        out_specs=pl.BlockSpec((TM, D_MODEL), lambda b, j: (b * nq + j, 0)),
        out_shape=jax.ShapeDtypeStruct((n_tok, D_MODEL), BF16),
        compiler_params=_params(2, independent=True),
        name="latent_attention",
    )(q, k, v, kc, vc, kct, vct, o_na, t['vec'])


def kernel(x_prompt, x_sample, cache_mla_ckv, cache_mla_krope, cache_na_k, cache_na_v, cache_df_k, cache_df_v, c, c_ctx, w_mod, b_mod, g_mix, w_in, g_qa, w_uq, g_kva, w_ukv, g_mla_q, g_mla_k, g_na_q, g_na_k, na_rpb, g_df_q, g_df_k, df_lq1, df_lk1, df_lq2, df_lk2, g_df_sub, w_out, g_ffn, w_gate, w_up, w_down):
    p = dict(g_mix=g_mix, w_in=w_in, g_qa=g_qa, w_uq=w_uq, g_kva=g_kva, w_ukv=w_ukv, g_mla_q=g_mla_q, g_mla_k=g_mla_k,
             g_na_q=g_na_q, g_na_k=g_na_k, na_rpb=na_rpb, g_df_q=g_df_q, g_df_k=g_df_k, df_lq1=df_lq1, df_lk1=df_lk1,
             df_lq2=df_lq2, df_lk2=df_lk2, g_df_sub=g_df_sub, g_ffn=g_ffn)
    consts = dict(bd64=_block_diag(NA_HD), bd32=_block_diag(DF_QK), rope=_rope_tables())
    t = _tables(p)

    c_all = jnp.concatenate([c_ctx[None, :], c, jnp.zeros((N_MOD - 1 - DEC_BATCH, D_MODEL), F32)], axis=0)
    mods = _modulation(c_all, w_mod, b_mod)

    krope_t = jnp.swapaxes(cache_mla_krope, -1, -2)
    na_kt, na_vt, df_kt, df_vt = (_feature_major(a) for a in (cache_na_k, cache_na_v, cache_df_k, cache_df_v))

    xp = x_prompt.reshape(BATCH * SEQ, D_MODEL)
    xs = x_sample.reshape(DEC_BATCH * DEC_SEQ, D_MODEL)
    new_caches = ()
    ffn_f32 = (w_out, w_gate, w_up, w_down)
    for l in range(DEPTH):
        lam_init = 0.8 - 0.6 * math.exp(-0.3 * l)
        q, k, v = _latent_front(l, xs, mods, t, consts)
        o_na, kc, vc = _latent_na(l, q, k, v, na_kt, na_vt, t['g_rows'], t['vec'], cache_mla_ckv, krope_t, t['w_ukv'])
        mix_s = _latent_attention(l, lam_init, q, k, v, kc, vc, df_kt, df_vt, o_na, t)
        mix, new_caches, ffn_bf16 = _context_mixer(l, lam_init, xp, mods, t, consts, ffn_f32, new_caches, after=(mix_s,))
        xp, xs = _finish(l, xp, mix, xs, mix_s, mods, t, ffn_bf16)
    ckv_new, *narrow = new_caches
    return (xp.reshape(BATCH, SEQ, D_MODEL), xs.reshape(DEC_BATCH, DEC_SEQ, D_MODEL), ckv_new,
            *(jnp.swapaxes(a, -1, -2) for a in narrow))
```
